```python
import jax, jax.numpy as jnp
from jax import lax
import numpy as np

D_MODEL = 2048
BATCH = 2
SEQ = 4096
DEPTH = 1

GM_WIDTH = D_MODEL // 2
GM_GROUP_DIM = 128
GM_GROUPS = GM_WIDTH // GM_GROUP_DIM
GM_CHUNK = 128
ML_HEADS = 4
ML_HEAD_DIM = (D_MODEL // 2) // ML_HEADS
ML_WIDTH = ML_HEADS * ML_HEAD_DIM
ML_CHUNK = 128
CONV_K = 4
D_FF = 4 * D_MODEL
EPS = 1e-6
IN_WIDTHS = (GM_WIDTH, GM_WIDTH, ML_WIDTH, ML_WIDTH, ML_WIDTH, ML_HEADS, ML_HEADS, D_MODEL, D_MODEL)
N_IN = 2 * GM_WIDTH + 3 * ML_WIDTH + 2 * ML_HEADS + 2 * D_MODEL

kernel_name = "hybrid_gmlp_mlstm_gated_block"


def _rms_norm(x, g):
    xf = x.astype(jnp.float32)
    y = xf * lax.rsqrt(jnp.mean(xf * xf, axis=-1, keepdims=True) + EPS)
    return (y * g.astype(jnp.float32)).astype(x.dtype)


def _layer_norm(x, g, b):
    xf = x.astype(jnp.float32)
    mu = jnp.mean(xf, axis=-1, keepdims=True)
    var = jnp.mean(jnp.square(xf - mu), axis=-1, keepdims=True)
    y = (xf - mu) * lax.rsqrt(var + EPS)
    return (y * g.astype(jnp.float32) + b.astype(jnp.float32)).astype(x.dtype)


def _gmlp_mixer(u, v, ln_g, ln_b, ws, bs):
    B, S, _ = u.shape
    u = jax.nn.gelu(u)
    v = _layer_norm(jax.nn.gelu(v), ln_g, ln_b)
    nc = S // GM_CHUNK
    vc = v.reshape(B, nc, GM_CHUNK, GM_GROUPS, GM_GROUP_DIM)
    causal = jnp.tril(jnp.ones((GM_CHUNK, GM_CHUNK), dtype=bool))
    ws_c = jnp.where(causal[None], ws, jnp.zeros_like(ws))
    s = jnp.einsum('gts,bnsgc->bntgc', ws_c, vc) + bs.T[:, :, None]
    return u * s.reshape(B, S, GM_WIDTH)


def _causal_conv(x, w, b):
    S = x.shape[1]
    xp = jnp.pad(x, ((0, 0), (CONV_K - 1, 0), (0, 0)))
    out = b
    for j in range(CONV_K):
        out = out + w[j] * xp[:, j:j + S]
    return out


def _mlstm_mixer(m_in, v_in, o_in, i_pre, f_pre, conv_w, conv_b, wq, wk, ig_b, fg_b, norm_g):
    B, S, _ = m_in.shape
    H, Dh, L = ML_HEADS, ML_HEAD_DIM, ML_CHUNK
    nc = S // L
    f32 = jnp.float32
    c = jax.nn.silu(_causal_conv(m_in, conv_w, conv_b)).reshape(B, S, H, Dh)
    q = jnp.einsum('bshd,hde->bhse', c, wq).astype(f32)
    k = (jnp.einsum('bshd,hde->bhse', c, wk) * (Dh ** -0.5)).astype(f32)
    v = v_in.reshape(B, S, H, Dh).transpose(0, 2, 1, 3).astype(f32)
    ig = (i_pre + ig_b).astype(f32).transpose(0, 2, 1)
    logf = jax.nn.log_sigmoid((f_pre + fg_b).astype(f32)).transpose(0, 2, 1)
    q = q.reshape(B, H, nc, L, Dh)
    k = k.reshape(B, H, nc, L, Dh)
    v = v.reshape(B, H, nc, L, Dh)
    ig = ig.reshape(B, H, nc, L)
    bcum = jnp.cumsum(logf.reshape(B, H, nc, L), axis=-1)
    b_last = bcum[..., -1]
    a = b_last[..., None] - bcum + ig
    a_max = jnp.max(a, axis=-1)
    wa = jnp.exp(a - a_max[..., None])
    kv = jnp.einsum('bhnl,bhnld,bhnle->bhnde', wa, k, v)
    ksum = jnp.einsum('bhnl,bhnld->bhnd', wa, k)

    def step(carry, inp):
        C, n, m = carry
        kv_c, ks_c, bl_c, am_c = inp
        m_new = jnp.maximum(bl_c + m, am_c)
        decay = jnp.exp(bl_c + m - m_new)
        scale = jnp.exp(am_c - m_new)
        C_new = decay[..., None, None] * C + scale[..., None, None] * kv_c
        n_new = decay[..., None] * n + scale[..., None] * ks_c
        return (C_new, n_new, m_new), (C, n, m)

    init = (jnp.zeros((B, H, Dh, Dh), f32), jnp.zeros((B, H, Dh), f32), jnp.zeros((B, H), f32))
    xs = (jnp.moveaxis(kv, 2, 0), jnp.moveaxis(ksum, 2, 0), jnp.moveaxis(b_last, 2, 0), jnp.moveaxis(a_max, 2, 0))
    _, (C_prev, n_prev, m_prev) = lax.scan(step, init, xs)
    C_prev = jnp.moveaxis(C_prev, 0, 2)
    n_prev = jnp.moveaxis(n_prev, 0, 2)
    m_prev = jnp.moveaxis(m_prev, 0, 2)

    causal = jnp.tril(jnp.ones((L, L), dtype=bool))
    Dlog = bcum[..., :, None] - bcum[..., None, :] + ig[..., None, :]
    Dlog = jnp.where(causal, Dlog, -jnp.inf)
    inter = bcum + m_prev[..., None]
    m_t = jnp.maximum(inter, jnp.max(Dlog, axis=-1))
    w_intra = jnp.exp(Dlog - m_t[..., None])
    w_inter = jnp.exp(inter - m_t)
    qk = jnp.einsum('bhntd,bhnsd->bhnts', q, k) * w_intra
    num = (w_inter[..., None] * jnp.einsum('bhntd,bhnde->bhnte', q, C_prev)
           + jnp.einsum('bhnts,bhnse->bhnte', qk, v))
    den = w_inter * jnp.einsum('bhntd,bhnd->bhnt', q, n_prev) + jnp.sum(qk, axis=-1)
    h = num / jnp.maximum(jnp.abs(den), jnp.exp(-m_t))[..., None]
    h = h.reshape(B, H, S, Dh).transpose(0, 2, 1, 3)
    mu = jnp.mean(h, axis=-1, keepdims=True)
    var = jnp.mean(jnp.square(h - mu), axis=-1, keepdims=True)
    h = ((h - mu) * lax.rsqrt(var + EPS)).reshape(B, S, ML_WIDTH) * norm_g.astype(f32)
    return (jax.nn.sigmoid(o_in.astype(f32)) * h).astype(m_in.dtype)


def setup_inputs(seed: int = 0) -> dict:
    key = jax.random.key(seed)
    ks = jax.random.split(key, 24)
    nrm = lambda k, shape, s: jax.random.normal(k, shape, jnp.float32) * s
    Ld = DEPTH
    return {
        "x": jax.random.normal(ks[0], (BATCH, SEQ, D_MODEL), jnp.float32),
        "norm1_g": 1.0 + nrm(ks[1], (Ld, D_MODEL), 0.02),
        "w_in": nrm(ks[2], (Ld, D_MODEL, N_IN), D_MODEL ** -0.5),
        "b_gate": nrm(ks[3], (Ld, 2, D_MODEL), 0.02),
        "gm_ln_g": 1.0 + nrm(ks[4], (Ld, GM_WIDTH), 0.02),
        "gm_ln_b": nrm(ks[5], (Ld, GM_WIDTH), 0.02),
        "gm_ws": nrm(ks[6], (Ld, GM_GROUPS, GM_CHUNK, GM_CHUNK), GM_CHUNK ** -0.5),
        "gm_bs": 1.0 + nrm(ks[7], (Ld, GM_GROUPS, GM_CHUNK), 0.02),
        "ml_conv_w": nrm(ks[8], (Ld, CONV_K, ML_WIDTH), CONV_K ** -0.5),
        "ml_conv_b": nrm(ks[9], (Ld, ML_WIDTH), 0.02),
        "ml_wq": nrm(ks[10], (Ld, ML_HEADS, ML_HEAD_DIM, ML_HEAD_DIM), ML_HEAD_DIM ** -0.5),
        "ml_wk": nrm(ks[11], (Ld, ML_HEADS, ML_HEAD_DIM, ML_HEAD_DIM), ML_HEAD_DIM ** -0.5),
        "ml_ig_b": nrm(ks[12], (Ld, ML_HEADS), 0.1),
        "ml_fg_b": jnp.linspace(3.0, 6.0, ML_HEADS, dtype=jnp.float32)[None] + nrm(ks[13], (Ld, ML_HEADS), 0.1),
        "ml_norm_g": 1.0 + nrm(ks[14], (Ld, ML_WIDTH), 0.02),
        "w_a": nrm(ks[15], (Ld, GM_WIDTH, D_MODEL), GM_WIDTH ** -0.5),
        "w_b": nrm(ks[16], (Ld, ML_WIDTH, D_MODEL), ML_WIDTH ** -0.5),
        "w_out": nrm(ks[17], (Ld, D_MODEL, D_MODEL), D_MODEL ** -0.5),
        "norm2_g": 1.0 + nrm(ks[18], (Ld, D_MODEL), 0.02),
        "w_ff1": nrm(ks[19], (Ld, D_MODEL, D_FF), D_MODEL ** -0.5),
        "w_ff2": nrm(ks[20], (Ld, D_FF, D_MODEL), D_FF ** -0.5),
        "norm_f_g": 1.0 + nrm(ks[21], (D_MODEL,), 0.02),
    }


def reference(x, norm1_g, w_in, b_gate, gm_ln_g, gm_ln_b, gm_ws, gm_bs, ml_conv_w, ml_conv_b,
              ml_wq, ml_wk, ml_ig_b, ml_fg_b, ml_norm_g, w_a, w_b, w_out, norm2_g, w_ff1, w_ff2,
              norm_f_g):
    split_idx = [int(i) for i in np.cumsum(IN_WIDTHS)[:-1]]
    for l in range(DEPTH):
        xn = _rms_norm(x, norm1_g[l])
        proj = xn @ w_in[l]
        gm_u, gm_v, ml_m, ml_v, ml_o, ml_i, ml_f, g_a, g_b = jnp.split(proj, split_idx, axis=-1)
        y_a = _gmlp_mixer(gm_u, gm_v, gm_ln_g[l], gm_ln_b[l], gm_ws[l], gm_bs[l])
        y_b = _mlstm_mixer(ml_m, ml_v, ml_o, ml_i, ml_f, ml_conv_w[l], ml_conv_b[l], ml_wq[l],
                           ml_wk[l], ml_ig_b[l], ml_fg_b[l], ml_norm_g[l])
        gate_a = jax.nn.sigmoid(g_a + b_gate[l, 0])
        gate_b = jax.nn.sigmoid(g_b + b_gate[l, 1])
        mixed = gate_a * (y_a @ w_a[l]) + gate_b * (y_b @ w_b[l])
        x = x + mixed @ w_out[l]
        hn = _rms_norm(x, norm2_g[l])
        x = x + jnp.square(jax.nn.relu(hn @ w_ff1[l])) @ w_ff2[l]
    return _rms_norm(x, norm_f_g)
```

```python
import functools

import jax
import jax.numpy as jnp
from jax import lax
from jax.experimental import pallas as pl
from jax.experimental.pallas import tpu as pltpu

EPS = 1e-6
GM_GROUP_DIM = 128
CHUNK = 128
ML_HEADS = 4
CONV_K = 4
LANES = 128
SUBLANES = 8
SECTION = 1024
SEC_U, SEC_V, SEC_M, SEC_MV, SEC_O, SEC_GA, SEC_GB = 0, 1, 2, 3, 4, 5, 7
N_SECTIONS = 9
VMEM_LIMIT = 56 * 1024 * 1024

BF16 = jnp.bfloat16
F32 = jnp.float32


def _sigmoid(x):
    return 1.0 / (1.0 + jnp.exp(-x))


def _log_sigmoid(x):
    return jnp.minimum(x, 0.0) - jnp.log1p(jnp.exp(-jnp.abs(x)))


def _rms_norm(x, g):
    return x * lax.rsqrt(jnp.mean(x * x, axis=-1, keepdims=True) + EPS) * g


def _dot(a, b):
    return jnp.dot(a, b, preferred_element_type=F32)


def _params(*semantics):
    return pltpu.CompilerParams(dimension_semantics=semantics, vmem_limit_bytes=VMEM_LIMIT)


def _inproj_kernel(x_ref, g1_ref, w_ref, wif_ref, bg_ref, lng_ref, lnb_ref, out_ref, gif_ref, xn_ref):
    j = pl.program_id(1)

    @pl.when(j == 0)
    def _():
        xn = _rms_norm(x_ref[...], g1_ref[...]).astype(BF16)
        xn_ref[...] = xn
        gif_ref[...] = _dot(xn, wif_ref[...])

    acc = _dot(xn_ref[...], w_ref[...])

    @pl.when(j == SEC_U)
    def _():
        out_ref[...] = jax.nn.gelu(acc).astype(BF16)

    @pl.when(j == SEC_V)
    def _():
        v = jax.nn.gelu(acc)
        mu = jnp.mean(v, axis=-1, keepdims=True)
        var = jnp.mean(jnp.square(v - mu), axis=-1, keepdims=True)
        out_ref[...] = ((v - mu) * lax.rsqrt(var + EPS) * lng_ref[...] + lnb_ref[...]).astype(BF16)

    @pl.when((j == SEC_M) | (j == SEC_MV))
    def _():
        out_ref[...] = acc.astype(BF16)

    @pl.when(j == SEC_O)
    def _():
        out_ref[...] = _sigmoid(acc).astype(BF16)

    @pl.when(j >= SEC_GA)
    def _():
        out_ref[...] = _sigmoid(acc + bg_ref[0]).astype(BF16)


def _inproj(x, g1, w_cat, w_if, b_gate4, ln_g, ln_b, tm):
    t, d = x.shape
    grid = (t // tm, N_SECTIONS)
    return pl.pallas_call(
        _inproj_kernel,
        grid=grid,
        in_specs=[
            pl.BlockSpec((tm, d), lambda i, j: (i, 0)),
            pl.BlockSpec((1, d), lambda i, j: (0, 0)),
            pl.BlockSpec((d, SECTION), lambda i, j: (0, j)),
            pl.BlockSpec((d, LANES), lambda i, j: (0, 0)),
            pl.BlockSpec((1, 1, SECTION), lambda i, j: (jnp.maximum(j - SEC_GA, 0), 0, 0)),
            pl.BlockSpec((1, SECTION), lambda i, j: (0, 0)),
            pl.BlockSpec((1, SECTION), lambda i, j: (0, 0)),
        ],
        out_specs=[
            pl.BlockSpec((None, tm, SECTION), lambda i, j: (j, i, 0)),
            pl.BlockSpec((tm, LANES), lambda i, j: (i, 0)),
        ],
        out_shape=[
            jax.ShapeDtypeStruct((N_SECTIONS, t, SECTION), BF16),
            jax.ShapeDtypeStruct((t, LANES), F32),
        ],
        scratch_shapes=[pltpu.VMEM((tm, d), BF16)],
        compiler_params=_params("arbitrary", "arbitrary"),
        name="inproj",
    )(x, g1, w_cat, w_if, b_gate4, ln_g, ln_b)


def _gmlp_kernel(u_ref, v_ref, ws_ref, bs_ref, out_ref, *, n_chunks, n_groups):
    row = lax.broadcasted_iota(jnp.int32, (CHUNK, CHUNK), 0)
    col = lax.broadcasted_iota(jnp.int32, (CHUNK, CHUNK), 1)
    causal = col <= row
    for g in range(n_groups):
        gs = slice(g * GM_GROUP_DIM, (g + 1) * GM_GROUP_DIM)
        w = jnp.where(causal, ws_ref[g], 0.0).astype(BF16)
        vg = jnp.concatenate([v_ref[c * CHUNK:(c + 1) * CHUNK, gs] for c in range(n_chunks)], axis=1)
        s = _dot(w, vg) + bs_ref[:, g:g + 1]
        for c in range(n_chunks):
            rs = slice(c * CHUNK, (c + 1) * CHUNK)
            sc = s[:, c * GM_GROUP_DIM:(c + 1) * GM_GROUP_DIM]
            out_ref[rs, gs] = (u_ref[rs, gs].astype(F32) * sc).astype(BF16)


def _gmlp(proj, ws, bs_t, tm):
    _, t, width = proj.shape
    n_groups = width // GM_GROUP_DIM
    kern = functools.partial(_gmlp_kernel, n_chunks=tm // CHUNK, n_groups=n_groups)
    return pl.pallas_call(
        kern,
        grid=(t // tm,),
        in_specs=[
            pl.BlockSpec((None, tm, width), lambda i: (SEC_U, i, 0)),
            pl.BlockSpec((None, tm, width), lambda i: (SEC_V, i, 0)),
            pl.BlockSpec((n_groups, CHUNK, CHUNK), lambda i: (0, 0, 0)),
            pl.BlockSpec((CHUNK, n_groups), lambda i: (0, 0)),
        ],
        out_specs=pl.BlockSpec((tm, width), lambda i: (i, 0)),
        out_shape=jax.ShapeDtypeStruct((t, width), BF16),
        compiler_params=_params("arbitrary"),
        name="gmlp",
    )(proj, proj, ws, bs_t)


def _mlstm_kernel(m_ref, v_ref, o_ref, gcol_ref, grow_ref, bcol_ref, brow_ref, cw_ref, cb_ref,
                  wq_ref, wk_ref, ng_ref, out_ref, ext_ref, c_ref, n_ref, mx_ref, *, head_dim):
    n = pl.program_id(1)

    @pl.when(n == 0)
    def _():
        ext_ref[0:SUBLANES, :] = jnp.zeros((SUBLANES, ext_ref.shape[1]), F32)
        c_ref[...] = jnp.zeros(c_ref.shape, F32)
        n_ref[...] = jnp.zeros(n_ref.shape, F32)
        mx_ref[...] = jnp.zeros(mx_ref.shape, F32)

    ext_ref[SUBLANES:, :] = m_ref[...].astype(F32)
    conv = cb_ref[...] + cw_ref[CONV_K - 1:CONV_K, :] * ext_ref[SUBLANES:, :]
    for d in range(1, CONV_K):
        conv = conv + cw_ref[CONV_K - 1 - d:CONV_K - d, :] * ext_ref[SUBLANES - d:SUBLANES - d + CHUNK, :]
    ext_ref[0:SUBLANES, :] = ext_ref[CHUNK:, :]
    c_act = (conv * _sigmoid(conv)).astype(BF16)

    row = lax.broadcasted_iota(jnp.int32, (CHUNK, CHUNK), 0)
    col = lax.broadcasted_iota(jnp.int32, (CHUNK, CHUNK), 1)
    causal = col <= row
    tril = causal.astype(F32)
    triu = (row <= col).astype(F32)

    g_col = gcol_ref[...] + brow_ref[...]
    g_row = grow_ref[...] + bcol_ref[...]
    bcum_col = jnp.dot(tril, _log_sigmoid(g_col), precision=lax.Precision.HIGHEST,
                       preferred_element_type=F32)
    bcum_row = jnp.dot(_log_sigmoid(g_row), triu, precision=lax.Precision.HIGHEST,
                       preferred_element_type=F32)

    for h in range(ML_HEADS):
        hs = slice(h * head_dim, (h + 1) * head_dim)
        q = _dot(c_act[:, hs], wq_ref[h])
        k = _dot(c_act[:, hs], wk_ref[h]) * (head_dim ** -0.5)
        qb = q.astype(BF16)
        kb = k.astype(BF16)
        vb = v_ref[:, hs]

        ig_c = g_col[:, h:h + 1]
        ig_r = g_row[h:h + 1, :]
        bc_c = bcum_col[:, ML_HEADS + h:ML_HEADS + h + 1]
        bc_r = bcum_row[ML_HEADS + h:ML_HEADS + h + 1, :]
        b_last = bc_r[:, CHUNK - 1:CHUNK]
        m_prev = mx_ref[h:h + 1, 0:1]
        c_prev = c_ref[h]
        n_prev = n_ref[h:h + 1, :]

        inter = bc_c + m_prev
        dlog = jnp.where(causal, bc_c - bc_r + ig_r, -jnp.inf)
        m_t = jnp.maximum(inter, jnp.max(dlog, axis=-1, keepdims=True))
        w_intra = jnp.exp(dlog - m_t)
        w_inter = jnp.exp(inter - m_t)
        qk = lax.dot_general(qb, kb, (((1,), (1,)), ((), ())), preferred_element_type=F32) * w_intra
        num = w_inter * _dot(qb, c_prev.astype(BF16)) + _dot(qk.astype(BF16), vb)
        den = w_inter * jnp.sum(q * n_prev, axis=-1, keepdims=True) + jnp.sum(qk, axis=-1, keepdims=True)
        hid = num / jnp.maximum(jnp.abs(den), jnp.exp(-m_t))
        mu = jnp.mean(hid, axis=-1, keepdims=True)
        var = jnp.mean(jnp.square(hid - mu), axis=-1, keepdims=True)
        hid = (hid - mu) * lax.rsqrt(var + EPS) * ng_ref[:, hs]
        out_ref[:, hs] = (o_ref[:, hs].astype(F32) * hid).astype(BF16)

        a_c = b_last - bc_c + ig_c
        a_max = jnp.max(a_c, axis=0, keepdims=True)
        kw = k * jnp.exp(a_c - a_max)
        kv = _dot(kw.T.astype(BF16), vb)
        ksum = jnp.sum(kw, axis=0, keepdims=True)
        m_new = jnp.maximum(b_last + m_prev, a_max)
        decay = jnp.exp(b_last + m_prev - m_new)
        scale = jnp.exp(a_max - m_new)
        c_ref[h] = decay * c_prev + scale * kv
        n_ref[h:h + 1, :] = decay * n_prev + scale * ksum
        mx_ref[h:h + 1, :] = jnp.broadcast_to(m_new, (1, LANES))


def _mlstm(proj, gates_col, gates_row, bias_col, bias_row, conv_w, conv_b, wq, wk, norm_g, batch):
    _, t, width = proj.shape
    head_dim = width // ML_HEADS
    n_chunks = t // batch // CHUNK
    rows = lambda b, n: b * n_chunks + n
    kern = functools.partial(_mlstm_kernel, head_dim=head_dim)
    const2 = lambda b, n: (0, 0)
    const3 = lambda b, n: (0, 0, 0)
    return pl.pallas_call(
        kern,
        grid=(batch, n_chunks),
        in_specs=[
            pl.BlockSpec((None, CHUNK, width), lambda b, n: (SEC_M, rows(b, n), 0)),
            pl.BlockSpec((None, CHUNK, width), lambda b, n: (SEC_MV, rows(b, n), 0)),
            pl.BlockSpec((None, CHUNK, width), lambda b, n: (SEC_O, rows(b, n), 0)),
            pl.BlockSpec((CHUNK, LANES), lambda b, n: (rows(b, n), 0)),
            pl.BlockSpec((2 * ML_HEADS, CHUNK), lambda b, n: (0, rows(b, n))),
            pl.BlockSpec((2 * ML_HEADS, 1), const2),
            pl.BlockSpec((1, LANES), const2),
            pl.BlockSpec((CONV_K, width), const2),
            pl.BlockSpec((1, width), const2),
            pl.BlockSpec((ML_HEADS, head_dim, head_dim), const3),
            pl.BlockSpec((ML_HEADS, head_dim, head_dim), const3),
            pl.BlockSpec((1, width), const2),
        ],
        out_specs=pl.BlockSpec((CHUNK, width), lambda b, n: (rows(b, n), 0)),
        out_shape=jax.ShapeDtypeStruct((t, width), BF16),
        scratch_shapes=[
            pltpu.VMEM((SUBLANES + CHUNK, width), F32),
            pltpu.VMEM((ML_HEADS, head_dim, head_dim), F32),
            pltpu.VMEM((ML_HEADS, head_dim), F32),
            pltpu.VMEM((ML_HEADS, LANES), F32),
        ],
        compiler_params=_params("arbitrary", "arbitrary"),
        name="mlstm",
    )(proj, proj, proj, gates_col, gates_row, bias_col, bias_row, conv_w, conv_b, wq, wk, norm_g)


def _merge_kernel(x_ref, ya_ref, yb_ref, ga0_ref, ga1_ref, gb0_ref, gb1_ref, wa_ref, wb_ref, wo_ref,
                  g2_ref, x1_ref, hn_ref):
    gate_a = jnp.concatenate([ga0_ref[...], ga1_ref[...]], axis=1).astype(F32)
    gate_b = jnp.concatenate([gb0_ref[...], gb1_ref[...]], axis=1).astype(F32)
    mixed = gate_a * _dot(ya_ref[...], wa_ref[...]) + gate_b * _dot(yb_ref[...], wb_ref[...])
    x1 = x_ref[...] + _dot(mixed.astype(BF16), wo_ref[...])
    x1_ref[...] = x1
    hn_ref[...] = _rms_norm(x1, g2_ref[...]).astype(BF16)


def _merge(x, y_a, y_b, proj, w_a, w_b, w_out, g2, tm):
    t, d = x.shape
    width = y_a.shape[1]
    sec = lambda s: pl.BlockSpec((None, tm, SECTION), lambda i: (s, i, 0))
    const = lambda i: (0, 0)
    return pl.pallas_call(
        _merge_kernel,
        grid=(t // tm,),
        in_specs=[
            pl.BlockSpec((tm, d), lambda i: (i, 0)),
            pl.BlockSpec((tm, width), lambda i: (i, 0)),
            pl.BlockSpec((tm, width), lambda i: (i, 0)),
            sec(SEC_GA), sec(SEC_GA + 1), sec(SEC_GB), sec(SEC_GB + 1),
            pl.BlockSpec((width, d), const),
            pl.BlockSpec((width, d), const),
            pl.BlockSpec((d, d), const),
            pl.BlockSpec((1, d), const),
        ],
        out_specs=[pl.BlockSpec((tm, d), lambda i: (i, 0)), pl.BlockSpec((tm, d), lambda i: (i, 0))],
        out_shape=[jax.ShapeDtypeStruct((t, d), F32), jax.ShapeDtypeStruct((t, d), BF16)],
        compiler_params=_params("arbitrary"),
        name="merge",
    )(x, y_a, y_b, proj, proj, proj, proj, w_a, w_b, w_out, g2)


def _ffn_kernel(hn_ref, x1_ref, w1_ref, w2_ref, gf_ref, out_ref, acc_ref, *, final_norm):
    j = pl.program_id(1)
    h = jnp.square(jnp.maximum(_dot(hn_ref[...], w1_ref[...]), 0.0)).astype(BF16)
    part = _dot(h, w2_ref[...])

    @pl.when(j == 0)
    def _():
        acc_ref[...] = part

    @pl.when(j > 0)
    def _():
        acc_ref[...] += part

    @pl.when(j == pl.num_programs(1) - 1)
    def _():
        y = x1_ref[...] + acc_ref[...]
        out_ref[...] = _rms_norm(y, gf_ref[...]) if final_norm else y


def _ffn(hn, x1, w1, w2, gf, tm, tf, final_norm):
    t, d = x1.shape
    d_ff = w1.shape[1]
    return pl.pallas_call(
        functools.partial(_ffn_kernel, final_norm=final_norm),
        grid=(t // tm, d_ff // tf),
        in_specs=[
            pl.BlockSpec((tm, d), lambda i, j: (i, 0)),
            pl.BlockSpec((tm, d), lambda i, j: (i, 0)),
            pl.BlockSpec((d, tf), lambda i, j: (0, j)),
            pl.BlockSpec((tf, d), lambda i, j: (j, 0)),
            pl.BlockSpec((1, d), lambda i, j: (0, 0)),
        ],
        out_specs=pl.BlockSpec((tm, d), lambda i, j: (i, 0)),
        out_shape=jax.ShapeDtypeStruct((t, d), F32),
        scratch_shapes=[pltpu.VMEM((tm, d), F32)],
        compiler_params=_params("arbitrary", "arbitrary"),
        name="ffn",
    )(hn, x1, w1, w2, gf)


def _tiles(t):
    return dict(inproj=min(t, 1024), gmlp=min(t, 1024), merge=min(t, 256), ffn_m=min(t, 512), ffn_f=1024)


def kernel(x, norm1_g, w_in, b_gate, gm_ln_g, gm_ln_b, gm_ws, gm_bs, ml_conv_w, ml_conv_b, ml_wq, ml_wk,
           ml_ig_b, ml_fg_b, ml_norm_g, w_a, w_b, w_out, norm2_g, w_ff1, w_ff2, norm_f_g):
    batch, seq, d = x.shape
    depth = w_in.shape[0]
    t = batch * seq
    tiles = _tiles(t)
    n_main = 5 * SECTION
    n_if = 2 * ML_HEADS
    xt = x.reshape(t, d)
    for l in range(depth):
        w_cat = jnp.concatenate([w_in[l, :, :n_main], w_in[l, :, n_main + n_if:]], axis=1).astype(BF16)
        w_if = jnp.pad(w_in[l, :, n_main:n_main + n_if], ((0, 0), (0, LANES - n_if))).astype(BF16)
        gate_bias = jnp.concatenate([ml_ig_b[l], ml_fg_b[l]])
        proj, gates_col = _inproj(
            xt, norm1_g[l][None], w_cat, w_if, b_gate[l].reshape(4, 1, SECTION),
            gm_ln_g[l][None], gm_ln_b[l][None], tiles["inproj"])
        y_a = _gmlp(proj, gm_ws[l], gm_bs[l].T, tiles["gmlp"])
        y_b = _mlstm(
            proj, gates_col, gates_col[:, :n_if].T, gate_bias[:, None],
            jnp.pad(gate_bias, (0, LANES - n_if))[None], ml_conv_w[l], ml_conv_b[l][None],
            ml_wq[l].astype(BF16), ml_wk[l].astype(BF16), ml_norm_g[l][None], batch)
        x1, hn = _merge(xt, y_a, y_b, proj, w_a[l].astype(BF16), w_b[l].astype(BF16),
                        w_out[l].astype(BF16), norm2_g[l][None], tiles["merge"])
        xt = _ffn(hn, x1, w_ff1[l].astype(BF16), w_ff2[l].astype(BF16), norm_f_g[None],
                  tiles["ffn_m"], tiles["ffn_f"], final_norm=l == depth - 1)
    return xt.reshape(batch, seq, d)
```

```python
import functools

import jax
import jax.numpy as jnp
from jax import lax
from jax.experimental import pallas as pl
from jax.experimental.pallas import tpu as pltpu

EPS = 1e-6
GM_GROUP_DIM = 128
CHUNK = 128
ML_HEADS = 4
CONV_K = 4
LANES = 128
SUBLANES = 8
BF16_ROWS = 16
SECTION = 1024
SEC_U, SEC_V, SEC_M, SEC_MV, SEC_O, SEC_GA, SEC_GB = 0, 1, 2, 3, 4, 5, 7
N_SECTIONS = 9
VMEM_LIMIT = 56 * 1024 * 1024

BF16 = jnp.bfloat16
F32 = jnp.float32


def _sigmoid(x):
    return 1.0 / (1.0 + jnp.exp(-x))


def _log_sigmoid(x):
    return jnp.minimum(x, 0.0) - jnp.log1p(jnp.exp(-jnp.abs(x)))


def _rms_norm(x, g):
    return x * lax.rsqrt(jnp.mean(x * x, axis=-1, keepdims=True) + EPS) * g


def _dot(a, b):
    return jnp.dot(a, b, preferred_element_type=F32)


def _params(*semantics):
    return pltpu.CompilerParams(dimension_semantics=semantics, vmem_limit_bytes=VMEM_LIMIT)


def _cast_rows(n_rows, n_steps):
    rows = n_rows // n_steps
    assert rows * n_steps == n_rows and rows % BF16_ROWS == 0, (n_rows, n_steps)
    return rows


def _prologue_kernel(x_ref, g1_ref, wif_ref, xn_ref, gif_ref):
    xn = _rms_norm(x_ref[...], g1_ref[...]).astype(BF16)
    xn_ref[...] = xn
    gif_ref[...] = _dot(xn, wif_ref[...].astype(BF16))


def _prologue(x, g1, w_in, layer, if_col, tm):
    t, d = x.shape
    return pl.pallas_call(
        _prologue_kernel,
        grid=(t // tm,),
        in_specs=[
            pl.BlockSpec((tm, d), lambda i: (i, 0)),
            pl.BlockSpec((1, d), lambda i: (0, 0)),
            pl.BlockSpec((None, d, LANES), lambda i: (layer, 0, if_col // LANES)),
        ],
        out_specs=[pl.BlockSpec((tm, d), lambda i: (i, 0)), pl.BlockSpec((tm, LANES), lambda i: (i, 0))],
        out_shape=[jax.ShapeDtypeStruct((t, d), BF16), jax.ShapeDtypeStruct((t, LANES), F32)],
        compiler_params=_params("arbitrary"),
        name="prologue",
    )(x, g1, w_in)


def _inproj_kernel(xn_ref, w_ref, wx_ref, bias_ref, lng_ref, lnb_ref, wa_ref, wb_ref, wo_ref, w1_ref,
                   out_ref, wa_out, wb_out, wo_out, w1_out, wbf_ref, *, shift, n_side):
    j = pl.program_id(0)
    i = pl.program_id(1)
    d = w_ref.shape[0]
    rows = 256

    @pl.when((i == 0) & (j < SEC_GA))
    def _():
        for r in range(0, d, rows):
            wbf_ref[r:r + rows, :] = w_ref[r:r + rows, :].astype(BF16)

    @pl.when((i == 0) & (j >= SEC_GA))
    def _():
        lane = lax.broadcasted_iota(jnp.int32, (rows, LANES), 1)
        for r in range(0, d, rows):
            rw = pltpu.roll(w_ref[r:r + rows, :], SECTION - shift, axis=1)
            rx = pltpu.roll(wx_ref[r:r + rows, :], LANES - shift, axis=1)
            last = jnp.where(lane < LANES - shift, rw[:, SECTION - LANES:], rx)
            wbf_ref[r:r + rows, :] = jnp.concatenate([rw[:, :SECTION - LANES], last], axis=1).astype(BF16)

    @pl.when(j * pl.num_programs(1) + i < n_side)
    def _():
        wa_out[...] = wa_ref[...].astype(BF16)
        wb_out[...] = wb_ref[...].astype(BF16)
        wo_out[...] = wo_ref[...].astype(BF16)
        w1_out[...] = w1_ref[...].astype(BF16)

    def proj():
        return _dot(xn_ref[...], wbf_ref[...])

    @pl.when(j == SEC_U)
    def _():
        out_ref[...] = jax.nn.gelu(proj()).astype(BF16)

    @pl.when(j == SEC_V)
    def _():
        v = jax.nn.gelu(proj())
        mu = jnp.mean(v, axis=-1, keepdims=True)
        var = jnp.mean(jnp.square(v - mu), axis=-1, keepdims=True)
        out_ref[...] = ((v - mu) * lax.rsqrt(var + EPS) * lng_ref[...] + lnb_ref[...]).astype(BF16)

    @pl.when((j == SEC_M) | (j == SEC_MV))
    def _():
        out_ref[...] = proj().astype(BF16)

    @pl.when(j >= SEC_O)
    def _():
        out_ref[...] = _sigmoid(proj() + bias_ref[0]).astype(BF16)


def _inproj(xn, w_in, layer, gate_col, bias5, ln_g, ln_b, w_a, w_b, w_out, w_ff1, tm):
    t, d = xn.shape
    n_i = t // tm
    n_side = (N_SECTIONS - 1) * n_i
    shift = gate_col % SECTION
    assert gate_col - shift == SEC_GA * SECTION and 0 < shift < LANES
    side = lambda j, i: (layer, jnp.minimum(j * n_i + i, n_side - 1), 0)
    side_out = lambda j, i: (jnp.minimum(j * n_i + i, n_side - 1), 0)
    side_ws = (w_a, w_b, w_out, w_ff1)
    side_rows = [_cast_rows(w.shape[1], n_side) for w in side_ws]
    kern = functools.partial(_inproj_kernel, shift=shift, n_side=n_side)
    return pl.pallas_call(
        kern,
        grid=(N_SECTIONS, n_i),
        in_specs=[
            pl.BlockSpec((tm, d), lambda j, i: (i, 0)),
            pl.BlockSpec((None, d, SECTION), lambda j, i: (layer, 0, j)),
            pl.BlockSpec((None, d, LANES),
                         lambda j, i: (layer, 0, (jnp.maximum(j, SEC_GA) + 1) * (SECTION // LANES))),
            pl.BlockSpec((1, 1, SECTION), lambda j, i: (jnp.maximum(j - SEC_O, 0), 0, 0)),
            pl.BlockSpec((1, SECTION), lambda j, i: (0, 0)),
            pl.BlockSpec((1, SECTION), lambda j, i: (0, 0)),
        ] + [pl.BlockSpec((None, r, w.shape[2]), side) for r, w in zip(side_rows, side_ws)],
        out_specs=[pl.BlockSpec((None, tm, SECTION), lambda j, i: (j, i, 0))]
        + [pl.BlockSpec((r, w.shape[2]), side_out) for r, w in zip(side_rows, side_ws)],
        out_shape=[jax.ShapeDtypeStruct((N_SECTIONS, t, SECTION), BF16)]
        + [jax.ShapeDtypeStruct(w.shape[1:], BF16) for w in side_ws],
        scratch_shapes=[pltpu.VMEM((d, SECTION), BF16)],
        compiler_params=_params("arbitrary", "arbitrary"),
        name="inproj",
    )(xn, w_in, w_in, bias5, ln_g, ln_b, *side_ws)


def _gmlp_kernel(u_ref, v_ref, ws_ref, bs_ref, out_ref, *, n_chunks, n_groups):
    row = lax.broadcasted_iota(jnp.int32, (CHUNK, CHUNK), 0)
    col = lax.broadcasted_iota(jnp.int32, (CHUNK, CHUNK), 1)
    causal = col <= row
    for g in range(n_groups):
        gs = slice(g * GM_GROUP_DIM, (g + 1) * GM_GROUP_DIM)
        w = jnp.where(causal, ws_ref[g], 0.0).astype(BF16)
        vg = jnp.concatenate([v_ref[c * CHUNK:(c + 1) * CHUNK, gs] for c in range(n_chunks)], axis=1)
        s = _dot(w, vg) + bs_ref[:, g:g + 1]
        for c in range(n_chunks):
            rs = slice(c * CHUNK, (c + 1) * CHUNK)
            sc = s[:, c * GM_GROUP_DIM:(c + 1) * GM_GROUP_DIM]
            out_ref[rs, gs] = (u_ref[rs, gs].astype(F32) * sc).astype(BF16)


def _gmlp(proj, ws, bs_t, tm):
    _, t, width = proj.shape
    n_groups = width // GM_GROUP_DIM
    kern = functools.partial(_gmlp_kernel, n_chunks=tm // CHUNK, n_groups=n_groups)
    return pl.pallas_call(
        kern,
        grid=(t // tm,),
        in_specs=[
            pl.BlockSpec((None, tm, width), lambda i: (SEC_U, i, 0)),
            pl.BlockSpec((None, tm, width), lambda i: (SEC_V, i, 0)),
            pl.BlockSpec((n_groups, CHUNK, CHUNK), lambda i: (0, 0, 0)),
            pl.BlockSpec((CHUNK, n_groups), lambda i: (0, 0)),
        ],
        out_specs=pl.BlockSpec((tm, width), lambda i: (i, 0)),
        out_shape=jax.ShapeDtypeStruct((t, width), BF16),
        compiler_params=_params("arbitrary"),
        name="gmlp",
    )(proj, proj, ws, bs_t)


def _mlstm_kernel(m_ref, v_ref, o_ref, gcol_ref, grow_ref, bcol_ref, brow_ref, cw_ref, cb_ref,
                  wq_ref, wk_ref, ng_ref, out_ref, ext_ref, c_ref, n_ref, mx_ref, wqb_ref, wkb_ref, *, head_dim):
    n = pl.program_id(1)

    @pl.when((pl.program_id(0) == 0) & (n == 0))
    def _():
        wqb_ref[...] = wq_ref[...].astype(BF16)
        wkb_ref[...] = wk_ref[...].astype(BF16)

    @pl.when(n == 0)
    def _():
        ext_ref[0:SUBLANES, :] = jnp.zeros((SUBLANES, ext_ref.shape[1]), F32)
        c_ref[...] = jnp.zeros(c_ref.shape, F32)
        n_ref[...] = jnp.zeros(n_ref.shape, F32)
        mx_ref[...] = jnp.zeros(mx_ref.shape, F32)

    ext_ref[SUBLANES:, :] = m_ref[...].astype(F32)
    conv = cb_ref[...] + cw_ref[CONV_K - 1:CONV_K, :] * ext_ref[SUBLANES:, :]
    for d in range(1, CONV_K):
        conv = conv + cw_ref[CONV_K - 1 - d:CONV_K - d, :] * ext_ref[SUBLANES - d:SUBLANES - d + CHUNK, :]
    ext_ref[0:SUBLANES, :] = ext_ref[CHUNK:, :]
    c_act = (conv * _sigmoid(conv)).astype(BF16)

    row = lax.broadcasted_iota(jnp.int32, (CHUNK, CHUNK), 0)
    col = lax.broadcasted_iota(jnp.int32, (CHUNK, CHUNK), 1)
    causal = col <= row
    tril = causal.astype(F32)
    triu = (row <= col).astype(F32)

    g_col = gcol_ref[...] + brow_ref[...]
    g_row = grow_ref[...] + bcol_ref[...]
    bcum_col = jnp.dot(tril, _log_sigmoid(g_col), precision=lax.Precision.HIGHEST,
                       preferred_element_type=F32)
    bcum_row = jnp.dot(_log_sigmoid(g_row), triu, precision=lax.Precision.HIGHEST,
                       preferred_element_type=F32)

    for h in range(ML_HEADS):
        hs = slice(h * head_dim, (h + 1) * head_dim)
        q = _dot(c_act[:, hs], wqb_ref[h])
        k = _dot(c_act[:, hs], wkb_ref[h]) * (head_dim ** -0.5)
        qb = q.astype(BF16)
        kb = k.astype(BF16)
        vb = v_ref[:, hs]

        ig_c = g_col[:, h:h + 1]
        ig_r = g_row[h:h + 1, :]
        bc_c = bcum_col[:, ML_HEADS + h:ML_HEADS + h + 1]
        bc_r = bcum_row[ML_HEADS + h:ML_HEADS + h + 1, :]
        b_last = bc_r[:, CHUNK - 1:CHUNK]
        m_prev = mx_ref[h:h + 1, 0:1]
        c_prev = c_ref[h]
        n_prev = n_ref[h:h + 1, :]

        inter = bc_c + m_prev
        dlog = jnp.where(causal, bc_c - bc_r + ig_r, -jnp.inf)
        m_t = jnp.maximum(inter, jnp.max(dlog, axis=-1, keepdims=True))
        w_intra = jnp.exp(dlog - m_t)
        w_inter = jnp.exp(inter - m_t)
        qk = lax.dot_general(qb, kb, (((1,), (1,)), ((), ())), preferred_element_type=F32) * w_intra
        num = w_inter * _dot(qb, c_prev.astype(BF16)) + _dot(qk.astype(BF16), vb)
        den = w_inter * jnp.sum(q * n_prev, axis=-1, keepdims=True) + jnp.sum(qk, axis=-1, keepdims=True)
        hid = num / jnp.maximum(jnp.abs(den), jnp.exp(-m_t))
        mu = jnp.mean(hid, axis=-1, keepdims=True)
        var = jnp.mean(jnp.square(hid - mu), axis=-1, keepdims=True)
        hid = (hid - mu) * lax.rsqrt(var + EPS) * ng_ref[:, hs]
        out_ref[:, hs] = (o_ref[:, hs].astype(F32) * hid).astype(BF16)

        a_c = b_last - bc_c + ig_c
        a_max = jnp.max(a_c, axis=0, keepdims=True)
        kw = k * jnp.exp(a_c - a_max)
        kv = _dot(kw.T.astype(BF16), vb)
        ksum = jnp.sum(kw, axis=0, keepdims=True)
        m_new = jnp.maximum(b_last + m_prev, a_max)
        decay = jnp.exp(b_last + m_prev - m_new)
        scale = jnp.exp(a_max - m_new)
        c_ref[h] = decay * c_prev + scale * kv
        n_ref[h:h + 1, :] = decay * n_prev + scale * ksum
        mx_ref[h:h + 1, :] = jnp.broadcast_to(m_new, (1, LANES))


def _mlstm(proj, gates_col, gates_row, bias_col, bias_row, conv_w, conv_b, wq, wk, norm_g, batch):
    _, t, width = proj.shape
    head_dim = width // ML_HEADS
    n_chunks = t // batch // CHUNK
    rows = lambda b, n: b * n_chunks + n
    kern = functools.partial(_mlstm_kernel, head_dim=head_dim)
    const2 = lambda b, n: (0, 0)
    const3 = lambda b, n: (0, 0, 0)
    return pl.pallas_call(
        kern,
        grid=(batch, n_chunks),
        in_specs=[
            pl.BlockSpec((None, CHUNK, width), lambda b, n: (SEC_M, rows(b, n), 0)),
            pl.BlockSpec((None, CHUNK, width), lambda b, n: (SEC_MV, rows(b, n), 0)),
            pl.BlockSpec((None, CHUNK, width), lambda b, n: (SEC_O, rows(b, n), 0)),
            pl.BlockSpec((CHUNK, LANES), lambda b, n: (rows(b, n), 0)),
            pl.BlockSpec((2 * ML_HEADS, CHUNK), lambda b, n: (0, rows(b, n))),
            pl.BlockSpec((2 * ML_HEADS, 1), const2),
            pl.BlockSpec((1, LANES), const2),
            pl.BlockSpec((CONV_K, width), const2),
            pl.BlockSpec((1, width), const2),
            pl.BlockSpec((ML_HEADS, head_dim, head_dim), const3),
            pl.BlockSpec((ML_HEADS, head_dim, head_dim), const3),
            pl.BlockSpec((1, width), const2),
        ],
        out_specs=pl.BlockSpec((CHUNK, width), lambda b, n: (rows(b, n), 0)),
        out_shape=jax.ShapeDtypeStruct((t, width), BF16),
        scratch_shapes=[
            pltpu.VMEM((SUBLANES + CHUNK, width), F32),
            pltpu.VMEM((ML_HEADS, head_dim, head_dim), F32),
            pltpu.VMEM((ML_HEADS, head_dim), F32),
            pltpu.VMEM((ML_HEADS, LANES), F32),
            pltpu.VMEM((ML_HEADS, head_dim, head_dim), BF16),
            pltpu.VMEM((ML_HEADS, head_dim, head_dim), BF16),
        ],
        compiler_params=_params("arbitrary", "arbitrary"),
        name="mlstm",
    )(proj, proj, proj, gates_col, gates_row, bias_col, bias_row, conv_w, conv_b, wq, wk, norm_g)


def _merge_kernel(x_ref, ya_ref, yb_ref, ga0_ref, ga1_ref, gb0_ref, gb1_ref, wa_ref, wb_ref, wo_ref,
                  g2_ref, w2_ref, x1_ref, hn_ref, w2_out):
    w2_out[...] = w2_ref[...].astype(BF16)
    gate_a = jnp.concatenate([ga0_ref[...], ga1_ref[...]], axis=1).astype(F32)
    gate_b = jnp.concatenate([gb0_ref[...], gb1_ref[...]], axis=1).astype(F32)
    mixed = gate_a * _dot(ya_ref[...], wa_ref[...]) + gate_b * _dot(yb_ref[...], wb_ref[...])
    x1 = x_ref[...] + _dot(mixed.astype(BF16), wo_ref[...])
    x1_ref[...] = x1
    hn_ref[...] = _rms_norm(x1, g2_ref[...]).astype(BF16)


def _merge(x, y_a, y_b, proj, w_a, w_b, w_out, g2, w_ff2, layer, tm):
    t, d = x.shape
    width = y_a.shape[1]
    n_steps = t // tm
    d_ff = w_ff2.shape[1]
    rows = _cast_rows(d_ff, n_steps)
    sec = lambda s: pl.BlockSpec((None, tm, SECTION), lambda i: (s, i, 0))
    const = lambda i: (0, 0)
    resident = dict(pipeline_mode=pl.Buffered(1))
    return pl.pallas_call(
        _merge_kernel,
        grid=(n_steps,),
        in_specs=[
            pl.BlockSpec((tm, d), lambda i: (i, 0)),
            pl.BlockSpec((tm, width), lambda i: (i, 0)),
            pl.BlockSpec((tm, width), lambda i: (i, 0)),
            sec(SEC_GA), sec(SEC_GA + 1), sec(SEC_GB), sec(SEC_GB + 1),
            pl.BlockSpec((width, d), const, **resident),
            pl.BlockSpec((width, d), const, **resident),
            pl.BlockSpec((d, d), const, **resident),
            pl.BlockSpec((1, d), const),
            pl.BlockSpec((None, rows, d), lambda i: (layer, i, 0)),
        ],
        out_specs=[
            pl.BlockSpec((tm, d), lambda i: (i, 0)),
            pl.BlockSpec((tm, d), lambda i: (i, 0)),
            pl.BlockSpec((rows, d), lambda i: (i, 0)),
        ],
        out_shape=[
            jax.ShapeDtypeStruct((t, d), F32),
            jax.ShapeDtypeStruct((t, d), BF16),
            jax.ShapeDtypeStruct((d_ff, d), BF16),
        ],
        compiler_params=_params("arbitrary"),
        name="merge",
    )(x, y_a, y_b, proj, proj, proj, proj, w_a, w_b, w_out, g2, w_ff2)


def _ffn_kernel(hn_ref, x1_ref, w1_ref, w2_ref, gf_ref, out_ref, acc_ref, *, final_norm):
    j = pl.program_id(1)

    @pl.when(j == 0)
    def _():
        acc_ref[...] = x1_ref[...]

    h = jnp.square(jnp.maximum(_dot(hn_ref[...], w1_ref[...]), 0.0)).astype(BF16)
    acc_ref[...] += _dot(h, w2_ref[...])

    @pl.when(j == pl.num_programs(1) - 1)
    def _():
        y = acc_ref[...]
        out_ref[...] = _rms_norm(y, gf_ref[...]) if final_norm else y


def _ffn(hn, x1, w1, w2, gf, tm, tf, final_norm):
    t, d = x1.shape
    d_ff = w1.shape[1]
    return pl.pallas_call(
        functools.partial(_ffn_kernel, final_norm=final_norm),
        grid=(t // tm, d_ff // tf),
        in_specs=[
            pl.BlockSpec((tm, d), lambda i, j: (i, 0)),
            pl.BlockSpec((tm, d), lambda i, j: (i, 0)),
            pl.BlockSpec((d, tf), lambda i, j: (0, j)),
            pl.BlockSpec((tf, d), lambda i, j: (j, 0)),
            pl.BlockSpec((1, d), lambda i, j: (0, 0)),
        ],
        out_specs=pl.BlockSpec((tm, d), lambda i, j: (i, 0)),
        out_shape=jax.ShapeDtypeStruct((t, d), F32),
        scratch_shapes=[pltpu.VMEM((tm, d), F32)],
        compiler_params=_params("arbitrary", "arbitrary"),
        name="ffn",
    )(hn, x1, w1, w2, gf)


def _tiles(t):
    return dict(prologue=min(t, 512), inproj=min(t, 1024), gmlp=min(t, 1024), merge=min(t, 256),
                ffn_m=min(t, 512), ffn_f=1024)


def kernel(x, norm1_g, w_in, b_gate, gm_ln_g, gm_ln_b, gm_ws, gm_bs, ml_conv_w, ml_conv_b, ml_wq, ml_wk,
           ml_ig_b, ml_fg_b, ml_norm_g, w_a, w_b, w_out, norm2_g, w_ff1, w_ff2, norm_f_g):
    batch, seq, d = x.shape
    depth = w_in.shape[0]
    t = batch * seq
    tiles = _tiles(t)
    if_col = SEC_GA * SECTION
    n_if = 2 * ML_HEADS
    xt = x.reshape(t, d)
    for l in range(depth):
        gate_bias = jnp.concatenate([ml_ig_b[l], ml_fg_b[l]])
        bias5 = jnp.concatenate([jnp.zeros((1, SECTION), F32), b_gate[l].reshape(4, SECTION)])[:, None]
        xn, gates_col = _prologue(xt, norm1_g[l][None], w_in, l, if_col, tiles["prologue"])
        proj, wa_bf, wb_bf, wo_bf, w1_bf = _inproj(
            xn, w_in, l, if_col + n_if, bias5, gm_ln_g[l][None], gm_ln_b[l][None],
            w_a, w_b, w_out, w_ff1, tiles["inproj"])
        y_a = _gmlp(proj, gm_ws[l], gm_bs[l].T, tiles["gmlp"])
        y_b = _mlstm(
            proj, gates_col, gates_col[:, :n_if].T, gate_bias[:, None],
            jnp.pad(gate_bias, (0, LANES - n_if))[None], ml_conv_w[l], ml_conv_b[l][None],
            ml_wq[l], ml_wk[l], ml_norm_g[l][None], batch)
        x1, hn, w2_bf = _merge(xt, y_a, y_b, proj, wa_bf, wb_bf, wo_bf, norm2_g[l][None], w_ff2, l,
                               tiles["merge"])
        xt = _ffn(hn, x1, w1_bf, w2_bf, norm_f_g[None], tiles["ffn_m"], tiles["ffn_f"],
                  final_norm=l == depth - 1)
    return xt.reshape(batch, seq, d)
```

```python
import functools

import jax
import jax.numpy as jnp
from jax import lax
from jax.experimental import pallas as pl
from jax.experimental.pallas import tpu as pltpu

EPS = 1e-6
GM_GROUP_DIM = 128
CHUNK = 128
ML_HEADS = 4
CONV_K = 4
LANES = 128
SUBLANES = 8
BF16_ROWS = 16
SECTION = 1024
SEC_U, SEC_V, SEC_M, SEC_MV, SEC_O, SEC_GA, SEC_GB = 0, 1, 2, 3, 4, 5, 7
N_SECTIONS = 9
VMEM_LIMIT = 56 * 1024 * 1024

BF16 = jnp.bfloat16
F32 = jnp.float32


def _sigmoid(x):
    return 1.0 / (1.0 + jnp.exp(-x))


def _log_sigmoid(x):
    return jnp.minimum(x, 0.0) - jnp.log1p(jnp.exp(-jnp.abs(x)))


def _rms_norm(x, g):
    return x * lax.rsqrt(jnp.mean(x * x, axis=-1, keepdims=True) + EPS) * g


def _dot(a, b):
    return jnp.dot(a, b, preferred_element_type=F32)


def _params(*semantics):
    return pltpu.CompilerParams(dimension_semantics=semantics, vmem_limit_bytes=VMEM_LIMIT)


def _cast_rows(n_rows, n_steps):
    rows = n_rows // n_steps
    assert rows * n_steps == n_rows and rows % BF16_ROWS == 0, (n_rows, n_steps)
    return rows


def _nt_dot(a, b):
    return lax.dot_general(a, b, (((1,), (1,)), ((), ())), preferred_element_type=F32)


def _prologue_kernel(x_ref, g1_ref, wif_ref, xn_ref, gcol_ref, grow_ref):
    xn = _rms_norm(x_ref[...], g1_ref[...]).astype(BF16)
    xn_ref[...] = xn
    wif = wif_ref[...].astype(BF16)
    gcol_ref[...] = _nt_dot(xn, wif)
    grow_ref[...] = _nt_dot(wif, xn)[:grow_ref.shape[0], :]


def _prologue(x, g1, w_t, layer, if_row, n_if, tm):
    t, d = x.shape
    return pl.pallas_call(
        _prologue_kernel,
        grid=(t // tm,),
        in_specs=[
            pl.BlockSpec((tm, d), lambda i: (i, 0)),
            pl.BlockSpec((1, d), lambda i: (0, 0)),
            pl.BlockSpec((None, LANES, d), lambda i: (layer, if_row // LANES, 0)),
        ],
        out_specs=[
            pl.BlockSpec((tm, d), lambda i: (i, 0)),
            pl.BlockSpec((tm, LANES), lambda i: (i, 0)),
            pl.BlockSpec((n_if, tm), lambda i: (0, i)),
        ],
        out_shape=[
            jax.ShapeDtypeStruct((t, d), BF16),
            jax.ShapeDtypeStruct((t, LANES), F32),
            jax.ShapeDtypeStruct((n_if, t), F32),
        ],
        compiler_params=_params("arbitrary"),
        name="prologue",
    )(x, g1, w_t)


def _inproj_kernel(xn_ref, w_ref, bias_ref, lng_ref, lnb_ref, wa_ref, wb_ref, wo_ref, w1_ref,
                   out_ref, wa_out, wb_out, wo_out, w1_out, wbf_ref, *, n_side):
    j = pl.program_id(0)
    i = pl.program_id(1)

    @pl.when(i == 0)
    def _():
        rows = 256
        for r in range(0, wbf_ref.shape[0], rows):
            wbf_ref[r:r + rows, :] = w_ref[r:r + rows, :].astype(BF16)

    @pl.when(j * pl.num_programs(1) + i < n_side)
    def _():
        wa_out[...] = wa_ref[...].astype(BF16)
        wb_out[...] = wb_ref[...].astype(BF16)
        wo_out[...] = wo_ref[...].astype(BF16)
        w1_out[...] = w1_ref[...].astype(BF16)

    def proj():
        return _nt_dot(xn_ref[...], wbf_ref[...])

    @pl.when(j == SEC_U)
    def _():
        out_ref[...] = jax.nn.gelu(proj()).astype(BF16)

    @pl.when(j == SEC_V)
    def _():
        v = jax.nn.gelu(proj())
        mu = jnp.mean(v, axis=-1, keepdims=True)
        var = jnp.mean(jnp.square(v - mu), axis=-1, keepdims=True)
        out_ref[...] = ((v - mu) * lax.rsqrt(var + EPS) * lng_ref[...] + lnb_ref[...]).astype(BF16)

    @pl.when((j == SEC_M) | (j == SEC_MV))
    def _():
        out_ref[...] = proj().astype(BF16)

    @pl.when(j >= SEC_O)
    def _():
        out_ref[...] = _sigmoid(proj() + bias_ref[0]).astype(BF16)


def _inproj(xn, w_t, layer, gate_row, bias5, ln_g, ln_b, w_a, w_b, w_out, w_ff1, tm):
    t, d = xn.shape
    n_i = t // tm
    n_side = (N_SECTIONS - 1) * n_i
    gate_skip = gate_row - SEC_GA * SECTION
    assert gate_skip % SUBLANES == 0
    side = lambda j, i: (layer, jnp.minimum(j * n_i + i, n_side - 1), 0)
    side_out = lambda j, i: (jnp.minimum(j * n_i + i, n_side - 1), 0)
    side_ws = (w_a, w_b, w_out, w_ff1)
    side_rows = [_cast_rows(w.shape[1], n_side) for w in side_ws]
    kern = functools.partial(_inproj_kernel, n_side=n_side)
    return pl.pallas_call(
        kern,
        grid=(N_SECTIONS, n_i),
        in_specs=[
            pl.BlockSpec((tm, d), lambda j, i: (i, 0)),
            pl.BlockSpec((None, pl.Element(SECTION), pl.Element(d)),
                         lambda j, i: (layer, SUBLANES * (j * (SECTION // SUBLANES) + jnp.where(
                             j >= SEC_GA, gate_skip // SUBLANES, 0)), 0)),
            pl.BlockSpec((1, 1, SECTION), lambda j, i: (jnp.maximum(j - SEC_O, 0), 0, 0)),
            pl.BlockSpec((1, SECTION), lambda j, i: (0, 0)),
            pl.BlockSpec((1, SECTION), lambda j, i: (0, 0)),
        ] + [pl.BlockSpec((None, r, w.shape[2]), side) for r, w in zip(side_rows, side_ws)],
        out_specs=[pl.BlockSpec((None, tm, SECTION), lambda j, i: (j, i, 0))]
        + [pl.BlockSpec((r, w.shape[2]), side_out) for r, w in zip(side_rows, side_ws)],
        out_shape=[jax.ShapeDtypeStruct((N_SECTIONS, t, SECTION), BF16)]
        + [jax.ShapeDtypeStruct(w.shape[1:], BF16) for w in side_ws],
        scratch_shapes=[pltpu.VMEM((SECTION, d), BF16)],
        compiler_params=_params("arbitrary", "arbitrary"),
        name="inproj",
    )(xn, w_t, bias5, ln_g, ln_b, *side_ws)


def _gmlp_kernel(u_ref, v_ref, ws_ref, bs_ref, out_ref, *, n_chunks, n_groups):
    row = lax.broadcasted_iota(jnp.int32, (CHUNK, CHUNK), 0)
    col = lax.broadcasted_iota(jnp.int32, (CHUNK, CHUNK), 1)
    causal = col <= row
    for g in range(n_groups):
        gs = slice(g * GM_GROUP_DIM, (g + 1) * GM_GROUP_DIM)
        w = jnp.where(causal, ws_ref[g], 0.0).astype(BF16)
        vg = jnp.concatenate([v_ref[c * CHUNK:(c + 1) * CHUNK, gs] for c in range(n_chunks)], axis=1)
        s = _dot(w, vg) + bs_ref[:, g:g + 1]
        for c in range(n_chunks):
            rs = slice(c * CHUNK, (c + 1) * CHUNK)
            sc = s[:, c * GM_GROUP_DIM:(c + 1) * GM_GROUP_DIM]
            out_ref[rs, gs] = (u_ref[rs, gs].astype(F32) * sc).astype(BF16)


def _gmlp(proj, ws, bs_t, tm):
    _, t, width = proj.shape
    n_groups = width // GM_GROUP_DIM
    kern = functools.partial(_gmlp_kernel, n_chunks=tm // CHUNK, n_groups=n_groups)
    return pl.pallas_call(
        kern,
        grid=(t // tm,),
        in_specs=[
            pl.BlockSpec((None, tm, width), lambda i: (SEC_U, i, 0)),
            pl.BlockSpec((None, tm, width), lambda i: (SEC_V, i, 0)),
            pl.BlockSpec((n_groups, CHUNK, CHUNK), lambda i: (0, 0, 0)),
            pl.BlockSpec((CHUNK, n_groups), lambda i: (0, 0)),
        ],
        out_specs=pl.BlockSpec((tm, width), lambda i: (i, 0)),
        out_shape=jax.ShapeDtypeStruct((t, width), BF16),
        compiler_params=_params("arbitrary"),
        name="gmlp",
    )(proj, proj, ws, bs_t)


def _mlstm_kernel(m_ref, v_ref, o_ref, gcol_ref, grow_ref, bcol_ref, brow_ref, cw_ref, cb_ref,
                  wq_ref, wk_ref, ng_ref, out_ref, ext_ref, c_ref, n_ref, mx_ref, wqb_ref, wkb_ref, *, head_dim):
    n = pl.program_id(1)

    @pl.when((pl.program_id(0) == 0) & (n == 0))
    def _():
        wqb_ref[...] = wq_ref[...].astype(BF16)
        wkb_ref[...] = wk_ref[...].astype(BF16)

    @pl.when(n == 0)
    def _():
        ext_ref[0:SUBLANES, :] = jnp.zeros((SUBLANES, ext_ref.shape[1]), F32)
        c_ref[...] = jnp.zeros(c_ref.shape, F32)
        n_ref[...] = jnp.zeros(n_ref.shape, F32)
        mx_ref[...] = jnp.zeros(mx_ref.shape, F32)

    ext_ref[SUBLANES:, :] = m_ref[...].astype(F32)
    conv = cb_ref[...] + cw_ref[CONV_K - 1:CONV_K, :] * ext_ref[SUBLANES:, :]
    for d in range(1, CONV_K):
        conv = conv + cw_ref[CONV_K - 1 - d:CONV_K - d, :] * ext_ref[SUBLANES - d:SUBLANES - d + CHUNK, :]
    ext_ref[0:SUBLANES, :] = ext_ref[CHUNK:, :]
    c_act = (conv * _sigmoid(conv)).astype(BF16)

    row = lax.broadcasted_iota(jnp.int32, (CHUNK, CHUNK), 0)
    col = lax.broadcasted_iota(jnp.int32, (CHUNK, CHUNK), 1)
    causal = col <= row
    tril = causal.astype(F32)
    triu = (row <= col).astype(F32)

    g_col = gcol_ref[...] + brow_ref[...]
    g_row = grow_ref[...] + bcol_ref[...]
    bcum_col = jnp.dot(tril, _log_sigmoid(g_col), precision=lax.Precision.HIGHEST,
                       preferred_element_type=F32)
    bcum_row = jnp.dot(_log_sigmoid(g_row), triu, precision=lax.Precision.HIGHEST,
                       preferred_element_type=F32)

    for h in range(ML_HEADS):
        hs = slice(h * head_dim, (h + 1) * head_dim)
        q = _dot(c_act[:, hs], wqb_ref[h])
        k = _dot(c_act[:, hs], wkb_ref[h]) * (head_dim ** -0.5)
        qb = q.astype(BF16)
        kb = k.astype(BF16)
        vb = v_ref[:, hs]

        ig_c = g_col[:, h:h + 1]
        ig_r = g_row[h:h + 1, :]
        bc_c = bcum_col[:, ML_HEADS + h:ML_HEADS + h + 1]
        bc_r = bcum_row[ML_HEADS + h:ML_HEADS + h + 1, :]
        b_last = bc_r[:, CHUNK - 1:CHUNK]
        m_prev = mx_ref[h:h + 1, 0:1]
        c_prev = c_ref[h]
        n_prev = n_ref[h:h + 1, :]

        inter = bc_c + m_prev
        dlog = jnp.where(causal, bc_c - bc_r + ig_r, -jnp.inf)
        m_t = jnp.maximum(inter, jnp.max(dlog, axis=-1, keepdims=True))
        w_intra = jnp.exp(dlog - m_t)
        w_inter = jnp.exp(inter - m_t)
        qk = lax.dot_general(qb, kb, (((1,), (1,)), ((), ())), preferred_element_type=F32) * w_intra
        num = w_inter * _dot(qb, c_prev.astype(BF16)) + _dot(qk.astype(BF16), vb)
        den = w_inter * jnp.sum(q * n_prev, axis=-1, keepdims=True) + jnp.sum(qk, axis=-1, keepdims=True)
        hid = num / jnp.maximum(jnp.abs(den), jnp.exp(-m_t))
        mu = jnp.mean(hid, axis=-1, keepdims=True)
        var = jnp.mean(jnp.square(hid - mu), axis=-1, keepdims=True)
        hid = (hid - mu) * lax.rsqrt(var + EPS) * ng_ref[:, hs]
        out_ref[:, hs] = (o_ref[:, hs].astype(F32) * hid).astype(BF16)

        a_c = b_last - bc_c + ig_c
        a_max = jnp.max(a_c, axis=0, keepdims=True)
        kw = k * jnp.exp(a_c - a_max)
        kv = _dot(kw.T.astype(BF16), vb)
        ksum = jnp.sum(kw, axis=0, keepdims=True)
        m_new = jnp.maximum(b_last + m_prev, a_max)
        decay = jnp.exp(b_last + m_prev - m_new)
        scale = jnp.exp(a_max - m_new)
        c_ref[h] = decay * c_prev + scale * kv
        n_ref[h:h + 1, :] = decay * n_prev + scale * ksum
        mx_ref[h:h + 1, :] = jnp.broadcast_to(m_new, (1, LANES))


def _mlstm(proj, gates_col, gates_row, bias_col, bias_row, conv_w, conv_b, wq, wk, norm_g, batch):
    _, t, width = proj.shape
    head_dim = width // ML_HEADS
    n_chunks = t // batch // CHUNK
    rows = lambda b, n: b * n_chunks + n
    kern = functools.partial(_mlstm_kernel, head_dim=head_dim)
    const2 = lambda b, n: (0, 0)
    const3 = lambda b, n: (0, 0, 0)
    return pl.pallas_call(
        kern,
        grid=(batch, n_chunks),
        in_specs=[
            pl.BlockSpec((None, CHUNK, width), lambda b, n: (SEC_M, rows(b, n), 0)),
            pl.BlockSpec((None, CHUNK, width), lambda b, n: (SEC_MV, rows(b, n), 0)),
            pl.BlockSpec((None, CHUNK, width), lambda b, n: (SEC_O, rows(b, n), 0)),
            pl.BlockSpec((CHUNK, LANES), lambda b, n: (rows(b, n), 0)),
            pl.BlockSpec((2 * ML_HEADS, CHUNK), lambda b, n: (0, rows(b, n))),
            pl.BlockSpec((2 * ML_HEADS, 1), const2),
            pl.BlockSpec((1, LANES), const2),
            pl.BlockSpec((CONV_K, width), const2),
            pl.BlockSpec((1, width), const2),
            pl.BlockSpec((ML_HEADS, head_dim, head_dim), const3),
            pl.BlockSpec((ML_HEADS, head_dim, head_dim), const3),
            pl.BlockSpec((1, width), const2),
        ],
        out_specs=pl.BlockSpec((CHUNK, width), lambda b, n: (rows(b, n), 0)),
        out_shape=jax.ShapeDtypeStruct((t, width), BF16),
        scratch_shapes=[
            pltpu.VMEM((SUBLANES + CHUNK, width), F32),
            pltpu.VMEM((ML_HEADS, head_dim, head_dim), F32),
            pltpu.VMEM((ML_HEADS, head_dim), F32),
            pltpu.VMEM((ML_HEADS, LANES), F32),
            pltpu.VMEM((ML_HEADS, head_dim, head_dim), BF16),
            pltpu.VMEM((ML_HEADS, head_dim, head_dim), BF16),
        ],
        compiler_params=_params("arbitrary", "arbitrary"),
        name="mlstm",
    )(proj, proj, proj, gates_col, gates_row, bias_col, bias_row, conv_w, conv_b, wq, wk, norm_g)


def _merge_kernel(x_ref, ya_ref, yb_ref, ga0_ref, ga1_ref, gb0_ref, gb1_ref, wa_ref, wb_ref, wo_ref,
                  g2_ref, w2_ref, x1_ref, hn_ref, w2_out):
    w2_out[...] = w2_ref[...].astype(BF16)
    gate_a = jnp.concatenate([ga0_ref[...], ga1_ref[...]], axis=1).astype(F32)
    gate_b = jnp.concatenate([gb0_ref[...], gb1_ref[...]], axis=1).astype(F32)
    mixed = gate_a * _dot(ya_ref[...], wa_ref[...]) + gate_b * _dot(yb_ref[...], wb_ref[...])
    x1 = x_ref[...] + _dot(mixed.astype(BF16), wo_ref[...])
    x1_ref[...] = x1
    hn_ref[...] = _rms_norm(x1, g2_ref[...]).astype(BF16)


def _merge(x, y_a, y_b, proj, w_a, w_b, w_out, g2, w_ff2, layer, tm):
    t, d = x.shape
    width = y_a.shape[1]
    n_steps = t // tm
    d_ff = w_ff2.shape[1]
    rows = _cast_rows(d_ff, n_steps)
    sec = lambda s: pl.BlockSpec((None, tm, SECTION), lambda i: (s, i, 0))
    const = lambda i: (0, 0)
    resident = dict(pipeline_mode=pl.Buffered(1))
    return pl.pallas_call(
        _merge_kernel,
        grid=(n_steps,),
        in_specs=[
            pl.BlockSpec((tm, d), lambda i: (i, 0)),
            pl.BlockSpec((tm, width), lambda i: (i, 0)),
            pl.BlockSpec((tm, width), lambda i: (i, 0)),
            sec(SEC_GA), sec(SEC_GA + 1), sec(SEC_GB), sec(SEC_GB + 1),
            pl.BlockSpec((width, d), const, **resident),
            pl.BlockSpec((width, d), const, **resident),
            pl.BlockSpec((d, d), const, **resident),
            pl.BlockSpec((1, d), const),
            pl.BlockSpec((None, rows, d), lambda i: (layer, i, 0)),
        ],
        out_specs=[
            pl.BlockSpec((tm, d), lambda i: (i, 0)),
            pl.BlockSpec((tm, d), lambda i: (i, 0)),
            pl.BlockSpec((rows, d), lambda i: (i, 0)),
        ],
        out_shape=[
            jax.ShapeDtypeStruct((t, d), F32),
            jax.ShapeDtypeStruct((t, d), BF16),
            jax.ShapeDtypeStruct((d_ff, d), BF16),
        ],
        compiler_params=_params("arbitrary"),
        name="merge",
    )(x, y_a, y_b, proj, proj, proj, proj, w_a, w_b, w_out, g2, w_ff2)


def _ffn_kernel(hn_ref, x1_ref, w1_ref, w2_ref, gf_ref, out_ref, acc_ref, *, final_norm):
    j = pl.program_id(1)

    @pl.when(j == 0)
    def _():
        acc_ref[...] = x1_ref[...]

    h = jnp.square(jnp.maximum(_dot(hn_ref[...], w1_ref[...]), 0.0)).astype(BF16)
    acc_ref[...] += _dot(h, w2_ref[...])

    @pl.when(j == pl.num_programs(1) - 1)
    def _():
        y = acc_ref[...]
        out_ref[...] = _rms_norm(y, gf_ref[...]) if final_norm else y


def _ffn(hn, x1, w1, w2, gf, tm, tf, final_norm):
    t, d = x1.shape
    d_ff = w1.shape[1]
    return pl.pallas_call(
        functools.partial(_ffn_kernel, final_norm=final_norm),
        grid=(t // tm, d_ff // tf),
        in_specs=[
            pl.BlockSpec((tm, d), lambda i, j: (i, 0)),
            pl.BlockSpec((tm, d), lambda i, j: (i, 0)),
            pl.BlockSpec((d, tf), lambda i, j: (0, j)),
            pl.BlockSpec((tf, d), lambda i, j: (j, 0)),
            pl.BlockSpec((1, d), lambda i, j: (0, 0)),
        ],
        out_specs=pl.BlockSpec((tm, d), lambda i, j: (i, 0)),
        out_shape=jax.ShapeDtypeStruct((t, d), F32),
        scratch_shapes=[pltpu.VMEM((tm, d), F32)],
        compiler_params=_params("arbitrary", "arbitrary"),
        name="ffn",
    )(hn, x1, w1, w2, gf)


def _tiles(t):
    return dict(prologue=min(t, 512), inproj=min(t, 1024), gmlp=min(t, 1024), merge=min(t, 256),
                ffn_m=min(t, 512), ffn_f=1024)


def kernel(x, norm1_g, w_in, b_gate, gm_ln_g, gm_ln_b, gm_ws, gm_bs, ml_conv_w, ml_conv_b, ml_wq, ml_wk,
           ml_ig_b, ml_fg_b, ml_norm_g, w_a, w_b, w_out, norm2_g, w_ff1, w_ff2, norm_f_g):
    batch, seq, d = x.shape
    depth = w_in.shape[0]
    t = batch * seq
    tiles = _tiles(t)
    if_row = SEC_GA * SECTION
    n_if = 2 * ML_HEADS
    xt = x.reshape(t, d)
    w_t = jnp.swapaxes(w_in, 1, 2)
    for l in range(depth):
        gate_bias = jnp.concatenate([ml_ig_b[l], ml_fg_b[l]])
        bias5 = jnp.concatenate([jnp.zeros((1, SECTION), F32), b_gate[l].reshape(4, SECTION)])[:, None]
        xn, gates_col, gates_row = _prologue(xt, norm1_g[l][None], w_t, l, if_row, n_if, tiles["prologue"])
        proj, wa_bf, wb_bf, wo_bf, w1_bf = _inproj(
            xn, w_t, l, if_row + n_if, bias5, gm_ln_g[l][None], gm_ln_b[l][None],
            w_a, w_b, w_out, w_ff1, tiles["inproj"])
        y_a = _gmlp(proj, gm_ws[l], gm_bs[l].T, tiles["gmlp"])
        y_b = _mlstm(
            proj, gates_col, gates_row, gate_bias[:, None],
            jnp.pad(gate_bias, (0, LANES - n_if))[None], ml_conv_w[l], ml_conv_b[l][None],
            ml_wq[l], ml_wk[l], ml_norm_g[l][None], batch)
        x1, hn, w2_bf = _merge(xt, y_a, y_b, proj, wa_bf, wb_bf, wo_bf, norm2_g[l][None], w_ff2, l,
                               tiles["merge"])
        xt = _ffn(hn, x1, w1_bf, w2_bf, norm_f_g[None], tiles["ffn_m"], tiles["ffn_f"],
                  final_norm=l == depth - 1)
    return xt.reshape(batch, seq, d)
```

```python
import functools

import jax
import jax.numpy as jnp
from jax import lax
from jax.experimental import pallas as pl
from jax.experimental.pallas import tpu as pltpu

EPS = 1e-6
GM_GROUP_DIM = 128
CHUNK = 128
ML_HEADS = 4
CONV_K = 4
LANES = 128
SUBLANES = 8
BF16_ROWS = 16
SECTION = 1024
SEC_U, SEC_V, SEC_M, SEC_MV, SEC_O, SEC_GA, SEC_GB = 0, 1, 2, 3, 4, 5, 7
N_SECTIONS = 9
PROJ_U, PROJ_V, PROJ_M, PROJ_O, PROJ_GA, PROJ_GB = 0, 1, 2, 3, 4, 6
N_PROJ = 8
VMEM_LIMIT = 56 * 1024 * 1024

BF16 = jnp.bfloat16
F32 = jnp.float32


def _sigmoid(x):
    return 1.0 / (1.0 + jnp.exp(-x))


def _log_sigmoid(x):
    return jnp.minimum(x, 0.0) - jnp.log1p(jnp.exp(-jnp.abs(x)))


def _rms_norm(x, g):
    return x * lax.rsqrt(jnp.mean(x * x, axis=-1, keepdims=True) + EPS) * g


def _dot(a, b):
    return jnp.dot(a, b, preferred_element_type=F32)


def _params(*semantics):
    return pltpu.CompilerParams(dimension_semantics=semantics, vmem_limit_bytes=VMEM_LIMIT)


def _cast_rows(n_rows, n_steps):
    rows = n_rows // n_steps
    assert rows * n_steps == n_rows and rows % BF16_ROWS == 0, (n_rows, n_steps)
    return rows


def _nt_dot(a, b):
    return lax.dot_general(a, b, (((1,), (1,)), ((), ())), preferred_element_type=F32)


def _prologue_kernel(x_ref, g1_ref, wif_ref, xn_ref, gcol_ref, grow_ref):
    xn = _rms_norm(x_ref[...], g1_ref[...]).astype(BF16)
    xn_ref[...] = xn
    wif = wif_ref[...].astype(BF16)
    gcol_ref[...] = _nt_dot(xn, wif)
    grow_ref[...] = _nt_dot(wif, xn)[:grow_ref.shape[0], :]


def _prologue(x, g1, w_t, layer, if_row, n_if, batch, tm):
    t, d = x.shape
    seq = t // batch
    tiles_per_seq = seq // tm
    assert tiles_per_seq * tm == seq
    return pl.pallas_call(
        _prologue_kernel,
        grid=(t // tm,),
        in_specs=[
            pl.BlockSpec((tm, d), lambda i: (i, 0)),
            pl.BlockSpec((1, d), lambda i: (0, 0)),
            pl.BlockSpec((None, LANES, d), lambda i: (layer, if_row // LANES, 0)),
        ],
        out_specs=[
            pl.BlockSpec((tm, d), lambda i: (i, 0)),
            pl.BlockSpec((tm, LANES), lambda i: (i, 0)),
            pl.BlockSpec((None, n_if, tm), lambda i: (i // tiles_per_seq, 0, i % tiles_per_seq)),
        ],
        out_shape=[
            jax.ShapeDtypeStruct((t, d), BF16),
            jax.ShapeDtypeStruct((t, LANES), F32),
            jax.ShapeDtypeStruct((batch, n_if, seq), F32),
        ],
        compiler_params=_params("arbitrary"),
        name="prologue",
    )(x, g1, w_t)


def _inproj_kernel(xn_ref, w_ref, bias_ref, lng_ref, lnb_ref, wa_ref, wb_ref, wo_ref, w1_ref,
                   out_ref, vt_out, wa_out, wb_out, wo_out, w1_out, wbf_ref, *, n_side):
    j = pl.program_id(0)
    i = pl.program_id(1)

    @pl.when(i == 0)
    def _():
        rows = 256
        for r in range(0, wbf_ref.shape[0], rows):
            wbf_ref[r:r + rows, :] = w_ref[r:r + rows, :].astype(BF16)

    @pl.when(j * pl.num_programs(1) + i < n_side)
    def _():
        wa_out[...] = wa_ref[...].astype(BF16)
        wb_out[...] = wb_ref[...].astype(BF16)
        wo_out[...] = wo_ref[...].astype(BF16)
        w1_out[...] = w1_ref[...].astype(BF16)

    def proj():
        return _nt_dot(xn_ref[...], wbf_ref[...])

    @pl.when(j == SEC_U)
    def _():
        out_ref[...] = jax.nn.gelu(proj()).astype(BF16)

    @pl.when(j == SEC_V)
    def _():
        v = jax.nn.gelu(proj())
        mu = jnp.mean(v, axis=-1, keepdims=True)
        var = jnp.mean(jnp.square(v - mu), axis=-1, keepdims=True)
        out_ref[...] = ((v - mu) * lax.rsqrt(var + EPS) * lng_ref[...] + lnb_ref[...]).astype(BF16)

    @pl.when(j == SEC_M)
    def _():
        out_ref[...] = proj().astype(BF16)

    @pl.when(j == SEC_MV)
    def _():
        vt_out[...] = _nt_dot(wbf_ref[...], xn_ref[...]).astype(BF16)

    @pl.when(j >= SEC_O)
    def _():
        out_ref[...] = _sigmoid(proj() + bias_ref[0]).astype(BF16)


def _inproj(xn, w_t, layer, gate_row, bias5, ln_g, ln_b, w_a, w_b, w_out, w_ff1, batch, tm):
    t, d = xn.shape
    n_i = t // tm
    seq = t // batch
    tiles_per_seq = seq // tm
    assert tiles_per_seq * tm == seq
    n_side = (N_SECTIONS - 1) * n_i

    def proj_block(j, i):
        mv = j == SEC_MV
        return (j - (j >= SEC_MV).astype(jnp.int32), jnp.where(mv, n_i - 1, i), 0)

    def vt_block(j, i):
        tile = jnp.clip((j - SEC_MV) * n_i + i, 0, n_i - 1)
        return (tile // tiles_per_seq, 0, tile % tiles_per_seq)

    gate_skip = gate_row - SEC_GA * SECTION
    assert gate_skip % SUBLANES == 0
    side = lambda j, i: (layer, jnp.minimum(j * n_i + i, n_side - 1), 0)
    side_out = lambda j, i: (jnp.minimum(j * n_i + i, n_side - 1), 0)
    side_ws = (w_a, w_b, w_out, w_ff1)
    side_rows = [_cast_rows(w.shape[1], n_side) for w in side_ws]
    kern = functools.partial(_inproj_kernel, n_side=n_side)
    return pl.pallas_call(
        kern,
        grid=(N_SECTIONS, n_i),
        in_specs=[
            pl.BlockSpec((tm, d), lambda j, i: (i, 0)),
            pl.BlockSpec((None, pl.Element(SECTION), pl.Element(d)),
                         lambda j, i: (layer, SUBLANES * (j * (SECTION // SUBLANES) + jnp.where(
                             j >= SEC_GA, gate_skip // SUBLANES, 0)), 0)),
            pl.BlockSpec((1, 1, SECTION), lambda j, i: (jnp.maximum(j - SEC_O, 0), 0, 0)),
            pl.BlockSpec((1, SECTION), lambda j, i: (0, 0)),
            pl.BlockSpec((1, SECTION), lambda j, i: (0, 0)),
        ] + [pl.BlockSpec((None, r, w.shape[2]), side) for r, w in zip(side_rows, side_ws)],
        out_specs=[pl.BlockSpec((None, tm, SECTION), proj_block), pl.BlockSpec((None, SECTION, tm), vt_block)]
        + [pl.BlockSpec((r, w.shape[2]), side_out) for r, w in zip(side_rows, side_ws)],
        out_shape=[jax.ShapeDtypeStruct((N_PROJ, t, SECTION), BF16),
                   jax.ShapeDtypeStruct((batch, SECTION, seq), BF16)]
        + [jax.ShapeDtypeStruct(w.shape[1:], BF16) for w in side_ws],
        scratch_shapes=[pltpu.VMEM((SECTION, d), BF16)],
        compiler_params=_params("arbitrary", "arbitrary"),
        name="inproj",
    )(xn, w_t, bias5, ln_g, ln_b, *side_ws)


def _gmlp_kernel(u_ref, v_ref, ws_ref, bs_ref, out_ref, *, n_chunks, n_groups):
    row = lax.broadcasted_iota(jnp.int32, (CHUNK, CHUNK), 0)
    col = lax.broadcasted_iota(jnp.int32, (CHUNK, CHUNK), 1)
    causal = col <= row
    for g in range(n_groups):
        gs = slice(g * GM_GROUP_DIM, (g + 1) * GM_GROUP_DIM)
        w = jnp.where(causal, ws_ref[g], 0.0).astype(BF16)
        vg = jnp.concatenate([v_ref[c * CHUNK:(c + 1) * CHUNK, gs] for c in range(n_chunks)], axis=1)
        s = _dot(w, vg) + bs_ref[:, g:g + 1]
        for c in range(n_chunks):
            rs = slice(c * CHUNK, (c + 1) * CHUNK)
            sc = s[:, c * GM_GROUP_DIM:(c + 1) * GM_GROUP_DIM]
            out_ref[rs, gs] = (u_ref[rs, gs].astype(F32) * sc).astype(BF16)


def _gmlp(proj, ws, bs_t, tm):
    _, t, width = proj.shape
    n_groups = width // GM_GROUP_DIM
    kern = functools.partial(_gmlp_kernel, n_chunks=tm // CHUNK, n_groups=n_groups)
    return pl.pallas_call(
        kern,
        grid=(t // tm,),
        in_specs=[
            pl.BlockSpec((None, tm, width), lambda i: (PROJ_U, i, 0)),
            pl.BlockSpec((None, tm, width), lambda i: (PROJ_V, i, 0)),
            pl.BlockSpec((n_groups, CHUNK, CHUNK), lambda i: (0, 0, 0)),
            pl.BlockSpec((CHUNK, n_groups), lambda i: (0, 0)),
        ],
        out_specs=pl.BlockSpec((tm, width), lambda i: (i, 0)),
        out_shape=jax.ShapeDtypeStruct((t, width), BF16),
        compiler_params=_params("arbitrary"),
        name="gmlp",
    )(proj, proj, ws, bs_t)


def _mlstm_kernel(m_ref, vt_ref, o_ref, gcol_ref, grow_ref, bcol_ref, brow_ref, cw_ref, cb_ref,
                  wq_ref, wk_ref, ng_ref, out_ref, ext_ref, ct_ref, n_ref, mx_ref, wqb_ref, wkb_ref,
                  *, head_dim, batch):
    @pl.when(pl.program_id(0) == 0)
    def _():
        wqb_ref[...] = wq_ref[...].astype(BF16)
        wkb_ref[...] = (wk_ref[...] * (head_dim ** -0.5)).astype(BF16)
        ext_ref[:, 0:SUBLANES, :] = jnp.zeros((batch, SUBLANES, ext_ref.shape[2]), F32)
        ct_ref[...] = jnp.zeros(ct_ref.shape, F32)
        n_ref[...] = jnp.zeros(n_ref.shape, F32)
        mx_ref[...] = jnp.zeros(mx_ref.shape, F32)

    row = lax.broadcasted_iota(jnp.int32, (CHUNK, CHUNK), 0)
    col = lax.broadcasted_iota(jnp.int32, (CHUNK, CHUNK), 1)
    causal = col <= row
    tril = causal.astype(F32)
    triu = (row <= col).astype(F32)

    for b in range(batch):
        ext_ref[b, SUBLANES:, :] = m_ref[b].astype(F32)
        conv = cb_ref[...] + cw_ref[CONV_K - 1:CONV_K, :] * ext_ref[b, SUBLANES:, :]
        for d in range(1, CONV_K):
            conv = conv + (cw_ref[CONV_K - 1 - d:CONV_K - d, :]
                           * ext_ref[b, SUBLANES - d:SUBLANES - d + CHUNK, :])
        ext_ref[b, 0:SUBLANES, :] = ext_ref[b, CHUNK:, :]
        c_act = (conv * _sigmoid(conv)).astype(BF16)

        g_col = gcol_ref[b] + brow_ref[...]
        g_row = grow_ref[b] + bcol_ref[...]
        bcum_col = jnp.dot(tril, _log_sigmoid(g_col), precision=lax.Precision.HIGHEST,
                           preferred_element_type=F32)
        bcum_row = jnp.dot(_log_sigmoid(g_row), triu, precision=lax.Precision.HIGHEST,
                           preferred_element_type=F32)

        for h in range(ML_HEADS):
            s = b * ML_HEADS + h
            hs = slice(h * head_dim, (h + 1) * head_dim)
            q = _dot(c_act[:, hs], wqb_ref[h])
            k = _dot(c_act[:, hs], wkb_ref[h])
            qb = q.astype(BF16)
            kb = k.astype(BF16)
            vt = vt_ref[b, hs, :]

            ig_c = g_col[:, h:h + 1]
            ig_r = g_row[h:h + 1, :]
            bc_c = bcum_col[:, ML_HEADS + h:ML_HEADS + h + 1]
            bc_r = bcum_row[ML_HEADS + h:ML_HEADS + h + 1, :]
            b_last = bc_r[:, CHUNK - 1:CHUNK]
            m_prev = mx_ref[s:s + 1, 0:1]
            ct_prev = ct_ref[s]
            n_prev = n_ref[s:s + 1, :]

            inter = bc_c + m_prev
            dlog = jnp.where(causal, bc_c - bc_r + ig_r, -jnp.inf)
            m_t = jnp.maximum(inter, jnp.max(dlog, axis=-1, keepdims=True))
            w_intra = jnp.exp(dlog - m_t)
            w_inter = jnp.exp(inter - m_t)
            qk = _nt_dot(qb, kb) * w_intra
            num = w_inter * _nt_dot(qb, ct_prev.astype(BF16)) + _nt_dot(qk.astype(BF16), vt)
            den = (w_inter * jnp.sum(q * n_prev, axis=-1, keepdims=True)
                   + jnp.sum(qk, axis=-1, keepdims=True))
            hid = num * (1.0 / jnp.maximum(jnp.abs(den), jnp.exp(-m_t)))
            mu = jnp.mean(hid, axis=-1, keepdims=True)
            var = jnp.mean(jnp.square(hid - mu), axis=-1, keepdims=True)
            hid = (hid - mu) * lax.rsqrt(var + EPS) * ng_ref[:, hs]
            out_ref[b, :, hs] = (o_ref[b, :, hs].astype(F32) * hid).astype(BF16)

            a_c = b_last - bc_c + ig_c
            a_max = jnp.max(a_c, axis=0, keepdims=True)
            kw = k * jnp.exp(a_c - a_max)
            kvt = _dot(vt, kw.astype(BF16))
            ksum = jnp.sum(kw, axis=0, keepdims=True)
            m_new = jnp.maximum(b_last + m_prev, a_max)
            decay = jnp.exp(b_last + m_prev - m_new)
            scale = jnp.exp(a_max - m_new)
            ct_ref[s] = decay * ct_prev + scale * kvt
            n_ref[s:s + 1, :] = decay * n_prev + scale * ksum
            mx_ref[s:s + 1, :] = jnp.broadcast_to(m_new, (1, LANES))


def _mlstm(proj, vt, gates_col, gates_row, bias_col, bias_row, conv_w, conv_b, wq, wk, norm_g, batch):
    _, t, width = proj.shape
    seq = t // batch
    head_dim = width // ML_HEADS
    n_if = gates_row.shape[1]
    proj4 = proj.reshape(proj.shape[0], batch, seq, width)
    kern = functools.partial(_mlstm_kernel, head_dim=head_dim, batch=batch)
    const2 = lambda n: (0, 0)
    const3 = lambda n: (0, 0, 0)
    n_state = batch * ML_HEADS
    out = pl.pallas_call(
        kern,
        grid=(seq // CHUNK,),
        in_specs=[
            pl.BlockSpec((None, batch, CHUNK, width), lambda n: (PROJ_M, 0, n, 0)),
            pl.BlockSpec((batch, width, CHUNK), lambda n: (0, 0, n)),
            pl.BlockSpec((None, batch, CHUNK, width), lambda n: (PROJ_O, 0, n, 0)),
            pl.BlockSpec((batch, CHUNK, LANES), lambda n: (0, n, 0)),
            pl.BlockSpec((batch, n_if, CHUNK), lambda n: (0, 0, n)),
            pl.BlockSpec((n_if, 1), const2),
            pl.BlockSpec((1, LANES), const2),
            pl.BlockSpec((CONV_K, width), const2),
            pl.BlockSpec((1, width), const2),
            pl.BlockSpec((ML_HEADS, head_dim, head_dim), const3),
            pl.BlockSpec((ML_HEADS, head_dim, head_dim), const3),
            pl.BlockSpec((1, width), const2),
        ],
        out_specs=pl.BlockSpec((batch, CHUNK, width), lambda n: (0, n, 0)),
        out_shape=jax.ShapeDtypeStruct((batch, seq, width), BF16),
        scratch_shapes=[
            pltpu.VMEM((batch, SUBLANES + CHUNK, width), F32),
            pltpu.VMEM((n_state, head_dim, head_dim), F32),
            pltpu.VMEM((n_state, head_dim), F32),
            pltpu.VMEM((n_state, LANES), F32),
            pltpu.VMEM((ML_HEADS, head_dim, head_dim), BF16),
            pltpu.VMEM((ML_HEADS, head_dim, head_dim), BF16),
        ],
        compiler_params=_params("arbitrary"),
        name="mlstm",
    )(proj4, vt, proj4, gates_col.reshape(batch, seq, LANES), gates_row, bias_col, bias_row,
      conv_w, conv_b, wq, wk, norm_g)
    return out.reshape(t, width)


def _merge_kernel(x_ref, ya_ref, yb_ref, ga0_ref, ga1_ref, gb0_ref, gb1_ref, wa_ref, wb_ref, wo_ref,
                  g2_ref, w2_ref, x1_ref, hn_ref, w2_out):
    w2_out[...] = w2_ref[...].astype(BF16)
    gate_a = jnp.concatenate([ga0_ref[...], ga1_ref[...]], axis=1).astype(F32)
    gate_b = jnp.concatenate([gb0_ref[...], gb1_ref[...]], axis=1).astype(F32)
    mixed = gate_a * _dot(ya_ref[...], wa_ref[...]) + gate_b * _dot(yb_ref[...], wb_ref[...])
    x1 = x_ref[...] + _dot(mixed.astype(BF16), wo_ref[...])
    x1_ref[...] = x1
    hn_ref[...] = _rms_norm(x1, g2_ref[...]).astype(BF16)


def _merge(x, y_a, y_b, proj, w_a, w_b, w_out, g2, w_ff2, layer, tm):
    t, d = x.shape
    width = y_a.shape[1]
    n_steps = t // tm
    d_ff = w_ff2.shape[1]
    rows = _cast_rows(d_ff, n_steps)
    sec = lambda s: pl.BlockSpec((None, tm, SECTION), lambda i: (s, i, 0))
    const = lambda i: (0, 0)
    resident = dict(pipeline_mode=pl.Buffered(1))
    return pl.pallas_call(
        _merge_kernel,
        grid=(n_steps,),
        in_specs=[
            pl.BlockSpec((tm, d), lambda i: (i, 0)),
            pl.BlockSpec((tm, width), lambda i: (i, 0)),
            pl.BlockSpec((tm, width), lambda i: (i, 0)),
            sec(PROJ_GA), sec(PROJ_GA + 1), sec(PROJ_GB), sec(PROJ_GB + 1),
            pl.BlockSpec((width, d), const, **resident),
            pl.BlockSpec((width, d), const, **resident),
            pl.BlockSpec((d, d), const, **resident),
            pl.BlockSpec((1, d), const),
            pl.BlockSpec((None, rows, d), lambda i: (layer, i, 0)),
        ],
        out_specs=[
            pl.BlockSpec((tm, d), lambda i: (i, 0)),
            pl.BlockSpec((tm, d), lambda i: (i, 0)),
            pl.BlockSpec((rows, d), lambda i: (i, 0)),
        ],
        out_shape=[
            jax.ShapeDtypeStruct((t, d), F32),
            jax.ShapeDtypeStruct((t, d), BF16),
            jax.ShapeDtypeStruct((d_ff, d), BF16),
        ],
        compiler_params=_params("arbitrary"),
        name="merge",
    )(x, y_a, y_b, proj, proj, proj, proj, w_a, w_b, w_out, g2, w_ff2)


def _ffn_kernel(hn_ref, x1_ref, w1_ref, w2_ref, gf_ref, out_ref, acc_ref, *, final_norm):
    j = pl.program_id(1)

    @pl.when(j == 0)
    def _():
        acc_ref[...] = x1_ref[...]

    h = jnp.square(jnp.maximum(_dot(hn_ref[...], w1_ref[...]), 0.0)).astype(BF16)
    acc_ref[...] += _dot(h, w2_ref[...])

    @pl.when(j == pl.num_programs(1) - 1)
    def _():
        y = acc_ref[...]
        out_ref[...] = _rms_norm(y, gf_ref[...]) if final_norm else y


def _ffn(hn, x1, w1, w2, gf, tm, tf, final_norm):
    t, d = x1.shape
    d_ff = w1.shape[1]
    return pl.pallas_call(
        functools.partial(_ffn_kernel, final_norm=final_norm),
        grid=(t // tm, d_ff // tf),
        in_specs=[
            pl.BlockSpec((tm, d), lambda i, j: (i, 0)),
            pl.BlockSpec((tm, d), lambda i, j: (i, 0)),
            pl.BlockSpec((d, tf), lambda i, j: (0, j)),
            pl.BlockSpec((tf, d), lambda i, j: (j, 0)),
            pl.BlockSpec((1, d), lambda i, j: (0, 0)),
        ],
        out_specs=pl.BlockSpec((tm, d), lambda i, j: (i, 0)),
        out_shape=jax.ShapeDtypeStruct((t, d), F32),
        scratch_shapes=[pltpu.VMEM((tm, d), F32)],
        compiler_params=_params("arbitrary", "arbitrary"),
        name="ffn",
    )(hn, x1, w1, w2, gf)


def _tiles(t):
    return dict(prologue=min(t, 512), inproj=min(t, 1024), gmlp=min(t, 1024), merge=min(t, 256),
                ffn_m=min(t, 512), ffn_f=1024)


def kernel(x, norm1_g, w_in, b_gate, gm_ln_g, gm_ln_b, gm_ws, gm_bs, ml_conv_w, ml_conv_b, ml_wq, ml_wk,
           ml_ig_b, ml_fg_b, ml_norm_g, w_a, w_b, w_out, norm2_g, w_ff1, w_ff2, norm_f_g):
    batch, seq, d = x.shape
    depth = w_in.shape[0]
    t = batch * seq
    tiles = _tiles(t)
    if_row = SEC_GA * SECTION
    n_if = 2 * ML_HEADS
    xt = x.reshape(t, d)
    w_t = jnp.swapaxes(w_in, 1, 2)
    for l in range(depth):
        gate_bias = jnp.concatenate([ml_ig_b[l], ml_fg_b[l]])
        bias5 = jnp.concatenate([jnp.zeros((1, SECTION), F32), b_gate[l].reshape(4, SECTION)])[:, None]
        xn, gates_col, gates_row = _prologue(xt, norm1_g[l][None], w_t, l, if_row, n_if, batch,
                                             tiles["prologue"])
        proj, vt, wa_bf, wb_bf, wo_bf, w1_bf = _inproj(
            xn, w_t, l, if_row + n_if, bias5, gm_ln_g[l][None], gm_ln_b[l][None],
            w_a, w_b, w_out, w_ff1, batch, tiles["inproj"])
        y_a = _gmlp(proj, gm_ws[l], gm_bs[l].T, tiles["gmlp"])
        y_b = _mlstm(
            proj, vt, gates_col, gates_row, gate_bias[:, None],
            jnp.pad(gate_bias, (0, LANES - n_if))[None], ml_conv_w[l], ml_conv_b[l][None],
            ml_wq[l], ml_wk[l], ml_norm_g[l][None], batch)
        x1, hn, w2_bf = _merge(xt, y_a, y_b, proj, wa_bf, wb_bf, wo_bf, norm2_g[l][None], w_ff2, l,
                               tiles["merge"])
        xt = _ffn(hn, x1, w1_bf, w2_bf, norm_f_g[None], tiles["ffn_m"], tiles["ffn_f"],
                  final_norm=l == depth - 1)
    return xt.reshape(batch, seq, d)
```

```python
import functools

import jax
import jax.numpy as jnp
from jax import lax
from jax.experimental import pallas as pl
from jax.experimental.pallas import tpu as pltpu

EPS = 1e-6
GM_GROUP_DIM = 128
CHUNK = 128
ML_HEADS = 4
CONV_K = 4
LANES = 128
SUBLANES = 8
BF16_ROWS = 16
SECTION = 1024
SEC_U, SEC_V, SEC_M, SEC_MV, SEC_O, SEC_GA, SEC_GB = 0, 1, 2, 3, 4, 5, 7
N_SECTIONS = 9
PROJ_U, PROJ_V, PROJ_M, PROJ_O, PROJ_GA, PROJ_GB = 0, 1, 2, 3, 4, 6
N_PROJ = 8
VMEM_LIMIT = 56 * 1024 * 1024

BF16 = jnp.bfloat16
F32 = jnp.float32


def _sigmoid(x):
    return 1.0 / (1.0 + jnp.exp(-x))


def _log_sigmoid(x):
    return jnp.minimum(x, 0.0) - jnp.log1p(jnp.exp(-jnp.abs(x)))


def _rms_norm(x, g):
    return x * lax.rsqrt(jnp.mean(x * x, axis=-1, keepdims=True) + EPS) * g


def _dot(a, b):
    return jnp.dot(a, b, preferred_element_type=F32)


def _params(*semantics):
    return pltpu.CompilerParams(dimension_semantics=semantics, vmem_limit_bytes=VMEM_LIMIT)


def _cast_rows(n_rows, n_steps):
    rows = n_rows // n_steps
    assert rows * n_steps == n_rows and rows % BF16_ROWS == 0, (n_rows, n_steps)
    return rows


def _nt_dot(a, b):
    return lax.dot_general(a, b, (((1,), (1,)), ((), ())), preferred_element_type=F32)


def _prologue_kernel(x_ref, g1_ref, wif_ref, xn_ref, gcol_ref, grow_ref):
    xn = _rms_norm(x_ref[...], g1_ref[...]).astype(BF16)
    xn_ref[...] = xn
    wif = wif_ref[...].astype(BF16)
    gcol_ref[...] = _nt_dot(xn, wif)
    grow_ref[...] = _nt_dot(wif, xn)[:grow_ref.shape[0], :]


def _prologue(x, g1, w_t, layer, if_row, n_if, batch, tm):
    t, d = x.shape
    seq = t // batch
    tiles_per_seq = seq // tm
    assert tiles_per_seq * tm == seq
    return pl.pallas_call(
        _prologue_kernel,
        grid=(t // tm,),
        in_specs=[
            pl.BlockSpec((tm, d), lambda i: (i, 0)),
            pl.BlockSpec((1, d), lambda i: (0, 0)),
            pl.BlockSpec((None, LANES, d), lambda i: (layer, if_row // LANES, 0)),
        ],
        out_specs=[
            pl.BlockSpec((tm, d), lambda i: (i, 0)),
            pl.BlockSpec((tm, LANES), lambda i: (i, 0)),
            pl.BlockSpec((None, n_if, tm), lambda i: (i // tiles_per_seq, 0, i % tiles_per_seq)),
        ],
        out_shape=[
            jax.ShapeDtypeStruct((t, d), BF16),
            jax.ShapeDtypeStruct((t, LANES), F32),
            jax.ShapeDtypeStruct((batch, n_if, seq), F32),
        ],
        compiler_params=_params("arbitrary"),
        name="prologue",
    )(x, g1, w_t)


def _inproj_kernel(xn_ref, w_ref, bias_ref, lng_ref, lnb_ref, cw_ref, cb_ref, wa_ref, wb_ref, wo_ref, w1_ref,
                   out_ref, vt_out, wa_out, wb_out, wo_out, w1_out, wbf_ref, tail_ref, *, n_side, tiles_per_seq):
    j = pl.program_id(0)
    i = pl.program_id(1)
    tm = xn_ref.shape[0]

    @pl.when(i == 0)
    def _():
        rows = 256
        for r in range(0, wbf_ref.shape[0], rows):
            wbf_ref[r:r + rows, :] = w_ref[r:r + rows, :].astype(BF16)

    @pl.when(j * pl.num_programs(1) + i < n_side)
    def _():
        wa_out[...] = wa_ref[...].astype(BF16)
        wb_out[...] = wb_ref[...].astype(BF16)
        wo_out[...] = wo_ref[...].astype(BF16)
        w1_out[...] = w1_ref[...].astype(BF16)

    def proj():
        return _nt_dot(xn_ref[...], wbf_ref[...])

    @pl.when(j == SEC_U)
    def _():
        out_ref[...] = jax.nn.gelu(proj()).astype(BF16)

    @pl.when(j == SEC_V)
    def _():
        v = jax.nn.gelu(proj())
        mu = jnp.mean(v, axis=-1, keepdims=True)
        var = jnp.mean(jnp.square(v - mu), axis=-1, keepdims=True)
        out_ref[...] = ((v - mu) * lax.rsqrt(var + EPS) * lng_ref[...] + lnb_ref[...]).astype(BF16)

    @pl.when(j == SEC_M)
    def _():
        @pl.when(i % tiles_per_seq == 0)
        def _():
            tail_ref[...] = jnp.zeros(tail_ref.shape, F32)

        x = proj()
        xe = jnp.concatenate([tail_ref[...], x], axis=0)
        conv = cb_ref[...] + cw_ref[CONV_K - 1:CONV_K, :] * x
        for d in range(1, CONV_K):
            conv = conv + cw_ref[CONV_K - 1 - d:CONV_K - d, :] * xe[SUBLANES - d:SUBLANES - d + tm, :]
        tail_ref[...] = x[tm - SUBLANES:, :]
        out_ref[...] = (conv * _sigmoid(conv)).astype(BF16)

    @pl.when(j == SEC_MV)
    def _():
        vt_out[...] = _nt_dot(wbf_ref[...], xn_ref[...]).astype(BF16)

    @pl.when(j >= SEC_O)
    def _():
        out_ref[...] = _sigmoid(proj() + bias_ref[0]).astype(BF16)


def _inproj(xn, w_t, layer, gate_row, bias5, ln_g, ln_b, conv_w, conv_b, w_a, w_b, w_out, w_ff1, batch, tm):
    t, d = xn.shape
    n_i = t // tm
    seq = t // batch
    tiles_per_seq = seq // tm
    assert tiles_per_seq * tm == seq
    n_side = (N_SECTIONS - 1) * n_i

    def proj_block(j, i):
        mv = j == SEC_MV
        return (j - (j >= SEC_MV).astype(jnp.int32), jnp.where(mv, n_i - 1, i), 0)

    def vt_block(j, i):
        tile = jnp.clip((j - SEC_MV) * n_i + i, 0, n_i - 1)
        return (tile // tiles_per_seq, 0, tile % tiles_per_seq)

    gate_skip = gate_row - SEC_GA * SECTION
    assert gate_skip % SUBLANES == 0
    side = lambda j, i: (layer, jnp.minimum(j * n_i + i, n_side - 1), 0)
    side_out = lambda j, i: (jnp.minimum(j * n_i + i, n_side - 1), 0)
    side_ws = (w_a, w_b, w_out, w_ff1)
    side_rows = [_cast_rows(w.shape[1], n_side) for w in side_ws]
    kern = functools.partial(_inproj_kernel, n_side=n_side, tiles_per_seq=tiles_per_seq)
    return pl.pallas_call(
        kern,
        grid=(N_SECTIONS, n_i),
        in_specs=[
            pl.BlockSpec((tm, d), lambda j, i: (i, 0)),
            pl.BlockSpec((None, pl.Element(SECTION), pl.Element(d)),
                         lambda j, i: (layer, SUBLANES * (j * (SECTION // SUBLANES) + jnp.where(
                             j >= SEC_GA, gate_skip // SUBLANES, 0)), 0)),
            pl.BlockSpec((1, 1, SECTION), lambda j, i: (jnp.maximum(j - SEC_O, 0), 0, 0)),
            pl.BlockSpec((1, SECTION), lambda j, i: (0, 0)),
            pl.BlockSpec((1, SECTION), lambda j, i: (0, 0)),
            pl.BlockSpec((CONV_K, SECTION), lambda j, i: (0, 0)),
            pl.BlockSpec((1, SECTION), lambda j, i: (0, 0)),
        ] + [pl.BlockSpec((None, r, w.shape[2]), side) for r, w in zip(side_rows, side_ws)],
        out_specs=[pl.BlockSpec((None, tm, SECTION), proj_block), pl.BlockSpec((None, SECTION, tm), vt_block)]
        + [pl.BlockSpec((r, w.shape[2]), side_out) for r, w in zip(side_rows, side_ws)],
        out_shape=[jax.ShapeDtypeStruct((N_PROJ, t, SECTION), BF16),
                   jax.ShapeDtypeStruct((batch, SECTION, seq), BF16)]
        + [jax.ShapeDtypeStruct(w.shape[1:], BF16) for w in side_ws],
        scratch_shapes=[
            pltpu.VMEM((SECTION, d), BF16),
            pltpu.VMEM((SUBLANES, SECTION), F32),
        ],
        compiler_params=_params("arbitrary", "arbitrary"),
        name="inproj",
    )(xn, w_t, bias5, ln_g, ln_b, conv_w, conv_b, *side_ws)


def _gmlp_kernel(u_ref, v_ref, ws_ref, bs_ref, out_ref, *, n_chunks, n_groups):
    row = lax.broadcasted_iota(jnp.int32, (CHUNK, CHUNK), 0)
    col = lax.broadcasted_iota(jnp.int32, (CHUNK, CHUNK), 1)
    causal = col <= row
    for g in range(n_groups):
        gs = slice(g * GM_GROUP_DIM, (g + 1) * GM_GROUP_DIM)
        w = jnp.where(causal, ws_ref[g], 0.0).astype(BF16)
        vg = jnp.concatenate([v_ref[c * CHUNK:(c + 1) * CHUNK, gs] for c in range(n_chunks)], axis=1)
        s = _dot(w, vg) + bs_ref[:, g:g + 1]
        for c in range(n_chunks):
            rs = slice(c * CHUNK, (c + 1) * CHUNK)
            sc = s[:, c * GM_GROUP_DIM:(c + 1) * GM_GROUP_DIM]
            out_ref[rs, gs] = (u_ref[rs, gs].astype(F32) * sc).astype(BF16)


def _gmlp(proj, ws, bs_t, tm):
    _, t, width = proj.shape
    n_groups = width // GM_GROUP_DIM
    kern = functools.partial(_gmlp_kernel, n_chunks=tm // CHUNK, n_groups=n_groups)
    return pl.pallas_call(
        kern,
        grid=(t // tm,),
        in_specs=[
            pl.BlockSpec((None, tm, width), lambda i: (PROJ_U, i, 0)),
            pl.BlockSpec((None, tm, width), lambda i: (PROJ_V, i, 0)),
            pl.BlockSpec((n_groups, CHUNK, CHUNK), lambda i: (0, 0, 0)),
            pl.BlockSpec((CHUNK, n_groups), lambda i: (0, 0)),
        ],
        out_specs=pl.BlockSpec((tm, width), lambda i: (i, 0)),
        out_shape=jax.ShapeDtypeStruct((t, width), BF16),
        compiler_params=_params("arbitrary"),
        name="gmlp",
    )(proj, proj, ws, bs_t)


def _lanes(col):
    return jnp.broadcast_to(col, (col.shape[0], LANES))


def _wide(x, n):
    return jnp.concatenate([x] * n, axis=1)


def _mlstm_kernel(c_ref, vt_ref, o_ref, gcol_ref, grow_ref, bcol_ref, brow_ref, wq_ref, wk_ref, ng_ref,
                  out_ref, st_ref, mx_ref, wqb_ref, wkb_ref, *, head_dim, batch):
    @pl.when(pl.program_id(0) == 0)
    def _():
        wqb_ref[...] = wq_ref[...].astype(BF16)
        wkb_ref[...] = (wk_ref[...] * (head_dim ** -0.5)).astype(BF16)
        st_ref[...] = jnp.zeros(st_ref.shape, F32)
        mx_ref[...] = jnp.zeros(mx_ref.shape, F32)

    row = lax.broadcasted_iota(jnp.int32, (CHUNK, CHUNK), 0)
    col = lax.broadcasted_iota(jnp.int32, (CHUNK, CHUNK), 1)
    causal = col <= row
    tril = causal.astype(F32)
    triu = (row <= col).astype(F32)
    ones_rows = jnp.ones((LANES, CHUNK), BF16)
    nd = head_dim // LANES
    heads = [(b, h) for b in range(batch) for h in range(ML_HEADS)]
    slot = lambda b, h: b * ML_HEADS + h
    hsl = lambda h: slice(h * head_dim, (h + 1) * head_dim)


    g_col, g_row, bcum_col, bcum_row, total = {}, {}, {}, {}, {}
    for b in range(batch):
        g_col[b] = gcol_ref[b] + brow_ref[...]
        g_row[b] = grow_ref[b] + bcol_ref[...]
        lf_row = _log_sigmoid(g_row[b])
        bcum_col[b] = jnp.dot(tril, _log_sigmoid(g_col[b]), precision=lax.Precision.HIGHEST,
                              preferred_element_type=F32)
        bcum_row[b] = jnp.dot(lf_row, triu, precision=lax.Precision.HIGHEST, preferred_element_type=F32)
        total[b] = jnp.sum(lf_row, axis=-1, keepdims=True)

    q, k = {}, {}
    for b, h in heads:
        c = c_ref[b, :, hsl(h)]
        q[b, h] = _dot(c, wqb_ref[h])
        k[b, h] = _dot(c, wkb_ref[h])

    w_intra, w_inter, inv_floor, w_state, decay, scale, m_new = {}, {}, {}, {}, {}, {}, {}
    for b, h in heads:
        f = ML_HEADS + h
        bc = _lanes(bcum_col[b][:, f:f + 1])
        ig = _lanes(g_col[b][:, h:h + 1])
        src = g_row[b][h:h + 1, :] - bcum_row[b][f:f + 1, :]
        b_last = jnp.broadcast_to(total[b][f:f + 1, :], (1, LANES))
        m_prev = mx_ref[slot(b, h):slot(b, h) + 1, :]
        inter = bc + m_prev
        dlog = jnp.where(causal, bc + src, -jnp.inf)
        m_t = jnp.maximum(inter, _lanes(jnp.max(dlog, axis=-1, keepdims=True)))
        w_intra[b, h] = jnp.exp(dlog - m_t)
        w_inter[b, h] = jnp.exp(inter - m_t)
        inv_floor[b, h] = jnp.exp(-m_t)
        a = b_last - bc + ig
        a_max = jnp.max(a, axis=0, keepdims=True)
        w_state[b, h] = jnp.exp(a - a_max)
        m_new[b, h] = jnp.maximum(b_last + m_prev, a_max)
        decay[b, h] = jnp.exp(b_last + m_prev - m_new[b, h])
        scale[b, h] = jnp.exp(a_max - m_new[b, h])

    qk = {}
    for b, h in heads:
        qk[b, h] = _nt_dot(q[b, h].astype(BF16), k[b, h].astype(BF16)) * w_intra[b, h]

    for b, h in heads:
        vt_ext = jnp.concatenate([vt_ref[b, hsl(h), :], ones_rows], axis=0)
        ext = (_wide(w_inter[b, h], nd + 1) * _nt_dot(q[b, h].astype(BF16), st_ref[slot(b, h)].astype(BF16))
               + _nt_dot(qk[b, h].astype(BF16), vt_ext))
        den = ext[:, head_dim:]
        hid = ext[:, :head_dim] * _wide(1.0 / jnp.maximum(jnp.abs(den), inv_floor[b, h]), nd)
        mu = jnp.mean(hid, axis=-1, keepdims=True)
        var = jnp.mean(jnp.square(hid - mu), axis=-1, keepdims=True)
        hid = (hid - mu) * lax.rsqrt(var + EPS) * ng_ref[:, hsl(h)]
        out_ref[b, :, hsl(h)] = (o_ref[b, :, hsl(h)].astype(F32) * hid).astype(BF16)

    for b, h in heads:
        s = slot(b, h)
        vt_ext = jnp.concatenate([vt_ref[b, hsl(h), :], ones_rows], axis=0)
        kw = k[b, h] * _wide(w_state[b, h], nd)
        upd = _dot(vt_ext, kw.astype(BF16))
        st_ref[s] = _wide(decay[b, h], nd) * st_ref[s] + _wide(scale[b, h], nd) * upd
        mx_ref[s:s + 1, :] = m_new[b, h]


def _mlstm(proj, vt, gates_col, gates_row, bias_col, bias_row, wq, wk, norm_g, batch):
    _, t, width = proj.shape
    seq = t // batch
    head_dim = width // ML_HEADS
    n_if = gates_row.shape[1]
    proj4 = proj.reshape(proj.shape[0], batch, seq, width)
    kern = functools.partial(_mlstm_kernel, head_dim=head_dim, batch=batch)
    const2 = lambda n: (0, 0)
    const3 = lambda n: (0, 0, 0)
    n_state = batch * ML_HEADS
    out = pl.pallas_call(
        kern,
        grid=(seq // CHUNK,),
        in_specs=[
            pl.BlockSpec((None, batch, CHUNK, width), lambda n: (PROJ_M, 0, n, 0)),
            pl.BlockSpec((batch, width, CHUNK), lambda n: (0, 0, n)),
            pl.BlockSpec((None, batch, CHUNK, width), lambda n: (PROJ_O, 0, n, 0)),
            pl.BlockSpec((batch, CHUNK, LANES), lambda n: (0, n, 0)),
            pl.BlockSpec((batch, n_if, CHUNK), lambda n: (0, 0, n)),
            pl.BlockSpec((n_if, 1), const2),
            pl.BlockSpec((1, LANES), const2),
            pl.BlockSpec((ML_HEADS, head_dim, head_dim), const3),
            pl.BlockSpec((ML_HEADS, head_dim, head_dim), const3),
            pl.BlockSpec((1, width), const2),
        ],
        out_specs=pl.BlockSpec((batch, CHUNK, width), lambda n: (0, n, 0)),
        out_shape=jax.ShapeDtypeStruct((batch, seq, width), BF16),
        scratch_shapes=[
            pltpu.VMEM((n_state, head_dim + LANES, head_dim), F32),
            pltpu.VMEM((n_state, LANES), F32),
            pltpu.VMEM((ML_HEADS, head_dim, head_dim), BF16),
            pltpu.VMEM((ML_HEADS, head_dim, head_dim), BF16),
        ],
        compiler_params=_params("arbitrary"),
        name="mlstm",
    )(proj4, vt, proj4, gates_col.reshape(batch, seq, LANES), gates_row, bias_col, bias_row, wq, wk, norm_g)
    return out.reshape(t, width)


def _merge_kernel(x_ref, ya_ref, yb_ref, ga0_ref, ga1_ref, gb0_ref, gb1_ref, wa_ref, wb_ref, wo_ref,
                  g2_ref, w2_ref, x1_ref, hn_ref, w2_out):
    w2_out[...] = w2_ref[...].astype(BF16)
    gate_a = jnp.concatenate([ga0_ref[...], ga1_ref[...]], axis=1).astype(F32)
    gate_b = jnp.concatenate([gb0_ref[...], gb1_ref[...]], axis=1).astype(F32)
    mixed = gate_a * _dot(ya_ref[...], wa_ref[...]) + gate_b * _dot(yb_ref[...], wb_ref[...])
    x1 = x_ref[...] + _dot(mixed.astype(BF16), wo_ref[...])
    x1_ref[...] = x1
    hn_ref[...] = _rms_norm(x1, g2_ref[...]).astype(BF16)


def _merge(x, y_a, y_b, proj, w_a, w_b, w_out, g2, w_ff2, layer, tm):
    t, d = x.shape
    width = y_a.shape[1]
    n_steps = t // tm
    d_ff = w_ff2.shape[1]
    rows = _cast_rows(d_ff, n_steps)
    sec = lambda s: pl.BlockSpec((None, tm, SECTION), lambda i: (s, i, 0))
    const = lambda i: (0, 0)
    resident = dict(pipeline_mode=pl.Buffered(1))
    return pl.pallas_call(
        _merge_kernel,
        grid=(n_steps,),
        in_specs=[
            pl.BlockSpec((tm, d), lambda i: (i, 0)),
            pl.BlockSpec((tm, width), lambda i: (i, 0)),
            pl.BlockSpec((tm, width), lambda i: (i, 0)),
            sec(PROJ_GA), sec(PROJ_GA + 1), sec(PROJ_GB), sec(PROJ_GB + 1),
            pl.BlockSpec((width, d), const, **resident),
            pl.BlockSpec((width, d), const, **resident),
            pl.BlockSpec((d, d), const, **resident),
            pl.BlockSpec((1, d), const),
            pl.BlockSpec((None, rows, d), lambda i: (layer, i, 0)),
        ],
        out_specs=[
            pl.BlockSpec((tm, d), lambda i: (i, 0)),
            pl.BlockSpec((tm, d), lambda i: (i, 0)),
            pl.BlockSpec((rows, d), lambda i: (i, 0)),
        ],
        out_shape=[
            jax.ShapeDtypeStruct((t, d), F32),
            jax.ShapeDtypeStruct((t, d), BF16),
            jax.ShapeDtypeStruct((d_ff, d), BF16),
        ],
        compiler_params=_params("arbitrary"),
        name="merge",
    )(x, y_a, y_b, proj, proj, proj, proj, w_a, w_b, w_out, g2, w_ff2)


def _ffn_kernel(hn_ref, x1_ref, w1_ref, w2_ref, gf_ref, out_ref, acc_ref, *, final_norm):
    j = pl.program_id(1)

    @pl.when(j == 0)
    def _():
        acc_ref[...] = x1_ref[...]

    h = jnp.square(jnp.maximum(_dot(hn_ref[...], w1_ref[...]), 0.0)).astype(BF16)
    acc_ref[...] += _dot(h, w2_ref[...])

    @pl.when(j == pl.num_programs(1) - 1)
    def _():
        y = acc_ref[...]
        out_ref[...] = _rms_norm(y, gf_ref[...]) if final_norm else y


def _ffn(hn, x1, w1, w2, gf, tm, tf, final_norm):
    t, d = x1.shape
    d_ff = w1.shape[1]
    return pl.pallas_call(
        functools.partial(_ffn_kernel, final_norm=final_norm),
        grid=(t // tm, d_ff // tf),
        in_specs=[
            pl.BlockSpec((tm, d), lambda i, j: (i, 0)),
            pl.BlockSpec((tm, d), lambda i, j: (i, 0)),
            pl.BlockSpec((d, tf), lambda i, j: (0, j)),
            pl.BlockSpec((tf, d), lambda i, j: (j, 0)),
            pl.BlockSpec((1, d), lambda i, j: (0, 0)),
        ],
        out_specs=pl.BlockSpec((tm, d), lambda i, j: (i, 0)),
        out_shape=jax.ShapeDtypeStruct((t, d), F32),
        scratch_shapes=[pltpu.VMEM((tm, d), F32)],
        compiler_params=_params("arbitrary", "arbitrary"),
        name="ffn",
    )(hn, x1, w1, w2, gf)


def _tiles(t):
    return dict(prologue=min(t, 512), inproj=min(t, 1024), gmlp=min(t, 1024), merge=min(t, 256),
                ffn_m=min(t, 512), ffn_f=1024)


def kernel(x, norm1_g, w_in, b_gate, gm_ln_g, gm_ln_b, gm_ws, gm_bs, ml_conv_w, ml_conv_b, ml_wq, ml_wk,
           ml_ig_b, ml_fg_b, ml_norm_g, w_a, w_b, w_out, norm2_g, w_ff1, w_ff2, norm_f_g):
    batch, seq, d = x.shape
    depth = w_in.shape[0]
    t = batch * seq
    tiles = _tiles(t)
    if_row = SEC_GA * SECTION
    n_if = 2 * ML_HEADS
    xt = x.reshape(t, d)
    w_t = jnp.swapaxes(w_in, 1, 2)
    for l in range(depth):
        gate_bias = jnp.concatenate([ml_ig_b[l], ml_fg_b[l]])
        bias5 = jnp.concatenate([jnp.zeros((1, SECTION), F32), b_gate[l].reshape(4, SECTION)])[:, None]
        xn, gates_col, gates_row = _prologue(xt, norm1_g[l][None], w_t, l, if_row, n_if, batch,
                                             tiles["prologue"])
        proj, vt, wa_bf, wb_bf, wo_bf, w1_bf = _inproj(
            xn, w_t, l, if_row + n_if, bias5, gm_ln_g[l][None], gm_ln_b[l][None],
            ml_conv_w[l], ml_conv_b[l][None], w_a, w_b, w_out, w_ff1, batch, tiles["inproj"])
        y_a = _gmlp(proj, gm_ws[l], gm_bs[l].T, tiles["gmlp"])
        y_b = _mlstm(
            proj, vt, gates_col, gates_row, gate_bias[:, None],
            jnp.pad(gate_bias, (0, LANES - n_if))[None], ml_wq[l], ml_wk[l], ml_norm_g[l][None], batch)
        x1, hn, w2_bf = _merge(xt, y_a, y_b, proj, wa_bf, wb_bf, wo_bf, norm2_g[l][None], w_ff2, l,
                               tiles["merge"])
        xt = _ffn(hn, x1, w1_bf, w2_bf, norm_f_g[None], tiles["ffn_m"], tiles["ffn_f"],
                  final_norm=l == depth - 1)
    return xt.reshape(batch, seq, d)
```

```python
import functools

import jax
import jax.numpy as jnp
from jax import lax
from jax.experimental import pallas as pl
from jax.experimental.pallas import tpu as pltpu

EPS = 1e-6
GM_GROUP_DIM = 128
CHUNK = 128
ML_HEADS = 4
CONV_K = 4
LANES = 128
SUBLANES = 8
BF16_ROWS = 16
SECTION = 1024
SEC_U, SEC_V, SEC_M, SEC_MV, SEC_O, SEC_GA, SEC_GB = 0, 1, 2, 3, 4, 5, 7
N_SECTIONS = 9
PROJ_U, PROJ_V, PROJ_M, PROJ_O, PROJ_GA, PROJ_GB = 0, 1, 2, 3, 4, 6
N_PROJ = 8
VMEM_LIMIT = 56 * 1024 * 1024

BF16 = jnp.bfloat16
F32 = jnp.float32


def _sigmoid(x):
    return 0.5 * jnp.tanh(0.5 * x) + 0.5


def _log_sigmoid(x):
    return jnp.minimum(x, 0.0) - jnp.log1p(jnp.exp(-jnp.abs(x)))


def _rms_norm(x, g):
    return x * lax.rsqrt(jnp.mean(x * x, axis=-1, keepdims=True) + EPS) * g


def _dot(a, b):
    return jnp.dot(a, b, preferred_element_type=F32)


def _params(*semantics):
    return pltpu.CompilerParams(dimension_semantics=semantics, vmem_limit_bytes=VMEM_LIMIT)


def _cast_rows(n_rows, n_steps):
    rows = n_rows // n_steps
    assert rows * n_steps == n_rows and rows % BF16_ROWS == 0, (n_rows, n_steps)
    return rows


def _nt_dot(a, b):
    return lax.dot_general(a, b, (((1,), (1,)), ((), ())), preferred_element_type=F32)


def _prologue_kernel(x_ref, g1_ref, wif_ref, xn_ref, gcol_ref, grow_ref):
    xn = _rms_norm(x_ref[...], g1_ref[...]).astype(BF16)
    xn_ref[...] = xn
    wif = wif_ref[...].astype(BF16)
    gcol_ref[...] = _nt_dot(xn, wif)
    grow_ref[...] = _nt_dot(wif, xn)[:grow_ref.shape[0], :]


def _prologue(x, g1, w_t, layer, if_row, n_if, batch, tm):
    t, d = x.shape
    seq = t // batch
    tiles_per_seq = seq // tm
    assert tiles_per_seq * tm == seq
    return pl.pallas_call(
        _prologue_kernel,
        grid=(t // tm,),
        in_specs=[
            pl.BlockSpec((tm, d), lambda i: (i, 0)),
            pl.BlockSpec((1, d), lambda i: (0, 0)),
            pl.BlockSpec((None, LANES, d), lambda i: (layer, if_row // LANES, 0)),
        ],
        out_specs=[
            pl.BlockSpec((tm, d), lambda i: (i, 0)),
            pl.BlockSpec((tm, LANES), lambda i: (i, 0)),
            pl.BlockSpec((None, n_if, tm), lambda i: (i // tiles_per_seq, 0, i % tiles_per_seq)),
        ],
        out_shape=[
            jax.ShapeDtypeStruct((t, d), BF16),
            jax.ShapeDtypeStruct((t, LANES), F32),
            jax.ShapeDtypeStruct((batch, n_if, seq), F32),
        ],
        compiler_params=_params("arbitrary"),
        name="prologue",
    )(x, g1, w_t)


VEC_LN_G, VEC_LN_B, VEC_CONV_B, VEC_CONV_W, VEC_GATE_B = 0, 1, 2, 3, 7
N_VEC_ROWS = 16


def _inproj_kernel(xn_ref, w_ref, vec_ref, cast_ref, out_ref, vt_out, cast_out, wbf_ref, tail_ref,
                   *, tiles_per_seq, n_cast_steps):
    j = pl.program_id(0)
    i = pl.program_id(1)
    tm = xn_ref.shape[0]
    vec = lambda r, n=1: vec_ref[pl.ds(r, n), :]

    @pl.when(j * pl.num_programs(1) + i < n_cast_steps)
    def _():
        cast_out[...] = cast_ref[...].astype(BF16)

    @pl.when(i == 0)
    def _():
        rows = 256
        for r in range(0, wbf_ref.shape[0], rows):
            wbf_ref[r:r + rows, :] = w_ref[r:r + rows, :].astype(BF16)

    def proj():
        return _nt_dot(xn_ref[...], wbf_ref[...])

    @pl.when(j == SEC_U)
    def _():
        out_ref[...] = jax.nn.gelu(proj()).astype(BF16)

    @pl.when(j == SEC_V)
    def _():
        v = jax.nn.gelu(proj())
        mu = jnp.mean(v, axis=-1, keepdims=True)
        var = jnp.mean(jnp.square(v - mu), axis=-1, keepdims=True)
        out_ref[...] = ((v - mu) * lax.rsqrt(var + EPS) * vec(VEC_LN_G) + vec(VEC_LN_B)).astype(BF16)

    @pl.when(j == SEC_M)
    def _():
        @pl.when(i % tiles_per_seq == 0)
        def _():
            tail_ref[...] = jnp.zeros(tail_ref.shape, F32)

        x = proj()
        xe = jnp.concatenate([tail_ref[...], x], axis=0)
        conv = vec(VEC_CONV_B) + vec(VEC_CONV_W + CONV_K - 1) * x
        for d in range(1, CONV_K):
            conv = conv + vec(VEC_CONV_W + CONV_K - 1 - d) * xe[SUBLANES - d:SUBLANES - d + tm, :]
        tail_ref[...] = x[tm - SUBLANES:, :]
        out_ref[...] = (conv * _sigmoid(conv)).astype(BF16)

    @pl.when(j == SEC_MV)
    def _():
        vt_out[...] = _nt_dot(wbf_ref[...], xn_ref[...]).astype(BF16)

    @pl.when(j >= SEC_O)
    def _():
        out_ref[...] = _sigmoid(proj() + vec(VEC_GATE_B + j - SEC_O)).astype(BF16)


def _inproj(xn, w_t, layer, gate_row, vecs, cast_w, batch, tm):
    t, d = xn.shape
    n_i = t // tm
    seq = t // batch
    tiles_per_seq = seq // tm
    assert tiles_per_seq * tm == seq
    n_cast_steps = (N_SECTIONS - 1) * n_i
    cast_rows = _cast_rows(cast_w.shape[1], n_cast_steps)
    cast_step = lambda j, i: jnp.minimum(j * n_i + i, n_cast_steps - 1)

    def proj_block(j, i):
        mv = j == SEC_MV
        return (j - (j >= SEC_MV).astype(jnp.int32), jnp.where(mv, n_i - 1, i), 0)

    def vt_block(j, i):
        tile = jnp.clip((j - SEC_MV) * n_i + i, 0, n_i - 1)
        return (tile // tiles_per_seq, 0, tile % tiles_per_seq)

    gate_skip = gate_row - SEC_GA * SECTION
    assert gate_skip % SUBLANES == 0
    kern = functools.partial(_inproj_kernel, tiles_per_seq=tiles_per_seq, n_cast_steps=n_cast_steps)
    return pl.pallas_call(
        kern,
        grid=(N_SECTIONS, n_i),
        in_specs=[
            pl.BlockSpec((tm, d), lambda j, i: (i, 0)),
            pl.BlockSpec((None, pl.Element(SECTION), pl.Element(d)),
                         lambda j, i: (layer, SUBLANES * (j * (SECTION // SUBLANES) + jnp.where(
                             j >= SEC_GA, gate_skip // SUBLANES, 0)), 0)),
            pl.BlockSpec((N_VEC_ROWS, SECTION), lambda j, i: (0, 0)),
            pl.BlockSpec((None, cast_rows, cast_w.shape[2]), lambda j, i: (layer, cast_step(j, i), 0)),
        ],
        out_specs=[pl.BlockSpec((None, tm, SECTION), proj_block), pl.BlockSpec((None, SECTION, tm), vt_block),
                   pl.BlockSpec((cast_rows, cast_w.shape[2]), lambda j, i: (cast_step(j, i), 0))],
        out_shape=[jax.ShapeDtypeStruct((N_PROJ, t, SECTION), BF16),
                   jax.ShapeDtypeStruct((batch, SECTION, seq), BF16),
                   jax.ShapeDtypeStruct(cast_w.shape[1:], BF16)],
        scratch_shapes=[
            pltpu.VMEM((SECTION, d), BF16),
            pltpu.VMEM((SUBLANES, SECTION), F32),
        ],
        compiler_params=_params("arbitrary", "arbitrary"),
        name="inproj",
    )(xn, w_t, vecs, cast_w)


def _gmlp_kernel(u_ref, v_ref, ws_ref, bs_ref, out_ref, *, n_chunks, n_groups):
    row = lax.broadcasted_iota(jnp.int32, (CHUNK, CHUNK), 0)
    col = lax.broadcasted_iota(jnp.int32, (CHUNK, CHUNK), 1)
    causal = col <= row
    for g in range(n_groups):
        gs = slice(g * GM_GROUP_DIM, (g + 1) * GM_GROUP_DIM)
        w = jnp.where(causal, ws_ref[g], 0.0).astype(BF16)
        vg = jnp.concatenate([v_ref[c * CHUNK:(c + 1) * CHUNK, gs] for c in range(n_chunks)], axis=1)
        s = _dot(w, vg) + bs_ref[:, g:g + 1]
        for c in range(n_chunks):
            rs = slice(c * CHUNK, (c + 1) * CHUNK)
            sc = s[:, c * GM_GROUP_DIM:(c + 1) * GM_GROUP_DIM]
            out_ref[rs, gs] = (u_ref[rs, gs].astype(F32) * sc).astype(BF16)


def _gmlp(proj, ws, bs_t, tm):
    _, t, width = proj.shape
    n_groups = width // GM_GROUP_DIM
    kern = functools.partial(_gmlp_kernel, n_chunks=tm // CHUNK, n_groups=n_groups)
    return pl.pallas_call(
        kern,
        grid=(t // tm,),
        in_specs=[
            pl.BlockSpec((None, tm, width), lambda i: (PROJ_U, i, 0)),
            pl.BlockSpec((None, tm, width), lambda i: (PROJ_V, i, 0)),
            pl.BlockSpec((n_groups, CHUNK, CHUNK), lambda i: (0, 0, 0)),
            pl.BlockSpec((CHUNK, n_groups), lambda i: (0, 0)),
        ],
        out_specs=pl.BlockSpec((tm, width), lambda i: (i, 0)),
        out_shape=jax.ShapeDtypeStruct((t, width), BF16),
        compiler_params=_params("arbitrary"),
        name="gmlp",
    )(proj, proj, ws, bs_t)


def _lanes(col):
    return jnp.broadcast_to(col, (col.shape[0], LANES))


def _wide(x, n):
    return jnp.concatenate([x] * n, axis=1)


def _mlstm_kernel(co_ref, vt_ref, gcol_ref, grow_ref, bcol_ref, brow_ref, wq_ref, wk_ref, ng_ref, *rest,
                  head_dim, batch, n_cast):
    cast_in, (out_ref, *cast_out), (st_ref, mx_ref, wqb_ref, wkb_ref) = (
        rest[:n_cast], rest[n_cast:2 * n_cast + 1], rest[2 * n_cast + 1:])
    c_ref, o_ref = co_ref.at[0], co_ref.at[1]

    for src, dst in zip(cast_in, cast_out):
        dst[...] = src[...].astype(BF16)

    @pl.when(pl.program_id(0) == 0)
    def _():
        wqb_ref[...] = wq_ref[...].astype(BF16)
        wkb_ref[...] = (wk_ref[...] * (head_dim ** -0.5)).astype(BF16)
        st_ref[...] = jnp.zeros(st_ref.shape, F32)
        mx_ref[...] = jnp.zeros(mx_ref.shape, F32)

    row = lax.broadcasted_iota(jnp.int32, (CHUNK, CHUNK), 0)
    col = lax.broadcasted_iota(jnp.int32, (CHUNK, CHUNK), 1)
    causal = col <= row
    tril = causal.astype(F32)
    triu = (row <= col).astype(F32)
    ones_rows = jnp.ones((LANES, CHUNK), BF16)
    nd = head_dim // LANES
    heads = [(b, h) for b in range(batch) for h in range(ML_HEADS)]
    slot = lambda b, h: b * ML_HEADS + h
    hsl = lambda h: slice(h * head_dim, (h + 1) * head_dim)


    g_col, g_row, bcum_col, bcum_row, total = {}, {}, {}, {}, {}
    for b in range(batch):
        g_col[b] = gcol_ref[b] + brow_ref[...]
        g_row[b] = grow_ref[b] + bcol_ref[...]
        lf_row = _log_sigmoid(g_row[b])
        bcum_col[b] = jnp.dot(tril, _log_sigmoid(g_col[b]), precision=lax.Precision.HIGHEST,
                              preferred_element_type=F32)
        bcum_row[b] = jnp.dot(lf_row, triu, precision=lax.Precision.HIGHEST, preferred_element_type=F32)
        total[b] = jnp.sum(lf_row, axis=-1, keepdims=True)

    q, k = {}, {}
    for b, h in heads:
        c = c_ref[b, :, hsl(h)]
        q[b, h] = _dot(c, wqb_ref[h])
        k[b, h] = _dot(c, wkb_ref[h])

    w_intra, w_inter, inv_floor, w_state, decay, scale, m_new = {}, {}, {}, {}, {}, {}, {}
    for b, h in heads:
        f = ML_HEADS + h
        bc = _lanes(bcum_col[b][:, f:f + 1])
        ig = _lanes(g_col[b][:, h:h + 1])
        src = g_row[b][h:h + 1, :] - bcum_row[b][f:f + 1, :]
        b_last = jnp.broadcast_to(total[b][f:f + 1, :], (1, LANES))
        m_prev = mx_ref[slot(b, h):slot(b, h) + 1, :]
        inter = bc + m_prev
        dlog = jnp.where(causal, bc + src, -jnp.inf)
        m_t = jnp.maximum(inter, _lanes(jnp.max(dlog, axis=-1, keepdims=True)))
        w_intra[b, h] = jnp.exp(dlog - m_t)
        w_inter[b, h] = jnp.exp(inter - m_t)
        inv_floor[b, h] = jnp.exp(-m_t)
        a = b_last - bc + ig
        a_max = jnp.max(a, axis=0, keepdims=True)
        w_state[b, h] = jnp.exp(a - a_max)
        m_new[b, h] = jnp.maximum(b_last + m_prev, a_max)
        decay[b, h] = jnp.exp(b_last + m_prev - m_new[b, h])
        scale[b, h] = jnp.exp(a_max - m_new[b, h])

    qk = {}
    for b, h in heads:
        qk[b, h] = _nt_dot(q[b, h].astype(BF16), k[b, h].astype(BF16)) * w_intra[b, h]

    for b, h in heads:
        vt_ext = jnp.concatenate([vt_ref[b, hsl(h), :], ones_rows], axis=0)
        ext = (_wide(w_inter[b, h], nd + 1) * _nt_dot(q[b, h].astype(BF16), st_ref[slot(b, h)].astype(BF16))
               + _nt_dot(qk[b, h].astype(BF16), vt_ext))
        den = ext[:, head_dim:]
        hid = ext[:, :head_dim] * _wide(1.0 / jnp.maximum(jnp.abs(den), inv_floor[b, h]), nd)
        mu = jnp.mean(hid, axis=-1, keepdims=True)
        var = jnp.mean(jnp.square(hid - mu), axis=-1, keepdims=True)
        hid = (hid - mu) * lax.rsqrt(var + EPS) * ng_ref[:, hsl(h)]
        out_ref[b, :, hsl(h)] = (o_ref[b, :, hsl(h)].astype(F32) * hid).astype(BF16)

    for b, h in heads:
        s = slot(b, h)
        vt_ext = jnp.concatenate([vt_ref[b, hsl(h), :], ones_rows], axis=0)
        kw = k[b, h] * _wide(w_state[b, h], nd)
        upd = _dot(vt_ext, kw.astype(BF16))
        st_ref[s] = _wide(decay[b, h], nd) * st_ref[s] + _wide(scale[b, h], nd) * upd
        mx_ref[s:s + 1, :] = m_new[b, h]


def _mlstm(proj, vt, gates_col, gates_row, bias_col, bias_row, wq, wk, norm_g, cast_ws, layer, batch):
    _, t, width = proj.shape
    seq = t // batch
    head_dim = width // ML_HEADS
    n_if = gates_row.shape[1]
    n_steps = seq // CHUNK
    assert PROJ_O == PROJ_M + 1 and PROJ_M % 2 == 0
    proj4 = proj.reshape(proj.shape[0], batch, seq, width)
    cast_rows = [_cast_rows(w.shape[1], n_steps) for w in cast_ws]
    kern = functools.partial(_mlstm_kernel, head_dim=head_dim, batch=batch, n_cast=len(cast_ws))
    const2 = lambda n: (0, 0)
    const3 = lambda n: (0, 0, 0)
    n_state = batch * ML_HEADS
    out, *cast = pl.pallas_call(
        kern,
        grid=(n_steps,),
        in_specs=[
            pl.BlockSpec((2, batch, CHUNK, width), lambda n: (PROJ_M // 2, 0, n, 0)),
            pl.BlockSpec((batch, width, CHUNK), lambda n: (0, 0, n)),
            pl.BlockSpec((batch, CHUNK, LANES), lambda n: (0, n, 0)),
            pl.BlockSpec((batch, n_if, CHUNK), lambda n: (0, 0, n)),
            pl.BlockSpec((n_if, 1), const2),
            pl.BlockSpec((1, LANES), const2),
            pl.BlockSpec((ML_HEADS, head_dim, head_dim), const3),
            pl.BlockSpec((ML_HEADS, head_dim, head_dim), const3),
            pl.BlockSpec((1, width), const2),
        ] + [pl.BlockSpec((None, r, w.shape[2]), lambda n: (layer, n, 0)) for r, w in zip(cast_rows, cast_ws)],
        out_specs=[pl.BlockSpec((batch, CHUNK, width), lambda n: (0, n, 0))]
        + [pl.BlockSpec((r, w.shape[2]), lambda n: (n, 0)) for r, w in zip(cast_rows, cast_ws)],
        out_shape=[jax.ShapeDtypeStruct((batch, seq, width), BF16)]
        + [jax.ShapeDtypeStruct(w.shape[1:], BF16) for w in cast_ws],
        scratch_shapes=[
            pltpu.VMEM((n_state, head_dim + LANES, head_dim), F32),
            pltpu.VMEM((n_state, LANES), F32),
            pltpu.VMEM((ML_HEADS, head_dim, head_dim), BF16),
            pltpu.VMEM((ML_HEADS, head_dim, head_dim), BF16),
        ],
        compiler_params=_params("arbitrary"),
        name="mlstm",
    )(proj4, vt, gates_col.reshape(batch, seq, LANES), gates_row, bias_col, bias_row, wq, wk, norm_g, *cast_ws)
    return out.reshape(t, width), cast


def _merge_kernel(x_ref, ya_ref, yb_ref, gate_ref, wa_ref, wb_ref, wo_ref, g2_ref, x1_ref, hn_ref):
    gate_a = jnp.concatenate([gate_ref[0], gate_ref[1]], axis=1).astype(F32)
    gate_b = jnp.concatenate([gate_ref[2], gate_ref[3]], axis=1).astype(F32)
    mixed = gate_a * _dot(ya_ref[...], wa_ref[...]) + gate_b * _dot(yb_ref[...], wb_ref[...])
    x1 = x_ref[...] + _dot(mixed.astype(BF16), wo_ref[...])
    x1_ref[...] = x1
    hn_ref[...] = _rms_norm(x1, g2_ref[...]).astype(BF16)


def _merge(x, y_a, y_b, proj, w_a, w_b, w_out, g2, tm):
    t, d = x.shape
    width = y_a.shape[1]
    n_steps = t // tm
    n_gate = N_PROJ - PROJ_GA
    assert PROJ_GA % n_gate == 0
    const = lambda i: (0, 0)
    resident = dict(pipeline_mode=pl.Buffered(1))
    return pl.pallas_call(
        _merge_kernel,
        grid=(n_steps,),
        in_specs=[
            pl.BlockSpec((tm, d), lambda i: (i, 0)),
            pl.BlockSpec((tm, width), lambda i: (i, 0)),
            pl.BlockSpec((tm, width), lambda i: (i, 0)),
            pl.BlockSpec((n_gate, tm, SECTION), lambda i: (PROJ_GA // n_gate, i, 0)),
            pl.BlockSpec((width, d), const, **resident),
            pl.BlockSpec((width, d), const, **resident),
            pl.BlockSpec((d, d), const, **resident),
            pl.BlockSpec((1, d), const),
        ],
        out_specs=[pl.BlockSpec((tm, d), lambda i: (i, 0)), pl.BlockSpec((tm, d), lambda i: (i, 0))],
        out_shape=[jax.ShapeDtypeStruct((t, d), F32), jax.ShapeDtypeStruct((t, d), BF16)],
        compiler_params=_params("arbitrary"),
        name="merge",
    )(x, y_a, y_b, proj, w_a, w_b, w_out, g2)


def _ffn_kernel(hn_ref, x1_ref, w1_ref, w2_ref, gf_ref, out_ref, acc_ref, *, final_norm):
    j = pl.program_id(1)

    @pl.when(j == 0)
    def _():
        acc_ref[...] = x1_ref[...]

    h = jnp.square(jnp.maximum(_dot(hn_ref[...], w1_ref[...]), 0.0)).astype(BF16)
    acc_ref[...] += _dot(h, w2_ref[...])

    @pl.when(j == pl.num_programs(1) - 1)
    def _():
        y = acc_ref[...]
        out_ref[...] = _rms_norm(y, gf_ref[...]) if final_norm else y


def _ffn(hn, x1, w1, w2, gf, tm, tf, final_norm):
    t, d = x1.shape
    d_ff = w1.shape[1]
    return pl.pallas_call(
        functools.partial(_ffn_kernel, final_norm=final_norm),
        grid=(t // tm, d_ff // tf),
        in_specs=[
            pl.BlockSpec((tm, d), lambda i, j: (i, 0)),
            pl.BlockSpec((tm, d), lambda i, j: (i, 0)),
            pl.BlockSpec((d, tf), lambda i, j: (0, j)),
            pl.BlockSpec((tf, d), lambda i, j: (j, 0)),
            pl.BlockSpec((1, d), lambda i, j: (0, 0)),
        ],
        out_specs=pl.BlockSpec((tm, d), lambda i, j: (i, 0)),
        out_shape=jax.ShapeDtypeStruct((t, d), F32),
        scratch_shapes=[pltpu.VMEM((tm, d), F32)],
        compiler_params=_params("arbitrary", "arbitrary"),
        name="ffn",
    )(hn, x1, w1, w2, gf)


def _tiles(t):
    return dict(prologue=min(t, 1024), inproj=min(t, 1024), gmlp=min(t, 1024), merge=min(t, 512),
                ffn_m=min(t, 512), ffn_f=1024)


def kernel(x, norm1_g, w_in, b_gate, gm_ln_g, gm_ln_b, gm_ws, gm_bs, ml_conv_w, ml_conv_b, ml_wq, ml_wk,
           ml_ig_b, ml_fg_b, ml_norm_g, w_a, w_b, w_out, norm2_g, w_ff1, w_ff2, norm_f_g):
    batch, seq, d = x.shape
    depth = w_in.shape[0]
    t = batch * seq
    tiles = _tiles(t)
    if_row = SEC_GA * SECTION
    n_if = 2 * ML_HEADS
    xt = x.reshape(t, d)
    w_t = jnp.swapaxes(w_in, 1, 2)
    for l in range(depth):
        gate_bias = jnp.concatenate([ml_ig_b[l], ml_fg_b[l]])
        vecs = jnp.concatenate([
            gm_ln_g[l][None], gm_ln_b[l][None], ml_conv_b[l][None], ml_conv_w[l],
            jnp.zeros((1, SECTION), F32), b_gate[l].reshape(4, SECTION),
            jnp.zeros((N_VEC_ROWS - VEC_GATE_B - 5, SECTION), F32)])
        xn, gates_col, gates_row = _prologue(xt, norm1_g[l][None], w_t, l, if_row, n_if, batch,
                                             tiles["prologue"])
        proj, vt, w2_bf = _inproj(xn, w_t, l, if_row + n_if, vecs, w_ff2, batch, tiles["inproj"])
        y_a = _gmlp(proj, gm_ws[l], gm_bs[l].T, tiles["gmlp"])
        y_b, (wa_bf, wb_bf, wo_bf, w1_bf) = _mlstm(
            proj, vt, gates_col, gates_row, gate_bias[:, None],
            jnp.pad(gate_bias, (0, LANES - n_if))[None], ml_wq[l], ml_wk[l], ml_norm_g[l][None],
            (w_a, w_b, w_out, w_ff1), l, batch)
        x1, hn = _merge(xt, y_a, y_b, proj, wa_bf, wb_bf, wo_bf, norm2_g[l][None], tiles["merge"])
        xt = _ffn(hn, x1, w1_bf, w2_bf, norm_f_g[None], tiles["ffn_m"], tiles["ffn_f"],
                  final_norm=l == depth - 1)
    return xt.reshape(batch, seq, d)
```

```python
import functools

import jax
import jax.numpy as jnp
from jax import lax
from jax.experimental import pallas as pl
from jax.experimental.pallas import tpu as pltpu

EPS = 1e-6
GM_GROUP_DIM = 128
CHUNK = 128
ML_HEADS = 4
CONV_K = 4
LANES = 128
SUBLANES = 8
BF16_ROWS = 16
SECTION = 1024
SEC_U, SEC_V, SEC_M, SEC_MV, SEC_O, SEC_GA, SEC_GB = 0, 1, 2, 3, 4, 5, 7
N_SECTIONS = 9
PROJ_U, PROJ_V, PROJ_M, PROJ_O, PROJ_GA, PROJ_GB = 0, 1, 2, 3, 4, 6
N_PROJ = 8
VMEM_LIMIT = 56 * 1024 * 1024

BF16 = jnp.bfloat16
F32 = jnp.float32


def _sigmoid(x):
    return 0.5 * jnp.tanh(0.5 * x) + 0.5


def _log_sigmoid(x):
    return jnp.minimum(x, 0.0) - jnp.log(1.0 + jnp.exp(-jnp.abs(x)))


def _rms_norm(x, g):
    return x * lax.rsqrt(jnp.mean(x * x, axis=-1, keepdims=True) + EPS) * g


def _dot(a, b):
    return jnp.dot(a, b, preferred_element_type=F32)


def _params(*semantics):
    return pltpu.CompilerParams(dimension_semantics=semantics, vmem_limit_bytes=VMEM_LIMIT)


def _cast_rows(n_rows, n_steps):
    rows = n_rows // n_steps
    assert rows * n_steps == n_rows and rows % BF16_ROWS == 0, (n_rows, n_steps)
    return rows


def _nt_dot(a, b):
    return lax.dot_general(a, b, (((1,), (1,)), ((), ())), preferred_element_type=F32)


def _prologue_kernel(x_ref, g1_ref, wif_ref, xn_ref, gcol_ref, grow_ref):
    xn = _rms_norm(x_ref[...], g1_ref[...]).astype(BF16)
    xn_ref[...] = xn
    wif = wif_ref[...].astype(BF16)
    gcol_ref[...] = _nt_dot(xn, wif)
    grow_ref[...] = _nt_dot(wif, xn)[:grow_ref.shape[0], :]


def _prologue(x, g1, w_t, layer, if_row, n_if, batch, tm):
    t, d = x.shape
    seq = t // batch
    tiles_per_seq = seq // tm
    assert tiles_per_seq * tm == seq
    return pl.pallas_call(
        _prologue_kernel,
        grid=(t // tm,),
        in_specs=[
            pl.BlockSpec((tm, d), lambda i: (i, 0)),
            pl.BlockSpec((1, d), lambda i: (0, 0)),
            pl.BlockSpec((None, LANES, d), lambda i: (layer, if_row // LANES, 0)),
        ],
        out_specs=[
            pl.BlockSpec((tm, d), lambda i: (i, 0)),
            pl.BlockSpec((tm, LANES), lambda i: (i, 0)),
            pl.BlockSpec((None, n_if, tm), lambda i: (i // tiles_per_seq, 0, i % tiles_per_seq)),
        ],
        out_shape=[
            jax.ShapeDtypeStruct((t, d), BF16),
            jax.ShapeDtypeStruct((t, LANES), F32),
            jax.ShapeDtypeStruct((batch, n_if, seq), F32),
        ],
        compiler_params=_params("arbitrary"),
        name="prologue",
    )(x, g1, w_t)


VEC_LN_G, VEC_LN_B, VEC_CONV_B, VEC_CONV_W, VEC_GATE_B = 0, 1, 2, 3, 7
N_VEC_ROWS = 16


def _inproj_kernel(xn_ref, w_ref, vec_ref, cast_ref, out_ref, vt_out, cast_out, wbf_ref, tail_ref,
                   *, tiles_per_seq, n_cast_steps):
    j = pl.program_id(0)
    i = pl.program_id(1)
    tm = xn_ref.shape[0]
    vec = lambda r, n=1: vec_ref[pl.ds(r, n), :]

    @pl.when(j * pl.num_programs(1) + i < n_cast_steps)
    def _():
        cast_out[...] = cast_ref[...].astype(BF16)

    @pl.when(i == 0)
    def _():
        rows = 256
        for r in range(0, wbf_ref.shape[0], rows):
            wbf_ref[r:r + rows, :] = w_ref[r:r + rows, :].astype(BF16)

    def proj():
        return _nt_dot(xn_ref[...], wbf_ref[...])

    @pl.when(j == SEC_U)
    def _():
        out_ref[...] = jax.nn.gelu(proj()).astype(BF16)

    @pl.when(j == SEC_V)
    def _():
        v = jax.nn.gelu(proj())
        mu = jnp.mean(v, axis=-1, keepdims=True)
        var = jnp.mean(jnp.square(v - mu), axis=-1, keepdims=True)
        out_ref[...] = ((v - mu) * lax.rsqrt(var + EPS) * vec(VEC_LN_G) + vec(VEC_LN_B)).astype(BF16)

    @pl.when(j == SEC_M)
    def _():
        @pl.when(i % tiles_per_seq == 0)
        def _():
            tail_ref[...] = jnp.zeros(tail_ref.shape, F32)

        x = proj()
        xe = jnp.concatenate([tail_ref[...], x], axis=0)
        conv = vec(VEC_CONV_B) + vec(VEC_CONV_W + CONV_K - 1) * x
        for d in range(1, CONV_K):
            conv = conv + vec(VEC_CONV_W + CONV_K - 1 - d) * xe[SUBLANES - d:SUBLANES - d + tm, :]
        tail_ref[...] = x[tm - SUBLANES:, :]
        out_ref[...] = (conv * _sigmoid(conv)).astype(BF16)

    @pl.when(j == SEC_MV)
    def _():
        vt_out[...] = _nt_dot(wbf_ref[...], xn_ref[...]).astype(BF16)

    @pl.when(j >= SEC_O)
    def _():
        out_ref[...] = _sigmoid(proj() + vec(VEC_GATE_B + j - SEC_O)).astype(BF16)


def _inproj(xn, w_t, layer, gate_row, vecs, cast_w, batch, tm):
    t, d = xn.shape
    n_i = t // tm
    seq = t // batch
    tiles_per_seq = seq // tm
    assert tiles_per_seq * tm == seq
    n_cast_steps = (N_SECTIONS - 1) * n_i
    cast_rows = _cast_rows(cast_w.shape[1], n_cast_steps)
    cast_step = lambda j, i: jnp.minimum(j * n_i + i, n_cast_steps - 1)

    def proj_block(j, i):
        mv = j == SEC_MV
        return (j - (j >= SEC_MV).astype(jnp.int32), jnp.where(mv, n_i - 1, i), 0)

    def vt_block(j, i):
        tile = jnp.clip((j - SEC_MV) * n_i + i, 0, n_i - 1)
        return (tile // tiles_per_seq, 0, tile % tiles_per_seq)

    gate_skip = gate_row - SEC_GA * SECTION
    assert gate_skip % SUBLANES == 0
    kern = functools.partial(_inproj_kernel, tiles_per_seq=tiles_per_seq, n_cast_steps=n_cast_steps)
    return pl.pallas_call(
        kern,
        grid=(N_SECTIONS, n_i),
        in_specs=[
            pl.BlockSpec((tm, d), lambda j, i: (i, 0)),
            pl.BlockSpec((None, pl.Element(SECTION), pl.Element(d)),
                         lambda j, i: (layer, SUBLANES * (j * (SECTION // SUBLANES) + jnp.where(
                             j >= SEC_GA, gate_skip // SUBLANES, 0)), 0)),
            pl.BlockSpec((N_VEC_ROWS, SECTION), lambda j, i: (0, 0)),
            pl.BlockSpec((None, cast_rows, cast_w.shape[2]), lambda j, i: (layer, cast_step(j, i), 0)),
        ],
        out_specs=[pl.BlockSpec((None, tm, SECTION), proj_block), pl.BlockSpec((None, SECTION, tm), vt_block),
                   pl.BlockSpec((cast_rows, cast_w.shape[2]), lambda j, i: (cast_step(j, i), 0))],
        out_shape=[jax.ShapeDtypeStruct((N_PROJ, t, SECTION), BF16),
                   jax.ShapeDtypeStruct((batch, SECTION, seq), BF16),
                   jax.ShapeDtypeStruct(cast_w.shape[1:], BF16)],
        scratch_shapes=[
            pltpu.VMEM((SECTION, d), BF16),
            pltpu.VMEM((SUBLANES, SECTION), F32),
        ],
        compiler_params=_params("arbitrary", "arbitrary"),
        name="inproj",
    )(xn, w_t, vecs, cast_w)


def _gmlp_kernel(u_ref, v_ref, ws_ref, bs_ref, out_ref, *, n_chunks, n_groups):
    row = lax.broadcasted_iota(jnp.int32, (CHUNK, CHUNK), 0)
    col = lax.broadcasted_iota(jnp.int32, (CHUNK, CHUNK), 1)
    causal = col <= row
    for g in range(n_groups):
        gs = slice(g * GM_GROUP_DIM, (g + 1) * GM_GROUP_DIM)
        w = jnp.where(causal, ws_ref[g], 0.0).astype(BF16)
        vg = jnp.concatenate([v_ref[c * CHUNK:(c + 1) * CHUNK, gs] for c in range(n_chunks)], axis=1)
        s = _dot(w, vg) + bs_ref[:, g:g + 1]
        for c in range(n_chunks):
            rs = slice(c * CHUNK, (c + 1) * CHUNK)
            sc = s[:, c * GM_GROUP_DIM:(c + 1) * GM_GROUP_DIM]
            out_ref[rs, gs] = (u_ref[rs, gs].astype(F32) * sc).astype(BF16)


def _gmlp(proj, ws, bs_t, tm):
    _, t, width = proj.shape
    n_groups = width // GM_GROUP_DIM
    kern = functools.partial(_gmlp_kernel, n_chunks=tm // CHUNK, n_groups=n_groups)
    return pl.pallas_call(
        kern,
        grid=(t // tm,),
        in_specs=[
            pl.BlockSpec((None, tm, width), lambda i: (PROJ_U, i, 0)),
            pl.BlockSpec((None, tm, width), lambda i: (PROJ_V, i, 0)),
            pl.BlockSpec((n_groups, CHUNK, CHUNK), lambda i: (0, 0, 0)),
            pl.BlockSpec((CHUNK, n_groups), lambda i: (0, 0)),
        ],
        out_specs=pl.BlockSpec((tm, width), lambda i: (i, 0)),
        out_shape=jax.ShapeDtypeStruct((t, width), BF16),
        compiler_params=_params("arbitrary"),
        name="gmlp",
    )(proj, proj, ws, bs_t)


def _lanes(col):
    return jnp.broadcast_to(col, (col.shape[0], LANES))


def _wide(x, n):
    return jnp.concatenate([x] * n, axis=1)


def _mlstm_kernel(co_ref, vt_ref, gcol_ref, grow_ref, bcol_ref, brow_ref, wq_ref, wk_ref, ng_ref, *rest,
                  head_dim, batch, n_cast):
    cast_in, (out_ref, *cast_out), (st_ref, mx_ref, wqb_ref, wkb_ref) = (
        rest[:n_cast], rest[n_cast:2 * n_cast + 1], rest[2 * n_cast + 1:])
    c_ref, o_ref = co_ref.at[0], co_ref.at[1]

    for src, dst in zip(cast_in, cast_out):
        dst[...] = src[...].astype(BF16)

    @pl.when(pl.program_id(0) == 0)
    def _():
        wqb_ref[...] = wq_ref[...].astype(BF16)
        wkb_ref[...] = (wk_ref[...] * (head_dim ** -0.5)).astype(BF16)
        st_ref[...] = jnp.zeros(st_ref.shape, F32)
        mx_ref[...] = jnp.zeros(mx_ref.shape, F32)

    row = lax.broadcasted_iota(jnp.int32, (CHUNK, CHUNK), 0)
    col = lax.broadcasted_iota(jnp.int32, (CHUNK, CHUNK), 1)
    causal = col <= row
    tril = causal.astype(F32)
    triu = (row <= col).astype(F32)
    ones_rows = jnp.ones((LANES, CHUNK), BF16)
    nd = head_dim // LANES
    heads = [(b, h) for b in range(batch) for h in range(ML_HEADS)]
    slot = lambda b, h: b * ML_HEADS + h
    hsl = lambda h: slice(h * head_dim, (h + 1) * head_dim)


    g_col, g_row, bcum_col, bcum_row, total = {}, {}, {}, {}, {}
    for b in range(batch):
        g_col[b] = gcol_ref[b] + brow_ref[...]
        g_row[b] = grow_ref[b] + bcol_ref[...]
        lf_row = _log_sigmoid(g_row[b])
        bcum_col[b] = jnp.dot(tril, _log_sigmoid(g_col[b]), precision=lax.Precision.HIGHEST,
                              preferred_element_type=F32)
        bcum_row[b] = jnp.dot(lf_row, triu, precision=lax.Precision.HIGHEST, preferred_element_type=F32)
        total[b] = jnp.sum(lf_row, axis=-1, keepdims=True)

    q, k = {}, {}
    for b, h in heads:
        c = c_ref[b, :, hsl(h)]
        q[b, h] = _dot(c, wqb_ref[h])
        k[b, h] = _dot(c, wkb_ref[h])

    w_intra, w_inter, inv_floor, w_state, decay, scale, m_new = {}, {}, {}, {}, {}, {}, {}
    for b, h in heads:
        f = ML_HEADS + h
        bc = _lanes(bcum_col[b][:, f:f + 1])
        ig = _lanes(g_col[b][:, h:h + 1])
        src = g_row[b][h:h + 1, :] - bcum_row[b][f:f + 1, :]
        b_last = jnp.broadcast_to(total[b][f:f + 1, :], (1, LANES))
        m_prev = mx_ref[slot(b, h):slot(b, h) + 1, :]
        inter = bc + m_prev
        dlog = jnp.where(causal, bc + src, -jnp.inf)
        m_t = jnp.maximum(inter, _lanes(jnp.max(dlog, axis=-1, keepdims=True)))
        w_intra[b, h] = jnp.exp(dlog - m_t)
        w_inter[b, h] = jnp.exp(inter - m_t)
        inv_floor[b, h] = jnp.exp(-m_t)
        a = b_last - bc + ig
        a_max = jnp.max(a, axis=0, keepdims=True)
        w_state[b, h] = jnp.exp(a - a_max)
        m_new[b, h] = jnp.maximum(b_last + m_prev, a_max)
        decay[b, h] = jnp.exp(b_last + m_prev - m_new[b, h])
        scale[b, h] = jnp.exp(a_max - m_new[b, h])

    qk = {}
    for b, h in heads:
        qk[b, h] = _nt_dot(q[b, h].astype(BF16), k[b, h].astype(BF16)) * w_intra[b, h]

    for b, h in heads:
        vt_ext = jnp.concatenate([vt_ref[b, hsl(h), :], ones_rows], axis=0)
        lhs = jnp.concatenate([(q[b, h] * _wide(w_inter[b, h], nd)).astype(BF16), qk[b, h].astype(BF16)], axis=1)
        rhs = jnp.concatenate([st_ref[slot(b, h)].astype(BF16), vt_ext], axis=1)
        ext = _nt_dot(lhs, rhs)
        den = ext[:, head_dim:]
        hid = ext[:, :head_dim] * _wide(1.0 / jnp.maximum(jnp.abs(den), inv_floor[b, h]), nd)
        mu = jnp.mean(hid, axis=-1, keepdims=True)
        var = jnp.mean(jnp.square(hid - mu), axis=-1, keepdims=True)
        hid = (hid - mu) * lax.rsqrt(var + EPS) * ng_ref[:, hsl(h)]
        out_ref[b, :, hsl(h)] = (o_ref[b, :, hsl(h)].astype(F32) * hid).astype(BF16)

    for b, h in heads:
        s = slot(b, h)
        vt_ext = jnp.concatenate([vt_ref[b, hsl(h), :], ones_rows], axis=0)
        kw = k[b, h] * _wide(w_state[b, h] * scale[b, h], nd)
        upd = _dot(vt_ext, kw.astype(BF16))
        st_ref[s] = _wide(decay[b, h], nd) * st_ref[s] + upd
        mx_ref[s:s + 1, :] = m_new[b, h]


def _mlstm(proj, vt, gates_col, gates_row, bias_col, bias_row, wq, wk, norm_g, cast_ws, layer, batch):
    _, t, width = proj.shape
    seq = t // batch
    head_dim = width // ML_HEADS
    n_if = gates_row.shape[1]
    n_steps = seq // CHUNK
    assert PROJ_O == PROJ_M + 1 and PROJ_M % 2 == 0
    proj4 = proj.reshape(proj.shape[0], batch, seq, width)
    cast_rows = [_cast_rows(w.shape[1], n_steps) for w in cast_ws]
    kern = functools.partial(_mlstm_kernel, head_dim=head_dim, batch=batch, n_cast=len(cast_ws))
    const2 = lambda n: (0, 0)
    const3 = lambda n: (0, 0, 0)
    n_state = batch * ML_HEADS
    out, *cast = pl.pallas_call(
        kern,
        grid=(n_steps,),
        in_specs=[
            pl.BlockSpec((2, batch, CHUNK, width), lambda n: (PROJ_M // 2, 0, n, 0)),
            pl.BlockSpec((batch, width, CHUNK), lambda n: (0, 0, n)),
            pl.BlockSpec((batch, CHUNK, LANES), lambda n: (0, n, 0)),
            pl.BlockSpec((batch, n_if, CHUNK), lambda n: (0, 0, n)),
            pl.BlockSpec((n_if, 1), const2),
            pl.BlockSpec((1, LANES), const2),
            pl.BlockSpec((ML_HEADS, head_dim, head_dim), const3),
            pl.BlockSpec((ML_HEADS, head_dim, head_dim), const3),
            pl.BlockSpec((1, width), const2),
        ] + [pl.BlockSpec((None, r, w.shape[2]), lambda n: (layer, n, 0)) for r, w in zip(cast_rows, cast_ws)],
        out_specs=[pl.BlockSpec((batch, CHUNK, width), lambda n: (0, n, 0))]
        + [pl.BlockSpec((r, w.shape[2]), lambda n: (n, 0)) for r, w in zip(cast_rows, cast_ws)],
        out_shape=[jax.ShapeDtypeStruct((batch, seq, width), BF16)]
        + [jax.ShapeDtypeStruct(w.shape[1:], BF16) for w in cast_ws],
        scratch_shapes=[
            pltpu.VMEM((n_state, head_dim + LANES, head_dim), F32),
            pltpu.VMEM((n_state, LANES), F32),
            pltpu.VMEM((ML_HEADS, head_dim, head_dim), BF16),
            pltpu.VMEM((ML_HEADS, head_dim, head_dim), BF16),
        ],
        compiler_params=_params("arbitrary"),
        name="mlstm",
    )(proj4, vt, gates_col.reshape(batch, seq, LANES), gates_row, bias_col, bias_row, wq, wk, norm_g, *cast_ws)
    return out.reshape(t, width), cast


def _merge_kernel(x_ref, ya_ref, yb_ref, gate_ref, wa_ref, wb_ref, wo_ref, g2_ref, x1_ref, hn_ref):
    gate_a = jnp.concatenate([gate_ref[0], gate_ref[1]], axis=1).astype(F32)
    gate_b = jnp.concatenate([gate_ref[2], gate_ref[3]], axis=1).astype(F32)
    mixed = gate_a * _dot(ya_ref[...], wa_ref[...]) + gate_b * _dot(yb_ref[...], wb_ref[...])
    x1 = x_ref[...] + _dot(mixed.astype(BF16), wo_ref[...])
    x1_ref[...] = x1
    hn_ref[...] = _rms_norm(x1, g2_ref[...]).astype(BF16)


def _merge(x, y_a, y_b, proj, w_a, w_b, w_out, g2, tm):
    t, d = x.shape
    width = y_a.shape[1]
    n_steps = t // tm
    n_gate = N_PROJ - PROJ_GA
    assert PROJ_GA % n_gate == 0
    const = lambda i: (0, 0)
    resident = dict(pipeline_mode=pl.Buffered(1))
    return pl.pallas_call(
        _merge_kernel,
        grid=(n_steps,),
        in_specs=[
            pl.BlockSpec((tm, d), lambda i: (i, 0)),
            pl.BlockSpec((tm, width), lambda i: (i, 0)),
            pl.BlockSpec((tm, width), lambda i: (i, 0)),
            pl.BlockSpec((n_gate, tm, SECTION), lambda i: (PROJ_GA // n_gate, i, 0)),
            pl.BlockSpec((width, d), const, **resident),
            pl.BlockSpec((width, d), const, **resident),
            pl.BlockSpec((d, d), const, **resident),
            pl.BlockSpec((1, d), const),
        ],
        out_specs=[pl.BlockSpec((tm, d), lambda i: (i, 0)), pl.BlockSpec((tm, d), lambda i: (i, 0))],
        out_shape=[jax.ShapeDtypeStruct((t, d), F32), jax.ShapeDtypeStruct((t, d), BF16)],
        compiler_params=_params("arbitrary"),
        name="merge",
    )(x, y_a, y_b, proj, w_a, w_b, w_out, g2)


def _ffn_kernel(hn_ref, x1_ref, w1_ref, w2_ref, gf_ref, out_ref, *, final_norm):
    j = pl.program_id(1)

    @pl.when(j == 0)
    def _():
        out_ref[...] = x1_ref[...]

    h = jnp.square(jnp.maximum(_dot(hn_ref[...], w1_ref[...]), 0.0)).astype(BF16)
    out_ref[...] += _dot(h, w2_ref[...])

    if final_norm:
        @pl.when(j == pl.num_programs(1) - 1)
        def _():
            out_ref[...] = _rms_norm(out_ref[...], gf_ref[...])


def _ffn(hn, x1, w1, w2, gf, tm, tf, final_norm):
    t, d = x1.shape
    d_ff = w1.shape[1]
    return pl.pallas_call(
        functools.partial(_ffn_kernel, final_norm=final_norm),
        grid=(t // tm, d_ff // tf),
        in_specs=[
            pl.BlockSpec((tm, d), lambda i, j: (i, 0)),
            pl.BlockSpec((tm, d), lambda i, j: (i, 0)),
            pl.BlockSpec((d, tf), lambda i, j: (0, j)),
            pl.BlockSpec((tf, d), lambda i, j: (j, 0)),
            pl.BlockSpec((1, d), lambda i, j: (0, 0)),
        ],
        out_specs=pl.BlockSpec((tm, d), lambda i, j: (i, 0)),
        out_shape=jax.ShapeDtypeStruct((t, d), F32),
        compiler_params=_params("arbitrary", "arbitrary"),
        name="ffn",
    )(hn, x1, w1, w2, gf)


def _tiles(t):
    return dict(prologue=min(t, 1024), inproj=min(t, 1024), gmlp=min(t, 1024), merge=min(t, 512),
                ffn_m=min(t, 512), ffn_f=2048)


def kernel(x, norm1_g, w_in, b_gate, gm_ln_g, gm_ln_b, gm_ws, gm_bs, ml_conv_w, ml_conv_b, ml_wq, ml_wk,
           ml_ig_b, ml_fg_b, ml_norm_g, w_a, w_b, w_out, norm2_g, w_ff1, w_ff2, norm_f_g):
    batch, seq, d = x.shape
    depth = w_in.shape[0]
    t = batch * seq
    tiles = _tiles(t)
    if_row = SEC_GA * SECTION
    n_if = 2 * ML_HEADS
    xt = x.reshape(t, d)
    w_t = jnp.swapaxes(w_in, 1, 2)
    for l in range(depth):
        gate_bias = jnp.concatenate([ml_ig_b[l], ml_fg_b[l]])
        vecs = jnp.concatenate([
            gm_ln_g[l][None], gm_ln_b[l][None], ml_conv_b[l][None], ml_conv_w[l],
            jnp.zeros((1, SECTION), F32), b_gate[l].reshape(4, SECTION),
            jnp.zeros((N_VEC_ROWS - VEC_GATE_B - 5, SECTION), F32)])
        xn, gates_col, gates_row = _prologue(xt, norm1_g[l][None], w_t, l, if_row, n_if, batch,
                                             tiles["prologue"])
        proj, vt, w2_bf = _inproj(xn, w_t, l, if_row + n_if, vecs, w_ff2, batch, tiles["inproj"])
        y_a = _gmlp(proj, gm_ws[l], gm_bs[l].T, tiles["gmlp"])
        y_b, (wa_bf, wb_bf, wo_bf, w1_bf) = _mlstm(
            proj, vt, gates_col, gates_row, gate_bias[:, None],
            jnp.pad(gate_bias, (0, LANES - n_if))[None], ml_wq[l], ml_wk[l], ml_norm_g[l][None],
            (w_a, w_b, w_out, w_ff1), l, batch)
        x1, hn = _merge(xt, y_a, y_b, proj, wa_bf, wb_bf, wo_bf, norm2_g[l][None], tiles["merge"])
        xt = _ffn(hn, x1, w1_bf, w2_bf, norm_f_g[None], tiles["ffn_m"], tiles["ffn_f"],
                  final_norm=l == depth - 1)
    return xt.reshape(batch, seq, d)
```

```python
import functools

import jax
import jax.numpy as jnp
from jax import lax
from jax.experimental import pallas as pl
from jax.experimental.pallas import tpu as pltpu

EPS = 1e-6
GM_GROUP_DIM = 128
CHUNK = 128
ML_HEADS = 4
CONV_K = 4
LANES = 128
SUBLANES = 8
BF16_ROWS = 16
SECTION = 1024
SEC_U, SEC_V, SEC_M, SEC_MV, SEC_O, SEC_GA, SEC_GB = 0, 1, 2, 3, 4, 5, 7
N_SECTIONS = 9
PROJ_U, PROJ_V, PROJ_M, PROJ_O, PROJ_GA, PROJ_GB = 0, 1, 2, 3, 4, 6
N_PROJ = 8
VMEM_LIMIT = 56 * 1024 * 1024

BF16 = jnp.bfloat16
F32 = jnp.float32


def _sigmoid(x):
    return 0.5 * jnp.tanh(0.5 * x) + 0.5


def _log_sigmoid(x):
    return jnp.minimum(x, 0.0) - jnp.log(1.0 + jnp.exp(-jnp.abs(x)))


def _rms_norm(x, g):
    return x * lax.rsqrt(jnp.mean(x * x, axis=-1, keepdims=True) + EPS) * g


def _dot(a, b):
    return jnp.dot(a, b, preferred_element_type=F32)


def _params(*semantics):
    return pltpu.CompilerParams(dimension_semantics=semantics, vmem_limit_bytes=VMEM_LIMIT)


def _cast_rows(n_rows, n_steps):
    rows = n_rows // n_steps
    assert rows * n_steps == n_rows and rows % BF16_ROWS == 0, (n_rows, n_steps)
    return rows


def _nt_dot(a, b):
    return lax.dot_general(a, b, (((1,), (1,)), ((), ())), preferred_element_type=F32)


def _prologue_kernel(x_ref, g1_ref, wif_ref, xn_ref, gcol_ref, grow_ref):
    xn = _rms_norm(x_ref[...], g1_ref[...]).astype(BF16)
    xn_ref[...] = xn
    wif = wif_ref[...].astype(BF16)
    gcol_ref[...] = _nt_dot(xn, wif)
    grow_ref[...] = _nt_dot(wif, xn)[:grow_ref.shape[0], :]


def _prologue(x, g1, w_t, layer, if_row, n_if, batch, tm):
    t, d = x.shape
    seq = t // batch
    tiles_per_seq = seq // tm
    assert tiles_per_seq * tm == seq
    return pl.pallas_call(
        _prologue_kernel,
        grid=(t // tm,),
        in_specs=[
            pl.BlockSpec((tm, d), lambda i: (i, 0)),
            pl.BlockSpec((1, d), lambda i: (0, 0)),
            pl.BlockSpec((None, LANES, d), lambda i: (layer, if_row // LANES, 0)),
        ],
        out_specs=[
            pl.BlockSpec((tm, d), lambda i: (i, 0)),
            pl.BlockSpec((tm, LANES), lambda i: (i, 0)),
            pl.BlockSpec((None, n_if, tm), lambda i: (i // tiles_per_seq, 0, i % tiles_per_seq)),
        ],
        out_shape=[
            jax.ShapeDtypeStruct((t, d), BF16),
            jax.ShapeDtypeStruct((t, LANES), F32),
            jax.ShapeDtypeStruct((batch, n_if, seq), F32),
        ],
        compiler_params=_params("arbitrary"),
        name="prologue",
    )(x, g1, w_t)


VEC_LN_G, VEC_LN_B, VEC_CONV_B, VEC_CONV_W, VEC_GATE_B = 0, 1, 2, 3, 7
N_VEC_ROWS = 16


def _inproj_kernel(xn_ref, w_ref, vec_ref, cast_ref, out_ref, vt_out, cast_out, wbf_ref, tail_ref,
                   *, tiles_per_seq, n_cast_steps):
    j = pl.program_id(0)
    i = pl.program_id(1)
    tm = xn_ref.shape[0]
    vec = lambda r, n=1: vec_ref[pl.ds(r, n), :]

    @pl.when(j * pl.num_programs(1) + i < n_cast_steps)
    def _():
        cast_out[...] = cast_ref[...].astype(BF16)

    @pl.when(i == 0)
    def _():
        rows = 256
        for r in range(0, wbf_ref.shape[0], rows):
            wbf_ref[r:r + rows, :] = w_ref[r:r + rows, :].astype(BF16)

    hm, hn = tm // 2, SECTION // 2
    rows = lambda r: slice(r * hm, (r + 1) * hm)
    cols = lambda c: slice(c * hn, (c + 1) * hn)

    def proj_rows(r):
        return jnp.concatenate([_nt_dot(xn_ref[rows(r), :], wbf_ref[cols(c), :]) for c in range(2)], axis=1)

    @pl.when(j == SEC_U)
    def _():
        for r in range(2):
            out_ref[rows(r), :] = jax.nn.gelu(proj_rows(r)).astype(BF16)

    @pl.when(j == SEC_V)
    def _():
        for r in range(2):
            v = jax.nn.gelu(proj_rows(r))
            mu = jnp.mean(v, axis=-1, keepdims=True)
            var = jnp.mean(jnp.square(v - mu), axis=-1, keepdims=True)
            out_ref[rows(r), :] = ((v - mu) * lax.rsqrt(var + EPS) * vec(VEC_LN_G) + vec(VEC_LN_B)).astype(BF16)

    @pl.when(j == SEC_M)
    def _():
        tail = jnp.where(i % tiles_per_seq == 0, 0.0, tail_ref[...])
        for r in range(2):
            x = proj_rows(r)
            xe = jnp.concatenate([tail, x], axis=0)
            conv = vec(VEC_CONV_B) + vec(VEC_CONV_W + CONV_K - 1) * x
            for d in range(1, CONV_K):
                conv = conv + vec(VEC_CONV_W + CONV_K - 1 - d) * xe[SUBLANES - d:SUBLANES - d + hm, :]
            tail = x[hm - SUBLANES:, :]
            out_ref[rows(r), :] = (conv * _sigmoid(conv)).astype(BF16)
        tail_ref[...] = tail

    @pl.when(j == SEC_MV)
    def _():
        for r in range(2):
            for c in range(2):
                vt_out[cols(c), rows(r)] = _nt_dot(wbf_ref[cols(c), :], xn_ref[rows(r), :]).astype(BF16)

    @pl.when(j >= SEC_O)
    def _():
        bias = vec(VEC_GATE_B + j - SEC_O)
        for r in range(2):
            out_ref[rows(r), :] = _sigmoid(proj_rows(r) + bias).astype(BF16)


def _inproj(xn, w_t, layer, gate_row, vecs, cast_w, batch, tm):
    t, d = xn.shape
    n_i = t // tm
    seq = t // batch
    tiles_per_seq = seq // tm
    assert tiles_per_seq * tm == seq
    n_cast_steps = (N_SECTIONS - 1) * n_i
    cast_rows = _cast_rows(cast_w.shape[1], n_cast_steps)
    cast_step = lambda j, i: jnp.minimum(j * n_i + i, n_cast_steps - 1)

    def proj_block(j, i):
        mv = j == SEC_MV
        return (j - (j >= SEC_MV).astype(jnp.int32), jnp.where(mv, n_i - 1, i), 0)

    def vt_block(j, i):
        tile = jnp.clip((j - SEC_MV) * n_i + i, 0, n_i - 1)
        return (tile // tiles_per_seq, 0, tile % tiles_per_seq)

    gate_skip = gate_row - SEC_GA * SECTION
    assert gate_skip % SUBLANES == 0
    kern = functools.partial(_inproj_kernel, tiles_per_seq=tiles_per_seq, n_cast_steps=n_cast_steps)
    return pl.pallas_call(
        kern,
        grid=(N_SECTIONS, n_i),
        in_specs=[
            pl.BlockSpec((tm, d), lambda j, i: (i, 0)),
            pl.BlockSpec((None, pl.Element(SECTION), pl.Element(d)),
                         lambda j, i: (layer, SUBLANES * (j * (SECTION // SUBLANES) + jnp.where(
                             j >= SEC_GA, gate_skip // SUBLANES, 0)), 0)),
            pl.BlockSpec((N_VEC_ROWS, SECTION), lambda j, i: (0, 0)),
            pl.BlockSpec((None, cast_rows, cast_w.shape[2]), lambda j, i: (layer, cast_step(j, i), 0)),
        ],
        out_specs=[pl.BlockSpec((None, tm, SECTION), proj_block), pl.BlockSpec((None, SECTION, tm), vt_block),
                   pl.BlockSpec((cast_rows, cast_w.shape[2]), lambda j, i: (cast_step(j, i), 0))],
        out_shape=[jax.ShapeDtypeStruct((N_PROJ, t, SECTION), BF16),
                   jax.ShapeDtypeStruct((batch, SECTION, seq), BF16),
                   jax.ShapeDtypeStruct(cast_w.shape[1:], BF16)],
        scratch_shapes=[
            pltpu.VMEM((SECTION, d), BF16),
            pltpu.VMEM((SUBLANES, SECTION), F32),
        ],
        compiler_params=_params("arbitrary", "arbitrary"),
        name="inproj",
    )(xn, w_t, vecs, cast_w)


def _gmlp_kernel(u_ref, v_ref, ws_ref, bs_ref, out_ref, *, n_chunks, n_groups):
    row = lax.broadcasted_iota(jnp.int32, (CHUNK, CHUNK), 0)
    col = lax.broadcasted_iota(jnp.int32, (CHUNK, CHUNK), 1)
    causal = col <= row
    for g in range(n_groups):
        gs = slice(g * GM_GROUP_DIM, (g + 1) * GM_GROUP_DIM)
        w = jnp.where(causal, ws_ref[g], 0.0).astype(BF16)
        vg = jnp.concatenate([v_ref[c * CHUNK:(c + 1) * CHUNK, gs] for c in range(n_chunks)], axis=1)
        s = _dot(w, vg) + bs_ref[:, g:g + 1]
        for c in range(n_chunks):
            rs = slice(c * CHUNK, (c + 1) * CHUNK)
            sc = s[:, c * GM_GROUP_DIM:(c + 1) * GM_GROUP_DIM]
            out_ref[rs, gs] = (u_ref[rs, gs].astype(F32) * sc).astype(BF16)


def _gmlp(proj, ws, bs_t, tm):
    _, t, width = proj.shape
    n_groups = width // GM_GROUP_DIM
    kern = functools.partial(_gmlp_kernel, n_chunks=tm // CHUNK, n_groups=n_groups)
    return pl.pallas_call(
        kern,
        grid=(t // tm,),
        in_specs=[
            pl.BlockSpec((None, tm, width), lambda i: (PROJ_U, i, 0)),
            pl.BlockSpec((None, tm, width), lambda i: (PROJ_V, i, 0)),
            pl.BlockSpec((n_groups, CHUNK, CHUNK), lambda i: (0, 0, 0)),
            pl.BlockSpec((CHUNK, n_groups), lambda i: (0, 0)),
        ],
        out_specs=pl.BlockSpec((tm, width), lambda i: (i, 0)),
        out_shape=jax.ShapeDtypeStruct((t, width), BF16),
        compiler_params=_params("arbitrary"),
        name="gmlp",
    )(proj, proj, ws, bs_t)


def _lanes(col):
    return jnp.broadcast_to(col, (col.shape[0], LANES))


def _wide(x, n):
    return jnp.concatenate([x] * n, axis=1)


def _mlstm_kernel(co_ref, vt_ref, gcol_ref, grow_ref, bcol_ref, brow_ref, wq_ref, wk_ref, ng_ref, *rest,
                  head_dim, batch, n_cast):
    cast_in, (out_ref, *cast_out), (st_ref, mx_ref, wqb_ref, wkb_ref) = (
        rest[:n_cast], rest[n_cast:2 * n_cast + 1], rest[2 * n_cast + 1:])
    c_ref, o_ref = co_ref.at[0], co_ref.at[1]

    for src, dst in zip(cast_in, cast_out):
        dst[...] = src[...].astype(BF16)

    @pl.when(pl.program_id(0) == 0)
    def _():
        wqb_ref[...] = wq_ref[...].astype(BF16)
        wkb_ref[...] = (wk_ref[...] * (head_dim ** -0.5)).astype(BF16)
        st_ref[...] = jnp.zeros(st_ref.shape, F32)
        mx_ref[...] = jnp.zeros(mx_ref.shape, F32)

    row = lax.broadcasted_iota(jnp.int32, (CHUNK, CHUNK), 0)
    col = lax.broadcasted_iota(jnp.int32, (CHUNK, CHUNK), 1)
    causal = col <= row
    tril = causal.astype(F32)
    triu = (row <= col).astype(F32)
    ones_rows = jnp.ones((LANES, CHUNK), BF16)
    nd = head_dim // LANES
    heads = [(b, h) for b in range(batch) for h in range(ML_HEADS)]
    slot = lambda b, h: b * ML_HEADS + h
    hsl = lambda h: slice(h * head_dim, (h + 1) * head_dim)


    g_col, g_row, bcum_col, bcum_row, total = {}, {}, {}, {}, {}
    for b in range(batch):
        g_col[b] = gcol_ref[b] + brow_ref[...]
        g_row[b] = grow_ref[b] + bcol_ref[...]
        lf_row = _log_sigmoid(g_row[b])
        bcum_col[b] = jnp.dot(tril, _log_sigmoid(g_col[b]), precision=lax.Precision.HIGHEST,
                              preferred_element_type=F32)
        bcum_row[b] = jnp.dot(lf_row, triu, precision=lax.Precision.HIGHEST, preferred_element_type=F32)
        total[b] = jnp.sum(lf_row, axis=-1, keepdims=True)

    q, k = {}, {}
    for b, h in heads:
        c = c_ref[b, :, hsl(h)]
        q[b, h] = _dot(c, wqb_ref[h])
        k[b, h] = _dot(c, wkb_ref[h])

    w_intra, w_inter, inv_floor, w_state, decay, scale, m_new = {}, {}, {}, {}, {}, {}, {}
    for b, h in heads:
        f = ML_HEADS + h
        bc = _lanes(bcum_col[b][:, f:f + 1])
        ig = _lanes(g_col[b][:, h:h + 1])
        src = g_row[b][h:h + 1, :] - bcum_row[b][f:f + 1, :]
        b_last = jnp.broadcast_to(total[b][f:f + 1, :], (1, LANES))
        m_prev = mx_ref[slot(b, h):slot(b, h) + 1, :]
        inter = bc + m_prev
        dlog = jnp.where(causal, bc + src, -jnp.inf)
        m_t = jnp.maximum(inter, _lanes(jnp.max(dlog, axis=-1, keepdims=True)))
        w_intra[b, h] = jnp.exp(dlog - m_t)
        w_inter[b, h] = jnp.exp(inter - m_t)
        inv_floor[b, h] = jnp.exp(-m_t)
        a = b_last - bc + ig
        a_max = jnp.max(a, axis=0, keepdims=True)
        w_state[b, h] = jnp.exp(a - a_max)
        m_new[b, h] = jnp.maximum(b_last + m_prev, a_max)
        decay[b, h] = jnp.exp(b_last + m_prev - m_new[b, h])
        scale[b, h] = jnp.exp(a_max - m_new[b, h])

    qk = {}
    for b, h in heads:
        qk[b, h] = _nt_dot(q[b, h].astype(BF16), k[b, h].astype(BF16)) * w_intra[b, h]

    for b, h in heads:
        vt_ext = jnp.concatenate([vt_ref[b, hsl(h), :], ones_rows], axis=0)
        lhs = jnp.concatenate([(q[b, h] * _wide(w_inter[b, h], nd)).astype(BF16), qk[b, h].astype(BF16)], axis=1)
        rhs = jnp.concatenate([st_ref[slot(b, h)].astype(BF16), vt_ext], axis=1)
        ext = _nt_dot(lhs, rhs)
        den = ext[:, head_dim:]
        hid = ext[:, :head_dim] * _wide(1.0 / jnp.maximum(jnp.abs(den), inv_floor[b, h]), nd)
        mu = jnp.mean(hid, axis=-1, keepdims=True)
        var = jnp.mean(jnp.square(hid - mu), axis=-1, keepdims=True)
        hid = (hid - mu) * lax.rsqrt(var + EPS) * ng_ref[:, hsl(h)]
        out_ref[b, :, hsl(h)] = (o_ref[b, :, hsl(h)].astype(F32) * hid).astype(BF16)

    for b, h in heads:
        s = slot(b, h)
        vt_ext = jnp.concatenate([vt_ref[b, hsl(h), :], ones_rows], axis=0)
        kw = k[b, h] * _wide(w_state[b, h] * scale[b, h], nd)
        upd = _dot(vt_ext, kw.astype(BF16))
        st_ref[s] = _wide(decay[b, h], nd) * st_ref[s] + upd
        mx_ref[s:s + 1, :] = m_new[b, h]


def _mlstm(proj, vt, gates_col, gates_row, bias_col, bias_row, wq, wk, norm_g, cast_ws, layer, batch):
    _, t, width = proj.shape
    seq = t // batch
    head_dim = width // ML_HEADS
    n_if = gates_row.shape[1]
    n_steps = seq // CHUNK
    assert PROJ_O == PROJ_M + 1 and PROJ_M % 2 == 0
    proj4 = proj.reshape(proj.shape[0], batch, seq, width)
    cast_rows = [_cast_rows(w.shape[1], n_steps) for w in cast_ws]
    kern = functools.partial(_mlstm_kernel, head_dim=head_dim, batch=batch, n_cast=len(cast_ws))
    const2 = lambda n: (0, 0)
    const3 = lambda n: (0, 0, 0)
    n_state = batch * ML_HEADS
    out, *cast = pl.pallas_call(
        kern,
        grid=(n_steps,),
        in_specs=[
            pl.BlockSpec((2, batch, CHUNK, width), lambda n: (PROJ_M // 2, 0, n, 0)),
            pl.BlockSpec((batch, width, CHUNK), lambda n: (0, 0, n)),
            pl.BlockSpec((batch, CHUNK, LANES), lambda n: (0, n, 0)),
            pl.BlockSpec((batch, n_if, CHUNK), lambda n: (0, 0, n)),
            pl.BlockSpec((n_if, 1), const2),
            pl.BlockSpec((1, LANES), const2),
            pl.BlockSpec((ML_HEADS, head_dim, head_dim), const3),
            pl.BlockSpec((ML_HEADS, head_dim, head_dim), const3),
            pl.BlockSpec((1, width), const2),
        ] + [pl.BlockSpec((None, r, w.shape[2]), lambda n: (layer, n, 0)) for r, w in zip(cast_rows, cast_ws)],
        out_specs=[pl.BlockSpec((batch, CHUNK, width), lambda n: (0, n, 0))]
        + [pl.BlockSpec((r, w.shape[2]), lambda n: (n, 0)) for r, w in zip(cast_rows, cast_ws)],
        out_shape=[jax.ShapeDtypeStruct((batch, seq, width), BF16)]
        + [jax.ShapeDtypeStruct(w.shape[1:], BF16) for w in cast_ws],
        scratch_shapes=[
            pltpu.VMEM((n_state, head_dim + LANES, head_dim), F32),
            pltpu.VMEM((n_state, LANES), F32),
            pltpu.VMEM((ML_HEADS, head_dim, head_dim), BF16),
            pltpu.VMEM((ML_HEADS, head_dim, head_dim), BF16),
        ],
        compiler_params=_params("arbitrary"),
        name="mlstm",
    )(proj4, vt, gates_col.reshape(batch, seq, LANES), gates_row, bias_col, bias_row, wq, wk, norm_g, *cast_ws)
    return out.reshape(t, width), cast


def _merge_kernel(x_ref, ya_ref, yb_ref, gate_ref, wa_ref, wb_ref, wo_ref, g2_ref, x1_ref, hn_ref):
    gate_a = jnp.concatenate([gate_ref[0], gate_ref[1]], axis=1).astype(F32)
    gate_b = jnp.concatenate([gate_ref[2], gate_ref[3]], axis=1).astype(F32)
    mixed = gate_a * _dot(ya_ref[...], wa_ref[...]) + gate_b * _dot(yb_ref[...], wb_ref[...])
    x1 = x_ref[...] + _dot(mixed.astype(BF16), wo_ref[...])
    x1_ref[...] = x1
    hn_ref[...] = _rms_norm(x1, g2_ref[...]).astype(BF16)


def _merge(x, y_a, y_b, proj, w_a, w_b, w_out, g2, tm):
    t, d = x.shape
    width = y_a.shape[1]
    n_steps = t // tm
    n_gate = N_PROJ - PROJ_GA
    assert PROJ_GA % n_gate == 0
    const = lambda i: (0, 0)
    resident = dict(pipeline_mode=pl.Buffered(1))
    return pl.pallas_call(
        _merge_kernel,
        grid=(n_steps,),
        in_specs=[
            pl.BlockSpec((tm, d), lambda i: (i, 0)),
            pl.BlockSpec((tm, width), lambda i: (i, 0)),
            pl.BlockSpec((tm, width), lambda i: (i, 0)),
            pl.BlockSpec((n_gate, tm, SECTION), lambda i: (PROJ_GA // n_gate, i, 0)),
            pl.BlockSpec((width, d), const, **resident),
            pl.BlockSpec((width, d), const, **resident),
            pl.BlockSpec((d, d), const, **resident),
            pl.BlockSpec((1, d), const),
        ],
        out_specs=[pl.BlockSpec((tm, d), lambda i: (i, 0)), pl.BlockSpec((tm, d), lambda i: (i, 0))],
        out_shape=[jax.ShapeDtypeStruct((t, d), F32), jax.ShapeDtypeStruct((t, d), BF16)],
        compiler_params=_params("arbitrary"),
        name="merge",
    )(x, y_a, y_b, proj, w_a, w_b, w_out, g2)


def _ffn_kernel(hn_ref, x1_ref, w1_ref, w2_ref, gf_ref, out_ref, *, final_norm):
    j = pl.program_id(1)

    @pl.when(j == 0)
    def _():
        out_ref[...] = x1_ref[...]

    h = jnp.square(jnp.maximum(_dot(hn_ref[...], w1_ref[...]), 0.0)).astype(BF16)
    out_ref[...] += _dot(h, w2_ref[...])

    if final_norm:
        @pl.when(j == pl.num_programs(1) - 1)
        def _():
            out_ref[...] = _rms_norm(out_ref[...], gf_ref[...])


def _ffn(hn, x1, w1, w2, gf, tm, tf, final_norm):
    t, d = x1.shape
    d_ff = w1.shape[1]
    return pl.pallas_call(
        functools.partial(_ffn_kernel, final_norm=final_norm),
        grid=(t // tm, d_ff // tf),
        in_specs=[
            pl.BlockSpec((tm, d), lambda i, j: (i, 0)),
            pl.BlockSpec((tm, d), lambda i, j: (i, 0)),
            pl.BlockSpec((d, tf), lambda i, j: (0, j)),
            pl.BlockSpec((tf, d), lambda i, j: (j, 0)),
            pl.BlockSpec((1, d), lambda i, j: (0, 0)),
        ],
        out_specs=pl.BlockSpec((tm, d), lambda i, j: (i, 0)),
        out_shape=jax.ShapeDtypeStruct((t, d), F32),
        compiler_params=_params("arbitrary", "arbitrary"),
        name="ffn",
    )(hn, x1, w1, w2, gf)


def _tiles(t):
    return dict(prologue=min(t, 1024), inproj=min(t, 1024), gmlp=min(t, 1024), merge=min(t, 512),
                ffn_m=min(t, 512), ffn_f=2048)


def kernel(x, norm1_g, w_in, b_gate, gm_ln_g, gm_ln_b, gm_ws, gm_bs, ml_conv_w, ml_conv_b, ml_wq, ml_wk,
           ml_ig_b, ml_fg_b, ml_norm_g, w_a, w_b, w_out, norm2_g, w_ff1, w_ff2, norm_f_g):
    batch, seq, d = x.shape
    depth = w_in.shape[0]
    t = batch * seq
    tiles = _tiles(t)
    if_row = SEC_GA * SECTION
    n_if = 2 * ML_HEADS
    xt = x.reshape(t, d)
    w_t = jnp.swapaxes(w_in, 1, 2)
    for l in range(depth):
        gate_bias = jnp.concatenate([ml_ig_b[l], ml_fg_b[l]])
        vecs = jnp.concatenate([
            gm_ln_g[l][None], gm_ln_b[l][None], ml_conv_b[l][None], ml_conv_w[l],
            jnp.zeros((1, SECTION), F32), b_gate[l].reshape(4, SECTION),
            jnp.zeros((N_VEC_ROWS - VEC_GATE_B - 5, SECTION), F32)])
        xn, gates_col, gates_row = _prologue(xt, norm1_g[l][None], w_t, l, if_row, n_if, batch,
                                             tiles["prologue"])
        proj, vt, w2_bf = _inproj(xn, w_t, l, if_row + n_if, vecs, w_ff2, batch, tiles["inproj"])
        y_a = _gmlp(proj, gm_ws[l], gm_bs[l].T, tiles["gmlp"])
        y_b, (wa_bf, wb_bf, wo_bf, w1_bf) = _mlstm(
            proj, vt, gates_col, gates_row, gate_bias[:, None],
            jnp.pad(gate_bias, (0, LANES - n_if))[None], ml_wq[l], ml_wk[l], ml_norm_g[l][None],
            (w_a, w_b, w_out, w_ff1), l, batch)
        x1, hn = _merge(xt, y_a, y_b, proj, wa_bf, wb_bf, wo_bf, norm2_g[l][None], tiles["merge"])
        xt = _ffn(hn, x1, w1_bf, w2_bf, norm_f_g[None], tiles["ffn_m"], tiles["ffn_f"],
                  final_norm=l == depth - 1)
    return xt.reshape(batch, seq, d)
```

```python
import functools

import jax
import jax.numpy as jnp
from jax import lax
from jax.experimental import pallas as pl
from jax.experimental.pallas import tpu as pltpu

EPS = 1e-6
GM_GROUP_DIM = 128
CHUNK = 128
ML_HEADS = 4
CONV_K = 4
LANES = 128
SUBLANES = 8
BF16_ROWS = 16
SECTION = 1024
SEC_U, SEC_V, SEC_M, SEC_MV, SEC_O, SEC_GA, SEC_GB = 0, 1, 2, 3, 4, 5, 7
N_SECTIONS = 9
PROJ_U, PROJ_V, PROJ_M, PROJ_O, PROJ_GA, PROJ_GB = 0, 1, 2, 3, 4, 6
N_PROJ = 8
VMEM_LIMIT = 56 * 1024 * 1024

BF16 = jnp.bfloat16
F32 = jnp.float32


def _sigmoid(x):
    return 0.5 * jnp.tanh(0.5 * x) + 0.5


def _log_sigmoid(x):
    return jnp.minimum(x, 0.0) - jnp.log(1.0 + jnp.exp(-jnp.abs(x)))


def _rms_norm(x, g):
    return x * lax.rsqrt(jnp.mean(x * x, axis=-1, keepdims=True) + EPS) * g


def _dot(a, b):
    return jnp.dot(a, b, preferred_element_type=F32)


def _params(*semantics):
    return pltpu.CompilerParams(dimension_semantics=semantics, vmem_limit_bytes=VMEM_LIMIT)


def _cast_rows(n_rows, n_steps):
    rows = n_rows // n_steps
    assert rows * n_steps == n_rows and rows % BF16_ROWS == 0, (n_rows, n_steps)
    return rows


def _nt_dot(a, b):
    return lax.dot_general(a, b, (((1,), (1,)), ((), ())), preferred_element_type=F32)


def _prologue_kernel(x_ref, g1_ref, wif_ref, xn_ref, gcol_ref, grow_ref):
    xn = _rms_norm(x_ref[...], g1_ref[...]).astype(BF16)
    xn_ref[...] = xn
    wif = wif_ref[...].astype(BF16)
    gcol_ref[...] = _nt_dot(xn, wif)
    grow_ref[...] = _nt_dot(wif, xn)[:grow_ref.shape[0], :]


def _prologue(x, g1, w_t, layer, if_row, n_if, batch, tm):
    t, d = x.shape
    seq = t // batch
    tiles_per_seq = seq // tm
    assert tiles_per_seq * tm == seq
    return pl.pallas_call(
        _prologue_kernel,
        grid=(t // tm,),
        in_specs=[
            pl.BlockSpec((tm, d), lambda i: (i, 0)),
            pl.BlockSpec((1, d), lambda i: (0, 0)),
            pl.BlockSpec((None, LANES, d), lambda i: (layer, if_row // LANES, 0)),
        ],
        out_specs=[
            pl.BlockSpec((tm, d), lambda i: (i, 0)),
            pl.BlockSpec((tm, LANES), lambda i: (i, 0)),
            pl.BlockSpec((None, n_if, tm), lambda i: (i // tiles_per_seq, 0, i % tiles_per_seq)),
        ],
        out_shape=[
            jax.ShapeDtypeStruct((t, d), BF16),
            jax.ShapeDtypeStruct((t, LANES), F32),
            jax.ShapeDtypeStruct((batch, n_if, seq), F32),
        ],
        compiler_params=_params("arbitrary"),
        name="prologue",
    )(x, g1, w_t)


VEC_LN_G, VEC_LN_B, VEC_CONV_B, VEC_CONV_W, VEC_GATE_B = 0, 1, 2, 3, 7
N_VEC_ROWS = 16


def _inproj_kernel(xn_ref, w_ref, vec_ref, cast_ref, out_ref, vt_out, cast_out, tail_ref,
                   *, tiles_per_seq, n_cast_steps):
    j = pl.program_id(0)
    i = pl.program_id(1)
    tm = xn_ref.shape[0]
    vec = lambda r, n=1: vec_ref[pl.ds(r, n), :]

    @pl.when(j * pl.num_programs(1) + i < n_cast_steps)
    def _():
        cast_out[...] = cast_ref[...].astype(BF16)

    def proj():
        return _nt_dot(xn_ref[...], w_ref[...].astype(BF16))

    @pl.when(j == SEC_U)
    def _():
        out_ref[...] = jax.nn.gelu(proj()).astype(BF16)

    @pl.when(j == SEC_V)
    def _():
        v = jax.nn.gelu(proj())
        mu = jnp.mean(v, axis=-1, keepdims=True)
        var = jnp.mean(jnp.square(v - mu), axis=-1, keepdims=True)
        out_ref[...] = ((v - mu) * lax.rsqrt(var + EPS) * vec(VEC_LN_G) + vec(VEC_LN_B)).astype(BF16)

    @pl.when(j == SEC_M)
    def _():
        @pl.when(i % tiles_per_seq == 0)
        def _():
            tail_ref[...] = jnp.zeros(tail_ref.shape, F32)

        x = proj()
        xe = jnp.concatenate([tail_ref[...], x], axis=0)
        conv = vec(VEC_CONV_B) + vec(VEC_CONV_W + CONV_K - 1) * x
        for d in range(1, CONV_K):
            conv = conv + vec(VEC_CONV_W + CONV_K - 1 - d) * xe[SUBLANES - d:SUBLANES - d + tm, :]
        tail_ref[...] = x[tm - SUBLANES:, :]
        out_ref[...] = (conv * _sigmoid(conv)).astype(BF16)

    @pl.when(j == SEC_MV)
    def _():
        vt_out[...] = _nt_dot(w_ref[...].astype(BF16), xn_ref[...]).astype(BF16)

    @pl.when(j == SEC_O)
    def _():
        out_ref[...] = _sigmoid(proj()).astype(BF16)

    @pl.when(j >= SEC_GA)
    def _():
        out_ref[...] = (proj() + vec(VEC_GATE_B + j - SEC_O)).astype(BF16)


def _inproj(xn, w_t, layer, gate_row, vecs, cast_w, batch, tm):
    t, d = xn.shape
    n_i = t // tm
    seq = t // batch
    tiles_per_seq = seq // tm
    assert tiles_per_seq * tm == seq
    n_cast_steps = (N_SECTIONS - 1) * n_i
    cast_rows = _cast_rows(cast_w.shape[1], n_cast_steps)
    cast_step = lambda j, i: jnp.minimum(j * n_i + i, n_cast_steps - 1)

    def proj_block(j, i):
        mv = j == SEC_MV
        return (j - (j >= SEC_MV).astype(jnp.int32), jnp.where(mv, n_i - 1, i), 0)

    def vt_block(j, i):
        tile = jnp.clip((j - SEC_MV) * n_i + i, 0, n_i - 1)
        return (tile // tiles_per_seq, 0, tile % tiles_per_seq)

    gate_skip = gate_row - SEC_GA * SECTION
    assert gate_skip % SUBLANES == 0
    kern = functools.partial(_inproj_kernel, tiles_per_seq=tiles_per_seq, n_cast_steps=n_cast_steps)
    return pl.pallas_call(
        kern,
        grid=(N_SECTIONS, n_i),
        in_specs=[
            pl.BlockSpec((tm, d), lambda j, i: (i, 0)),
            pl.BlockSpec((None, pl.Element(SECTION), pl.Element(d)),
                         lambda j, i: (layer, SUBLANES * (j * (SECTION // SUBLANES) + jnp.where(
                             j >= SEC_GA, gate_skip // SUBLANES, 0)), 0)),
            pl.BlockSpec((N_VEC_ROWS, SECTION), lambda j, i: (0, 0)),
            pl.BlockSpec((None, cast_rows, cast_w.shape[2]), lambda j, i: (layer, cast_step(j, i), 0)),
        ],
        out_specs=[pl.BlockSpec((None, tm, SECTION), proj_block), pl.BlockSpec((None, SECTION, tm), vt_block),
                   pl.BlockSpec((cast_rows, cast_w.shape[2]), lambda j, i: (cast_step(j, i), 0))],
        out_shape=[jax.ShapeDtypeStruct((N_PROJ, t, SECTION), BF16),
                   jax.ShapeDtypeStruct((batch, SECTION, seq), BF16),
                   jax.ShapeDtypeStruct(cast_w.shape[1:], BF16)],
        scratch_shapes=[pltpu.VMEM((SUBLANES, SECTION), F32)],
        compiler_params=_params("arbitrary", "arbitrary"),
        name="inproj",
    )(xn, w_t, vecs, cast_w)


def _gmlp_kernel(u_ref, v_ref, ws_ref, bs_ref, out_ref, *, n_chunks, n_groups):
    row = lax.broadcasted_iota(jnp.int32, (CHUNK, CHUNK), 0)
    col = lax.broadcasted_iota(jnp.int32, (CHUNK, CHUNK), 1)
    causal = col <= row
    for g in range(n_groups):
        gs = slice(g * GM_GROUP_DIM, (g + 1) * GM_GROUP_DIM)
        w = jnp.where(causal, ws_ref[g], 0.0).astype(BF16)
        vg = jnp.concatenate([v_ref[c * CHUNK:(c + 1) * CHUNK, gs] for c in range(n_chunks)], axis=1)
        s = _dot(w, vg) + bs_ref[:, g:g + 1]
        for c in range(n_chunks):
            rs = slice(c * CHUNK, (c + 1) * CHUNK)
            sc = s[:, c * GM_GROUP_DIM:(c + 1) * GM_GROUP_DIM]
            out_ref[rs, gs] = (u_ref[rs, gs].astype(F32) * sc).astype(BF16)


def _gmlp(proj, ws, bs_t, tm):
    _, t, width = proj.shape
    n_groups = width // GM_GROUP_DIM
    kern = functools.partial(_gmlp_kernel, n_chunks=tm // CHUNK, n_groups=n_groups)
    return pl.pallas_call(
        kern,
        grid=(t // tm,),
        in_specs=[
            pl.BlockSpec((None, tm, width), lambda i: (PROJ_U, i, 0)),
            pl.BlockSpec((None, tm, width), lambda i: (PROJ_V, i, 0)),
            pl.BlockSpec((n_groups, CHUNK, CHUNK), lambda i: (0, 0, 0)),
            pl.BlockSpec((CHUNK, n_groups), lambda i: (0, 0)),
        ],
        out_specs=pl.BlockSpec((tm, width), lambda i: (i, 0)),
        out_shape=jax.ShapeDtypeStruct((t, width), BF16),
        compiler_params=_params("arbitrary"),
        name="gmlp",
    )(proj, proj, ws, bs_t)


def _lanes(col):
    return jnp.broadcast_to(col, (col.shape[0], LANES))


def _wide(x, n):
    return jnp.concatenate([x] * n, axis=1)


def _mlstm_kernel(co_ref, vt_ref, gcol_ref, grow_ref, bcol_ref, brow_ref, wq_ref, wk_ref, ng_ref, *rest,
                  head_dim, batch, n_cast):
    cast_in, (out_ref, *cast_out), (st_ref, mx_ref, wqb_ref, wkb_ref) = (
        rest[:n_cast], rest[n_cast:2 * n_cast + 1], rest[2 * n_cast + 1:])
    c_ref, o_ref = co_ref.at[0], co_ref.at[1]

    for src, dst in zip(cast_in, cast_out):
        dst[...] = src[...].astype(BF16)

    @pl.when(pl.program_id(0) == 0)
    def _():
        wqb_ref[...] = wq_ref[...].astype(BF16)
        wkb_ref[...] = (wk_ref[...] * (head_dim ** -0.5)).astype(BF16)
        st_ref[...] = jnp.zeros(st_ref.shape, F32)
        mx_ref[...] = jnp.zeros(mx_ref.shape, F32)

    row = lax.broadcasted_iota(jnp.int32, (CHUNK, CHUNK), 0)
    col = lax.broadcasted_iota(jnp.int32, (CHUNK, CHUNK), 1)
    causal = col <= row
    tril = causal.astype(F32)
    triu = (row <= col).astype(F32)
    ones_rows = jnp.ones((LANES, CHUNK), BF16)
    nd = head_dim // LANES
    heads = [(b, h) for b in range(batch) for h in range(ML_HEADS)]
    slot = lambda b, h: b * ML_HEADS + h
    hsl = lambda h: slice(h * head_dim, (h + 1) * head_dim)


    g_col, g_row, bcum_col, bcum_row, total = {}, {}, {}, {}, {}
    for b in range(batch):
        g_col[b] = gcol_ref[b] + brow_ref[...]
        g_row[b] = grow_ref[b] + bcol_ref[...]
        lf_row = _log_sigmoid(g_row[b])
        bcum_col[b] = jnp.dot(tril, _log_sigmoid(g_col[b]), precision=lax.Precision.HIGHEST,
                              preferred_element_type=F32)
        bcum_row[b] = jnp.dot(lf_row, triu, precision=lax.Precision.HIGHEST, preferred_element_type=F32)
        total[b] = jnp.sum(lf_row, axis=-1, keepdims=True)

    q, k = {}, {}
    for b, h in heads:
        c = c_ref[b, :, hsl(h)]
        q[b, h] = _dot(c, wqb_ref[h])
        k[b, h] = _dot(c, wkb_ref[h])

    w_intra, w_inter, inv_floor, w_state, decay, scale, m_new = {}, {}, {}, {}, {}, {}, {}
    for b, h in heads:
        f = ML_HEADS + h
        bc = _lanes(bcum_col[b][:, f:f + 1])
        ig = _lanes(g_col[b][:, h:h + 1])
        src = g_row[b][h:h + 1, :] - bcum_row[b][f:f + 1, :]
        b_last = jnp.broadcast_to(total[b][f:f + 1, :], (1, LANES))
        m_prev = mx_ref[slot(b, h):slot(b, h) + 1, :]
        inter = bc + m_prev
        dlog = jnp.where(causal, bc + src, -jnp.inf)
        m_t = jnp.maximum(inter, _lanes(jnp.max(dlog, axis=-1, keepdims=True)))
        w_intra[b, h] = jnp.exp(dlog - m_t)
        w_inter[b, h] = jnp.exp(inter - m_t)
        inv_floor[b, h] = jnp.exp(-m_t)
        a = b_last - bc + ig
        a_max = jnp.max(a, axis=0, keepdims=True)
        w_state[b, h] = jnp.exp(a - a_max)
        m_new[b, h] = jnp.maximum(b_last + m_prev, a_max)
        decay[b, h] = jnp.exp(b_last + m_prev - m_new[b, h])
        scale[b, h] = jnp.exp(a_max - m_new[b, h])

    qk = {}
    for b, h in heads:
        qk[b, h] = _nt_dot(q[b, h].astype(BF16), k[b, h].astype(BF16)) * w_intra[b, h]

    for b, h in heads:
        vt_ext = jnp.concatenate([vt_ref[b, hsl(h), :], ones_rows], axis=0)
        lhs = jnp.concatenate([(q[b, h] * _wide(w_inter[b, h], nd)).astype(BF16), qk[b, h].astype(BF16)], axis=1)
        rhs = jnp.concatenate([st_ref[slot(b, h)].astype(BF16), vt_ext], axis=1)
        ext = _nt_dot(lhs, rhs)
        den = ext[:, head_dim:]
        hid = ext[:, :head_dim] * _wide(1.0 / jnp.maximum(jnp.abs(den), inv_floor[b, h]), nd)
        mu = jnp.mean(hid, axis=-1, keepdims=True)
        var = jnp.mean(jnp.square(hid - mu), axis=-1, keepdims=True)
        hid = (hid - mu) * lax.rsqrt(var + EPS) * ng_ref[:, hsl(h)]
        out_ref[b, :, hsl(h)] = (o_ref[b, :, hsl(h)].astype(F32) * hid).astype(BF16)

    for b, h in heads:
        s = slot(b, h)
        vt_ext = jnp.concatenate([vt_ref[b, hsl(h), :], ones_rows], axis=0)
        kw = k[b, h] * _wide(w_state[b, h] * scale[b, h], nd)
        upd = _dot(vt_ext, kw.astype(BF16))
        st_ref[s] = _wide(decay[b, h], nd) * st_ref[s] + upd
        mx_ref[s:s + 1, :] = m_new[b, h]


def _mlstm(proj, vt, gates_col, gates_row, bias_col, bias_row, wq, wk, norm_g, cast_ws, layer, batch):
    _, t, width = proj.shape
    seq = t // batch
    head_dim = width // ML_HEADS
    n_if = gates_row.shape[1]
    n_steps = seq // CHUNK
    assert PROJ_O == PROJ_M + 1 and PROJ_M % 2 == 0
    proj4 = proj.reshape(proj.shape[0], batch, seq, width)
    cast_rows = [_cast_rows(w.shape[1], n_steps) for w in cast_ws]
    kern = functools.partial(_mlstm_kernel, head_dim=head_dim, batch=batch, n_cast=len(cast_ws))
    const2 = lambda n: (0, 0)
    const3 = lambda n: (0, 0, 0)
    n_state = batch * ML_HEADS
    out, *cast = pl.pallas_call(
        kern,
        grid=(n_steps,),
        in_specs=[
            pl.BlockSpec((2, batch, CHUNK, width), lambda n: (PROJ_M // 2, 0, n, 0)),
            pl.BlockSpec((batch, width, CHUNK), lambda n: (0, 0, n)),
            pl.BlockSpec((batch, CHUNK, LANES), lambda n: (0, n, 0)),
            pl.BlockSpec((batch, n_if, CHUNK), lambda n: (0, 0, n)),
            pl.BlockSpec((n_if, 1), const2),
            pl.BlockSpec((1, LANES), const2),
            pl.BlockSpec((ML_HEADS, head_dim, head_dim), const3),
            pl.BlockSpec((ML_HEADS, head_dim, head_dim), const3),
            pl.BlockSpec((1, width), const2),
        ] + [pl.BlockSpec((None, r, w.shape[2]), lambda n: (layer, n, 0)) for r, w in zip(cast_rows, cast_ws)],
        out_specs=[pl.BlockSpec((batch, CHUNK, width), lambda n: (0, n, 0))]
        + [pl.BlockSpec((r, w.shape[2]), lambda n: (n, 0)) for r, w in zip(cast_rows, cast_ws)],
        out_shape=[jax.ShapeDtypeStruct((batch, seq, width), BF16)]
        + [jax.ShapeDtypeStruct(w.shape[1:], BF16) for w in cast_ws],
        scratch_shapes=[
            pltpu.VMEM((n_state, head_dim + LANES, head_dim), F32),
            pltpu.VMEM((n_state, LANES), F32),
            pltpu.VMEM((ML_HEADS, head_dim, head_dim), BF16),
            pltpu.VMEM((ML_HEADS, head_dim, head_dim), BF16),
        ],
        compiler_params=_params("arbitrary"),
        name="mlstm",
    )(proj4, vt, gates_col.reshape(batch, seq, LANES), gates_row, bias_col, bias_row, wq, wk, norm_g, *cast_ws)
    return out.reshape(t, width), cast


def _merge_kernel(x_ref, ya_ref, yb_ref, gate_ref, wa_ref, wb_ref, wo_ref, g2_ref, x1_ref, hn_ref):
    gate_a = _sigmoid(jnp.concatenate([gate_ref[0], gate_ref[1]], axis=1).astype(F32))
    gate_b = _sigmoid(jnp.concatenate([gate_ref[2], gate_ref[3]], axis=1).astype(F32))
    mixed = gate_a * _dot(ya_ref[...], wa_ref[...]) + gate_b * _dot(yb_ref[...], wb_ref[...])
    x1 = x_ref[...] + _dot(mixed.astype(BF16), wo_ref[...])
    x1_ref[...] = x1
    hn_ref[...] = _rms_norm(x1, g2_ref[...]).astype(BF16)


def _merge(x, y_a, y_b, proj, w_a, w_b, w_out, g2, tm):
    t, d = x.shape
    width = y_a.shape[1]
    n_steps = t // tm
    n_gate = N_PROJ - PROJ_GA
    assert PROJ_GA % n_gate == 0
    const = lambda i: (0, 0)
    resident = dict(pipeline_mode=pl.Buffered(1))
    return pl.pallas_call(
        _merge_kernel,
        grid=(n_steps,),
        in_specs=[
            pl.BlockSpec((tm, d), lambda i: (i, 0)),
            pl.BlockSpec((tm, width), lambda i: (i, 0)),
            pl.BlockSpec((tm, width), lambda i: (i, 0)),
            pl.BlockSpec((n_gate, tm, SECTION), lambda i: (PROJ_GA // n_gate, i, 0)),
            pl.BlockSpec((width, d), const, **resident),
            pl.BlockSpec((width, d), const, **resident),
            pl.BlockSpec((d, d), const, **resident),
            pl.BlockSpec((1, d), const),
        ],
        out_specs=[pl.BlockSpec((tm, d), lambda i: (i, 0)), pl.BlockSpec((tm, d), lambda i: (i, 0))],
        out_shape=[jax.ShapeDtypeStruct((t, d), F32), jax.ShapeDtypeStruct((t, d), BF16)],
        compiler_params=_params("arbitrary"),
        name="merge",
    )(x, y_a, y_b, proj, w_a, w_b, w_out, g2)


def _ffn_kernel(hn_ref, x1_ref, w1_ref, w2_ref, gf_ref, out_ref, *, final_norm):
    j = pl.program_id(1)

    @pl.when(j == 0)
    def _():
        out_ref[...] = x1_ref[...]

    h = jnp.square(jnp.maximum(_dot(hn_ref[...], w1_ref[...]), 0.0)).astype(BF16)
    out_ref[...] += _dot(h, w2_ref[...])

    if final_norm:
        @pl.when(j == pl.num_programs(1) - 1)
        def _():
            out_ref[...] = _rms_norm(out_ref[...], gf_ref[...])


def _ffn(hn, x1, w1, w2, gf, tm, tf, final_norm):
    t, d = x1.shape
    d_ff = w1.shape[1]
    return pl.pallas_call(
        functools.partial(_ffn_kernel, final_norm=final_norm),
        grid=(t // tm, d_ff // tf),
        in_specs=[
            pl.BlockSpec((tm, d), lambda i, j: (i, 0)),
            pl.BlockSpec((tm, d), lambda i, j: (i, 0)),
            pl.BlockSpec((d, tf), lambda i, j: (0, j)),
            pl.BlockSpec((tf, d), lambda i, j: (j, 0)),
            pl.BlockSpec((1, d), lambda i, j: (0, 0)),
        ],
        out_specs=pl.BlockSpec((tm, d), lambda i, j: (i, 0)),
        out_shape=jax.ShapeDtypeStruct((t, d), F32),
        compiler_params=_params("arbitrary", "arbitrary"),
        name="ffn",
    )(hn, x1, w1, w2, gf)


def _tiles(t):
    return dict(prologue=min(t, 1024), inproj=min(t, 1024), gmlp=min(t, 1024), merge=min(t, 512),
                ffn_m=min(t, 512), ffn_f=2048)


def kernel(x, norm1_g, w_in, b_gate, gm_ln_g, gm_ln_b, gm_ws, gm_bs, ml_conv_w, ml_conv_b, ml_wq, ml_wk,
           ml_ig_b, ml_fg_b, ml_norm_g, w_a, w_b, w_out, norm2_g, w_ff1, w_ff2, norm_f_g):
    batch, seq, d = x.shape
    depth = w_in.shape[0]
    t = batch * seq
    tiles = _tiles(t)
    if_row = SEC_GA * SECTION
    n_if = 2 * ML_HEADS
    xt = x.reshape(t, d)
    w_t = jnp.swapaxes(w_in, 1, 2)
    for l in range(depth):
        gate_bias = jnp.concatenate([ml_ig_b[l], ml_fg_b[l]])
        vecs = jnp.concatenate([
            gm_ln_g[l][None], gm_ln_b[l][None], ml_conv_b[l][None], ml_conv_w[l],
            jnp.zeros((1, SECTION), F32), b_gate[l].reshape(4, SECTION),
            jnp.zeros((N_VEC_ROWS - VEC_GATE_B - 5, SECTION), F32)])
        xn, gates_col, gates_row = _prologue(xt, norm1_g[l][None], w_t, l, if_row, n_if, batch,
                                             tiles["prologue"])
        proj, vt, w2_bf = _inproj(xn, w_t, l, if_row + n_if, vecs, w_ff2, batch, tiles["inproj"])
        y_a = _gmlp(proj, gm_ws[l], gm_bs[l].T, tiles["gmlp"])
        y_b, (wa_bf, wb_bf, wo_bf, w1_bf) = _mlstm(
            proj, vt, gates_col, gates_row, gate_bias[:, None],
            jnp.pad(gate_bias, (0, LANES - n_if))[None], ml_wq[l], ml_wk[l], ml_norm_g[l][None],
            (w_a, w_b, w_out, w_ff1), l, batch)
        x1, hn = _merge(xt, y_a, y_b, proj, wa_bf, wb_bf, wo_bf, norm2_g[l][None], tiles["merge"])
        xt = _ffn(hn, x1, w1_bf, w2_bf, norm_f_g[None], tiles["ffn_m"], tiles["ffn_f"],
                  final_norm=l == depth - 1)
    return xt.reshape(batch, seq, d)
```

```python
import functools

import jax
import jax.numpy as jnp
from jax import lax
from jax.experimental import pallas as pl
from jax.experimental.pallas import tpu as pltpu

EPS = 1e-6
GM_GROUP_DIM = 128
CHUNK = 128
ML_HEADS = 4
CONV_K = 4
LANES = 128
SUBLANES = 8
BF16_ROWS = 16
SECTION = 1024
SEC_U, SEC_V, SEC_M, SEC_MV, SEC_O, SEC_GA, SEC_GB = 0, 1, 2, 3, 4, 5, 7
N_SECTIONS = 9
PROJ_U, PROJ_V, PROJ_M, PROJ_O, PROJ_GA, PROJ_GB = 0, 1, 2, 3, 4, 6
N_PROJ = 8
VMEM_LIMIT = 56 * 1024 * 1024

BF16 = jnp.bfloat16
F32 = jnp.float32


def _sigmoid(x):
    return 0.5 * jnp.tanh(0.5 * x) + 0.5


def _log_sigmoid(x):
    return jnp.minimum(x, 0.0) - jnp.log(1.0 + jnp.exp(-jnp.abs(x)))


def _rms_norm(x, g):
    return x * lax.rsqrt(jnp.mean(x * x, axis=-1, keepdims=True) + EPS) * g


def _dot(a, b):
    return jnp.dot(a, b, preferred_element_type=F32)


def _params(*semantics):
    return pltpu.CompilerParams(dimension_semantics=semantics, vmem_limit_bytes=VMEM_LIMIT)


def _cast_rows(n_rows, n_steps):
    rows = n_rows // n_steps
    assert rows * n_steps == n_rows and rows % BF16_ROWS == 0, (n_rows, n_steps)
    return rows


def _nt_dot(a, b):
    return lax.dot_general(a, b, (((1,), (1,)), ((), ())), preferred_element_type=F32)


def _prologue_kernel(x_ref, g1_ref, wif_ref, xn_ref, gcol_ref, grow_ref):
    xn = _rms_norm(x_ref[...], g1_ref[...]).astype(BF16)
    xn_ref[...] = xn
    wif = wif_ref[...].astype(BF16)
    gcol_ref[...] = _nt_dot(xn, wif)
    grow_ref[...] = _nt_dot(wif, xn)[:grow_ref.shape[0], :]


def _prologue(x, g1, w_t, layer, if_row, n_if, batch, tm):
    t, d = x.shape
    seq = t // batch
    tiles_per_seq = seq // tm
    assert tiles_per_seq * tm == seq
    return pl.pallas_call(
        _prologue_kernel,
        grid=(t // tm,),
        in_specs=[
            pl.BlockSpec((tm, d), lambda i: (i, 0)),
            pl.BlockSpec((1, d), lambda i: (0, 0)),
            pl.BlockSpec((None, LANES, d), lambda i: (layer, if_row // LANES, 0)),
        ],
        out_specs=[
            pl.BlockSpec((tm, d), lambda i: (i, 0)),
            pl.BlockSpec((tm, LANES), lambda i: (i, 0)),
            pl.BlockSpec((None, n_if, tm), lambda i: (i // tiles_per_seq, 0, i % tiles_per_seq)),
        ],
        out_shape=[
            jax.ShapeDtypeStruct((t, d), BF16),
            jax.ShapeDtypeStruct((t, LANES), F32),
            jax.ShapeDtypeStruct((batch, n_if, seq), F32),
        ],
        compiler_params=_params("arbitrary"),
        name="prologue",
    )(x, g1, w_t)


VEC_LN_G, VEC_LN_B, VEC_CONV_B, VEC_CONV_W, VEC_GATE_B = 0, 1, 2, 3, 7
N_VEC_ROWS = 16


def _inproj_kernel(xn_ref, w_ref, vec_ref, cast_ref, out_ref, vt_out, cast_out, wbf_ref, tail_ref,
                   *, tiles_per_seq, n_cast_steps):
    j = pl.program_id(0)
    i = pl.program_id(1)
    tm = xn_ref.shape[0]
    vec = lambda r, n=1: vec_ref[pl.ds(r, n), :]

    @pl.when(j * pl.num_programs(1) + i < n_cast_steps)
    def _():
        cast_out[...] = cast_ref[...].astype(BF16)

    @pl.when((i == 0) & (j != SEC_MV))
    def _():
        rows = 256
        for r in range(0, SECTION, rows):
            wbf_ref[:, r:r + rows] = w_ref[r:r + rows, :].T.astype(BF16)

    def proj():
        return _dot(xn_ref[...], wbf_ref[...])

    @pl.when(j == SEC_U)
    def _():
        out_ref[...] = jax.nn.gelu(proj()).astype(BF16)

    @pl.when(j == SEC_V)
    def _():
        v = jax.nn.gelu(proj())
        mu = jnp.mean(v, axis=-1, keepdims=True)
        var = jnp.mean(jnp.square(v - mu), axis=-1, keepdims=True)
        out_ref[...] = ((v - mu) * lax.rsqrt(var + EPS) * vec(VEC_LN_G) + vec(VEC_LN_B)).astype(BF16)

    @pl.when(j == SEC_M)
    def _():
        @pl.when(i % tiles_per_seq == 0)
        def _():
            tail_ref[...] = jnp.zeros(tail_ref.shape, F32)

        x = proj()
        xe = jnp.concatenate([tail_ref[...], x], axis=0)
        conv = vec(VEC_CONV_B) + vec(VEC_CONV_W + CONV_K - 1) * x
        for d in range(1, CONV_K):
            conv = conv + vec(VEC_CONV_W + CONV_K - 1 - d) * xe[SUBLANES - d:SUBLANES - d + tm, :]
        tail_ref[...] = x[tm - SUBLANES:, :]
        out_ref[...] = (conv * _sigmoid(conv)).astype(BF16)

    @pl.when(j == SEC_MV)
    def _():
        vt_out[...] = _nt_dot(w_ref[...].astype(BF16), xn_ref[...]).astype(BF16)

    @pl.when(j >= SEC_O)
    def _():
        out_ref[...] = _sigmoid(proj() + vec(VEC_GATE_B + j - SEC_O)).astype(BF16)


def _inproj(xn, w_t, layer, gate_row, vecs, cast_w, batch, tm):
    t, d = xn.shape
    n_i = t // tm
    seq = t // batch
    tiles_per_seq = seq // tm
    assert tiles_per_seq * tm == seq
    n_cast_steps = (N_SECTIONS - 1) * n_i
    cast_rows = _cast_rows(cast_w.shape[1], n_cast_steps)
    cast_step = lambda j, i: jnp.minimum(j * n_i + i, n_cast_steps - 1)

    def proj_block(j, i):
        mv = j == SEC_MV
        return (j - (j >= SEC_MV).astype(jnp.int32), jnp.where(mv, n_i - 1, i), 0)

    def vt_block(j, i):
        tile = jnp.clip((j - SEC_MV) * n_i + i, 0, n_i - 1)
        return (tile // tiles_per_seq, 0, tile % tiles_per_seq)

    gate_skip = gate_row - SEC_GA * SECTION
    assert gate_skip % SUBLANES == 0
    kern = functools.partial(_inproj_kernel, tiles_per_seq=tiles_per_seq, n_cast_steps=n_cast_steps)
    return pl.pallas_call(
        kern,
        grid=(N_SECTIONS, n_i),
        in_specs=[
            pl.BlockSpec((tm, d), lambda j, i: (i, 0)),
            pl.BlockSpec((None, pl.Element(SECTION), pl.Element(d)),
                         lambda j, i: (layer, SUBLANES * (j * (SECTION // SUBLANES) + jnp.where(
                             j >= SEC_GA, gate_skip // SUBLANES, 0)), 0)),
            pl.BlockSpec((N_VEC_ROWS, SECTION), lambda j, i: (0, 0)),
            pl.BlockSpec((None, cast_rows, cast_w.shape[2]), lambda j, i: (layer, cast_step(j, i), 0)),
        ],
        out_specs=[pl.BlockSpec((None, tm, SECTION), proj_block), pl.BlockSpec((None, SECTION, tm), vt_block),
                   pl.BlockSpec((cast_rows, cast_w.shape[2]), lambda j, i: (cast_step(j, i), 0))],
        out_shape=[jax.ShapeDtypeStruct((N_PROJ, t, SECTION), BF16),
                   jax.ShapeDtypeStruct((batch, SECTION, seq), BF16),
                   jax.ShapeDtypeStruct(cast_w.shape[1:], BF16)],
        scratch_shapes=[
            pltpu.VMEM((d, SECTION), BF16),
            pltpu.VMEM((SUBLANES, SECTION), F32),
        ],
        compiler_params=_params("arbitrary", "arbitrary"),
        name="inproj",
    )(xn, w_t, vecs, cast_w)


def _gmlp_kernel(u_ref, v_ref, ws_ref, bs_ref, out_ref, *, n_chunks, n_groups):
    row = lax.broadcasted_iota(jnp.int32, (CHUNK, CHUNK), 0)
    col = lax.broadcasted_iota(jnp.int32, (CHUNK, CHUNK), 1)
    causal = col <= row
    for g in range(n_groups):
        gs = slice(g * GM_GROUP_DIM, (g + 1) * GM_GROUP_DIM)
        w = jnp.where(causal, ws_ref[g], 0.0).astype(BF16)
        vg = jnp.concatenate([v_ref[c * CHUNK:(c + 1) * CHUNK, gs] for c in range(n_chunks)], axis=1)
        s = _dot(w, vg) + bs_ref[:, g:g + 1]
        for c in range(n_chunks):
            rs = slice(c * CHUNK, (c + 1) * CHUNK)
            sc = s[:, c * GM_GROUP_DIM:(c + 1) * GM_GROUP_DIM]
            out_ref[rs, gs] = (u_ref[rs, gs].astype(F32) * sc).astype(BF16)


def _gmlp(proj, ws, bs_t, tm):
    _, t, width = proj.shape
    n_groups = width // GM_GROUP_DIM
    kern = functools.partial(_gmlp_kernel, n_chunks=tm // CHUNK, n_groups=n_groups)
    return pl.pallas_call(
        kern,
        grid=(t // tm,),
        in_specs=[
            pl.BlockSpec((None, tm, width), lambda i: (PROJ_U, i, 0)),
            pl.BlockSpec((None, tm, width), lambda i: (PROJ_V, i, 0)),
            pl.BlockSpec((n_groups, CHUNK, CHUNK), lambda i: (0, 0, 0)),
            pl.BlockSpec((CHUNK, n_groups), lambda i: (0, 0)),
        ],
        out_specs=pl.BlockSpec((tm, width), lambda i: (i, 0)),
        out_shape=jax.ShapeDtypeStruct((t, width), BF16),
        compiler_params=_params("arbitrary"),
        name="gmlp",
    )(proj, proj, ws, bs_t)


def _lanes(col):
    return jnp.broadcast_to(col, (col.shape[0], LANES))


def _wide(x, n):
    return jnp.concatenate([x] * n, axis=1)


def _mlstm_kernel(co_ref, vt_ref, gcol_ref, grow_ref, bcol_ref, brow_ref, wq_ref, wk_ref, ng_ref, *rest,
                  head_dim, batch, n_cast):
    cast_in, (out_ref, *cast_out), (st_ref, mx_ref, wqb_ref, wkb_ref) = (
        rest[:n_cast], rest[n_cast:2 * n_cast + 1], rest[2 * n_cast + 1:])
    c_ref, o_ref = co_ref.at[0], co_ref.at[1]

    for src, dst in zip(cast_in, cast_out):
        dst[...] = src[...].astype(BF16)

    @pl.when(pl.program_id(0) == 0)
    def _():
        wqb_ref[...] = wq_ref[...].astype(BF16)
        wkb_ref[...] = (wk_ref[...] * (head_dim ** -0.5)).astype(BF16)
        st_ref[...] = jnp.zeros(st_ref.shape, F32)
        mx_ref[...] = jnp.zeros(mx_ref.shape, F32)

    row = lax.broadcasted_iota(jnp.int32, (CHUNK, CHUNK), 0)
    col = lax.broadcasted_iota(jnp.int32, (CHUNK, CHUNK), 1)
    causal = col <= row
    tril = causal.astype(F32)
    triu = (row <= col).astype(F32)
    ones_rows = jnp.ones((LANES, CHUNK), BF16)
    nd = head_dim // LANES
    heads = [(b, h) for b in range(batch) for h in range(ML_HEADS)]
    slot = lambda b, h: b * ML_HEADS + h
    hsl = lambda h: slice(h * head_dim, (h + 1) * head_dim)


    g_col, g_row, bcum_col, bcum_row, total = {}, {}, {}, {}, {}
    for b in range(batch):
        g_col[b] = gcol_ref[b] + brow_ref[...]
        g_row[b] = grow_ref[b] + bcol_ref[...]
        lf_row = _log_sigmoid(g_row[b])
        bcum_col[b] = jnp.dot(tril, _log_sigmoid(g_col[b]), precision=lax.Precision.HIGHEST,
                              preferred_element_type=F32)
        bcum_row[b] = jnp.dot(lf_row, triu, precision=lax.Precision.HIGHEST, preferred_element_type=F32)
        total[b] = jnp.sum(lf_row, axis=-1, keepdims=True)

    q, k = {}, {}
    for b, h in heads:
        c = c_ref[b, :, hsl(h)]
        q[b, h] = _dot(c, wqb_ref[h])
        k[b, h] = _dot(c, wkb_ref[h])

    w_intra, w_inter, inv_floor, w_state, decay, scale, m_new = {}, {}, {}, {}, {}, {}, {}
    for b, h in heads:
        f = ML_HEADS + h
        bc = _lanes(bcum_col[b][:, f:f + 1])
        ig = _lanes(g_col[b][:, h:h + 1])
        src = g_row[b][h:h + 1, :] - bcum_row[b][f:f + 1, :]
        b_last = jnp.broadcast_to(total[b][f:f + 1, :], (1, LANES))
        m_prev = mx_ref[slot(b, h):slot(b, h) + 1, :]
        inter = bc + m_prev
        dlog = jnp.where(causal, bc + src, -jnp.inf)
        m_t = jnp.maximum(inter, _lanes(jnp.max(dlog, axis=-1, keepdims=True)))
        w_intra[b, h] = jnp.exp(dlog - m_t)
        w_inter[b, h] = jnp.exp(inter - m_t)
        inv_floor[b, h] = jnp.exp(-m_t)
        a = b_last - bc + ig
        a_max = jnp.max(a, axis=0, keepdims=True)
        w_state[b, h] = jnp.exp(a - a_max)
        m_new[b, h] = jnp.maximum(b_last + m_prev, a_max)
        decay[b, h] = jnp.exp(b_last + m_prev - m_new[b, h])
        scale[b, h] = jnp.exp(a_max - m_new[b, h])

    qk = {}
    for b, h in heads:
        qk[b, h] = _nt_dot(q[b, h].astype(BF16), k[b, h].astype(BF16)) * w_intra[b, h]

    for b, h in heads:
        vt_ext = jnp.concatenate([vt_ref[b, hsl(h), :], ones_rows], axis=0)
        lhs = jnp.concatenate([(q[b, h] * _wide(w_inter[b, h], nd)).astype(BF16), qk[b, h].astype(BF16)], axis=1)
        rhs = jnp.concatenate([st_ref[slot(b, h)].astype(BF16), vt_ext], axis=1)
        ext = _nt_dot(lhs, rhs)
        den = ext[:, head_dim:]
        hid = ext[:, :head_dim] * _wide(1.0 / jnp.maximum(jnp.abs(den), inv_floor[b, h]), nd)
        mu = jnp.mean(hid, axis=-1, keepdims=True)
        var = jnp.mean(jnp.square(hid - mu), axis=-1, keepdims=True)
        hid = (hid - mu) * lax.rsqrt(var + EPS) * ng_ref[:, hsl(h)]
        out_ref[b, :, hsl(h)] = (o_ref[b, :, hsl(h)].astype(F32) * hid).astype(BF16)

    for b, h in heads:
        s = slot(b, h)
        vt_ext = jnp.concatenate([vt_ref[b, hsl(h), :], ones_rows], axis=0)
        kw = k[b, h] * _wide(w_state[b, h] * scale[b, h], nd)
        upd = _dot(vt_ext, kw.astype(BF16))
        st_ref[s] = _wide(decay[b, h], nd) * st_ref[s] + upd
        mx_ref[s:s + 1, :] = m_new[b, h]


def _mlstm(proj, vt, gates_col, gates_row, bias_col, bias_row, wq, wk, norm_g, cast_ws, layer, batch):
    _, t, width = proj.shape
    seq = t // batch
    head_dim = width // ML_HEADS
    n_if = gates_row.shape[1]
    n_steps = seq // CHUNK
    assert PROJ_O == PROJ_M + 1 and PROJ_M % 2 == 0
    proj4 = proj.reshape(proj.shape[0], batch, seq, width)
    cast_rows = [_cast_rows(w.shape[1], n_steps) for w in cast_ws]
    kern = functools.partial(_mlstm_kernel, head_dim=head_dim, batch=batch, n_cast=len(cast_ws))
    const2 = lambda n: (0, 0)
    const3 = lambda n: (0, 0, 0)
    n_state = batch * ML_HEADS
    out, *cast = pl.pallas_call(
        kern,
        grid=(n_steps,),
        in_specs=[
            pl.BlockSpec((2, batch, CHUNK, width), lambda n: (PROJ_M // 2, 0, n, 0)),
            pl.BlockSpec((batch, width, CHUNK), lambda n: (0, 0, n)),
            pl.BlockSpec((batch, CHUNK, LANES), lambda n: (0, n, 0)),
            pl.BlockSpec((batch, n_if, CHUNK), lambda n: (0, 0, n)),
            pl.BlockSpec((n_if, 1), const2),
            pl.BlockSpec((1, LANES), const2),
            pl.BlockSpec((ML_HEADS, head_dim, head_dim), const3),
            pl.BlockSpec((ML_HEADS, head_dim, head_dim), const3),
            pl.BlockSpec((1, width), const2),
        ] + [pl.BlockSpec((None, r, w.shape[2]), lambda n: (layer, n, 0)) for r, w in zip(cast_rows, cast_ws)],
        out_specs=[pl.BlockSpec((batch, CHUNK, width), lambda n: (0, n, 0))]
        + [pl.BlockSpec((r, w.shape[2]), lambda n: (n, 0)) for r, w in zip(cast_rows, cast_ws)],
        out_shape=[jax.ShapeDtypeStruct((batch, seq, width), BF16)]
        + [jax.ShapeDtypeStruct(w.shape[1:], BF16) for w in cast_ws],
        scratch_shapes=[
            pltpu.VMEM((n_state, head_dim + LANES, head_dim), F32),
            pltpu.VMEM((n_state, LANES), F32),
            pltpu.VMEM((ML_HEADS, head_dim, head_dim), BF16),
            pltpu.VMEM((ML_HEADS, head_dim, head_dim), BF16),
        ],
        compiler_params=_params("arbitrary"),
        name="mlstm",
    )(proj4, vt, gates_col.reshape(batch, seq, LANES), gates_row, bias_col, bias_row, wq, wk, norm_g, *cast_ws)
    return out.reshape(t, width), cast


def _merge_kernel(x_ref, ya_ref, yb_ref, gate_ref, wa_ref, wb_ref, wo_ref, g2_ref, x1_ref, hn_ref):
    gate_a = jnp.concatenate([gate_ref[0], gate_ref[1]], axis=1).astype(F32)
    gate_b = jnp.concatenate([gate_ref[2], gate_ref[3]], axis=1).astype(F32)
    mixed = gate_a * _dot(ya_ref[...], wa_ref[...]) + gate_b * _dot(yb_ref[...], wb_ref[...])
    x1 = x_ref[...] + _dot(mixed.astype(BF16), wo_ref[...])
    x1_ref[...] = x1
    hn_ref[...] = _rms_norm(x1, g2_ref[...]).astype(BF16)


def _merge(x, y_a, y_b, proj, w_a, w_b, w_out, g2, tm):
    t, d = x.shape
    width = y_a.shape[1]
    n_steps = t // tm
    n_gate = N_PROJ - PROJ_GA
    assert PROJ_GA % n_gate == 0
    const = lambda i: (0, 0)
    resident = dict(pipeline_mode=pl.Buffered(1))
    return pl.pallas_call(
        _merge_kernel,
        grid=(n_steps,),
        in_specs=[
            pl.BlockSpec((tm, d), lambda i: (i, 0)),
            pl.BlockSpec((tm, width), lambda i: (i, 0)),
            pl.BlockSpec((tm, width), lambda i: (i, 0)),
            pl.BlockSpec((n_gate, tm, SECTION), lambda i: (PROJ_GA // n_gate, i, 0)),
            pl.BlockSpec((width, d), const, **resident),
            pl.BlockSpec((width, d), const, **resident),
            pl.BlockSpec((d, d), const, **resident),
            pl.BlockSpec((1, d), const),
        ],
        out_specs=[pl.BlockSpec((tm, d), lambda i: (i, 0)), pl.BlockSpec((tm, d), lambda i: (i, 0))],
        out_shape=[jax.ShapeDtypeStruct((t, d), F32), jax.ShapeDtypeStruct((t, d), BF16)],
        compiler_params=_params("arbitrary"),
        name="merge",
    )(x, y_a, y_b, proj, w_a, w_b, w_out, g2)


def _ffn_kernel(hn_ref, x1_ref, w1_ref, w2_ref, gf_ref, out_ref, *, final_norm):
    j = pl.program_id(1)

    @pl.when(j == 0)
    def _():
        out_ref[...] = x1_ref[...]

    h = jnp.square(jnp.maximum(_dot(hn_ref[...], w1_ref[...]), 0.0)).astype(BF16)
    out_ref[...] += _dot(h, w2_ref[...])

    if final_norm:
        @pl.when(j == pl.num_programs(1) - 1)
        def _():
            out_ref[...] = _rms_norm(out_ref[...], gf_ref[...])


def _ffn(hn, x1, w1, w2, gf, tm, tf, final_norm):
    t, d = x1.shape
    d_ff = w1.shape[1]
    return pl.pallas_call(
        functools.partial(_ffn_kernel, final_norm=final_norm),
        grid=(t // tm, d_ff // tf),
        in_specs=[
            pl.BlockSpec((tm, d), lambda i, j: (i, 0)),
            pl.BlockSpec((tm, d), lambda i, j: (i, 0)),
            pl.BlockSpec((d, tf), lambda i, j: (0, j)),
            pl.BlockSpec((tf, d), lambda i, j: (j, 0)),
            pl.BlockSpec((1, d), lambda i, j: (0, 0)),
        ],
        out_specs=pl.BlockSpec((tm, d), lambda i, j: (i, 0)),
        out_shape=jax.ShapeDtypeStruct((t, d), F32),
        compiler_params=_params("arbitrary", "arbitrary"),
        name="ffn",
    )(hn, x1, w1, w2, gf)


def _tiles(t):
    return dict(prologue=min(t, 1024), inproj=min(t, 1024), gmlp=min(t, 1024), merge=min(t, 512),
                ffn_m=min(t, 512), ffn_f=2048)


def kernel(x, norm1_g, w_in, b_gate, gm_ln_g, gm_ln_b, gm_ws, gm_bs, ml_conv_w, ml_conv_b, ml_wq, ml_wk,
           ml_ig_b, ml_fg_b, ml_norm_g, w_a, w_b, w_out, norm2_g, w_ff1, w_ff2, norm_f_g):
    batch, seq, d = x.shape
    depth = w_in.shape[0]
    t = batch * seq
    tiles = _tiles(t)
    if_row = SEC_GA * SECTION
    n_if = 2 * ML_HEADS
    xt = x.reshape(t, d)
    w_t = jnp.swapaxes(w_in, 1, 2)
    for l in range(depth):
        gate_bias = jnp.concatenate([ml_ig_b[l], ml_fg_b[l]])
        vecs = jnp.concatenate([
            gm_ln_g[l][None], gm_ln_b[l][None], ml_conv_b[l][None], ml_conv_w[l],
            jnp.zeros((1, SECTION), F32), b_gate[l].reshape(4, SECTION),
            jnp.zeros((N_VEC_ROWS - VEC_GATE_B - 5, SECTION), F32)])
        xn, gates_col, gates_row = _prologue(xt, norm1_g[l][None], w_t, l, if_row, n_if, batch,
                                             tiles["prologue"])
        proj, vt, w2_bf = _inproj(xn, w_t, l, if_row + n_if, vecs, w_ff2, batch, tiles["inproj"])
        y_a = _gmlp(proj, gm_ws[l], gm_bs[l].T, tiles["gmlp"])
        y_b, (wa_bf, wb_bf, wo_bf, w1_bf) = _mlstm(
            proj, vt, gates_col, gates_row, gate_bias[:, None],
            jnp.pad(gate_bias, (0, LANES - n_if))[None], ml_wq[l], ml_wk[l], ml_norm_g[l][None],
            (w_a, w_b, w_out, w_ff1), l, batch)
        x1, hn = _merge(xt, y_a, y_b, proj, wa_bf, wb_bf, wo_bf, norm2_g[l][None], tiles["merge"])
        xt = _ffn(hn, x1, w1_bf, w2_bf, norm_f_g[None], tiles["ffn_m"], tiles["ffn_f"],
                  final_norm=l == depth - 1)
    return xt.reshape(batch, seq, d)
```

```python
import functools

import jax
import jax.numpy as jnp
from jax import lax
from jax.experimental import pallas as pl
from jax.experimental.pallas import tpu as pltpu

EPS = 1e-6
GM_GROUP_DIM = 128
CHUNK = 128
ML_HEADS = 4
CONV_K = 4
LANES = 128
SUBLANES = 8
BF16_ROWS = 16
SECTION = 1024
SEC_U, SEC_V, SEC_M, SEC_MV, SEC_O, SEC_GA, SEC_GB = 0, 1, 2, 3, 4, 5, 7
N_SECTIONS = 9
PROJ_U, PROJ_V, PROJ_M, PROJ_O, PROJ_GA, PROJ_GB = 0, 1, 2, 3, 4, 6
N_PROJ = 8
VMEM_LIMIT = 56 * 1024 * 1024

BF16 = jnp.bfloat16
F32 = jnp.float32


def _sigmoid(x):
    return 0.5 * jnp.tanh(0.5 * x) + 0.5


def _log_sigmoid(x):
    return jnp.minimum(x, 0.0) - jnp.log(1.0 + jnp.exp(-jnp.abs(x)))


def _rms_norm(x, g):
    return x * lax.rsqrt(jnp.mean(x * x, axis=-1, keepdims=True) + EPS) * g


def _dot(a, b):
    return jnp.dot(a, b, preferred_element_type=F32)


def _params(*semantics):
    return pltpu.CompilerParams(dimension_semantics=semantics, vmem_limit_bytes=VMEM_LIMIT)


def _cast_rows(n_rows, n_steps):
    rows = n_rows // n_steps
    assert rows * n_steps == n_rows and rows % BF16_ROWS == 0, (n_rows, n_steps)
    return rows


def _nt_dot(a, b):
    return lax.dot_general(a, b, (((1,), (1,)), ((), ())), preferred_element_type=F32)


def _prologue_kernel(x_ref, g1_ref, wif_ref, xn_ref, gcol_ref, grow_ref):
    xn = _rms_norm(x_ref[...], g1_ref[...]).astype(BF16)
    xn_ref[...] = xn
    wif = wif_ref[...].astype(BF16)
    gcol_ref[...] = _nt_dot(xn, wif)
    grow_ref[...] = _nt_dot(wif, xn)[:grow_ref.shape[0], :]


def _prologue(x, g1, w_t, layer, if_row, n_if, batch, tm):
    t, d = x.shape
    seq = t // batch
    tiles_per_seq = seq // tm
    assert tiles_per_seq * tm == seq
    return pl.pallas_call(
        _prologue_kernel,
        grid=(t // tm,),
        in_specs=[
            pl.BlockSpec((tm, d), lambda i: (i, 0)),
            pl.BlockSpec((1, d), lambda i: (0, 0)),
            pl.BlockSpec((None, LANES, d), lambda i: (layer, if_row // LANES, 0)),
        ],
        out_specs=[
            pl.BlockSpec((tm, d), lambda i: (i, 0)),
            pl.BlockSpec((tm, LANES), lambda i: (i, 0)),
            pl.BlockSpec((None, n_if, tm), lambda i: (i // tiles_per_seq, 0, i % tiles_per_seq)),
        ],
        out_shape=[
            jax.ShapeDtypeStruct((t, d), BF16),
            jax.ShapeDtypeStruct((t, LANES), F32),
            jax.ShapeDtypeStruct((batch, n_if, seq), F32),
        ],
        compiler_params=_params("arbitrary"),
        name="prologue",
    )(x, g1, w_t)


VEC_LN_G, VEC_LN_B, VEC_CONV_B, VEC_CONV_W, VEC_GATE_B = 0, 1, 2, 3, 7
N_VEC_ROWS = 16


def _inproj_kernel(xn_ref, w_ref, vec_ref, cast_ref, out_ref, vt_out, cast_out, wbf_ref, tail_ref,
                   *, tiles_per_seq, n_cast_steps):
    j = pl.program_id(0)
    i = pl.program_id(1)
    tm = xn_ref.shape[0]
    vec = lambda r, n=1: vec_ref[pl.ds(r, n), :]

    @pl.when(j * pl.num_programs(1) + i < n_cast_steps)
    def _():
        cast_out[...] = cast_ref[...].astype(BF16)

    @pl.when(i == 0)
    def _():
        rows = 256
        for r in range(0, wbf_ref.shape[0], rows):
            wbf_ref[r:r + rows, :] = w_ref[r:r + rows, :].astype(BF16)

    def proj():
        return _nt_dot(xn_ref[...], wbf_ref[...])

    @pl.when(j == SEC_U)
    def _():
        out_ref[...] = jax.nn.gelu(proj()).astype(BF16)

    @pl.when(j == SEC_V)
    def _():
        v = jax.nn.gelu(proj())
        mu = jnp.mean(v, axis=-1, keepdims=True)
        var = jnp.mean(jnp.square(v - mu), axis=-1, keepdims=True)
        out_ref[...] = ((v - mu) * lax.rsqrt(var + EPS) * vec(VEC_LN_G) + vec(VEC_LN_B)).astype(BF16)

    @pl.when(j == SEC_M)
    def _():
        @pl.when(i % tiles_per_seq == 0)
        def _():
            tail_ref[...] = jnp.zeros(tail_ref.shape, F32)

        x = proj()
        xe = jnp.concatenate([tail_ref[...], x], axis=0)
        conv = vec(VEC_CONV_B) + vec(VEC_CONV_W + CONV_K - 1) * x
        for d in range(1, CONV_K):
            conv = conv + vec(VEC_CONV_W + CONV_K - 1 - d) * xe[SUBLANES - d:SUBLANES - d + tm, :]
        tail_ref[...] = x[tm - SUBLANES:, :]
        out_ref[...] = (conv * _sigmoid(conv)).astype(BF16)

    @pl.when(j == SEC_MV)
    def _():
        vt_out[...] = _nt_dot(wbf_ref[...], xn_ref[...]).astype(BF16)

    @pl.when(j >= SEC_O)
    def _():
        out_ref[...] = _sigmoid(proj() + vec(VEC_GATE_B + j - SEC_O)).astype(BF16)


def _inproj(xn, w_t, layer, gate_row, vecs, cast_w, batch, tm):
    t, d = xn.shape
    n_i = t // tm
    seq = t // batch
    tiles_per_seq = seq // tm
    assert tiles_per_seq * tm == seq
    n_cast_steps = (N_SECTIONS - 1) * n_i
    cast_rows = _cast_rows(cast_w.shape[1], n_cast_steps)
    cast_step = lambda j, i: jnp.minimum(j * n_i + i, n_cast_steps - 1)

    def proj_block(j, i):
        mv = j == SEC_MV
        return (j - (j >= SEC_MV).astype(jnp.int32), jnp.where(mv, n_i - 1, i), 0)

    def vt_block(j, i):
        tile = jnp.clip((j - SEC_MV) * n_i + i, 0, n_i - 1)
        return (tile // tiles_per_seq, 0, tile % tiles_per_seq)

    gate_skip = gate_row - SEC_GA * SECTION
    assert gate_skip % SUBLANES == 0
    kern = functools.partial(_inproj_kernel, tiles_per_seq=tiles_per_seq, n_cast_steps=n_cast_steps)
    return pl.pallas_call(
        kern,
        grid=(N_SECTIONS, n_i),
        in_specs=[
            pl.BlockSpec((tm, d), lambda j, i: (i, 0)),
            pl.BlockSpec((None, pl.Element(SECTION), pl.Element(d)),
                         lambda j, i: (layer, SUBLANES * (j * (SECTION // SUBLANES) + jnp.where(
                             j >= SEC_GA, gate_skip // SUBLANES, 0)), 0)),
            pl.BlockSpec((N_VEC_ROWS, SECTION), lambda j, i: (0, 0)),
            pl.BlockSpec((None, cast_rows, cast_w.shape[2]), lambda j, i: (layer, cast_step(j, i), 0)),
        ],
        out_specs=[pl.BlockSpec((None, tm, SECTION), proj_block), pl.BlockSpec((None, SECTION, tm), vt_block),
                   pl.BlockSpec((cast_rows, cast_w.shape[2]), lambda j, i: (cast_step(j, i), 0))],
        out_shape=[jax.ShapeDtypeStruct((N_PROJ, t, SECTION), BF16),
                   jax.ShapeDtypeStruct((batch, SECTION, seq), BF16),
                   jax.ShapeDtypeStruct(cast_w.shape[1:], BF16)],
        scratch_shapes=[
            pltpu.VMEM((SECTION, d), BF16),
            pltpu.VMEM((SUBLANES, SECTION), F32),
        ],
        compiler_params=_params("arbitrary", "arbitrary"),
        name="inproj",
    )(xn, w_t, vecs, cast_w)


def _gmlp_kernel(u_ref, v_ref, ws_ref, bs_ref, out_ref, *, n_chunks, n_groups):
    row = lax.broadcasted_iota(jnp.int32, (CHUNK, CHUNK), 0)
    col = lax.broadcasted_iota(jnp.int32, (CHUNK, CHUNK), 1)
    causal = col <= row
    for g in range(n_groups):
        gs = slice(g * GM_GROUP_DIM, (g + 1) * GM_GROUP_DIM)
        w = jnp.where(causal, ws_ref[g], 0.0).astype(BF16)
        vg = jnp.concatenate([v_ref[c * CHUNK:(c + 1) * CHUNK, gs] for c in range(n_chunks)], axis=1)
        s = _dot(w, vg) + bs_ref[:, g:g + 1]
        for c in range(n_chunks):
            rs = slice(c * CHUNK, (c + 1) * CHUNK)
            sc = s[:, c * GM_GROUP_DIM:(c + 1) * GM_GROUP_DIM]
            out_ref[rs, gs] = (u_ref[rs, gs].astype(F32) * sc).astype(BF16)


def _gmlp(proj, ws, bs_t, tm):
    _, t, width = proj.shape
    n_groups = width // GM_GROUP_DIM
    kern = functools.partial(_gmlp_kernel, n_chunks=tm // CHUNK, n_groups=n_groups)
    return pl.pallas_call(
        kern,
        grid=(t // tm,),
        in_specs=[
            pl.BlockSpec((None, tm, width), lambda i: (PROJ_U, i, 0)),
            pl.BlockSpec((None, tm, width), lambda i: (PROJ_V, i, 0)),
            pl.BlockSpec((n_groups, CHUNK, CHUNK), lambda i: (0, 0, 0)),
            pl.BlockSpec((CHUNK, n_groups), lambda i: (0, 0)),
        ],
        out_specs=pl.BlockSpec((tm, width), lambda i: (i, 0)),
        out_shape=jax.ShapeDtypeStruct((t, width), BF16),
        compiler_params=_params("arbitrary"),
        name="gmlp",
    )(proj, proj, ws, bs_t)


def _lanes(col):
    return jnp.broadcast_to(col, (col.shape[0], LANES))


def _wide(x, n):
    return jnp.concatenate([x] * n, axis=1)


def _mlstm_kernel(co_ref, vt_ref, gcol_ref, grow_ref, bcol_ref, brow_ref, wq_ref, wk_ref, ng_ref, *rest,
                  head_dim, batch, n_cast):
    cast_in, (out_ref, *cast_out), (st_ref, mx_ref, wqb_ref, wkb_ref) = (
        rest[:n_cast], rest[n_cast:2 * n_cast + 1], rest[2 * n_cast + 1:])
    c_ref, o_ref = co_ref.at[0], co_ref.at[1]

    for src, dst in zip(cast_in, cast_out):
        dst[...] = src[...].astype(BF16)

    @pl.when(pl.program_id(0) == 0)
    def _():
        wqb_ref[...] = wq_ref[...].astype(BF16)
        wkb_ref[...] = (wk_ref[...] * (head_dim ** -0.5)).astype(BF16)
        st_ref[...] = jnp.zeros(st_ref.shape, F32)
        mx_ref[...] = jnp.zeros(mx_ref.shape, F32)

    row = lax.broadcasted_iota(jnp.int32, (CHUNK, CHUNK), 0)
    col = lax.broadcasted_iota(jnp.int32, (CHUNK, CHUNK), 1)
    causal = col <= row
    tril = causal.astype(F32)
    triu = (row <= col).astype(F32)
    ones_rows = jnp.ones((LANES, CHUNK), BF16)
    nd = head_dim // LANES
    heads = [(b, h) for b in range(batch) for h in range(ML_HEADS)]
    slot = lambda b, h: b * ML_HEADS + h
    hsl = lambda h: slice(h * head_dim, (h + 1) * head_dim)


    g_col, g_row, bcum_col, bcum_row, total = {}, {}, {}, {}, {}
    for b in range(batch):
        g_col[b] = gcol_ref[b] + brow_ref[...]
        g_row[b] = grow_ref[b] + bcol_ref[...]
        lf_row = _log_sigmoid(g_row[b])
        bcum_col[b] = jnp.dot(tril, _log_sigmoid(g_col[b]), precision=lax.Precision.HIGHEST,
                              preferred_element_type=F32)
        bcum_row[b] = jnp.dot(lf_row, triu, precision=lax.Precision.HIGHEST, preferred_element_type=F32)
        total[b] = jnp.sum(lf_row, axis=-1, keepdims=True)

    q, k = {}, {}
    for b, h in heads:
        c = c_ref[b, :, hsl(h)]
        q[b, h] = _dot(c, wqb_ref[h])
        k[b, h] = _dot(c, wkb_ref[h])

    w_intra, w_inter, inv_floor, w_state, decay, scale, m_new = {}, {}, {}, {}, {}, {}, {}
    for b, h in heads:
        f = ML_HEADS + h
        bc = _lanes(bcum_col[b][:, f:f + 1])
        ig = _lanes(g_col[b][:, h:h + 1])
        src = g_row[b][h:h + 1, :] - bcum_row[b][f:f + 1, :]
        b_last = jnp.broadcast_to(total[b][f:f + 1, :], (1, LANES))
        m_prev = mx_ref[slot(b, h):slot(b, h) + 1, :]
        inter = bc + m_prev
        dlog = jnp.where(causal, bc + src, -jnp.inf)
        m_t = jnp.maximum(inter, _lanes(jnp.max(dlog, axis=-1, keepdims=True)))
        w_intra[b, h] = jnp.exp(dlog - m_t)
        w_inter[b, h] = jnp.exp(inter - m_t)
        inv_floor[b, h] = jnp.exp(-m_t)
        a = b_last - bc + ig
        a_max = jnp.max(a, axis=0, keepdims=True)
        w_state[b, h] = jnp.exp(a - a_max)
        m_new[b, h] = jnp.maximum(b_last + m_prev, a_max)
        decay[b, h] = jnp.exp(b_last + m_prev - m_new[b, h])
        scale[b, h] = jnp.exp(a_max - m_new[b, h])

    qk = {}
    for b, h in heads:
        qk[b, h] = _nt_dot(q[b, h].astype(BF16), k[b, h].astype(BF16)) * w_intra[b, h]

    for b, h in heads:
        vt_ext = jnp.concatenate([vt_ref[b, hsl(h), :], ones_rows], axis=0)
        lhs = jnp.concatenate([(q[b, h] * _wide(w_inter[b, h], nd)).astype(BF16), qk[b, h].astype(BF16)], axis=1)
        rhs = jnp.concatenate([st_ref[slot(b, h)].astype(BF16), vt_ext], axis=1)
        ext = _nt_dot(lhs, rhs)
        den = ext[:, head_dim:]
        hid = ext[:, :head_dim] * _wide(1.0 / jnp.maximum(jnp.abs(den), inv_floor[b, h]), nd)
        mu = jnp.mean(hid, axis=-1, keepdims=True)
        var = jnp.mean(jnp.square(hid - mu), axis=-1, keepdims=True)
        hid = (hid - mu) * lax.rsqrt(var + EPS) * ng_ref[:, hsl(h)]
        out_ref[b, :, hsl(h)] = (o_ref[b, :, hsl(h)].astype(F32) * hid).astype(BF16)

    for b, h in heads:
        s = slot(b, h)
        vt_ext = jnp.concatenate([vt_ref[b, hsl(h), :], ones_rows], axis=0)
        kw = k[b, h] * _wide(w_state[b, h] * scale[b, h], nd)
        upd = _dot(vt_ext, kw.astype(BF16))
        st_ref[s] = _wide(decay[b, h], nd) * st_ref[s] + upd
        mx_ref[s:s + 1, :] = m_new[b, h]


def _mlstm(proj, vt, gates_col, gates_row, bias_col, bias_row, wq, wk, norm_g, cast_ws, layer, batch):
    _, t, width = proj.shape
    seq = t // batch
    head_dim = width // ML_HEADS
    n_if = gates_row.shape[1]
    n_steps = seq // CHUNK
    assert PROJ_O == PROJ_M + 1 and PROJ_M % 2 == 0
    proj4 = proj.reshape(proj.shape[0], batch, seq, width)
    cast_rows = [_cast_rows(w.shape[1], n_steps) for w in cast_ws]
    kern = functools.partial(_mlstm_kernel, head_dim=head_dim, batch=batch, n_cast=len(cast_ws))
    const2 = lambda n: (0, 0)
    const3 = lambda n: (0, 0, 0)
    n_state = batch * ML_HEADS
    out, *cast = pl.pallas_call(
        kern,
        grid=(n_steps,),
        in_specs=[
            pl.BlockSpec((2, batch, CHUNK, width), lambda n: (PROJ_M // 2, 0, n, 0)),
            pl.BlockSpec((batch, width, CHUNK), lambda n: (0, 0, n)),
            pl.BlockSpec((batch, CHUNK, LANES), lambda n: (0, n, 0)),
            pl.BlockSpec((batch, n_if, CHUNK), lambda n: (0, 0, n)),
            pl.BlockSpec((n_if, 1), const2),
            pl.BlockSpec((1, LANES), const2),
            pl.BlockSpec((ML_HEADS, head_dim, head_dim), const3),
            pl.BlockSpec((ML_HEADS, head_dim, head_dim), const3),
            pl.BlockSpec((1, width), const2),
        ] + [pl.BlockSpec((None, r, w.shape[2]), lambda n: (layer, n, 0)) for r, w in zip(cast_rows, cast_ws)],
        out_specs=[pl.BlockSpec((batch, CHUNK, width), lambda n: (0, n, 0))]
        + [pl.BlockSpec((r, w.shape[2]), lambda n: (n, 0)) for r, w in zip(cast_rows, cast_ws)],
        out_shape=[jax.ShapeDtypeStruct((batch, seq, width), BF16)]
        + [jax.ShapeDtypeStruct(w.shape[1:], BF16) for w in cast_ws],
        scratch_shapes=[
            pltpu.VMEM((n_state, head_dim + LANES, head_dim), F32),
            pltpu.VMEM((n_state, LANES), F32),
            pltpu.VMEM((ML_HEADS, head_dim, head_dim), BF16),
            pltpu.VMEM((ML_HEADS, head_dim, head_dim), BF16),
        ],
        compiler_params=_params("arbitrary"),
        name="mlstm",
    )(proj4, vt, gates_col.reshape(batch, seq, LANES), gates_row, bias_col, bias_row, wq, wk, norm_g, *cast_ws)
    return out.reshape(t, width), cast


def _merge_kernel(x_ref, ya_ref, yb_ref, gate_ref, wa_ref, wb_ref, wo_ref, g2_ref, x1_ref, hn_ref):
    gate_a = jnp.concatenate([gate_ref[0], gate_ref[1]], axis=1).astype(F32)
    gate_b = jnp.concatenate([gate_ref[2], gate_ref[3]], axis=1).astype(F32)
    mixed = gate_a * _dot(ya_ref[...], wa_ref[...]) + gate_b * _dot(yb_ref[...], wb_ref[...])
    x1 = x_ref[...] + _dot(mixed.astype(BF16), wo_ref[...])
    x1_ref[...] = x1
    hn_ref[...] = _rms_norm(x1, g2_ref[...]).astype(BF16)


def _merge(x, y_a, y_b, proj, w_a, w_b, w_out, g2, tm):
    t, d = x.shape
    width = y_a.shape[1]
    n_steps = t // tm
    n_gate = N_PROJ - PROJ_GA
    assert PROJ_GA % n_gate == 0
    const = lambda i: (0, 0)
    resident = dict(pipeline_mode=pl.Buffered(1))
    return pl.pallas_call(
        _merge_kernel,
        grid=(n_steps,),
        in_specs=[
            pl.BlockSpec((tm, d), lambda i: (i, 0)),
            pl.BlockSpec((tm, width), lambda i: (i, 0)),
            pl.BlockSpec((tm, width), lambda i: (i, 0)),
            pl.BlockSpec((n_gate, tm, SECTION), lambda i: (PROJ_GA // n_gate, i, 0)),
            pl.BlockSpec((width, d), const, **resident),
            pl.BlockSpec((width, d), const, **resident),
            pl.BlockSpec((d, d), const, **resident),
            pl.BlockSpec((1, d), const),
        ],
        out_specs=[pl.BlockSpec((tm, d), lambda i: (i, 0)), pl.BlockSpec((tm, d), lambda i: (i, 0))],
        out_shape=[jax.ShapeDtypeStruct((t, d), F32), jax.ShapeDtypeStruct((t, d), BF16)],
        compiler_params=_params("arbitrary"),
        name="merge",
    )(x, y_a, y_b, proj, w_a, w_b, w_out, g2)


def _ffn_kernel(hn_ref, x1_ref, w1_ref, w2_ref, gf_ref, out_ref, *, final_norm):
    j = pl.program_id(1)

    @pl.when(j == 0)
    def _():
        out_ref[...] = x1_ref[...]

    h = jnp.square(jnp.maximum(_dot(hn_ref[...], w1_ref[...]), 0.0)).astype(BF16)
    out_ref[...] += _dot(h, w2_ref[...])

    if final_norm:
        @pl.when(j == pl.num_programs(1) - 1)
        def _():
            out_ref[...] = _rms_norm(out_ref[...], gf_ref[...])


def _ffn(hn, x1, w1, w2, gf, tm, tf, final_norm):
    t, d = x1.shape
    d_ff = w1.shape[1]
    return pl.pallas_call(
        functools.partial(_ffn_kernel, final_norm=final_norm),
        grid=(t // tm, d_ff // tf),
        in_specs=[
            pl.BlockSpec((tm, d), lambda i, j: (i, 0)),
            pl.BlockSpec((tm, d), lambda i, j: (i, 0)),
            pl.BlockSpec((d, tf), lambda i, j: (0, j)),
            pl.BlockSpec((tf, d), lambda i, j: (j, 0)),
            pl.BlockSpec((1, d), lambda i, j: (0, 0)),
        ],
        out_specs=pl.BlockSpec((tm, d), lambda i, j: (i, 0)),
        out_shape=jax.ShapeDtypeStruct((t, d), F32),
        compiler_params=_params("arbitrary", "arbitrary"),
        name="ffn",
    )(hn, x1, w1, w2, gf)


def _tiles(t):
    return dict(prologue=min(t, 1024), inproj=min(t, 512), gmlp=min(t, 2048), merge=min(t, 512),
                ffn_m=min(t, 512), ffn_f=2048)


def kernel(x, norm1_g, w_in, b_gate, gm_ln_g, gm_ln_b, gm_ws, gm_bs, ml_conv_w, ml_conv_b, ml_wq, ml_wk,
           ml_ig_b, ml_fg_b, ml_norm_g, w_a, w_b, w_out, norm2_g, w_ff1, w_ff2, norm_f_g):
    batch, seq, d = x.shape
    depth = w_in.shape[0]
    t = batch * seq
    tiles = _tiles(t)
    if_row = SEC_GA * SECTION
    n_if = 2 * ML_HEADS
    xt = x.reshape(t, d)
    w_t = jnp.swapaxes(w_in, 1, 2)
    for l in range(depth):
        gate_bias = jnp.concatenate([ml_ig_b[l], ml_fg_b[l]])
        vecs = jnp.concatenate([
            gm_ln_g[l][None], gm_ln_b[l][None], ml_conv_b[l][None], ml_conv_w[l],
            jnp.zeros((1, SECTION), F32), b_gate[l].reshape(4, SECTION),
            jnp.zeros((N_VEC_ROWS - VEC_GATE_B - 5, SECTION), F32)])
        xn, gates_col, gates_row = _prologue(xt, norm1_g[l][None], w_t, l, if_row, n_if, batch,
                                             tiles["prologue"])
        proj, vt, w2_bf = _inproj(xn, w_t, l, if_row + n_if, vecs, w_ff2, batch, tiles["inproj"])
        y_a = _gmlp(proj, gm_ws[l], gm_bs[l].T, tiles["gmlp"])
        y_b, (wa_bf, wb_bf, wo_bf, w1_bf) = _mlstm(
            proj, vt, gates_col, gates_row, gate_bias[:, None],
            jnp.pad(gate_bias, (0, LANES - n_if))[None], ml_wq[l], ml_wk[l], ml_norm_g[l][None],
            (w_a, w_b, w_out, w_ff1), l, batch)
        x1, hn = _merge(xt, y_a, y_b, proj, wa_bf, wb_bf, wo_bf, norm2_g[l][None], tiles["merge"])
        xt = _ffn(hn, x1, w1_bf, w2_bf, norm_f_g[None], tiles["ffn_m"], tiles["ffn_f"],
                  final_norm=l == depth - 1)
    return xt.reshape(batch, seq, d)
```

```python
import functools

import jax
import jax.numpy as jnp
from jax import lax
from jax.experimental import pallas as pl
from jax.experimental.pallas import tpu as pltpu

EPS = 1e-6
GM_GROUP_DIM = 128
CHUNK = 128
ML_HEADS = 4
CONV_K = 4
LANES = 128
SUBLANES = 8
BF16_ROWS = 16
SECTION = 1024
SEC_U, SEC_V, SEC_M, SEC_MV, SEC_O, SEC_GA, SEC_GB = 0, 1, 2, 3, 4, 5, 7
N_SECTIONS = 9
PROJ_U, PROJ_V, PROJ_M, PROJ_O, PROJ_GA, PROJ_GB = 0, 1, 2, 3, 4, 6
N_PROJ = 8
VMEM_LIMIT = 56 * 1024 * 1024

BF16 = jnp.bfloat16
F32 = jnp.float32


def _sigmoid(x):
    return 0.5 * jnp.tanh(0.5 * x) + 0.5


def _log_sigmoid(x):
    return jnp.minimum(x, 0.0) - jnp.log(1.0 + jnp.exp(-jnp.abs(x)))


def _rms_norm(x, g):
    return x * lax.rsqrt(jnp.mean(x * x, axis=-1, keepdims=True) + EPS) * g


def _dot(a, b):
    return jnp.dot(a, b, preferred_element_type=F32)


def _params(*semantics):
    return pltpu.CompilerParams(dimension_semantics=semantics, vmem_limit_bytes=VMEM_LIMIT)


def _cast_rows(n_rows, n_steps):
    rows = n_rows // n_steps
    assert rows * n_steps == n_rows and rows % BF16_ROWS == 0, (n_rows, n_steps)
    return rows


def _nt_dot(a, b):
    return lax.dot_general(a, b, (((1,), (1,)), ((), ())), preferred_element_type=F32)


def _prologue_kernel(x_ref, g1_ref, wif_ref, xn_ref, gcol_ref, grow_ref):
    xn = _rms_norm(x_ref[...], g1_ref[...]).astype(BF16)
    xn_ref[...] = xn
    wif = wif_ref[...].astype(BF16)
    gcol_ref[...] = _nt_dot(xn, wif)
    grow_ref[...] = _nt_dot(wif, xn)[:grow_ref.shape[0], :]


def _prologue(x, g1, w_t, layer, if_row, n_if, batch, tm):
    t, d = x.shape
    seq = t // batch
    tiles_per_seq = seq // tm
    assert tiles_per_seq * tm == seq
    return pl.pallas_call(
        _prologue_kernel,
        grid=(t // tm,),
        in_specs=[
            pl.BlockSpec((tm, d), lambda i: (i, 0)),
            pl.BlockSpec((1, d), lambda i: (0, 0)),
            pl.BlockSpec((None, LANES, d), lambda i: (layer, if_row // LANES, 0)),
        ],
        out_specs=[
            pl.BlockSpec((tm, d), lambda i: (i, 0)),
            pl.BlockSpec((tm, LANES), lambda i: (i, 0)),
            pl.BlockSpec((None, n_if, tm), lambda i: (i // tiles_per_seq, 0, i % tiles_per_seq)),
        ],
        out_shape=[
            jax.ShapeDtypeStruct((t, d), BF16),
            jax.ShapeDtypeStruct((t, LANES), F32),
            jax.ShapeDtypeStruct((batch, n_if, seq), F32),
        ],
        compiler_params=_params("arbitrary"),
        name="prologue",
    )(x, g1, w_t)


VEC_LN_G, VEC_LN_B, VEC_CONV_B, VEC_CONV_W, VEC_GATE_B = 0, 1, 2, 3, 7
N_VEC_ROWS = 16


def _inproj_kernel(xn_ref, w_ref, vec_ref, *rest, tiles_per_seq, n_cast_steps, n_cast):
    cast_in, (out_ref, vt_out, *cast_out), (wbf_ref, tail_ref) = (
        rest[:n_cast], rest[n_cast:2 * n_cast + 2], rest[2 * n_cast + 2:])
    j = pl.program_id(0)
    i = pl.program_id(1)
    tm = xn_ref.shape[0]
    vec = lambda r, n=1: vec_ref[pl.ds(r, n), :]

    @pl.when(j * pl.num_programs(1) + i < n_cast_steps)
    def _():
        for src, dst in zip(cast_in, cast_out):
            dst[...] = src[...].astype(BF16)

    @pl.when(i == 0)
    def _():
        rows = 256
        for r in range(0, wbf_ref.shape[0], rows):
            wbf_ref[r:r + rows, :] = w_ref[r:r + rows, :].astype(BF16)

    def proj():
        return _nt_dot(xn_ref[...], wbf_ref[...])

    @pl.when(j == SEC_U)
    def _():
        out_ref[...] = jax.nn.gelu(proj()).astype(BF16)

    @pl.when(j == SEC_V)
    def _():
        v = jax.nn.gelu(proj())
        mu = jnp.mean(v, axis=-1, keepdims=True)
        var = jnp.mean(jnp.square(v - mu), axis=-1, keepdims=True)
        out_ref[...] = ((v - mu) * lax.rsqrt(var + EPS) * vec(VEC_LN_G) + vec(VEC_LN_B)).astype(BF16)

    @pl.when(j == SEC_M)
    def _():
        @pl.when(i % tiles_per_seq == 0)
        def _():
            tail_ref[...] = jnp.zeros(tail_ref.shape, F32)

        x = proj()
        xe = jnp.concatenate([tail_ref[...], x], axis=0)
        conv = vec(VEC_CONV_B) + vec(VEC_CONV_W + CONV_K - 1) * x
        for d in range(1, CONV_K):
            conv = conv + vec(VEC_CONV_W + CONV_K - 1 - d) * xe[SUBLANES - d:SUBLANES - d + tm, :]
        tail_ref[...] = x[tm - SUBLANES:, :]
        out_ref[...] = (conv * _sigmoid(conv)).astype(BF16)

    @pl.when(j == SEC_MV)
    def _():
        vt_out[...] = _nt_dot(wbf_ref[...], xn_ref[...]).astype(BF16)

    @pl.when(j >= SEC_O)
    def _():
        out_ref[...] = _sigmoid(proj() + vec(VEC_GATE_B + j - SEC_O)).astype(BF16)


def _inproj(xn, w_t, layer, gate_row, vecs, cast_ws, batch, tm):
    t, d = xn.shape
    n_i = t // tm
    seq = t // batch
    tiles_per_seq = seq // tm
    assert tiles_per_seq * tm == seq
    n_cast_steps = (N_SECTIONS - 1) * n_i
    cast_rows = [_cast_rows(w.shape[1], n_cast_steps) for w in cast_ws]
    cast_step = lambda j, i: jnp.minimum(j * n_i + i, n_cast_steps - 1)

    def proj_block(j, i):
        mv = j == SEC_MV
        return (j - (j >= SEC_MV).astype(jnp.int32), jnp.where(mv, n_i - 1, i), 0)

    def vt_block(j, i):
        tile = jnp.clip((j - SEC_MV) * n_i + i, 0, n_i - 1)
        return (tile // tiles_per_seq, 0, tile % tiles_per_seq)

    gate_skip = gate_row - SEC_GA * SECTION
    assert gate_skip % SUBLANES == 0
    kern = functools.partial(_inproj_kernel, tiles_per_seq=tiles_per_seq, n_cast_steps=n_cast_steps,
                             n_cast=len(cast_ws))
    return pl.pallas_call(
        kern,
        grid=(N_SECTIONS, n_i),
        in_specs=[
            pl.BlockSpec((tm, d), lambda j, i: (i, 0)),
            pl.BlockSpec((None, pl.Element(SECTION), pl.Element(d)),
                         lambda j, i: (layer, SUBLANES * (j * (SECTION // SUBLANES) + jnp.where(
                             j >= SEC_GA, gate_skip // SUBLANES, 0)), 0)),
            pl.BlockSpec((N_VEC_ROWS, SECTION), lambda j, i: (0, 0)),
        ] + [pl.BlockSpec((None, r, w.shape[2]), lambda j, i: (layer, cast_step(j, i), 0))
             for r, w in zip(cast_rows, cast_ws)],
        out_specs=[pl.BlockSpec((None, tm, SECTION), proj_block), pl.BlockSpec((None, SECTION, tm), vt_block)]
        + [pl.BlockSpec((r, w.shape[2]), lambda j, i: (cast_step(j, i), 0)) for r, w in zip(cast_rows, cast_ws)],
        out_shape=[jax.ShapeDtypeStruct((N_PROJ, t, SECTION), BF16),
                   jax.ShapeDtypeStruct((batch, SECTION, seq), BF16)]
        + [jax.ShapeDtypeStruct(w.shape[1:], BF16) for w in cast_ws],
        scratch_shapes=[
            pltpu.VMEM((SECTION, d), BF16),
            pltpu.VMEM((SUBLANES, SECTION), F32),
        ],
        compiler_params=_params("arbitrary", "arbitrary"),
        name="inproj",
    )(xn, w_t, vecs, *cast_ws)


def _gmlp_kernel(u_ref, v_ref, ws_ref, bs_ref, out_ref, *, n_chunks, n_groups):
    row = lax.broadcasted_iota(jnp.int32, (CHUNK, CHUNK), 0)
    col = lax.broadcasted_iota(jnp.int32, (CHUNK, CHUNK), 1)
    causal = col <= row
    for g in range(n_groups):
        gs = slice(g * GM_GROUP_DIM, (g + 1) * GM_GROUP_DIM)
        w = jnp.where(causal, ws_ref[g], 0.0).astype(BF16)
        vg = jnp.concatenate([v_ref[c * CHUNK:(c + 1) * CHUNK, gs] for c in range(n_chunks)], axis=1)
        s = _dot(w, vg) + bs_ref[:, g:g + 1]
        for c in range(n_chunks):
            rs = slice(c * CHUNK, (c + 1) * CHUNK)
            sc = s[:, c * GM_GROUP_DIM:(c + 1) * GM_GROUP_DIM]
            out_ref[rs, gs] = (u_ref[rs, gs].astype(F32) * sc).astype(BF16)


def _gmlp(proj, ws, bs_t, tm):
    _, t, width = proj.shape
    n_groups = width // GM_GROUP_DIM
    kern = functools.partial(_gmlp_kernel, n_chunks=tm // CHUNK, n_groups=n_groups)
    return pl.pallas_call(
        kern,
        grid=(t // tm,),
        in_specs=[
            pl.BlockSpec((None, tm, width), lambda i: (PROJ_U, i, 0)),
            pl.BlockSpec((None, tm, width), lambda i: (PROJ_V, i, 0)),
            pl.BlockSpec((n_groups, CHUNK, CHUNK), lambda i: (0, 0, 0)),
            pl.BlockSpec((CHUNK, n_groups), lambda i: (0, 0)),
        ],
        out_specs=pl.BlockSpec((tm, width), lambda i: (i, 0)),
        out_shape=jax.ShapeDtypeStruct((t, width), BF16),
        compiler_params=_params("arbitrary"),
        name="gmlp",
    )(proj, proj, ws, bs_t)


def _lanes(col):
    return jnp.broadcast_to(col, (col.shape[0], LANES))


def _wide(x, n):
    return jnp.concatenate([x] * n, axis=1)


def _mlstm_chunk(c_ref, o_ref, vt_ref, gcol_ref, grow_ref, bcol_ref, brow_ref, ng_ref,
                 st_ref, mx_ref, wqb_ref, wkb_ref, yb_ref, *, head_dim, batch):
    row = lax.broadcasted_iota(jnp.int32, (CHUNK, CHUNK), 0)
    col = lax.broadcasted_iota(jnp.int32, (CHUNK, CHUNK), 1)
    causal = col <= row
    tril = causal.astype(F32)
    triu = (row <= col).astype(F32)
    ones_rows = jnp.ones((LANES, CHUNK), BF16)
    nd = head_dim // LANES
    heads = [(b, h) for b in range(batch) for h in range(ML_HEADS)]
    slot = lambda b, h: b * ML_HEADS + h
    hsl = lambda h: slice(h * head_dim, (h + 1) * head_dim)


    g_col, g_row, bcum_col, bcum_row, total = {}, {}, {}, {}, {}
    for b in range(batch):
        g_col[b] = gcol_ref[b] + brow_ref[...]
        g_row[b] = grow_ref[b] + bcol_ref[...]
        lf_row = _log_sigmoid(g_row[b])
        bcum_col[b] = jnp.dot(tril, _log_sigmoid(g_col[b]), precision=lax.Precision.HIGHEST,
                              preferred_element_type=F32)
        bcum_row[b] = jnp.dot(lf_row, triu, precision=lax.Precision.HIGHEST, preferred_element_type=F32)
        total[b] = jnp.sum(lf_row, axis=-1, keepdims=True)

    q, k = {}, {}
    for b, h in heads:
        c = c_ref[b, :, hsl(h)]
        q[b, h] = _dot(c, wqb_ref[h])
        k[b, h] = _dot(c, wkb_ref[h])

    w_intra, w_inter, inv_floor, w_state, decay, scale, m_new = {}, {}, {}, {}, {}, {}, {}
    for b, h in heads:
        f = ML_HEADS + h
        bc = _lanes(bcum_col[b][:, f:f + 1])
        ig = _lanes(g_col[b][:, h:h + 1])
        src = g_row[b][h:h + 1, :] - bcum_row[b][f:f + 1, :]
        b_last = jnp.broadcast_to(total[b][f:f + 1, :], (1, LANES))
        m_prev = mx_ref[slot(b, h):slot(b, h) + 1, :]
        inter = bc + m_prev
        dlog = jnp.where(causal, bc + src, -jnp.inf)
        m_t = jnp.maximum(inter, _lanes(jnp.max(dlog, axis=-1, keepdims=True)))
        w_intra[b, h] = jnp.exp(dlog - m_t)
        w_inter[b, h] = jnp.exp(inter - m_t)
        inv_floor[b, h] = jnp.exp(-m_t)
        a = b_last - bc + ig
        a_max = jnp.max(a, axis=0, keepdims=True)
        w_state[b, h] = jnp.exp(a - a_max)
        m_new[b, h] = jnp.maximum(b_last + m_prev, a_max)
        decay[b, h] = jnp.exp(b_last + m_prev - m_new[b, h])
        scale[b, h] = jnp.exp(a_max - m_new[b, h])

    qk = {}
    for b, h in heads:
        qk[b, h] = _nt_dot(q[b, h].astype(BF16), k[b, h].astype(BF16)) * w_intra[b, h]

    for b, h in heads:
        vt_ext = jnp.concatenate([vt_ref[b, hsl(h), :], ones_rows], axis=0)
        lhs = jnp.concatenate([(q[b, h] * _wide(w_inter[b, h], nd)).astype(BF16), qk[b, h].astype(BF16)], axis=1)
        rhs = jnp.concatenate([st_ref[slot(b, h)].astype(BF16), vt_ext], axis=1)
        ext = _nt_dot(lhs, rhs)
        den = ext[:, head_dim:]
        hid = ext[:, :head_dim] * _wide(1.0 / jnp.maximum(jnp.abs(den), inv_floor[b, h]), nd)
        mu = jnp.mean(hid, axis=-1, keepdims=True)
        var = jnp.mean(jnp.square(hid - mu), axis=-1, keepdims=True)
        hid = (hid - mu) * lax.rsqrt(var + EPS) * ng_ref[:, hsl(h)]
        yb_ref[b, :, hsl(h)] = (o_ref[b, :, hsl(h)].astype(F32) * hid).astype(BF16)

    for b, h in heads:
        s = slot(b, h)
        vt_ext = jnp.concatenate([vt_ref[b, hsl(h), :], ones_rows], axis=0)
        kw = k[b, h] * _wide(w_state[b, h] * scale[b, h], nd)
        upd = _dot(vt_ext, kw.astype(BF16))
        st_ref[s] = _wide(decay[b, h], nd) * st_ref[s] + upd
        mx_ref[s:s + 1, :] = m_new[b, h]


def _mix_kernel(co_ref, vt_ref, gcol_ref, grow_ref, bcol_ref, brow_ref, wq_ref, wk_ref, ng_ref,
                x_ref, ya_ref, gate_ref, wa_ref, wb_ref, wo_ref, g2_ref, cast_ref,
                x1_ref, hn_ref, cast_out, st_ref, mx_ref, wqb_ref, wkb_ref, yb_ref,
                *, head_dim, batch, n_steps):
    n = pl.program_id(0)

    @pl.when(n == 0)
    def _():
        wqb_ref[...] = wq_ref[...].astype(BF16)
        wkb_ref[...] = (wk_ref[...] * (head_dim ** -0.5)).astype(BF16)
        st_ref[...] = jnp.zeros(st_ref.shape, F32)
        mx_ref[...] = jnp.zeros(mx_ref.shape, F32)

    @pl.when(n < n_steps)
    def _():
        cast_out[...] = cast_ref[...].astype(BF16)

    def mlstm(slot):
        _mlstm_chunk(co_ref.at[0], co_ref.at[1], vt_ref, gcol_ref, grow_ref, bcol_ref, brow_ref, ng_ref,
                     st_ref, mx_ref, wqb_ref, wkb_ref, yb_ref.at[slot], head_dim=head_dim, batch=batch)

    def merge(slot):
        flat = lambda v: v.reshape(batch * CHUNK, v.shape[-1])
        gate_a = jnp.concatenate([flat(gate_ref[0]), flat(gate_ref[1])], axis=1).astype(F32)
        gate_b = jnp.concatenate([flat(gate_ref[2]), flat(gate_ref[3])], axis=1).astype(F32)
        mixed = (gate_a * _dot(flat(ya_ref[...]), wa_ref[...])
                 + gate_b * _dot(flat(yb_ref[slot]), wb_ref[...]))
        x1 = flat(x_ref[...]) + _dot(mixed.astype(BF16), wo_ref[...])
        x1_ref[...] = x1.reshape(x1_ref.shape)
        hn_ref[...] = _rms_norm(x1, g2_ref[...]).astype(BF16).reshape(hn_ref.shape)

    @pl.when(n == 0)
    def _():
        mlstm(0)

    for parity in range(2):
        @pl.when((n > 0) & (n < n_steps) & (n % 2 == parity))
        def _(parity=parity):
            mlstm(parity)
            merge(1 - parity)

    @pl.when(n == n_steps)
    def _():
        merge((n_steps - 1) % 2)


def _mix(proj, vt, gates_col, gates_row, bias_col, bias_row, wq, wk, norm_g,
         x, y_a, w_a, w_b, w_out, g2, cast_w, layer, batch):
    _, t, width = proj.shape
    seq = t // batch
    d = x.shape[-1]
    head_dim = width // ML_HEADS
    n_if = gates_row.shape[1]
    n_steps = seq // CHUNK
    n_gate = N_PROJ - PROJ_GA
    assert PROJ_O == PROJ_M + 1 and PROJ_M % 2 == 0 and PROJ_GA % n_gate == 0
    proj4 = proj.reshape(proj.shape[0], batch, seq, width)
    cast_rows = _cast_rows(cast_w.shape[1], n_steps)
    at = lambda n: jnp.minimum(n, n_steps - 1)
    done = lambda n: jnp.maximum(n - 1, 0)
    const2 = lambda n: (0, 0)
    const3 = lambda n: (0, 0, 0)
    resident = dict(pipeline_mode=pl.Buffered(1))
    n_state = batch * ML_HEADS
    kern = functools.partial(_mix_kernel, head_dim=head_dim, batch=batch, n_steps=n_steps)
    x1, hn, cast = pl.pallas_call(
        kern,
        grid=(n_steps + 1,),
        in_specs=[
            pl.BlockSpec((2, batch, CHUNK, width), lambda n: (PROJ_M // 2, 0, at(n), 0)),
            pl.BlockSpec((batch, width, CHUNK), lambda n: (0, 0, at(n))),
            pl.BlockSpec((batch, CHUNK, LANES), lambda n: (0, at(n), 0)),
            pl.BlockSpec((batch, n_if, CHUNK), lambda n: (0, 0, at(n))),
            pl.BlockSpec((n_if, 1), const2),
            pl.BlockSpec((1, LANES), const2),
            pl.BlockSpec((ML_HEADS, head_dim, head_dim), const3),
            pl.BlockSpec((ML_HEADS, head_dim, head_dim), const3),
            pl.BlockSpec((1, width), const2),
            pl.BlockSpec((batch, CHUNK, d), lambda n: (0, done(n), 0)),
            pl.BlockSpec((batch, CHUNK, width), lambda n: (0, done(n), 0)),
            pl.BlockSpec((n_gate, batch, CHUNK, SECTION), lambda n: (PROJ_GA // n_gate, 0, done(n), 0)),
            pl.BlockSpec((width, d), const2, **resident),
            pl.BlockSpec((width, d), const2, **resident),
            pl.BlockSpec((d, d), const2, **resident),
            pl.BlockSpec((1, d), const2),
            pl.BlockSpec((None, cast_rows, cast_w.shape[2]), lambda n: (layer, at(n), 0)),
        ],
        out_specs=[
            pl.BlockSpec((batch, CHUNK, d), lambda n: (0, done(n), 0)),
            pl.BlockSpec((batch, CHUNK, d), lambda n: (0, done(n), 0)),
            pl.BlockSpec((cast_rows, cast_w.shape[2]), lambda n: (at(n), 0)),
        ],
        out_shape=[
            jax.ShapeDtypeStruct((batch, seq, d), F32),
            jax.ShapeDtypeStruct((batch, seq, d), BF16),
            jax.ShapeDtypeStruct(cast_w.shape[1:], BF16),
        ],
        scratch_shapes=[
            pltpu.VMEM((n_state, head_dim + LANES, head_dim), F32),
            pltpu.VMEM((n_state, LANES), F32),
            pltpu.VMEM((ML_HEADS, head_dim, head_dim), BF16),
            pltpu.VMEM((ML_HEADS, head_dim, head_dim), BF16),
            pltpu.VMEM((2, batch, CHUNK, width), BF16),
        ],
        compiler_params=_params("arbitrary"),
        name="mix",
    )(proj4, vt, gates_col.reshape(batch, seq, LANES), gates_row, bias_col, bias_row, wq, wk, norm_g,
      x, y_a.reshape(batch, seq, width), proj4, w_a, w_b, w_out, g2, cast_w)
    return x1.reshape(t, d), hn.reshape(t, d), cast


def _ffn_kernel(hn_ref, x1_ref, w1_ref, w2_ref, gf_ref, out_ref, *, final_norm):
    j = pl.program_id(1)

    @pl.when(j == 0)
    def _():
        out_ref[...] = x1_ref[...]

    h = jnp.square(jnp.maximum(_dot(hn_ref[...], w1_ref[...]), 0.0)).astype(BF16)
    out_ref[...] += _dot(h, w2_ref[...])

    if final_norm:
        @pl.when(j == pl.num_programs(1) - 1)
        def _():
            out_ref[...] = _rms_norm(out_ref[...], gf_ref[...])


def _ffn(hn, x1, w1, w2, gf, tm, tf, final_norm):
    t, d = x1.shape
    d_ff = w1.shape[1]
    return pl.pallas_call(
        functools.partial(_ffn_kernel, final_norm=final_norm),
        grid=(t // tm, d_ff // tf),
        in_specs=[
            pl.BlockSpec((tm, d), lambda i, j: (i, 0)),
            pl.BlockSpec((tm, d), lambda i, j: (i, 0)),
            pl.BlockSpec((d, tf), lambda i, j: (0, j)),
            pl.BlockSpec((tf, d), lambda i, j: (j, 0)),
            pl.BlockSpec((1, d), lambda i, j: (0, 0)),
        ],
        out_specs=pl.BlockSpec((tm, d), lambda i, j: (i, 0)),
        out_shape=jax.ShapeDtypeStruct((t, d), F32),
        compiler_params=_params("arbitrary", "arbitrary"),
        name="ffn",
    )(hn, x1, w1, w2, gf)


def _tiles(t):
    return dict(prologue=min(t, 1024), inproj=min(t, 1024), gmlp=min(t, 1024), ffn_m=min(t, 512), ffn_f=2048)


def kernel(x, norm1_g, w_in, b_gate, gm_ln_g, gm_ln_b, gm_ws, gm_bs, ml_conv_w, ml_conv_b, ml_wq, ml_wk,
           ml_ig_b, ml_fg_b, ml_norm_g, w_a, w_b, w_out, norm2_g, w_ff1, w_ff2, norm_f_g):
    batch, seq, d = x.shape
    depth = w_in.shape[0]
    t = batch * seq
    tiles = _tiles(t)
    if_row = SEC_GA * SECTION
    n_if = 2 * ML_HEADS
    xt = x.reshape(t, d)
    w_t = jnp.swapaxes(w_in, 1, 2)
    for l in range(depth):
        gate_bias = jnp.concatenate([ml_ig_b[l], ml_fg_b[l]])
        vecs = jnp.concatenate([
            gm_ln_g[l][None], gm_ln_b[l][None], ml_conv_b[l][None], ml_conv_w[l],
            jnp.zeros((1, SECTION), F32), b_gate[l].reshape(4, SECTION),
            jnp.zeros((N_VEC_ROWS - VEC_GATE_B - 5, SECTION), F32)])
        xn, gates_col, gates_row = _prologue(xt, norm1_g[l][None], w_t, l, if_row, n_if, batch,
                                             tiles["prologue"])
        proj, vt, w2_bf, wa_bf, wb_bf, wo_bf = _inproj(
            xn, w_t, l, if_row + n_if, vecs, (w_ff2, w_a, w_b, w_out), batch, tiles["inproj"])
        y_a = _gmlp(proj, gm_ws[l], gm_bs[l].T, tiles["gmlp"])
        x1, hn, w1_bf = _mix(
            proj, vt, gates_col, gates_row, gate_bias[:, None], jnp.pad(gate_bias, (0, LANES - n_if))[None],
            ml_wq[l], ml_wk[l], ml_norm_g[l][None], xt.reshape(batch, seq, d), y_a, wa_bf, wb_bf, wo_bf,
            norm2_g[l][None],
            w_ff1, l, batch)
        xt = _ffn(hn, x1, w1_bf, w2_bf, norm_f_g[None], tiles["ffn_m"], tiles["ffn_f"],
                  final_norm=l == depth - 1)
    return xt.reshape(batch, seq, d)
```

```python
import functools

import jax
import jax.numpy as jnp
from jax import lax
from jax.experimental import pallas as pl
from jax.experimental.pallas import tpu as pltpu

EPS = 1e-6
GM_GROUP_DIM = 128
CHUNK = 128
ML_HEADS = 4
CONV_K = 4
LANES = 128
SUBLANES = 8
BF16_ROWS = 16
SECTION = 1024
SEC_U, SEC_V, SEC_M, SEC_MV, SEC_O, SEC_GA, SEC_GB = 0, 1, 2, 3, 4, 5, 7
N_SECTIONS = 9
PROJ_U, PROJ_V, PROJ_M, PROJ_O, PROJ_GA, PROJ_GB = 0, 1, 2, 3, 4, 6
N_PROJ = 8
VMEM_LIMIT = 56 * 1024 * 1024

BF16 = jnp.bfloat16
F32 = jnp.float32


def _sigmoid(x):
    return 0.5 * jnp.tanh(0.5 * x) + 0.5


def _log_sigmoid(x):
    return jnp.minimum(x, 0.0) - jnp.log(1.0 + jnp.exp(-jnp.abs(x)))


def _rms_norm(x, g):
    return x * lax.rsqrt(jnp.mean(x * x, axis=-1, keepdims=True) + EPS) * g


def _dot(a, b):
    return jnp.dot(a, b, preferred_element_type=F32)


def _params(*semantics):
    return pltpu.CompilerParams(dimension_semantics=semantics, vmem_limit_bytes=VMEM_LIMIT)


def _cast_rows(n_rows, n_steps):
    rows = n_rows // n_steps
    assert rows * n_steps == n_rows and rows % BF16_ROWS == 0, (n_rows, n_steps)
    return rows


def _nt_dot(a, b):
    return lax.dot_general(a, b, (((1,), (1,)), ((), ())), preferred_element_type=F32)


def _prologue_kernel(x_ref, g1_ref, wif_ref, xn_ref, gcol_ref, grow_ref):
    xn = _rms_norm(x_ref[...], g1_ref[...]).astype(BF16)
    xn_ref[...] = xn
    wif = wif_ref[...].astype(BF16)
    gcol_ref[...] = _nt_dot(xn, wif)
    grow_ref[...] = _nt_dot(wif, xn)[:grow_ref.shape[0], :]


def _prologue(x, g1, w_t, layer, if_row, n_if, batch, tm):
    t, d = x.shape
    seq = t // batch
    tiles_per_seq = seq // tm
    assert tiles_per_seq * tm == seq
    return pl.pallas_call(
        _prologue_kernel,
        grid=(t // tm,),
        in_specs=[
            pl.BlockSpec((tm, d), lambda i: (i, 0)),
            pl.BlockSpec((1, d), lambda i: (0, 0)),
            pl.BlockSpec((None, LANES, d), lambda i: (layer, if_row // LANES, 0)),
        ],
        out_specs=[
            pl.BlockSpec((tm, d), lambda i: (i, 0)),
            pl.BlockSpec((tm, LANES), lambda i: (i, 0)),
            pl.BlockSpec((None, n_if, tm), lambda i: (i // tiles_per_seq, 0, i % tiles_per_seq)),
        ],
        out_shape=[
            jax.ShapeDtypeStruct((t, d), BF16),
            jax.ShapeDtypeStruct((t, LANES), F32),
            jax.ShapeDtypeStruct((batch, n_if, seq), F32),
        ],
        compiler_params=_params("arbitrary"),
        name="prologue",
    )(x, g1, w_t)


VEC_LN_G, VEC_LN_B, VEC_CONV_B, VEC_CONV_W, VEC_GATE_B = 0, 1, 2, 3, 7
N_VEC_ROWS = 16


def _inproj_kernel(xn_ref, w_ref, vec_ref, *rest, tiles_per_seq, n_cast_steps, n_cast):
    cast_in, (out_ref, vt_out, *cast_out), (wbf_ref, tail_ref) = (
        rest[:n_cast], rest[n_cast:2 * n_cast + 2], rest[2 * n_cast + 2:])
    j = pl.program_id(0)
    i = pl.program_id(1)
    tm = xn_ref.shape[0]
    vec = lambda r, n=1: vec_ref[pl.ds(r, n), :]

    @pl.when(j * pl.num_programs(1) + i < n_cast_steps)
    def _():
        for src, dst in zip(cast_in, cast_out):
            dst[...] = src[...].astype(BF16)

    @pl.when(i == 0)
    def _():
        rows = 256
        for r in range(0, wbf_ref.shape[0], rows):
            wbf_ref[r:r + rows, :] = w_ref[r:r + rows, :].astype(BF16)

    def proj():
        return _nt_dot(xn_ref[...], wbf_ref[...])

    @pl.when(j == SEC_U)
    def _():
        out_ref[...] = jax.nn.gelu(proj()).astype(BF16)

    @pl.when(j == SEC_V)
    def _():
        v = jax.nn.gelu(proj())
        mu = jnp.mean(v, axis=-1, keepdims=True)
        var = jnp.mean(jnp.square(v - mu), axis=-1, keepdims=True)
        out_ref[...] = ((v - mu) * lax.rsqrt(var + EPS) * vec(VEC_LN_G) + vec(VEC_LN_B)).astype(BF16)

    @pl.when(j == SEC_M)
    def _():
        @pl.when(i % tiles_per_seq == 0)
        def _():
            tail_ref[...] = jnp.zeros(tail_ref.shape, F32)

        x = proj()
        xe = jnp.concatenate([tail_ref[...], x], axis=0)
        conv = vec(VEC_CONV_B) + vec(VEC_CONV_W + CONV_K - 1) * x
        for d in range(1, CONV_K):
            conv = conv + vec(VEC_CONV_W + CONV_K - 1 - d) * xe[SUBLANES - d:SUBLANES - d + tm, :]
        tail_ref[...] = x[tm - SUBLANES:, :]
        out_ref[...] = (conv * _sigmoid(conv)).astype(BF16)

    @pl.when(j == SEC_MV)
    def _():
        vt_out[...] = _nt_dot(wbf_ref[...], xn_ref[...]).astype(BF16)

    @pl.when(j >= SEC_O)
    def _():
        out_ref[...] = _sigmoid(proj() + vec(VEC_GATE_B + j - SEC_O)).astype(BF16)


def _inproj(xn, w_t, layer, gate_row, vecs, cast_ws, batch, tm):
    t, d = xn.shape
    n_i = t // tm
    seq = t // batch
    tiles_per_seq = seq // tm
    assert tiles_per_seq * tm == seq
    n_cast_steps = (N_SECTIONS - 1) * n_i
    cast_rows = [_cast_rows(w.shape[1], n_cast_steps) for w in cast_ws]
    cast_step = lambda j, i: jnp.minimum(j * n_i + i, n_cast_steps - 1)

    def proj_block(j, i):
        mv = j == SEC_MV
        return (j - (j >= SEC_MV).astype(jnp.int32), jnp.where(mv, n_i - 1, i), 0)

    def vt_block(j, i):
        tile = jnp.clip((j - SEC_MV) * n_i + i, 0, n_i - 1)
        return (tile // tiles_per_seq, 0, tile % tiles_per_seq)

    gate_skip = gate_row - SEC_GA * SECTION
    assert gate_skip % SUBLANES == 0
    kern = functools.partial(_inproj_kernel, tiles_per_seq=tiles_per_seq, n_cast_steps=n_cast_steps,
                             n_cast=len(cast_ws))
    return pl.pallas_call(
        kern,
        grid=(N_SECTIONS, n_i),
        in_specs=[
            pl.BlockSpec((tm, d), lambda j, i: (i, 0)),
            pl.BlockSpec((None, pl.Element(SECTION), pl.Element(d)),
                         lambda j, i: (layer, SUBLANES * (j * (SECTION // SUBLANES) + jnp.where(
                             j >= SEC_GA, gate_skip // SUBLANES, 0)), 0)),
            pl.BlockSpec((N_VEC_ROWS, SECTION), lambda j, i: (0, 0)),
        ] + [pl.BlockSpec((None, r, w.shape[2]), lambda j, i: (layer, cast_step(j, i), 0))
             for r, w in zip(cast_rows, cast_ws)],
        out_specs=[pl.BlockSpec((None, tm, SECTION), proj_block), pl.BlockSpec((None, SECTION, tm), vt_block)]
        + [pl.BlockSpec((r, w.shape[2]), lambda j, i: (cast_step(j, i), 0)) for r, w in zip(cast_rows, cast_ws)],
        out_shape=[jax.ShapeDtypeStruct((N_PROJ, t, SECTION), BF16),
                   jax.ShapeDtypeStruct((batch, SECTION, seq), BF16)]
        + [jax.ShapeDtypeStruct(w.shape[1:], BF16) for w in cast_ws],
        scratch_shapes=[
            pltpu.VMEM((SECTION, d), BF16),
            pltpu.VMEM((SUBLANES, SECTION), F32),
        ],
        compiler_params=_params("arbitrary", "arbitrary"),
        name="inproj",
    )(xn, w_t, vecs, *cast_ws)


def _gmlp_kernel(u_ref, v_ref, ws_ref, bs_ref, out_ref, *, n_chunks, n_groups):
    row = lax.broadcasted_iota(jnp.int32, (CHUNK, CHUNK), 0)
    col = lax.broadcasted_iota(jnp.int32, (CHUNK, CHUNK), 1)
    causal = col <= row
    for g in range(n_groups):
        gs = slice(g * GM_GROUP_DIM, (g + 1) * GM_GROUP_DIM)
        w = jnp.where(causal, ws_ref[g], 0.0).astype(BF16)
        vg = jnp.concatenate([v_ref[c * CHUNK:(c + 1) * CHUNK, gs] for c in range(n_chunks)], axis=1)
        s = _dot(w, vg) + bs_ref[:, g:g + 1]
        for c in range(n_chunks):
            rs = slice(c * CHUNK, (c + 1) * CHUNK)
            sc = s[:, c * GM_GROUP_DIM:(c + 1) * GM_GROUP_DIM]
            out_ref[rs, gs] = (u_ref[rs, gs].astype(F32) * sc).astype(BF16)


def _gmlp(proj, ws, bs_t, tm):
    _, t, width = proj.shape
    n_groups = width // GM_GROUP_DIM
    kern = functools.partial(_gmlp_kernel, n_chunks=tm // CHUNK, n_groups=n_groups)
    return pl.pallas_call(
        kern,
        grid=(t // tm,),
        in_specs=[
            pl.BlockSpec((None, tm, width), lambda i: (PROJ_U, i, 0)),
            pl.BlockSpec((None, tm, width), lambda i: (PROJ_V, i, 0)),
            pl.BlockSpec((n_groups, CHUNK, CHUNK), lambda i: (0, 0, 0)),
            pl.BlockSpec((CHUNK, n_groups), lambda i: (0, 0)),
        ],
        out_specs=pl.BlockSpec((tm, width), lambda i: (i, 0)),
        out_shape=jax.ShapeDtypeStruct((t, width), BF16),
        compiler_params=_params("arbitrary"),
        name="gmlp",
    )(proj, proj, ws, bs_t)


def _lanes(col):
    return jnp.broadcast_to(col, (col.shape[0], LANES))


def _wide(x, n):
    return jnp.concatenate([x] * n, axis=1)


def _mlstm_chunk(c_ref, o_ref, vt_ref, gcol_ref, grow_ref, bcol_ref, brow_ref, ng_ref, zeros_ref,
                 st_ref, mx_ref, wqb_ref, wkb_ref, yb_ref, *, head_dim, batch):
    row = lax.broadcasted_iota(jnp.int32, (CHUNK, CHUNK), 0)
    col = lax.broadcasted_iota(jnp.int32, (CHUNK, CHUNK), 1)
    causal = col <= row
    tril = causal.astype(F32)
    triu = (row <= col).astype(F32)
    ones_rows = jnp.ones((LANES, CHUNK), BF16)
    nd = head_dim // LANES
    heads = [(b, h) for b in range(batch) for h in range(ML_HEADS)]
    slot = lambda b, h: b * ML_HEADS + h
    hsl = lambda h: slice(h * head_dim, (h + 1) * head_dim)


    g_col, g_row, bcum_col, bcum_row, total = {}, {}, {}, {}, {}
    for b in range(batch):
        g_col[b] = gcol_ref[b] + brow_ref[...]
        g_row[b] = grow_ref[b] + bcol_ref[...]
        lf_row = _log_sigmoid(g_row[b])
        bcum_col[b] = jnp.dot(tril, _log_sigmoid(g_col[b]), precision=lax.Precision.HIGHEST,
                              preferred_element_type=F32)
        bcum_row[b] = jnp.dot(lf_row, triu, precision=lax.Precision.HIGHEST, preferred_element_type=F32)
        total[b] = jnp.sum(lf_row, axis=-1, keepdims=True)

    q, k = {}, {}
    for b, h in heads:
        c = c_ref[b, :, hsl(h)]
        q[b, h] = _dot(c, wqb_ref[h])
        k[b, h] = _dot(c, wkb_ref[h])

    w_intra, w_inter, inv_floor, w_state, decay, scale, m_new = {}, {}, {}, {}, {}, {}, {}
    for b, h in heads:
        f = ML_HEADS + h
        bc = _lanes(bcum_col[b][:, f:f + 1])
        ig = _lanes(g_col[b][:, h:h + 1])
        src = g_row[b][h:h + 1, :] - bcum_row[b][f:f + 1, :]
        b_last = jnp.broadcast_to(total[b][f:f + 1, :], (1, LANES))
        m_prev = mx_ref[slot(b, h):slot(b, h) + 1, :]
        inter = bc + m_prev
        dlog = jnp.where(causal, bc + src, -jnp.inf)
        m_t = jnp.maximum(inter, _lanes(jnp.max(dlog, axis=-1, keepdims=True)))
        w_intra[b, h] = jnp.exp(dlog - m_t)
        w_inter[b, h] = jnp.exp(inter - m_t)
        inv_floor[b, h] = jnp.exp(-m_t)
        a = b_last - bc + ig
        a_max = jnp.max(a, axis=0, keepdims=True)
        w_state[b, h] = jnp.exp(a - a_max)
        m_new[b, h] = jnp.maximum(b_last + m_prev, a_max)
        decay[b, h] = jnp.exp(b_last + m_prev - m_new[b, h])
        scale[b, h] = jnp.exp(a_max - m_new[b, h])

    qk = {}
    for b, h in heads:
        qk[b, h] = _nt_dot(q[b, h].astype(BF16), k[b, h].astype(BF16)) * w_intra[b, h]

    token = jnp.zeros((SUBLANES, LANES), F32)
    for b, h in heads:
        vt_ext = jnp.concatenate([vt_ref[b, hsl(h), :], ones_rows], axis=0)
        lhs = jnp.concatenate([(q[b, h] * _wide(w_inter[b, h], nd)).astype(BF16), qk[b, h].astype(BF16)], axis=1)
        rhs = jnp.concatenate([st_ref[slot(b, h)].astype(BF16), vt_ext], axis=1)
        ext = _nt_dot(lhs, rhs)
        den = ext[:, head_dim:]
        hid = ext[:, :head_dim] * _wide(1.0 / jnp.maximum(jnp.abs(den), inv_floor[b, h]), nd)
        mu = jnp.mean(hid, axis=-1, keepdims=True)
        var = jnp.mean(jnp.square(hid - mu), axis=-1, keepdims=True)
        hid = (hid - mu) * lax.rsqrt(var + EPS) * ng_ref[:, hsl(h)]
        yb_ref[b, :, hsl(h)] = (o_ref[b, :, hsl(h)].astype(F32) * hid).astype(BF16)
        token = token + hid[0:SUBLANES, 0:LANES]

    for b, h in heads:
        s = slot(b, h)
        vt_ext = jnp.concatenate([vt_ref[b, hsl(h), :], ones_rows], axis=0)
        kw = k[b, h] * _wide(w_state[b, h] * scale[b, h], nd)
        upd = _dot(vt_ext, kw.astype(BF16))
        st_ref[s] = _wide(decay[b, h], nd) * st_ref[s] + upd
        mx_ref[s:s + 1, :] = m_new[b, h]
        token = token + upd[0:SUBLANES, 0:LANES]
    return pltpu.bitcast(pltpu.bitcast(token, jnp.uint32) & zeros_ref[...], F32)[0:1, :]


def _mix_kernel(co_ref, vt_ref, gcol_ref, grow_ref, bcol_ref, brow_ref, wq_ref, wk_ref, ng_ref,
                x_ref, ya_ref, gate_ref, wa_ref, wb_ref, wo_ref, g2_ref, cast_ref, zeros_ref,
                x1_ref, hn_ref, cast_out, st_ref, mx_ref, wqb_ref, wkb_ref, yb_ref,
                *, head_dim, batch, n_steps):
    n = pl.program_id(0)

    @pl.when(n == 0)
    def _():
        wqb_ref[...] = wq_ref[...].astype(BF16)
        wkb_ref[...] = (wk_ref[...] * (head_dim ** -0.5)).astype(BF16)
        st_ref[...] = jnp.zeros(st_ref.shape, F32)
        mx_ref[...] = jnp.zeros(mx_ref.shape, F32)

    @pl.when(n < n_steps)
    def _():
        cast_out[...] = cast_ref[...].astype(BF16)

    def mlstm(slot):
        return _mlstm_chunk(co_ref.at[0], co_ref.at[1], vt_ref, gcol_ref, grow_ref, bcol_ref, brow_ref, ng_ref,
                            zeros_ref, st_ref, mx_ref, wqb_ref, wkb_ref, yb_ref.at[slot], head_dim=head_dim, batch=batch)

    def merge(slot, zero=None):
        flat = lambda v: v.reshape(batch * CHUNK, v.shape[-1])
        gate_a = jnp.concatenate([flat(gate_ref[0]), flat(gate_ref[1])], axis=1).astype(F32)
        if zero is not None:
            gate_a = gate_a + jnp.concatenate([zero] * (gate_a.shape[1] // LANES), axis=1)
        gate_b = jnp.concatenate([flat(gate_ref[2]), flat(gate_ref[3])], axis=1).astype(F32)
        mixed = (gate_a * _dot(flat(ya_ref[...]), wa_ref[...])
                 + gate_b * _dot(flat(yb_ref[slot]), wb_ref[...]))
        x1 = flat(x_ref[...]) + _dot(mixed.astype(BF16), wo_ref[...])
        x1_ref[...] = x1.reshape(x1_ref.shape)
        hn_ref[...] = _rms_norm(x1, g2_ref[...]).astype(BF16).reshape(hn_ref.shape)

    @pl.when(n == 0)
    def _():
        mlstm(0)

    for parity in range(2):
        @pl.when((n > 0) & (n < n_steps) & (n % 2 == parity))
        def _(parity=parity):
            merge(1 - parity, zero=mlstm(parity))

    @pl.when(n == n_steps)
    def _():
        merge((n_steps - 1) % 2)


def _mix(proj, vt, gates_col, gates_row, bias_col, bias_row, wq, wk, norm_g,
         x, y_a, w_a, w_b, w_out, g2, cast_w, layer, batch):
    _, t, width = proj.shape
    seq = t // batch
    d = x.shape[-1]
    head_dim = width // ML_HEADS
    n_if = gates_row.shape[1]
    n_steps = seq // CHUNK
    n_gate = N_PROJ - PROJ_GA
    assert PROJ_O == PROJ_M + 1 and PROJ_M % 2 == 0 and PROJ_GA % n_gate == 0
    proj4 = proj.reshape(proj.shape[0], batch, seq, width)
    cast_rows = _cast_rows(cast_w.shape[1], n_steps)
    at = lambda n: jnp.minimum(n, n_steps - 1)
    done = lambda n: jnp.maximum(n - 1, 0)
    const2 = lambda n: (0, 0)
    const3 = lambda n: (0, 0, 0)
    resident = dict(pipeline_mode=pl.Buffered(1))
    n_state = batch * ML_HEADS
    kern = functools.partial(_mix_kernel, head_dim=head_dim, batch=batch, n_steps=n_steps)
    x1, hn, cast = pl.pallas_call(
        kern,
        grid=(n_steps + 1,),
        in_specs=[
            pl.BlockSpec((2, batch, CHUNK, width), lambda n: (PROJ_M // 2, 0, at(n), 0)),
            pl.BlockSpec((batch, width, CHUNK), lambda n: (0, 0, at(n))),
            pl.BlockSpec((batch, CHUNK, LANES), lambda n: (0, at(n), 0)),
            pl.BlockSpec((batch, n_if, CHUNK), lambda n: (0, 0, at(n))),
            pl.BlockSpec((n_if, 1), const2),
            pl.BlockSpec((1, LANES), const2),
            pl.BlockSpec((ML_HEADS, head_dim, head_dim), const3),
            pl.BlockSpec((ML_HEADS, head_dim, head_dim), const3),
            pl.BlockSpec((1, width), const2),
            pl.BlockSpec((batch, CHUNK, d), lambda n: (0, done(n), 0)),
            pl.BlockSpec((batch, CHUNK, width), lambda n: (0, done(n), 0)),
            pl.BlockSpec((n_gate, batch, CHUNK, SECTION), lambda n: (PROJ_GA // n_gate, 0, done(n), 0)),
            pl.BlockSpec((width, d), const2, **resident),
            pl.BlockSpec((width, d), const2, **resident),
            pl.BlockSpec((d, d), const2, **resident),
            pl.BlockSpec((1, d), const2),
            pl.BlockSpec((None, cast_rows, cast_w.shape[2]), lambda n: (layer, at(n), 0)),
            pl.BlockSpec((SUBLANES, LANES), const2),
        ],
        out_specs=[
            pl.BlockSpec((batch, CHUNK, d), lambda n: (0, done(n), 0)),
            pl.BlockSpec((batch, CHUNK, d), lambda n: (0, done(n), 0)),
            pl.BlockSpec((cast_rows, cast_w.shape[2]), lambda n: (at(n), 0)),
        ],
        out_shape=[
            jax.ShapeDtypeStruct((batch, seq, d), F32),
            jax.ShapeDtypeStruct((batch, seq, d), BF16),
            jax.ShapeDtypeStruct(cast_w.shape[1:], BF16),
        ],
        scratch_shapes=[
            pltpu.VMEM((n_state, head_dim + LANES, head_dim), F32),
            pltpu.VMEM((n_state, LANES), F32),
            pltpu.VMEM((ML_HEADS, head_dim, head_dim), BF16),
            pltpu.VMEM((ML_HEADS, head_dim, head_dim), BF16),
            pltpu.VMEM((2, batch, CHUNK, width), BF16),
        ],
        compiler_params=_params("arbitrary"),
        name="mix",
    )(proj4, vt, gates_col.reshape(batch, seq, LANES), gates_row, bias_col, bias_row, wq, wk, norm_g,
      x, y_a.reshape(batch, seq, width), proj4, w_a, w_b, w_out, g2, cast_w,
      jnp.zeros((SUBLANES, LANES), jnp.uint32))
    return x1.reshape(t, d), hn.reshape(t, d), cast


def _ffn_kernel(hn_ref, x1_ref, w1_ref, w2_ref, gf_ref, out_ref, *, final_norm):
    j = pl.program_id(1)

    @pl.when(j == 0)
    def _():
        out_ref[...] = x1_ref[...]

    h = jnp.square(jnp.maximum(_dot(hn_ref[...], w1_ref[...]), 0.0)).astype(BF16)
    out_ref[...] += _dot(h, w2_ref[...])

    if final_norm:
        @pl.when(j == pl.num_programs(1) - 1)
        def _():
            out_ref[...] = _rms_norm(out_ref[...], gf_ref[...])


def _ffn(hn, x1, w1, w2, gf, tm, tf, final_norm):
    t, d = x1.shape
    d_ff = w1.shape[1]
    return pl.pallas_call(
        functools.partial(_ffn_kernel, final_norm=final_norm),
        grid=(t // tm, d_ff // tf),
        in_specs=[
            pl.BlockSpec((tm, d), lambda i, j: (i, 0)),
            pl.BlockSpec((tm, d), lambda i, j: (i, 0)),
            pl.BlockSpec((d, tf), lambda i, j: (0, j)),
            pl.BlockSpec((tf, d), lambda i, j: (j, 0)),
            pl.BlockSpec((1, d), lambda i, j: (0, 0)),
        ],
        out_specs=pl.BlockSpec((tm, d), lambda i, j: (i, 0)),
        out_shape=jax.ShapeDtypeStruct((t, d), F32),
        compiler_params=_params("arbitrary", "arbitrary"),
        name="ffn",
    )(hn, x1, w1, w2, gf)


def _tiles(t):
    return dict(prologue=min(t, 1024), inproj=min(t, 1024), gmlp=min(t, 1024), ffn_m=min(t, 512), ffn_f=2048)


def kernel(x, norm1_g, w_in, b_gate, gm_ln_g, gm_ln_b, gm_ws, gm_bs, ml_conv_w, ml_conv_b, ml_wq, ml_wk,
           ml_ig_b, ml_fg_b, ml_norm_g, w_a, w_b, w_out, norm2_g, w_ff1, w_ff2, norm_f_g):
    batch, seq, d = x.shape
    depth = w_in.shape[0]
    t = batch * seq
    tiles = _tiles(t)
    if_row = SEC_GA * SECTION
    n_if = 2 * ML_HEADS
    xt = x.reshape(t, d)
    w_t = jnp.swapaxes(w_in, 1, 2)
    for l in range(depth):
        gate_bias = jnp.concatenate([ml_ig_b[l], ml_fg_b[l]])
        vecs = jnp.concatenate([
            gm_ln_g[l][None], gm_ln_b[l][None], ml_conv_b[l][None], ml_conv_w[l],
            jnp.zeros((1, SECTION), F32), b_gate[l].reshape(4, SECTION),
            jnp.zeros((N_VEC_ROWS - VEC_GATE_B - 5, SECTION), F32)])
        xn, gates_col, gates_row = _prologue(xt, norm1_g[l][None], w_t, l, if_row, n_if, batch,
                                             tiles["prologue"])
        proj, vt, w2_bf, wa_bf, wb_bf, wo_bf = _inproj(
            xn, w_t, l, if_row + n_if, vecs, (w_ff2, w_a, w_b, w_out), batch, tiles["inproj"])
        y_a = _gmlp(proj, gm_ws[l], gm_bs[l].T, tiles["gmlp"])
        x1, hn, w1_bf = _mix(
            proj, vt, gates_col, gates_row, gate_bias[:, None], jnp.pad(gate_bias, (0, LANES - n_if))[None],
            ml_wq[l], ml_wk[l], ml_norm_g[l][None], xt.reshape(batch, seq, d), y_a, wa_bf, wb_bf, wo_bf,
            norm2_g[l][None],
            w_ff1, l, batch)
        xt = _ffn(hn, x1, w1_bf, w2_bf, norm_f_g[None], tiles["ffn_m"], tiles["ffn_f"],
                  final_norm=l == depth - 1)
    return xt.reshape(batch, seq, d)
```

```python
import functools

import jax
import jax.numpy as jnp
from jax import lax
from jax.experimental import pallas as pl
from jax.experimental.pallas import tpu as pltpu

EPS = 1e-6
GM_GROUP_DIM = 128
CHUNK = 128
ML_HEADS = 4
CONV_K = 4
LANES = 128
SUBLANES = 8
BF16_ROWS = 16
SECTION = 1024
SEC_U, SEC_V, SEC_M, SEC_MV, SEC_O, SEC_GA, SEC_GB = 0, 1, 2, 3, 4, 5, 7
N_SECTIONS = 9
PROJ_U, PROJ_V, PROJ_M, PROJ_O, PROJ_GA, PROJ_GB = 0, 1, 2, 3, 4, 6
N_PROJ = 8
VMEM_LIMIT = 56 * 1024 * 1024

BF16 = jnp.bfloat16
F32 = jnp.float32


def _sigmoid(x):
    return 0.5 * jnp.tanh(0.5 * x) + 0.5


def _log_sigmoid(x):
    return jnp.minimum(x, 0.0) - jnp.log(1.0 + jnp.exp(-jnp.abs(x)))


def _rms_norm(x, g):
    return x * lax.rsqrt(jnp.mean(x * x, axis=-1, keepdims=True) + EPS) * g


def _dot(a, b):
    return jnp.dot(a, b, preferred_element_type=F32)


def _split3(x):
    hi = x.astype(BF16)
    rest = x - hi.astype(F32)
    mid = rest.astype(BF16)
    return hi, mid, (rest - mid.astype(F32)).astype(BF16)


def _params(*semantics):
    return pltpu.CompilerParams(dimension_semantics=semantics, vmem_limit_bytes=VMEM_LIMIT)


def _cast_rows(n_rows, n_steps):
    rows = n_rows // n_steps
    assert rows * n_steps == n_rows and rows % BF16_ROWS == 0, (n_rows, n_steps)
    return rows


def _nt_dot(a, b):
    return lax.dot_general(a, b, (((1,), (1,)), ((), ())), preferred_element_type=F32)


def _prologue_kernel(x_ref, g1_ref, wif_ref, xn_ref, gcol_ref, grow_ref):
    xn = _rms_norm(x_ref[...], g1_ref[...]).astype(BF16)
    xn_ref[...] = xn
    wif = wif_ref[...].astype(BF16)
    gcol_ref[...] = _nt_dot(xn, wif)
    grow_ref[...] = _nt_dot(wif, xn)[:grow_ref.shape[0], :]


def _prologue(x, g1, w_t, layer, if_row, n_if, batch, tm):
    t, d = x.shape
    seq = t // batch
    tiles_per_seq = seq // tm
    assert tiles_per_seq * tm == seq
    return pl.pallas_call(
        _prologue_kernel,
        grid=(t // tm,),
        in_specs=[
            pl.BlockSpec((tm, d), lambda i: (i, 0)),
            pl.BlockSpec((1, d), lambda i: (0, 0)),
            pl.BlockSpec((None, LANES, d), lambda i: (layer, if_row // LANES, 0)),
        ],
        out_specs=[
            pl.BlockSpec((tm, d), lambda i: (i, 0)),
            pl.BlockSpec((tm, LANES), lambda i: (i, 0)),
            pl.BlockSpec((None, n_if, tm), lambda i: (i // tiles_per_seq, 0, i % tiles_per_seq)),
        ],
        out_shape=[
            jax.ShapeDtypeStruct((t, d), BF16),
            jax.ShapeDtypeStruct((t, LANES), F32),
            jax.ShapeDtypeStruct((batch, n_if, seq), F32),
        ],
        compiler_params=_params("arbitrary"),
        name="prologue",
    )(x, g1, w_t)


VEC_LN_G, VEC_LN_B, VEC_CONV_B, VEC_CONV_W, VEC_GATE_B = 0, 1, 2, 3, 7
N_VEC_ROWS = 16


def _inproj_kernel(xn_ref, w_ref, vec_ref, *rest, tiles_per_seq, n_cast_steps, n_cast):
    cast_in, (out_ref, vt_out, *cast_out), (wbf_ref, tail_ref) = (
        rest[:n_cast], rest[n_cast:2 * n_cast + 2], rest[2 * n_cast + 2:])
    j = pl.program_id(0)
    i = pl.program_id(1)
    tm = xn_ref.shape[0]
    vec = lambda r, n=1: vec_ref[pl.ds(r, n), :]

    @pl.when(j * pl.num_programs(1) + i < n_cast_steps)
    def _():
        for src, dst in zip(cast_in, cast_out):
            dst[...] = src[...].astype(BF16)

    @pl.when(i == 0)
    def _():
        rows = 256
        for r in range(0, wbf_ref.shape[0], rows):
            wbf_ref[r:r + rows, :] = w_ref[r:r + rows, :].astype(BF16)

    def proj():
        return _nt_dot(xn_ref[...], wbf_ref[...])

    @pl.when(j == SEC_U)
    def _():
        out_ref[...] = jax.nn.gelu(proj()).astype(BF16)

    @pl.when(j == SEC_V)
    def _():
        v = jax.nn.gelu(proj())
        mu = jnp.mean(v, axis=-1, keepdims=True)
        var = jnp.mean(jnp.square(v - mu), axis=-1, keepdims=True)
        out_ref[...] = ((v - mu) * lax.rsqrt(var + EPS) * vec(VEC_LN_G) + vec(VEC_LN_B)).astype(BF16)

    @pl.when(j == SEC_M)
    def _():
        @pl.when(i % tiles_per_seq == 0)
        def _():
            tail_ref[...] = jnp.zeros(tail_ref.shape, F32)

        x = proj()
        xe = jnp.concatenate([tail_ref[...], x], axis=0)
        conv = vec(VEC_CONV_B) + vec(VEC_CONV_W + CONV_K - 1) * x
        for d in range(1, CONV_K):
            conv = conv + vec(VEC_CONV_W + CONV_K - 1 - d) * xe[SUBLANES - d:SUBLANES - d + tm, :]
        tail_ref[...] = x[tm - SUBLANES:, :]
        out_ref[...] = (conv * _sigmoid(conv)).astype(BF16)

    @pl.when(j == SEC_MV)
    def _():
        vt_out[...] = _nt_dot(wbf_ref[...], xn_ref[...]).astype(BF16)

    @pl.when(j >= SEC_O)
    def _():
        out_ref[...] = _sigmoid(proj() + vec(VEC_GATE_B + j - SEC_O)).astype(BF16)


def _inproj(xn, w_t, layer, gate_row, vecs, cast_ws, batch, tm):
    t, d = xn.shape
    n_i = t // tm
    seq = t // batch
    tiles_per_seq = seq // tm
    assert tiles_per_seq * tm == seq
    n_cast_steps = (N_SECTIONS - 1) * n_i
    cast_rows = [_cast_rows(w.shape[1], n_cast_steps) for w in cast_ws]
    cast_step = lambda j, i: jnp.minimum(j * n_i + i, n_cast_steps - 1)

    def proj_block(j, i):
        mv = j == SEC_MV
        return (j - (j >= SEC_MV).astype(jnp.int32), jnp.where(mv, n_i - 1, i), 0)

    def vt_block(j, i):
        tile = jnp.clip((j - SEC_MV) * n_i + i, 0, n_i - 1)
        return (tile // tiles_per_seq, 0, tile % tiles_per_seq)

    gate_skip = gate_row - SEC_GA * SECTION
    assert gate_skip % SUBLANES == 0
    kern = functools.partial(_inproj_kernel, tiles_per_seq=tiles_per_seq, n_cast_steps=n_cast_steps,
                             n_cast=len(cast_ws))
    return pl.pallas_call(
        kern,
        grid=(N_SECTIONS, n_i),
        in_specs=[
            pl.BlockSpec((tm, d), lambda j, i: (i, 0)),
            pl.BlockSpec((None, pl.Element(SECTION), pl.Element(d)),
                         lambda j, i: (layer, SUBLANES * (j * (SECTION // SUBLANES) + jnp.where(
                             j >= SEC_GA, gate_skip // SUBLANES, 0)), 0)),
            pl.BlockSpec((N_VEC_ROWS, SECTION), lambda j, i: (0, 0)),
        ] + [pl.BlockSpec((None, r, w.shape[2]), lambda j, i: (layer, cast_step(j, i), 0))
             for r, w in zip(cast_rows, cast_ws)],
        out_specs=[pl.BlockSpec((None, tm, SECTION), proj_block), pl.BlockSpec((None, SECTION, tm), vt_block)]
        + [pl.BlockSpec((r, w.shape[2]), lambda j, i: (cast_step(j, i), 0)) for r, w in zip(cast_rows, cast_ws)],
        out_shape=[jax.ShapeDtypeStruct((N_PROJ, t, SECTION), BF16),
                   jax.ShapeDtypeStruct((batch, SECTION, seq), BF16)]
        + [jax.ShapeDtypeStruct(w.shape[1:], BF16) for w in cast_ws],
        scratch_shapes=[
            pltpu.VMEM((SECTION, d), BF16),
            pltpu.VMEM((SUBLANES, SECTION), F32),
        ],
        compiler_params=_params("arbitrary", "arbitrary"),
        name="inproj",
    )(xn, w_t, vecs, *cast_ws)


def _gmlp_kernel(u_ref, v_ref, ws_ref, bs_ref, out_ref, *, n_chunks, n_groups):
    row = lax.broadcasted_iota(jnp.int32, (CHUNK, CHUNK), 0)
    col = lax.broadcasted_iota(jnp.int32, (CHUNK, CHUNK), 1)
    causal = col <= row
    for g in range(n_groups):
        gs = slice(g * GM_GROUP_DIM, (g + 1) * GM_GROUP_DIM)
        w = jnp.where(causal, ws_ref[g], 0.0).astype(BF16)
        vg = jnp.concatenate([v_ref[c * CHUNK:(c + 1) * CHUNK, gs] for c in range(n_chunks)], axis=1)
        s = _dot(w, vg) + bs_ref[:, g:g + 1]
        for c in range(n_chunks):
            rs = slice(c * CHUNK, (c + 1) * CHUNK)
            sc = s[:, c * GM_GROUP_DIM:(c + 1) * GM_GROUP_DIM]
            out_ref[rs, gs] = (u_ref[rs, gs].astype(F32) * sc).astype(BF16)


def _gmlp(proj, ws, bs_t, tm):
    _, t, width = proj.shape
    n_groups = width // GM_GROUP_DIM
    kern = functools.partial(_gmlp_kernel, n_chunks=tm // CHUNK, n_groups=n_groups)
    return pl.pallas_call(
        kern,
        grid=(t // tm,),
        in_specs=[
            pl.BlockSpec((None, tm, width), lambda i: (PROJ_U, i, 0)),
            pl.BlockSpec((None, tm, width), lambda i: (PROJ_V, i, 0)),
            pl.BlockSpec((n_groups, CHUNK, CHUNK), lambda i: (0, 0, 0)),
            pl.BlockSpec((CHUNK, n_groups), lambda i: (0, 0)),
        ],
        out_specs=pl.BlockSpec((tm, width), lambda i: (i, 0)),
        out_shape=jax.ShapeDtypeStruct((t, width), BF16),
        compiler_params=_params("arbitrary"),
        name="gmlp",
    )(proj, proj, ws, bs_t)


def _lanes(col):
    return jnp.broadcast_to(col, (col.shape[0], LANES))


def _wide(x, n):
    return jnp.concatenate([x] * n, axis=1)


def _mlstm_chunk(c_ref, o_ref, vt_ref, gcol_ref, grow_ref, bcol_ref, brow_ref, ng_ref,
                 st_ref, mx_ref, wqb_ref, wkb_ref, yb_ref, *, head_dim, batch):
    row = lax.broadcasted_iota(jnp.int32, (CHUNK, CHUNK), 0)
    col = lax.broadcasted_iota(jnp.int32, (CHUNK, CHUNK), 1)
    causal = col <= row
    tril = causal.astype(BF16)
    triu = (row <= col).astype(BF16)
    nd = head_dim // LANES
    groups = CHUNK // SUBLANES
    heads = [(b, h) for b in range(batch) for h in range(ML_HEADS)]
    slot = lambda b, h: b * ML_HEADS + h
    hsl = lambda h: slice(h * head_dim, (h + 1) * head_dim)


    n_if = grow_ref.shape[1]
    lane = lax.broadcasted_iota(jnp.int32, (CHUNK, LANES), 1)
    g_col = gcol_ref[0] + brow_ref[...]
    for b in range(1, batch):
        g_col = jnp.where(lane < b * n_if, g_col, pltpu.roll(gcol_ref[b] + brow_ref[...], b * n_if, 1))
    g_row = jnp.concatenate([grow_ref[b] + bcol_ref[...] for b in range(batch)], axis=0)
    lf_row = _log_sigmoid(g_row)
    bcum_col = _dot(jnp.concatenate([tril] * 3, axis=1), jnp.concatenate(_split3(_log_sigmoid(g_col)), axis=0))
    bcum_row = _dot(jnp.concatenate(_split3(lf_row), axis=1), jnp.concatenate([triu] * 3, axis=0))
    total = jnp.sum(lf_row, axis=-1, keepdims=True)

    q, k = {}, {}
    for h in range(ML_HEADS):
        c = jnp.concatenate([c_ref[b, :, hsl(h)] for b in range(batch)], axis=0)
        q_h, k_h = _dot(c, wqb_ref[h]), _dot(c, wkb_ref[h])
        for b in range(batch):
            q[b, h], k[b, h] = q_h[b * CHUNK:(b + 1) * CHUNK], k_h[b * CHUNK:(b + 1) * CHUNK]

    w_intra, w_inter, inv_floor, w_state, decay, scale, m_new = {}, {}, {}, {}, {}, {}, {}
    for b, h in heads:
        i, f = b * n_if + h, b * n_if + ML_HEADS + h
        bc = _lanes(bcum_col[:, f:f + 1])
        ig = _lanes(g_col[:, i:i + 1])
        src = g_row[i:i + 1, :] - bcum_row[f:f + 1, :]
        b_last = jnp.broadcast_to(total[f:f + 1, :], (1, LANES))
        m_prev = mx_ref[slot(b, h):slot(b, h) + 1, :]
        inter = bc + m_prev
        dlog = jnp.where(causal, bc + src, -jnp.inf)
        m_t = jnp.maximum(inter, _lanes(jnp.max(dlog, axis=-1, keepdims=True)))
        w_intra[b, h] = jnp.exp(dlog - m_t)
        w_inter[b, h] = jnp.exp(inter - m_t)
        inv_floor[b, h] = jnp.exp(-m_t)
        a = b_last - bc + ig
        a_max = jnp.max(a, axis=0, keepdims=True)
        w_state[b, h] = jnp.exp(a - a_max)
        m_new[b, h] = jnp.maximum(b_last + m_prev, a_max)
        decay[b, h] = jnp.exp(b_last + m_prev - m_new[b, h])
        scale[b, h] = jnp.exp(a_max - m_new[b, h])

    qk = {}
    for b, h in heads:
        qk[b, h] = _nt_dot(q[b, h].astype(BF16), k[b, h].astype(BF16)) * w_intra[b, h]

    for b, h in heads:
        s = slot(b, h)
        qw = q[b, h] * _wide(w_inter[b, h], nd)
        lhs = jnp.concatenate([qw.astype(BF16), qk[b, h].astype(BF16)], axis=1)
        rhs = jnp.concatenate([st_ref[s, :head_dim, :].astype(BF16), vt_ref[b, hsl(h), :]], axis=1)
        num = _nt_dot(lhs, rhs)
        qn = (qw.reshape(groups, SUBLANES, head_dim) * st_ref[s, head_dim:, :]).reshape(CHUNK, head_dim)
        part = qk[b, h] + sum(qn[:, i * LANES:(i + 1) * LANES] for i in range(nd))
        den = _lanes(jnp.sum(part, axis=-1, keepdims=True))
        hid = num * _wide(1.0 / jnp.maximum(jnp.abs(den), inv_floor[b, h]), nd)
        mu = jnp.mean(hid, axis=-1, keepdims=True)
        var = jnp.mean(jnp.square(hid - mu), axis=-1, keepdims=True)
        hid = (hid - mu) * lax.rsqrt(var + EPS) * ng_ref[:, hsl(h)]
        yb_ref[b, :, hsl(h)] = (o_ref[b, :, hsl(h)].astype(F32) * hid).astype(BF16)

    for b, h in heads:
        s = slot(b, h)
        kw = k[b, h] * _wide(w_state[b, h] * scale[b, h], nd)
        upd = _dot(vt_ref[b, hsl(h), :], kw.astype(BF16))
        kw_sum = jnp.sum(jnp.sum(kw.reshape(groups, SUBLANES, head_dim), axis=0), axis=0, keepdims=True)
        keep = _wide(decay[b, h], nd)
        st_ref[s, :head_dim, :] = keep * st_ref[s, :head_dim, :] + upd
        st_ref[s, head_dim:, :] = keep * st_ref[s, head_dim:, :] + kw_sum
        mx_ref[s:s + 1, :] = m_new[b, h]


def _mix_kernel(co_ref, vt_ref, gcol_ref, grow_ref, bcol_ref, brow_ref, wq_ref, wk_ref, ng_ref,
                x_ref, ya_ref, gate_ref, wa_ref, wb_ref, wo_ref, g2_ref, cast_ref,
                x1_ref, hn_ref, cast_out, st_ref, mx_ref, wqb_ref, wkb_ref, yb_ref,
                *, head_dim, batch, n_steps):
    n = pl.program_id(0)

    @pl.when(n == 0)
    def _():
        wqb_ref[...] = wq_ref[...].astype(BF16)
        wkb_ref[...] = (wk_ref[...] * (head_dim ** -0.5)).astype(BF16)
        st_ref[...] = jnp.zeros(st_ref.shape, F32)
        mx_ref[...] = jnp.zeros(mx_ref.shape, F32)

    @pl.when(n < n_steps)
    def _():
        cast_out[...] = cast_ref[...].astype(BF16)

    def mlstm(slot):
        _mlstm_chunk(co_ref.at[0], co_ref.at[1], vt_ref, gcol_ref, grow_ref, bcol_ref, brow_ref, ng_ref,
                     st_ref, mx_ref, wqb_ref, wkb_ref, yb_ref.at[slot], head_dim=head_dim, batch=batch)

    def merge(slot):
        flat = lambda v: v.reshape(batch * CHUNK, v.shape[-1])
        gate_a = jnp.concatenate([flat(gate_ref[0]), flat(gate_ref[1])], axis=1).astype(F32)
        gate_b = jnp.concatenate([flat(gate_ref[2]), flat(gate_ref[3])], axis=1).astype(F32)
        mixed = (gate_a * _dot(flat(ya_ref[...]), wa_ref[...])
                 + gate_b * _dot(flat(yb_ref[slot]), wb_ref[...]))
        x1 = flat(x_ref[...]) + _dot(mixed.astype(BF16), wo_ref[...])
        x1_ref[...] = x1.reshape(x1_ref.shape)
        hn_ref[...] = _rms_norm(x1, g2_ref[...]).astype(BF16).reshape(hn_ref.shape)

    @pl.when(n == 0)
    def _():
        mlstm(0)

    for parity in range(2):
        @pl.when((n > 0) & (n < n_steps) & (n % 2 == parity))
        def _(parity=parity):
            mlstm(parity)
            merge(1 - parity)

    @pl.when(n == n_steps)
    def _():
        merge((n_steps - 1) % 2)


def _mix(proj, vt, gates_col, gates_row, bias_col, bias_row, wq, wk, norm_g,
         x, y_a, w_a, w_b, w_out, g2, cast_w, layer, batch):
    _, t, width = proj.shape
    seq = t // batch
    d = x.shape[-1]
    head_dim = width // ML_HEADS
    n_if = gates_row.shape[1]
    n_steps = seq // CHUNK
    n_gate = N_PROJ - PROJ_GA
    assert PROJ_O == PROJ_M + 1 and PROJ_M % 2 == 0 and PROJ_GA % n_gate == 0
    assert batch * n_if <= LANES, (batch, n_if)
    proj4 = proj.reshape(proj.shape[0], batch, seq, width)
    cast_rows = _cast_rows(cast_w.shape[1], n_steps)
    at = lambda n: jnp.minimum(n, n_steps - 1)
    done = lambda n: jnp.maximum(n - 1, 0)
    const2 = lambda n: (0, 0)
    const3 = lambda n: (0, 0, 0)
    resident = dict(pipeline_mode=pl.Buffered(1))
    n_state = batch * ML_HEADS
    kern = functools.partial(_mix_kernel, head_dim=head_dim, batch=batch, n_steps=n_steps)
    x1, hn, cast = pl.pallas_call(
        kern,
        grid=(n_steps + 1,),
        in_specs=[
            pl.BlockSpec((2, batch, CHUNK, width), lambda n: (PROJ_M // 2, 0, at(n), 0)),
            pl.BlockSpec((batch, width, CHUNK), lambda n: (0, 0, at(n))),
            pl.BlockSpec((batch, CHUNK, LANES), lambda n: (0, at(n), 0)),
            pl.BlockSpec((batch, n_if, CHUNK), lambda n: (0, 0, at(n))),
            pl.BlockSpec((n_if, 1), const2),
            pl.BlockSpec((1, LANES), const2),
            pl.BlockSpec((ML_HEADS, head_dim, head_dim), const3),
            pl.BlockSpec((ML_HEADS, head_dim, head_dim), const3),
            pl.BlockSpec((1, width), const2),
            pl.BlockSpec((batch, CHUNK, d), lambda n: (0, done(n), 0)),
            pl.BlockSpec((batch, CHUNK, width), lambda n: (0, done(n), 0)),
            pl.BlockSpec((n_gate, batch, CHUNK, SECTION), lambda n: (PROJ_GA // n_gate, 0, done(n), 0)),
            pl.BlockSpec((width, d), const2, **resident),
            pl.BlockSpec((width, d), const2, **resident),
            pl.BlockSpec((d, d), const2, **resident),
            pl.BlockSpec((1, d), const2),
            pl.BlockSpec((None, cast_rows, cast_w.shape[2]), lambda n: (layer, at(n), 0)),
        ],
        out_specs=[
            pl.BlockSpec((batch, CHUNK, d), lambda n: (0, done(n), 0)),
            pl.BlockSpec((batch, CHUNK, d), lambda n: (0, done(n), 0)),
            pl.BlockSpec((cast_rows, cast_w.shape[2]), lambda n: (at(n), 0)),
        ],
        out_shape=[
            jax.ShapeDtypeStruct((batch, seq, d), F32),
            jax.ShapeDtypeStruct((batch, seq, d), BF16),
            jax.ShapeDtypeStruct(cast_w.shape[1:], BF16),
        ],
        scratch_shapes=[
            pltpu.VMEM((n_state, head_dim + SUBLANES, head_dim), F32),
            pltpu.VMEM((n_state, LANES), F32),
            pltpu.VMEM((ML_HEADS, head_dim, head_dim), BF16),
            pltpu.VMEM((ML_HEADS, head_dim, head_dim), BF16),
            pltpu.VMEM((2, batch, CHUNK, width), BF16),
        ],
        compiler_params=_params("arbitrary"),
        name="mix",
    )(proj4, vt, gates_col.reshape(batch, seq, LANES), gates_row, bias_col, bias_row, wq, wk, norm_g,
      x, y_a.reshape(batch, seq, width), proj4, w_a, w_b, w_out, g2, cast_w)
    return x1.reshape(t, d), hn.reshape(t, d), cast


def _ffn_kernel(hn_ref, x1_ref, w1_ref, w2_ref, gf_ref, out_ref, *, final_norm):
    j = pl.program_id(1)

    @pl.when(j == 0)
    def _():
        out_ref[...] = x1_ref[...]

    h = jnp.square(jnp.maximum(_dot(hn_ref[...], w1_ref[...]), 0.0)).astype(BF16)
    out_ref[...] += _dot(h, w2_ref[...])

    if final_norm:
        @pl.when(j == pl.num_programs(1) - 1)
        def _():
            out_ref[...] = _rms_norm(out_ref[...], gf_ref[...])


def _ffn(hn, x1, w1, w2, gf, tm, tf, final_norm):
    t, d = x1.shape
    d_ff = w1.shape[1]
    return pl.pallas_call(
        functools.partial(_ffn_kernel, final_norm=final_norm),
        grid=(t // tm, d_ff // tf),
        in_specs=[
            pl.BlockSpec((tm, d), lambda i, j: (i, 0)),
            pl.BlockSpec((tm, d), lambda i, j: (i, 0)),
            pl.BlockSpec((d, tf), lambda i, j: (0, j)),
            pl.BlockSpec((tf, d), lambda i, j: (j, 0)),
            pl.BlockSpec((1, d), lambda i, j: (0, 0)),
        ],
        out_specs=pl.BlockSpec((tm, d), lambda i, j: (i, 0)),
        out_shape=jax.ShapeDtypeStruct((t, d), F32),
        compiler_params=_params("arbitrary", "arbitrary"),
        name="ffn",
    )(hn, x1, w1, w2, gf)


def _tiles(t):
    return dict(prologue=min(t, 1024), inproj=min(t, 1024), gmlp=min(t, 1024), ffn_m=min(t, 512), ffn_f=2048)


def kernel(x, norm1_g, w_in, b_gate, gm_ln_g, gm_ln_b, gm_ws, gm_bs, ml_conv_w, ml_conv_b, ml_wq, ml_wk,
           ml_ig_b, ml_fg_b, ml_norm_g, w_a, w_b, w_out, norm2_g, w_ff1, w_ff2, norm_f_g):
    batch, seq, d = x.shape
    depth = w_in.shape[0]
    t = batch * seq
    tiles = _tiles(t)
    if_row = SEC_GA * SECTION
    n_if = 2 * ML_HEADS
    xt = x.reshape(t, d)
    w_t = jnp.swapaxes(w_in, 1, 2)
    for l in range(depth):
        gate_bias = jnp.concatenate([ml_ig_b[l], ml_fg_b[l]])
        vecs = jnp.concatenate([
            gm_ln_g[l][None], gm_ln_b[l][None], ml_conv_b[l][None], ml_conv_w[l],
            jnp.zeros((1, SECTION), F32), b_gate[l].reshape(4, SECTION),
            jnp.zeros((N_VEC_ROWS - VEC_GATE_B - 5, SECTION), F32)])
        xn, gates_col, gates_row = _prologue(xt, norm1_g[l][None], w_t, l, if_row, n_if, batch,
                                             tiles["prologue"])
        proj, vt, w2_bf, wa_bf, wb_bf, wo_bf = _inproj(
            xn, w_t, l, if_row + n_if, vecs, (w_ff2, w_a, w_b, w_out), batch, tiles["inproj"])
        y_a = _gmlp(proj, gm_ws[l], gm_bs[l].T, tiles["gmlp"])
        x1, hn, w1_bf = _mix(
            proj, vt, gates_col, gates_row, gate_bias[:, None], jnp.pad(gate_bias, (0, LANES - n_if))[None],
            ml_wq[l], ml_wk[l], ml_norm_g[l][None], xt.reshape(batch, seq, d), y_a, wa_bf, wb_bf, wo_bf,
            norm2_g[l][None],
            w_ff1, l, batch)
        xt = _ffn(hn, x1, w1_bf, w2_bf, norm_f_g[None], tiles["ffn_m"], tiles["ffn_f"],
                  final_norm=l == depth - 1)
    return xt.reshape(batch, seq, d)
```

```python
import functools

import jax
import jax.numpy as jnp
from jax import lax
from jax.experimental import pallas as pl
from jax.experimental.pallas import tpu as pltpu

EPS = 1e-6
GM_GROUP_DIM = 128
CHUNK = 128
ML_HEADS = 4
CONV_K = 4
LANES = 128
SUBLANES = 8
BF16_ROWS = 16
SECTION = 1024
SEC_U, SEC_V, SEC_M, SEC_MV, SEC_O, SEC_GA, SEC_GB = 0, 1, 2, 3, 4, 5, 7
N_SECTIONS = 9
PROJ_U, PROJ_V, PROJ_M, PROJ_O, PROJ_GA, PROJ_GB = 0, 1, 2, 3, 4, 6
N_PROJ = 8
VMEM_LIMIT = 56 * 1024 * 1024

BF16 = jnp.bfloat16
F32 = jnp.float32


def _sigmoid(x):
    return 0.5 * jnp.tanh(0.5 * x) + 0.5


def _log_sigmoid(x):
    return jnp.minimum(x, 0.0) - jnp.log(1.0 + jnp.exp(-jnp.abs(x)))


def _rms_norm(x, g):
    return x * lax.rsqrt(jnp.mean(x * x, axis=-1, keepdims=True) + EPS) * g


def _dot(a, b):
    return jnp.dot(a, b, preferred_element_type=F32)


def _split3(x):
    hi = x.astype(BF16)
    rest = x - hi.astype(F32)
    mid = rest.astype(BF16)
    return hi, mid, (rest - mid.astype(F32)).astype(BF16)


def _params(*semantics):
    return pltpu.CompilerParams(dimension_semantics=semantics, vmem_limit_bytes=VMEM_LIMIT)


def _cast_rows(n_rows, n_steps):
    rows = n_rows // n_steps
    assert rows * n_steps == n_rows and rows % BF16_ROWS == 0, (n_rows, n_steps)
    return rows


def _nt_dot(a, b):
    return lax.dot_general(a, b, (((1,), (1,)), ((), ())), preferred_element_type=F32)


def _prologue_kernel(x_ref, g1_ref, wif_ref, xn_ref, gcol_ref, grow_ref):
    xn = _rms_norm(x_ref[...], g1_ref[...]).astype(BF16)
    xn_ref[...] = xn
    wif = wif_ref[...].astype(BF16)
    gcol_ref[...] = _nt_dot(xn, wif)
    grow_ref[...] = _nt_dot(wif, xn)[:grow_ref.shape[0], :]


def _prologue(x, g1, w_t, layer, if_row, n_if, batch, tm):
    t, d = x.shape
    seq = t // batch
    tiles_per_seq = seq // tm
    assert tiles_per_seq * tm == seq
    return pl.pallas_call(
        _prologue_kernel,
        grid=(t // tm,),
        in_specs=[
            pl.BlockSpec((tm, d), lambda i: (i, 0)),
            pl.BlockSpec((1, d), lambda i: (0, 0)),
            pl.BlockSpec((None, LANES, d), lambda i: (layer, if_row // LANES, 0)),
        ],
        out_specs=[
            pl.BlockSpec((tm, d), lambda i: (i, 0)),
            pl.BlockSpec((tm, LANES), lambda i: (i, 0)),
            pl.BlockSpec((None, n_if, tm), lambda i: (i // tiles_per_seq, 0, i % tiles_per_seq)),
        ],
        out_shape=[
            jax.ShapeDtypeStruct((t, d), BF16),
            jax.ShapeDtypeStruct((t, LANES), F32),
            jax.ShapeDtypeStruct((batch, n_if, seq), F32),
        ],
        compiler_params=_params("arbitrary"),
        name="prologue",
    )(x, g1, w_t)


VEC_LN_G, VEC_LN_B, VEC_CONV_B, VEC_CONV_W, VEC_GATE_B = 0, 1, 2, 3, 7
N_VEC_ROWS = 16


def _inproj_kernel(xn_ref, w_ref, vec_ref, *rest, tiles_per_seq, n_cast_steps, n_cast):
    cast_in, (out_ref, vt_out, *cast_out), (wbf_ref, tail_ref) = (
        rest[:n_cast], rest[n_cast:2 * n_cast + 2], rest[2 * n_cast + 2:])
    j = pl.program_id(0)
    i = pl.program_id(1)
    tm = xn_ref.shape[0]
    vec = lambda r, n=1: vec_ref[pl.ds(r, n), :]

    @pl.when(j * pl.num_programs(1) + i < n_cast_steps)
    def _():
        for src, dst in zip(cast_in, cast_out):
            dst[...] = src[...].astype(BF16)

    @pl.when(i == 0)
    def _():
        rows = 256
        for r in range(0, wbf_ref.shape[0], rows):
            wbf_ref[r:r + rows, :] = w_ref[r:r + rows, :].astype(BF16)

    def proj():
        return _nt_dot(xn_ref[...], wbf_ref[...])

    @pl.when(j == SEC_U)
    def _():
        out_ref[...] = jax.nn.gelu(proj()).astype(BF16)

    @pl.when(j == SEC_V)
    def _():
        v = jax.nn.gelu(proj())
        mu = jnp.mean(v, axis=-1, keepdims=True)
        var = jnp.mean(jnp.square(v - mu), axis=-1, keepdims=True)
        out_ref[...] = ((v - mu) * lax.rsqrt(var + EPS) * vec(VEC_LN_G) + vec(VEC_LN_B)).astype(BF16)

    @pl.when(j == SEC_M)
    def _():
        @pl.when(i % tiles_per_seq == 0)
        def _():
            tail_ref[...] = jnp.zeros(tail_ref.shape, F32)

        x = proj()
        xe = jnp.concatenate([tail_ref[...], x], axis=0)
        conv = vec(VEC_CONV_B) + vec(VEC_CONV_W + CONV_K - 1) * x
        for d in range(1, CONV_K):
            conv = conv + vec(VEC_CONV_W + CONV_K - 1 - d) * xe[SUBLANES - d:SUBLANES - d + tm, :]
        tail_ref[...] = x[tm - SUBLANES:, :]
        out_ref[...] = (conv * _sigmoid(conv)).astype(BF16)

    @pl.when(j == SEC_MV)
    def _():
        vt_out[...] = _nt_dot(wbf_ref[...], xn_ref[...]).astype(BF16)

    @pl.when(j >= SEC_O)
    def _():
        out_ref[...] = _sigmoid(proj() + vec(VEC_GATE_B + j - SEC_O)).astype(BF16)


def _inproj(xn, w_t, layer, gate_row, vecs, cast_ws, batch, tm):
    t, d = xn.shape
    n_i = t // tm
    seq = t // batch
    tiles_per_seq = seq // tm
    assert tiles_per_seq * tm == seq
    n_cast_steps = (N_SECTIONS - 1) * n_i
    cast_rows = [_cast_rows(w.shape[1], n_cast_steps) for w in cast_ws]
    cast_step = lambda j, i: jnp.minimum(j * n_i + i, n_cast_steps - 1)

    def proj_block(j, i):
        mv = j == SEC_MV
        return (j - (j >= SEC_MV).astype(jnp.int32), jnp.where(mv, n_i - 1, i), 0)

    def vt_block(j, i):
        tile = jnp.clip((j - SEC_MV) * n_i + i, 0, n_i - 1)
        return (tile // tiles_per_seq, 0, tile % tiles_per_seq)

    gate_skip = gate_row - SEC_GA * SECTION
    assert gate_skip % SUBLANES == 0
    kern = functools.partial(_inproj_kernel, tiles_per_seq=tiles_per_seq, n_cast_steps=n_cast_steps,
                             n_cast=len(cast_ws))
    return pl.pallas_call(
        kern,
        grid=(N_SECTIONS, n_i),
        in_specs=[
            pl.BlockSpec((tm, d), lambda j, i: (i, 0)),
            pl.BlockSpec((None, pl.Element(SECTION), pl.Element(d)),
                         lambda j, i: (layer, SUBLANES * (j * (SECTION // SUBLANES) + jnp.where(
                             j >= SEC_GA, gate_skip // SUBLANES, 0)), 0)),
            pl.BlockSpec((N_VEC_ROWS, SECTION), lambda j, i: (0, 0)),
        ] + [pl.BlockSpec((None, r, w.shape[2]), lambda j, i: (layer, cast_step(j, i), 0))
             for r, w in zip(cast_rows, cast_ws)],
        out_specs=[pl.BlockSpec((None, tm, SECTION), proj_block), pl.BlockSpec((None, SECTION, tm), vt_block)]
        + [pl.BlockSpec((r, w.shape[2]), lambda j, i: (cast_step(j, i), 0)) for r, w in zip(cast_rows, cast_ws)],
        out_shape=[jax.ShapeDtypeStruct((N_PROJ, t, SECTION), BF16),
                   jax.ShapeDtypeStruct((batch, SECTION, seq), BF16)]
        + [jax.ShapeDtypeStruct(w.shape[1:], BF16) for w in cast_ws],
        scratch_shapes=[
            pltpu.VMEM((SECTION, d), BF16),
            pltpu.VMEM((SUBLANES, SECTION), F32),
        ],
        compiler_params=_params("arbitrary", "arbitrary"),
        name="inproj",
    )(xn, w_t, vecs, *cast_ws)


def _gmlp_kernel(u_ref, v_ref, ws_ref, bs_ref, out_ref, *, n_chunks, n_groups):
    row = lax.broadcasted_iota(jnp.int32, (CHUNK, CHUNK), 0)
    col = lax.broadcasted_iota(jnp.int32, (CHUNK, CHUNK), 1)
    causal = col <= row
    for g in range(n_groups):
        gs = slice(g * GM_GROUP_DIM, (g + 1) * GM_GROUP_DIM)
        w = jnp.where(causal, ws_ref[g], 0.0).astype(BF16)
        vg = jnp.concatenate([v_ref[c * CHUNK:(c + 1) * CHUNK, gs] for c in range(n_chunks)], axis=1)
        s = _dot(w, vg) + bs_ref[:, g:g + 1]
        for c in range(n_chunks):
            rs = slice(c * CHUNK, (c + 1) * CHUNK)
            sc = s[:, c * GM_GROUP_DIM:(c + 1) * GM_GROUP_DIM]
            out_ref[rs, gs] = (u_ref[rs, gs].astype(F32) * sc).astype(BF16)


def _gmlp(proj, ws, bs_t, tm):
    _, t, width = proj.shape
    n_groups = width // GM_GROUP_DIM
    kern = functools.partial(_gmlp_kernel, n_chunks=tm // CHUNK, n_groups=n_groups)
    return pl.pallas_call(
        kern,
        grid=(t // tm,),
        in_specs=[
            pl.BlockSpec((None, tm, width), lambda i: (PROJ_U, i, 0)),
            pl.BlockSpec((None, tm, width), lambda i: (PROJ_V, i, 0)),
            pl.BlockSpec((n_groups, CHUNK, CHUNK), lambda i: (0, 0, 0)),
            pl.BlockSpec((CHUNK, n_groups), lambda i: (0, 0)),
        ],
        out_specs=pl.BlockSpec((tm, width), lambda i: (i, 0)),
        out_shape=jax.ShapeDtypeStruct((t, width), BF16),
        compiler_params=_params("arbitrary"),
        name="gmlp",
    )(proj, proj, ws, bs_t)


def _lanes(col):
    return jnp.broadcast_to(col, (col.shape[0], LANES))


def _wide(x, n):
    return jnp.concatenate([x] * n, axis=1)


def _mlstm_chunk(c_ref, o_ref, vt_ref, gcol_ref, grow_ref, bcol_ref, brow_ref, ng_ref,
                 st_ref, mx_ref, wqb_ref, wkb_ref, yb_ref, *, head_dim, batch):
    row = lax.broadcasted_iota(jnp.int32, (CHUNK, CHUNK), 0)
    col = lax.broadcasted_iota(jnp.int32, (CHUNK, CHUNK), 1)
    causal = col <= row
    tril = causal.astype(BF16)
    triu = (row <= col).astype(BF16)
    nd = head_dim // LANES
    groups = CHUNK // SUBLANES
    heads = [(b, h) for b in range(batch) for h in range(ML_HEADS)]
    slot = lambda b, h: b * ML_HEADS + h
    hsl = lambda h: slice(h * head_dim, (h + 1) * head_dim)


    n_if = grow_ref.shape[1]
    lane = lax.broadcasted_iota(jnp.int32, (CHUNK, LANES), 1)
    g_col = gcol_ref[0] + brow_ref[...]
    for b in range(1, batch):
        g_col = jnp.where(lane < b * n_if, g_col, pltpu.roll(gcol_ref[b] + brow_ref[...], b * n_if, 1))
    g_row = jnp.concatenate([grow_ref[b] + bcol_ref[...] for b in range(batch)], axis=0)
    lf_row = _log_sigmoid(g_row)
    bcum_col = _dot(jnp.concatenate([tril] * 3, axis=1), jnp.concatenate(_split3(_log_sigmoid(g_col)), axis=0))
    bcum_row = _dot(jnp.concatenate(_split3(lf_row), axis=1), jnp.concatenate([triu] * 3, axis=0))
    total = jnp.sum(lf_row, axis=-1, keepdims=True)

    q, k = {}, {}
    for h in range(ML_HEADS):
        c = jnp.concatenate([c_ref[b, :, hsl(h)] for b in range(batch)], axis=0)
        q_h, k_h = _dot(c, wqb_ref[h]), _dot(c, wkb_ref[h])
        for b in range(batch):
            q[b, h], k[b, h] = q_h[b * CHUNK:(b + 1) * CHUNK], k_h[b * CHUNK:(b + 1) * CHUNK]

    w_intra, w_inter, inv_floor, w_state, decay, scale, m_new = {}, {}, {}, {}, {}, {}, {}
    for b, h in heads:
        i, f = b * n_if + h, b * n_if + ML_HEADS + h
        bc = _lanes(bcum_col[:, f:f + 1])
        ig = _lanes(g_col[:, i:i + 1])
        src = g_row[i:i + 1, :] - bcum_row[f:f + 1, :]
        b_last = jnp.broadcast_to(total[f:f + 1, :], (1, LANES))
        m_prev = mx_ref[slot(b, h):slot(b, h) + 1, :]
        inter = bc + m_prev
        dlog = jnp.where(causal, bc + src, -jnp.inf)
        m_t = jnp.maximum(inter, _lanes(jnp.max(dlog, axis=-1, keepdims=True)))
        w_intra[b, h] = jnp.exp(dlog - m_t)
        w_inter[b, h] = jnp.exp(inter - m_t)
        inv_floor[b, h] = jnp.exp(-m_t)
        a = b_last - bc + ig
        a_max = jnp.max(a, axis=0, keepdims=True)
        w_state[b, h] = jnp.exp(a - a_max)
        m_new[b, h] = jnp.maximum(b_last + m_prev, a_max)
        decay[b, h] = jnp.exp(b_last + m_prev - m_new[b, h])
        scale[b, h] = jnp.exp(a_max - m_new[b, h])

    qk = {}
    for b, h in heads:
        qk[b, h] = _nt_dot(q[b, h].astype(BF16), k[b, h].astype(BF16)) * w_intra[b, h]

    for b, h in heads:
        s = slot(b, h)
        qw = q[b, h] * _wide(w_inter[b, h], nd)
        lhs = jnp.concatenate([qw.astype(BF16), qk[b, h].astype(BF16)], axis=1)
        rhs = jnp.concatenate([st_ref[s, :head_dim, :].astype(BF16), vt_ref[b, hsl(h), :]], axis=1)
        num = _nt_dot(lhs, rhs)
        qn = (qw.reshape(groups, SUBLANES, head_dim) * st_ref[s, head_dim:, :]).reshape(CHUNK, head_dim)
        part = qk[b, h] + sum(qn[:, i * LANES:(i + 1) * LANES] for i in range(nd))
        den = _lanes(jnp.sum(part, axis=-1, keepdims=True))
        hid = num * _wide(1.0 / jnp.maximum(jnp.abs(den), inv_floor[b, h]), nd)
        mu = jnp.mean(hid, axis=-1, keepdims=True)
        var = jnp.mean(jnp.square(hid - mu), axis=-1, keepdims=True)
        hid = (hid - mu) * lax.rsqrt(var + EPS) * ng_ref[:, hsl(h)]
        yb_ref[b, :, hsl(h)] = (o_ref[b, :, hsl(h)].astype(F32) * hid).astype(BF16)

    for b, h in heads:
        s = slot(b, h)
        kw = k[b, h] * _wide(w_state[b, h] * scale[b, h], nd)
        upd = _dot(vt_ref[b, hsl(h), :], kw.astype(BF16))
        kw_sum = jnp.sum(jnp.sum(kw.reshape(groups, SUBLANES, head_dim), axis=0), axis=0, keepdims=True)
        keep = _wide(decay[b, h], nd)
        st_ref[s, :head_dim, :] = keep * st_ref[s, :head_dim, :] + upd
        st_ref[s, head_dim:, :] = keep * st_ref[s, head_dim:, :] + kw_sum
        mx_ref[s:s + 1, :] = m_new[b, h]


def _mix_kernel(co_ref, vt_ref, gcol_ref, grow_ref, bcol_ref, brow_ref, wq_ref, wk_ref, ng_ref,
                x_ref, ya_ref, gate_ref, wa_ref, wb_ref, wo_ref, g2_ref, cast_ref,
                x1_ref, hn_ref, cast_out, st_ref, mx_ref, wqb_ref, wkb_ref, yb_ref,
                *, head_dim, batch, n_steps):
    n = pl.program_id(0)

    @pl.when(n == 0)
    def _():
        wqb_ref[...] = wq_ref[...].astype(BF16)
        wkb_ref[...] = (wk_ref[...] * (head_dim ** -0.5)).astype(BF16)
        st_ref[...] = jnp.zeros(st_ref.shape, F32)
        mx_ref[...] = jnp.zeros(mx_ref.shape, F32)

    @pl.when(n < n_steps)
    def _():
        cast_out[...] = cast_ref[...].astype(BF16)

    def mlstm(slot):
        _mlstm_chunk(co_ref.at[0], co_ref.at[1], vt_ref, gcol_ref, grow_ref, bcol_ref, brow_ref, ng_ref,
                     st_ref, mx_ref, wqb_ref, wkb_ref, yb_ref.at[slot], head_dim=head_dim, batch=batch)

    def merge(slot):
        flat = lambda v: v.reshape(batch * CHUNK, v.shape[-1])
        gate_a = jnp.concatenate([flat(gate_ref[0]), flat(gate_ref[1])], axis=1).astype(F32)
        gate_b = jnp.concatenate([flat(gate_ref[2]), flat(gate_ref[3])], axis=1).astype(F32)
        mixed = (gate_a * _dot(flat(ya_ref[...]), wa_ref[...])
                 + gate_b * _dot(flat(yb_ref[slot]), wb_ref[...]))
        x1 = flat(x_ref[...]) + _dot(mixed.astype(BF16), wo_ref[...])
        x1_ref[...] = x1.reshape(x1_ref.shape)
        hn_ref[...] = _rms_norm(x1, g2_ref[...]).astype(BF16).reshape(hn_ref.shape)

    @pl.when(n == 0)
    def _():
        mlstm(0)

    for parity in range(2):
        @pl.when((n > 0) & (n < n_steps) & (n % 2 == parity))
        def _(parity=parity):
            mlstm(parity)
            merge(1 - parity)

    @pl.when(n == n_steps)
    def _():
        merge((n_steps - 1) % 2)


def _mix(proj, vt, gates_col, gates_row, bias_col, bias_row, wq, wk, norm_g,
         x, y_a, w_a, w_b, w_out, g2, cast_w, layer, batch):
    _, t, width = proj.shape
    seq = t // batch
    d = x.shape[-1]
    head_dim = width // ML_HEADS
    n_if = gates_row.shape[1]
    n_steps = seq // CHUNK
    n_gate = N_PROJ - PROJ_GA
    assert PROJ_O == PROJ_M + 1 and PROJ_M % 2 == 0 and PROJ_GA % n_gate == 0
    assert batch * n_if <= LANES, (batch, n_if)
    proj4 = proj.reshape(proj.shape[0], batch, seq, width)
    cast_rows = _cast_rows(cast_w.shape[1], n_steps)
    at = lambda n: jnp.minimum(n, n_steps - 1)
    done = lambda n: jnp.maximum(n - 1, 0)
    const2 = lambda n: (0, 0)
    const3 = lambda n: (0, 0, 0)
    resident = dict(pipeline_mode=pl.Buffered(1))
    n_state = batch * ML_HEADS
    kern = functools.partial(_mix_kernel, head_dim=head_dim, batch=batch, n_steps=n_steps)
    x1, hn, cast = pl.pallas_call(
        kern,
        grid=(n_steps + 1,),
        in_specs=[
            pl.BlockSpec((2, batch, CHUNK, width), lambda n: (PROJ_M // 2, 0, at(n), 0)),
            pl.BlockSpec((batch, width, CHUNK), lambda n: (0, 0, at(n))),
            pl.BlockSpec((batch, CHUNK, LANES), lambda n: (0, at(n), 0)),
            pl.BlockSpec((batch, n_if, CHUNK), lambda n: (0, 0, at(n))),
            pl.BlockSpec((n_if, 1), const2),
            pl.BlockSpec((1, LANES), const2),
            pl.BlockSpec((ML_HEADS, head_dim, head_dim), const3),
            pl.BlockSpec((ML_HEADS, head_dim, head_dim), const3),
            pl.BlockSpec((1, width), const2),
            pl.BlockSpec((batch, CHUNK, d), lambda n: (0, done(n), 0)),
            pl.BlockSpec((batch, CHUNK, width), lambda n: (0, done(n), 0)),
            pl.BlockSpec((n_gate, batch, CHUNK, SECTION), lambda n: (PROJ_GA // n_gate, 0, done(n), 0)),
            pl.BlockSpec((width, d), const2, **resident),
            pl.BlockSpec((width, d), const2, **resident),
            pl.BlockSpec((d, d), const2, **resident),
            pl.BlockSpec((1, d), const2),
            pl.BlockSpec((None, cast_rows, cast_w.shape[2]), lambda n: (layer, at(n), 0)),
        ],
        out_specs=[
            pl.BlockSpec((batch, CHUNK, d), lambda n: (0, done(n), 0)),
            pl.BlockSpec((batch, CHUNK, d), lambda n: (0, done(n), 0)),
            pl.BlockSpec((cast_rows, cast_w.shape[2]), lambda n: (at(n), 0)),
        ],
        out_shape=[
            jax.ShapeDtypeStruct((batch, seq, d), F32),
            jax.ShapeDtypeStruct((batch, seq, d), BF16),
            jax.ShapeDtypeStruct(cast_w.shape[1:], BF16),
        ],
        scratch_shapes=[
            pltpu.VMEM((n_state, head_dim + SUBLANES, head_dim), F32),
            pltpu.VMEM((n_state, LANES), F32),
            pltpu.VMEM((ML_HEADS, head_dim, head_dim), BF16),
            pltpu.VMEM((ML_HEADS, head_dim, head_dim), BF16),
            pltpu.VMEM((2, batch, CHUNK, width), BF16),
        ],
        compiler_params=_params("arbitrary"),
        name="mix",
    )(proj4, vt, gates_col.reshape(batch, seq, LANES), gates_row, bias_col, bias_row, wq, wk, norm_g,
      x, y_a.reshape(batch, seq, width), proj4, w_a, w_b, w_out, g2, cast_w)
    return x1.reshape(t, d), hn.reshape(t, d), cast


def _ffn_kernel(hn_ref, x1_ref, w1_ref, w2_ref, gf_ref, out_ref, *, final_norm):
    j = pl.program_id(1)

    @pl.when(j == 0)
    def _():
        out_ref[...] = x1_ref[...]

    h = jnp.square(jnp.maximum(_dot(hn_ref[...], w1_ref[...]), 0.0)).astype(BF16)
    out_ref[...] += _dot(h, w2_ref[...])

    if final_norm:
        @pl.when(j == pl.num_programs(1) - 1)
        def _():
            out_ref[...] = _rms_norm(out_ref[...], gf_ref[...])


def _ffn(hn, x1, w1, w2, gf, tm, tf, final_norm):
    t, d = x1.shape
    d_ff = w1.shape[1]
    return pl.pallas_call(
        functools.partial(_ffn_kernel, final_norm=final_norm),
        grid=(t // tm, d_ff // tf),
        in_specs=[
            pl.BlockSpec((tm, d), lambda i, j: (i, 0)),
            pl.BlockSpec((tm, d), lambda i, j: (i, 0)),
            pl.BlockSpec((d, tf), lambda i, j: (0, j)),
            pl.BlockSpec((tf, d), lambda i, j: (j, 0)),
            pl.BlockSpec((1, d), lambda i, j: (0, 0)),
        ],
        out_specs=pl.BlockSpec((tm, d), lambda i, j: (i, 0)),
        out_shape=jax.ShapeDtypeStruct((t, d), F32),
        compiler_params=_params("arbitrary", "arbitrary"),
        name="ffn",
    )(hn, x1, w1, w2, gf)


def _tiles(t):
    return dict(prologue=min(t, 1024), inproj=min(t, 1024), gmlp=min(t, 1024), ffn_m=min(t, 1024), ffn_f=512)


def kernel(x, norm1_g, w_in, b_gate, gm_ln_g, gm_ln_b, gm_ws, gm_bs, ml_conv_w, ml_conv_b, ml_wq, ml_wk,
           ml_ig_b, ml_fg_b, ml_norm_g, w_a, w_b, w_out, norm2_g, w_ff1, w_ff2, norm_f_g):
    batch, seq, d = x.shape
    depth = w_in.shape[0]
    t = batch * seq
    tiles = _tiles(t)
    if_row = SEC_GA * SECTION
    n_if = 2 * ML_HEADS
    xt = x.reshape(t, d)
    w_t = jnp.swapaxes(w_in, 1, 2)
    for l in range(depth):
        gate_bias = jnp.concatenate([ml_ig_b[l], ml_fg_b[l]])
        vecs = jnp.concatenate([
            gm_ln_g[l][None], gm_ln_b[l][None], ml_conv_b[l][None], ml_conv_w[l],
            jnp.zeros((1, SECTION), F32), b_gate[l].reshape(4, SECTION),
            jnp.zeros((N_VEC_ROWS - VEC_GATE_B - 5, SECTION), F32)])
        xn, gates_col, gates_row = _prologue(xt, norm1_g[l][None], w_t, l, if_row, n_if, batch,
                                             tiles["prologue"])
        proj, vt, w2_bf, wa_bf, wb_bf, wo_bf = _inproj(
            xn, w_t, l, if_row + n_if, vecs, (w_ff2, w_a, w_b, w_out), batch, tiles["inproj"])
        y_a = _gmlp(proj, gm_ws[l], gm_bs[l].T, tiles["gmlp"])
        x1, hn, w1_bf = _mix(
            proj, vt, gates_col, gates_row, gate_bias[:, None], jnp.pad(gate_bias, (0, LANES - n_if))[None],
            ml_wq[l], ml_wk[l], ml_norm_g[l][None], xt.reshape(batch, seq, d), y_a, wa_bf, wb_bf, wo_bf,
            norm2_g[l][None],
            w_ff1, l, batch)
        xt = _ffn(hn, x1, w1_bf, w2_bf, norm_f_g[None], tiles["ffn_m"], tiles["ffn_f"],
                  final_norm=l == depth - 1)
    return xt.reshape(batch, seq, d)
```

```python
import functools

import jax
import jax.numpy as jnp
from jax import lax
from jax.experimental import pallas as pl
from jax.experimental.pallas import tpu as pltpu

EPS = 1e-6
GM_GROUP_DIM = 128
CHUNK = 128
ML_HEADS = 4
CONV_K = 4
LANES = 128
SUBLANES = 8
BF16_ROWS = 16
CAST_ROWS = 128
SECTION = 1024
SEC_U, SEC_V, SEC_M, SEC_MV, SEC_O, SEC_GA, SEC_GB = 0, 1, 2, 3, 4, 5, 7
N_SECTIONS = 9
PROJ_U, PROJ_V, PROJ_M, PROJ_O, PROJ_GA, PROJ_GB = 0, 1, 2, 3, 4, 6
N_PROJ = 8
VMEM_LIMIT = 56 * 1024 * 1024

BF16 = jnp.bfloat16
F32 = jnp.float32


def _sigmoid(x):
    return 0.5 * jnp.tanh(0.5 * x) + 0.5


def _log_sigmoid(x):
    return jnp.minimum(x, 0.0) - jnp.log(1.0 + jnp.exp(-jnp.abs(x)))


def _rms_norm(x, g):
    return x * lax.rsqrt(jnp.mean(x * x, axis=-1, keepdims=True) + EPS) * g


def _dot(a, b):
    return jnp.dot(a, b, preferred_element_type=F32)


def _split3(x):
    hi = x.astype(BF16)
    rest = x - hi.astype(F32)
    mid = rest.astype(BF16)
    return hi, mid, (rest - mid.astype(F32)).astype(BF16)


def _params(*semantics):
    return pltpu.CompilerParams(dimension_semantics=semantics, vmem_limit_bytes=VMEM_LIMIT)


def _cast_rows(n_rows, n_steps):
    rows = n_rows // n_steps
    assert rows * n_steps == n_rows and rows % BF16_ROWS == 0, (n_rows, n_steps)
    return rows


def _cast_windows(rows, n_steps):
    counts = [r // CAST_ROWS for r in rows]
    assert all(c * CAST_ROWS == r for c, r in zip(counts, rows)), rows
    largest = counts.index(max(counts))
    windows, first = [], 0
    for k, count in enumerate(counts):
        windows.append((0 if k == largest else first, count))
        first += 0 if k == largest else count
    assert max(f + c for f, c in windows) <= n_steps, (windows, n_steps)
    return windows


def _nt_dot(a, b):
    return lax.dot_general(a, b, (((1,), (1,)), ((), ())), preferred_element_type=F32)


def _prologue_kernel(x_ref, g1_ref, wif_ref, xn_ref, gcol_ref, grow_ref):
    xn = _rms_norm(x_ref[...], g1_ref[...]).astype(BF16)
    xn_ref[...] = xn
    wif = wif_ref[...].astype(BF16)
    gcol_ref[...] = _nt_dot(xn, wif)
    grow_ref[...] = _nt_dot(wif, xn)[:grow_ref.shape[0], :]


def _prologue(x, g1, w_t, layer, if_row, n_if, batch, tm):
    t, d = x.shape
    seq = t // batch
    tiles_per_seq = seq // tm
    assert tiles_per_seq * tm == seq
    return pl.pallas_call(
        _prologue_kernel,
        grid=(t // tm,),
        in_specs=[
            pl.BlockSpec((tm, d), lambda i: (i, 0)),
            pl.BlockSpec((1, d), lambda i: (0, 0)),
            pl.BlockSpec((None, LANES, d), lambda i: (layer, if_row // LANES, 0)),
        ],
        out_specs=[
            pl.BlockSpec((tm, d), lambda i: (i, 0)),
            pl.BlockSpec((tm, LANES), lambda i: (i, 0)),
            pl.BlockSpec((None, n_if, tm), lambda i: (i // tiles_per_seq, 0, i % tiles_per_seq)),
        ],
        out_shape=[
            jax.ShapeDtypeStruct((t, d), BF16),
            jax.ShapeDtypeStruct((t, LANES), F32),
            jax.ShapeDtypeStruct((batch, n_if, seq), F32),
        ],
        compiler_params=_params("arbitrary"),
        name="prologue",
    )(x, g1, w_t)


VEC_LN_G, VEC_LN_B, VEC_CONV_B, VEC_CONV_W, VEC_GATE_B = 0, 1, 2, 3, 7
N_VEC_ROWS = 16


def _inproj_kernel(xn_ref, w_ref, vec_ref, *rest, tiles_per_seq, cast_windows):
    n_cast = len(cast_windows)
    cast_in, (out_ref, vt_out, *cast_out), (wbf_ref, tail_ref) = (
        rest[:n_cast], rest[n_cast:2 * n_cast + 2], rest[2 * n_cast + 2:])
    j = pl.program_id(0)
    i = pl.program_id(1)
    tm = xn_ref.shape[0]
    vec = lambda r, n=1: vec_ref[pl.ds(r, n), :]

    step = j * pl.num_programs(1) + i
    for src, dst, (first, count) in zip(cast_in, cast_out, cast_windows):
        @pl.when((step >= first) & (step < first + count))
        def _(src=src, dst=dst):
            dst[...] = src[...].astype(BF16)

    @pl.when(i == 0)
    def _():
        rows = 256
        for r in range(0, wbf_ref.shape[0], rows):
            wbf_ref[r:r + rows, :] = w_ref[r:r + rows, :].astype(BF16)

    def proj():
        return _nt_dot(xn_ref[...], wbf_ref[...])

    @pl.when(j == SEC_U)
    def _():
        out_ref[...] = jax.nn.gelu(proj()).astype(BF16)

    @pl.when(j == SEC_V)
    def _():
        v = jax.nn.gelu(proj())
        mu = jnp.mean(v, axis=-1, keepdims=True)
        var = jnp.mean(jnp.square(v - mu), axis=-1, keepdims=True)
        out_ref[...] = ((v - mu) * lax.rsqrt(var + EPS) * vec(VEC_LN_G) + vec(VEC_LN_B)).astype(BF16)

    @pl.when(j == SEC_M)
    def _():
        @pl.when(i % tiles_per_seq == 0)
        def _():
            tail_ref[...] = jnp.zeros(tail_ref.shape, F32)

        x = proj()
        xe = jnp.concatenate([tail_ref[...], x], axis=0)
        conv = vec(VEC_CONV_B) + vec(VEC_CONV_W + CONV_K - 1) * x
        for d in range(1, CONV_K):
            conv = conv + vec(VEC_CONV_W + CONV_K - 1 - d) * xe[SUBLANES - d:SUBLANES - d + tm, :]
        tail_ref[...] = x[tm - SUBLANES:, :]
        out_ref[...] = (conv * _sigmoid(conv)).astype(BF16)

    @pl.when(j == SEC_MV)
    def _():
        vt_out[...] = _nt_dot(wbf_ref[...], xn_ref[...]).astype(BF16)

    @pl.when(j >= SEC_O)
    def _():
        out_ref[...] = _sigmoid(proj() + vec(VEC_GATE_B + j - SEC_O)).astype(BF16)


def _inproj(xn, w_t, layer, gate_row, vecs, cast_ws, batch, tm):
    t, d = xn.shape
    n_i = t // tm
    seq = t // batch
    tiles_per_seq = seq // tm
    assert tiles_per_seq * tm == seq
    windows = _cast_windows([w.shape[1] for w in cast_ws], N_SECTIONS * n_i)

    def cast_block(first, count):
        return lambda j, i: jnp.clip(j * n_i + i - first, 0, count - 1)

    def proj_block(j, i):
        mv = j == SEC_MV
        return (j - (j >= SEC_MV).astype(jnp.int32), jnp.where(mv, n_i - 1, i), 0)

    def vt_block(j, i):
        tile = jnp.clip((j - SEC_MV) * n_i + i, 0, n_i - 1)
        return (tile // tiles_per_seq, 0, tile % tiles_per_seq)

    gate_skip = gate_row - SEC_GA * SECTION
    assert gate_skip % SUBLANES == 0
    kern = functools.partial(_inproj_kernel, tiles_per_seq=tiles_per_seq, cast_windows=tuple(windows))
    return pl.pallas_call(
        kern,
        grid=(N_SECTIONS, n_i),
        in_specs=[
            pl.BlockSpec((tm, d), lambda j, i: (i, 0)),
            pl.BlockSpec((None, pl.Element(SECTION), pl.Element(d)),
                         lambda j, i: (layer, SUBLANES * (j * (SECTION // SUBLANES) + jnp.where(
                             j >= SEC_GA, gate_skip // SUBLANES, 0)), 0)),
            pl.BlockSpec((N_VEC_ROWS, SECTION), lambda j, i: (0, 0)),
        ] + [pl.BlockSpec((None, CAST_ROWS, w.shape[2]), lambda j, i, blk=cast_block(*win): (layer, blk(j, i), 0))
             for win, w in zip(windows, cast_ws)],
        out_specs=[pl.BlockSpec((None, tm, SECTION), proj_block), pl.BlockSpec((None, SECTION, tm), vt_block)]
        + [pl.BlockSpec((CAST_ROWS, w.shape[2]), lambda j, i, blk=cast_block(*win): (blk(j, i), 0))
           for win, w in zip(windows, cast_ws)],
        out_shape=[jax.ShapeDtypeStruct((N_PROJ, t, SECTION), BF16),
                   jax.ShapeDtypeStruct((batch, SECTION, seq), BF16)]
        + [jax.ShapeDtypeStruct(w.shape[1:], BF16) for w in cast_ws],
        scratch_shapes=[
            pltpu.VMEM((SECTION, d), BF16),
            pltpu.VMEM((SUBLANES, SECTION), F32),
        ],
        compiler_params=_params("arbitrary", "arbitrary"),
        name="inproj",
    )(xn, w_t, vecs, *cast_ws)


def _gmlp_kernel(u_ref, v_ref, ws_ref, bs_ref, out_ref, *, n_chunks, n_groups):
    row = lax.broadcasted_iota(jnp.int32, (CHUNK, CHUNK), 0)
    col = lax.broadcasted_iota(jnp.int32, (CHUNK, CHUNK), 1)
    causal = col <= row
    for g in range(n_groups):
        gs = slice(g * GM_GROUP_DIM, (g + 1) * GM_GROUP_DIM)
        w = jnp.where(causal, ws_ref[g], 0.0).astype(BF16)
        vg = jnp.concatenate([v_ref[c * CHUNK:(c + 1) * CHUNK, gs] for c in range(n_chunks)], axis=1)
        s = _dot(w, vg) + bs_ref[:, g:g + 1]
        for c in range(n_chunks):
            rs = slice(c * CHUNK, (c + 1) * CHUNK)
            sc = s[:, c * GM_GROUP_DIM:(c + 1) * GM_GROUP_DIM]
            out_ref[rs, gs] = (u_ref[rs, gs].astype(F32) * sc).astype(BF16)


def _gmlp(proj, ws, bs_t, tm):
    _, t, width = proj.shape
    n_groups = width // GM_GROUP_DIM
    kern = functools.partial(_gmlp_kernel, n_chunks=tm // CHUNK, n_groups=n_groups)
    return pl.pallas_call(
        kern,
        grid=(t // tm,),
        in_specs=[
            pl.BlockSpec((None, tm, width), lambda i: (PROJ_U, i, 0)),
            pl.BlockSpec((None, tm, width), lambda i: (PROJ_V, i, 0)),
            pl.BlockSpec((n_groups, CHUNK, CHUNK), lambda i: (0, 0, 0)),
            pl.BlockSpec((CHUNK, n_groups), lambda i: (0, 0)),
        ],
        out_specs=pl.BlockSpec((tm, width), lambda i: (i, 0)),
        out_shape=jax.ShapeDtypeStruct((t, width), BF16),
        compiler_params=_params("arbitrary"),
        name="gmlp",
    )(proj, proj, ws, bs_t)


def _lanes(col):
    return jnp.broadcast_to(col, (col.shape[0], LANES))


def _wide(x, n):
    return jnp.concatenate([x] * n, axis=1)


def _mlstm_chunk(c_ref, o_ref, vt_ref, gcol_ref, grow_ref, bcol_ref, brow_ref, ng_ref,
                 st_ref, mx_ref, wqb_ref, wkb_ref, yb_ref, *, head_dim, batch):
    row = lax.broadcasted_iota(jnp.int32, (CHUNK, CHUNK), 0)
    col = lax.broadcasted_iota(jnp.int32, (CHUNK, CHUNK), 1)
    causal = col <= row
    tril = causal.astype(BF16)
    triu = (row <= col).astype(BF16)
    nd = head_dim // LANES
    groups = CHUNK // SUBLANES
    heads = [(b, h) for b in range(batch) for h in range(ML_HEADS)]
    slot = lambda b, h: b * ML_HEADS + h
    hsl = lambda h: slice(h * head_dim, (h + 1) * head_dim)


    n_if = grow_ref.shape[1]
    lane = lax.broadcasted_iota(jnp.int32, (CHUNK, LANES), 1)
    g_col = gcol_ref[0] + brow_ref[...]
    for b in range(1, batch):
        g_col = jnp.where(lane < b * n_if, g_col, pltpu.roll(gcol_ref[b] + brow_ref[...], b * n_if, 1))
    g_row = jnp.concatenate([grow_ref[b] + bcol_ref[...] for b in range(batch)], axis=0)
    lf_row = _log_sigmoid(g_row)
    bcum_col = _dot(jnp.concatenate([tril] * 3, axis=1), jnp.concatenate(_split3(_log_sigmoid(g_col)), axis=0))
    bcum_row = _dot(jnp.concatenate(_split3(lf_row), axis=1), jnp.concatenate([triu] * 3, axis=0))
    total = jnp.sum(lf_row, axis=-1, keepdims=True)

    q, k = {}, {}
    for h in range(ML_HEADS):
        c = jnp.concatenate([c_ref[b, :, hsl(h)] for b in range(batch)], axis=0)
        q_h, k_h = _dot(c, wqb_ref[h]), _dot(c, wkb_ref[h])
        for b in range(batch):
            q[b, h], k[b, h] = q_h[b * CHUNK:(b + 1) * CHUNK], k_h[b * CHUNK:(b + 1) * CHUNK]

    w_intra, w_inter, inv_floor, w_state, decay, scale, m_new = {}, {}, {}, {}, {}, {}, {}
    for b, h in heads:
        i, f = b * n_if + h, b * n_if + ML_HEADS + h
        bc = _lanes(bcum_col[:, f:f + 1])
        ig = _lanes(g_col[:, i:i + 1])
        src = g_row[i:i + 1, :] - bcum_row[f:f + 1, :]
        b_last = jnp.broadcast_to(total[f:f + 1, :], (1, LANES))
        m_prev = mx_ref[slot(b, h):slot(b, h) + 1, :]
        inter = bc + m_prev
        dlog = jnp.where(causal, bc + src, -jnp.inf)
        m_t = jnp.maximum(inter, _lanes(jnp.max(dlog, axis=-1, keepdims=True)))
        w_intra[b, h] = jnp.exp(dlog - m_t)
        w_inter[b, h] = jnp.exp(inter - m_t)
        inv_floor[b, h] = jnp.exp(-m_t)
        a = b_last - bc + ig
        a_max = jnp.max(a, axis=0, keepdims=True)
        w_state[b, h] = jnp.exp(a - a_max)
        m_new[b, h] = jnp.maximum(b_last + m_prev, a_max)
        decay[b, h] = jnp.exp(b_last + m_prev - m_new[b, h])
        scale[b, h] = jnp.exp(a_max - m_new[b, h])

    qk = {}
    for b, h in heads:
        qk[b, h] = _nt_dot(q[b, h].astype(BF16), k[b, h].astype(BF16)) * w_intra[b, h]

    for b, h in heads:
        s = slot(b, h)
        qw = q[b, h] * _wide(w_inter[b, h], nd)
        lhs = jnp.concatenate([qw.astype(BF16), qk[b, h].astype(BF16)], axis=1)
        rhs = jnp.concatenate([st_ref[s, :head_dim, :].astype(BF16), vt_ref[b, hsl(h), :]], axis=1)
        num = _nt_dot(lhs, rhs)
        qn = (qw.reshape(groups, SUBLANES, head_dim) * st_ref[s, head_dim:, :]).reshape(CHUNK, head_dim)
        part = qk[b, h] + sum(qn[:, i * LANES:(i + 1) * LANES] for i in range(nd))
        den = _lanes(jnp.sum(part, axis=-1, keepdims=True))
        hid = num * _wide(1.0 / jnp.maximum(jnp.abs(den), inv_floor[b, h]), nd)
        mu = jnp.mean(hid, axis=-1, keepdims=True)
        var = jnp.mean(jnp.square(hid - mu), axis=-1, keepdims=True)
        hid = (hid - mu) * lax.rsqrt(var + EPS) * ng_ref[:, hsl(h)]
        yb_ref[b, :, hsl(h)] = (o_ref[b, :, hsl(h)].astype(F32) * hid).astype(BF16)

    for b, h in heads:
        s = slot(b, h)
        kw = k[b, h] * _wide(w_state[b, h] * scale[b, h], nd)
        upd = _dot(vt_ref[b, hsl(h), :], kw.astype(BF16))
        kw_sum = jnp.sum(jnp.sum(kw.reshape(groups, SUBLANES, head_dim), axis=0), axis=0, keepdims=True)
        keep = _wide(decay[b, h], nd)
        st_ref[s, :head_dim, :] = keep * st_ref[s, :head_dim, :] + upd
        st_ref[s, head_dim:, :] = keep * st_ref[s, head_dim:, :] + kw_sum
        mx_ref[s:s + 1, :] = m_new[b, h]


def _mix_kernel(co_ref, vt_ref, gcol_ref, grow_ref, bcol_ref, brow_ref, wq_ref, wk_ref, ng_ref,
                x_ref, ya_ref, gate_ref, wa_ref, wb_ref, wo_ref, g2_ref, cast_ref,
                x1_ref, hn_ref, cast_out, st_ref, mx_ref, wqb_ref, wkb_ref, yb_ref,
                *, head_dim, batch, n_steps):
    n = pl.program_id(0)

    @pl.when(n == 0)
    def _():
        wqb_ref[...] = wq_ref[...].astype(BF16)
        wkb_ref[...] = (wk_ref[...] * (head_dim ** -0.5)).astype(BF16)
        st_ref[...] = jnp.zeros(st_ref.shape, F32)
        mx_ref[...] = jnp.zeros(mx_ref.shape, F32)

    @pl.when(n < n_steps)
    def _():
        cast_out[...] = cast_ref[...].astype(BF16)

    def mlstm(slot):
        _mlstm_chunk(co_ref.at[0], co_ref.at[1], vt_ref, gcol_ref, grow_ref, bcol_ref, brow_ref, ng_ref,
                     st_ref, mx_ref, wqb_ref, wkb_ref, yb_ref.at[slot], head_dim=head_dim, batch=batch)

    def merge(slot):
        flat = lambda v: v.reshape(batch * CHUNK, v.shape[-1])
        gate_a = jnp.concatenate([flat(gate_ref[0]), flat(gate_ref[1])], axis=1).astype(F32)
        gate_b = jnp.concatenate([flat(gate_ref[2]), flat(gate_ref[3])], axis=1).astype(F32)
        mixed = (gate_a * _dot(flat(ya_ref[...]), wa_ref[...])
                 + gate_b * _dot(flat(yb_ref[slot]), wb_ref[...]))
        x1 = flat(x_ref[...]) + _dot(mixed.astype(BF16), wo_ref[...])
        x1_ref[...] = x1.reshape(x1_ref.shape)
        hn_ref[...] = _rms_norm(x1, g2_ref[...]).astype(BF16).reshape(hn_ref.shape)

    @pl.when(n == 0)
    def _():
        mlstm(0)

    for parity in range(2):
        @pl.when((n > 0) & (n < n_steps) & (n % 2 == parity))
        def _(parity=parity):
            mlstm(parity)
            merge(1 - parity)

    @pl.when(n == n_steps)
    def _():
        merge((n_steps - 1) % 2)


def _mix(proj, vt, gates_col, gates_row, bias_col, bias_row, wq, wk, norm_g,
         x, y_a, w_a, w_b, w_out, g2, cast_w, layer, batch):
    _, t, width = proj.shape
    seq = t // batch
    d = x.shape[-1]
    head_dim = width // ML_HEADS
    n_if = gates_row.shape[1]
    n_steps = seq // CHUNK
    n_gate = N_PROJ - PROJ_GA
    assert PROJ_O == PROJ_M + 1 and PROJ_M % 2 == 0 and PROJ_GA % n_gate == 0
    assert batch * n_if <= LANES, (batch, n_if)
    proj4 = proj.reshape(proj.shape[0], batch, seq, width)
    cast_rows = _cast_rows(cast_w.shape[1], n_steps)
    at = lambda n: jnp.minimum(n, n_steps - 1)
    done = lambda n: jnp.maximum(n - 1, 0)
    const2 = lambda n: (0, 0)
    const3 = lambda n: (0, 0, 0)
    resident = dict(pipeline_mode=pl.Buffered(1))
    n_state = batch * ML_HEADS
    kern = functools.partial(_mix_kernel, head_dim=head_dim, batch=batch, n_steps=n_steps)
    x1, hn, cast = pl.pallas_call(
        kern,
        grid=(n_steps + 1,),
        in_specs=[
            pl.BlockSpec((2, batch, CHUNK, width), lambda n: (PROJ_M // 2, 0, at(n), 0)),
            pl.BlockSpec((batch, width, CHUNK), lambda n: (0, 0, at(n))),
            pl.BlockSpec((batch, CHUNK, LANES), lambda n: (0, at(n), 0)),
            pl.BlockSpec((batch, n_if, CHUNK), lambda n: (0, 0, at(n))),
            pl.BlockSpec((n_if, 1), const2),
            pl.BlockSpec((1, LANES), const2),
            pl.BlockSpec((ML_HEADS, head_dim, head_dim), const3),
            pl.BlockSpec((ML_HEADS, head_dim, head_dim), const3),
            pl.BlockSpec((1, width), const2),
            pl.BlockSpec((batch, CHUNK, d), lambda n: (0, done(n), 0)),
            pl.BlockSpec((batch, CHUNK, width), lambda n: (0, done(n), 0)),
            pl.BlockSpec((n_gate, batch, CHUNK, SECTION), lambda n: (PROJ_GA // n_gate, 0, done(n), 0)),
            pl.BlockSpec((width, d), const2, **resident),
            pl.BlockSpec((width, d), const2, **resident),
            pl.BlockSpec((d, d), const2, **resident),
            pl.BlockSpec((1, d), const2),
            pl.BlockSpec((None, cast_rows, cast_w.shape[2]), lambda n: (layer, at(n), 0)),
        ],
        out_specs=[
            pl.BlockSpec((batch, CHUNK, d), lambda n: (0, done(n), 0)),
            pl.BlockSpec((batch, CHUNK, d), lambda n: (0, done(n), 0)),
            pl.BlockSpec((cast_rows, cast_w.shape[2]), lambda n: (at(n), 0)),
        ],
        out_shape=[
            jax.ShapeDtypeStruct((batch, seq, d), F32),
            jax.ShapeDtypeStruct((batch, seq, d), BF16),
            jax.ShapeDtypeStruct(cast_w.shape[1:], BF16),
        ],
        scratch_shapes=[
            pltpu.VMEM((n_state, head_dim + SUBLANES, head_dim), F32),
            pltpu.VMEM((n_state, LANES), F32),
            pltpu.VMEM((ML_HEADS, head_dim, head_dim), BF16),
            pltpu.VMEM((ML_HEADS, head_dim, head_dim), BF16),
            pltpu.VMEM((2, batch, CHUNK, width), BF16),
        ],
        compiler_params=_params("arbitrary"),
        name="mix",
    )(proj4, vt, gates_col.reshape(batch, seq, LANES), gates_row, bias_col, bias_row, wq, wk, norm_g,
      x, y_a.reshape(batch, seq, width), proj4, w_a, w_b, w_out, g2, cast_w)
    return x1.reshape(t, d), hn.reshape(t, d), cast


def _ffn_kernel(hn_ref, x1_ref, w1_ref, w2_ref, gf_ref, out_ref, *, final_norm):
    j = pl.program_id(1)

    @pl.when(j == 0)
    def _():
        out_ref[...] = x1_ref[...]

    h = jnp.square(jnp.maximum(_dot(hn_ref[...], w1_ref[...]), 0.0)).astype(BF16)
    out_ref[...] += _dot(h, w2_ref[...])

    if final_norm:
        @pl.when(j == pl.num_programs(1) - 1)
        def _():
            out_ref[...] = _rms_norm(out_ref[...], gf_ref[...])


def _ffn(hn, x1, w1, w2, gf, tm, tf, final_norm):
    t, d = x1.shape
    d_ff = w1.shape[1]
    return pl.pallas_call(
        functools.partial(_ffn_kernel, final_norm=final_norm),
        grid=(t // tm, d_ff // tf),
        in_specs=[
            pl.BlockSpec((tm, d), lambda i, j: (i, 0)),
            pl.BlockSpec((tm, d), lambda i, j: (i, 0)),
            pl.BlockSpec((d, tf), lambda i, j: (0, j)),
            pl.BlockSpec((tf, d), lambda i, j: (j, 0)),
            pl.BlockSpec((1, d), lambda i, j: (0, 0)),
        ],
        out_specs=pl.BlockSpec((tm, d), lambda i, j: (i, 0)),
        out_shape=jax.ShapeDtypeStruct((t, d), F32),
        compiler_params=_params("arbitrary", "arbitrary"),
        name="ffn",
    )(hn, x1, w1, w2, gf)


def _tiles(t):
    return dict(prologue=min(t, 1024), inproj=min(t, 1024), gmlp=min(t, 1024), ffn_m=min(t, 512), ffn_f=2048)


def kernel(x, norm1_g, w_in, b_gate, gm_ln_g, gm_ln_b, gm_ws, gm_bs, ml_conv_w, ml_conv_b, ml_wq, ml_wk,
           ml_ig_b, ml_fg_b, ml_norm_g, w_a, w_b, w_out, norm2_g, w_ff1, w_ff2, norm_f_g):
    batch, seq, d = x.shape
    depth = w_in.shape[0]
    t = batch * seq
    tiles = _tiles(t)
    if_row = SEC_GA * SECTION
    n_if = 2 * ML_HEADS
    xt = x.reshape(t, d)
    w_t = jnp.swapaxes(w_in, 1, 2)
    for l in range(depth):
        gate_bias = jnp.concatenate([ml_ig_b[l], ml_fg_b[l]])
        vecs = jnp.concatenate([
            gm_ln_g[l][None], gm_ln_b[l][None], ml_conv_b[l][None], ml_conv_w[l],
            jnp.zeros((1, SECTION), F32), b_gate[l].reshape(4, SECTION),
            jnp.zeros((N_VEC_ROWS - VEC_GATE_B - 5, SECTION), F32)])
        xn, gates_col, gates_row = _prologue(xt, norm1_g[l][None], w_t, l, if_row, n_if, batch,
                                             tiles["prologue"])
        proj, vt, w2_bf, wa_bf, wb_bf, wo_bf = _inproj(
            xn, w_t, l, if_row + n_if, vecs, (w_ff2, w_a, w_b, w_out), batch, tiles["inproj"])
        y_a = _gmlp(proj, gm_ws[l], gm_bs[l].T, tiles["gmlp"])
        x1, hn, w1_bf = _mix(
            proj, vt, gates_col, gates_row, gate_bias[:, None], jnp.pad(gate_bias, (0, LANES - n_if))[None],
            ml_wq[l], ml_wk[l], ml_norm_g[l][None], xt.reshape(batch, seq, d), y_a, wa_bf, wb_bf, wo_bf,
            norm2_g[l][None],
            w_ff1, l, batch)
        xt = _ffn(hn, x1, w1_bf, w2_bf, norm_f_g[None], tiles["ffn_m"], tiles["ffn_f"],
                  final_norm=l == depth - 1)
    return xt.reshape(batch, seq, d)
```

```python
import functools

import jax
import jax.numpy as jnp
from jax import lax
from jax.experimental import pallas as pl
from jax.experimental.pallas import tpu as pltpu

EPS = 1e-6
GM_GROUP_DIM = 128
CHUNK = 128
ML_HEADS = 4
CONV_K = 4
LANES = 128
SUBLANES = 8
BF16_ROWS = 16
SECTION = 1024
SEC_U, SEC_V, SEC_M, SEC_MV, SEC_O, SEC_GA, SEC_GB = 0, 1, 2, 3, 4, 5, 7
N_SECTIONS = 9
PROJ_U, PROJ_V, PROJ_M, PROJ_O, PROJ_GA, PROJ_GB = 0, 1, 2, 3, 4, 6
N_PROJ = 8
VMEM_LIMIT = 56 * 1024 * 1024

BF16 = jnp.bfloat16
F32 = jnp.float32


def _sigmoid(x):
    return 0.5 * jnp.tanh(0.5 * x) + 0.5


def _log_sigmoid(x):
    return jnp.minimum(x, 0.0) - jnp.log(1.0 + jnp.exp(-jnp.abs(x)))


def _rms_norm(x, g):
    return x * lax.rsqrt(jnp.mean(x * x, axis=-1, keepdims=True) + EPS) * g


def _dot(a, b):
    return jnp.dot(a, b, preferred_element_type=F32)


def _split3(x):
    hi = x.astype(BF16)
    rest = x - hi.astype(F32)
    mid = rest.astype(BF16)
    return hi, mid, (rest - mid.astype(F32)).astype(BF16)


def _params(*semantics):
    return pltpu.CompilerParams(dimension_semantics=semantics, vmem_limit_bytes=VMEM_LIMIT)


def _cast_rows(n_rows, n_steps):
    rows = n_rows // n_steps
    assert rows * n_steps == n_rows and rows % BF16_ROWS == 0, (n_rows, n_steps)
    return rows


def _nt_dot(a, b):
    return lax.dot_general(a, b, (((1,), (1,)), ((), ())), preferred_element_type=F32)


def _prologue_kernel(x_ref, g1_ref, wif_ref, xn_ref, gcol_ref, grow_ref):
    xn = _rms_norm(x_ref[...], g1_ref[...]).astype(BF16)
    xn_ref[...] = xn
    wif = wif_ref[...].astype(BF16)
    gcol_ref[...] = _nt_dot(xn, wif)
    grow_ref[...] = _nt_dot(wif, xn)[:grow_ref.shape[0], :]


def _prologue(x, g1, w_t, layer, if_row, n_if, batch, tm):
    t, d = x.shape
    seq = t // batch
    tiles_per_seq = seq // tm
    assert tiles_per_seq * tm == seq
    return pl.pallas_call(
        _prologue_kernel,
        grid=(t // tm,),
        in_specs=[
            pl.BlockSpec((tm, d), lambda i: (i, 0)),
            pl.BlockSpec((1, d), lambda i: (0, 0)),
            pl.BlockSpec((None, LANES, d), lambda i: (layer, if_row // LANES, 0)),
        ],
        out_specs=[
            pl.BlockSpec((tm, d), lambda i: (i, 0)),
            pl.BlockSpec((tm, LANES), lambda i: (i, 0)),
            pl.BlockSpec((None, n_if, tm), lambda i: (i // tiles_per_seq, 0, i % tiles_per_seq)),
        ],
        out_shape=[
            jax.ShapeDtypeStruct((t, d), BF16),
            jax.ShapeDtypeStruct((t, LANES), F32),
            jax.ShapeDtypeStruct((batch, n_if, seq), F32),
        ],
        compiler_params=_params("arbitrary"),
        name="prologue",
    )(x, g1, w_t)


VEC_LN_G, VEC_LN_B, VEC_CONV_B, VEC_CONV_W, VEC_GATE_B = 0, 1, 2, 3, 7
N_VEC_ROWS = 16


def _inproj_kernel(xn_ref, w_ref, vec_ref, *rest, tiles_per_seq, n_cast_steps, n_cast):
    cast_in, (out_ref, vt_out, *cast_out), (wbf_ref, tail_ref) = (
        rest[:n_cast], rest[n_cast:2 * n_cast + 2], rest[2 * n_cast + 2:])
    j = pl.program_id(0)
    i = pl.program_id(1)
    tm = xn_ref.shape[0]
    vec = lambda r, n=1: vec_ref[pl.ds(r, n), :]

    @pl.when(j * pl.num_programs(1) + i < n_cast_steps)
    def _():
        for src, dst in zip(cast_in, cast_out):
            dst[...] = src[...].astype(BF16)

    @pl.when(i == 0)
    def _():
        rows = 256
        for r in range(0, wbf_ref.shape[0], rows):
            wbf_ref[r:r + rows, :] = w_ref[r:r + rows, :].astype(BF16)

    def proj():
        return _nt_dot(xn_ref[...], wbf_ref[...])

    @pl.when(j == SEC_U)
    def _():
        out_ref[...] = jax.nn.gelu(proj()).astype(BF16)

    @pl.when(j == SEC_V)
    def _():
        v = jax.nn.gelu(proj())
        mu = jnp.mean(v, axis=-1, keepdims=True)
        var = jnp.mean(jnp.square(v - mu), axis=-1, keepdims=True)
        out_ref[...] = ((v - mu) * lax.rsqrt(var + EPS) * vec(VEC_LN_G) + vec(VEC_LN_B)).astype(BF16)

    @pl.when(j == SEC_M)
    def _():
        @pl.when(i % tiles_per_seq == 0)
        def _():
            tail_ref[...] = jnp.zeros(tail_ref.shape, F32)

        x = proj()
        xe = jnp.concatenate([tail_ref[...], x], axis=0)
        conv = vec(VEC_CONV_B) + vec(VEC_CONV_W + CONV_K - 1) * x
        for d in range(1, CONV_K):
            conv = conv + vec(VEC_CONV_W + CONV_K - 1 - d) * xe[SUBLANES - d:SUBLANES - d + tm, :]
        tail_ref[...] = x[tm - SUBLANES:, :]
        out_ref[...] = (conv * _sigmoid(conv)).astype(BF16)

    @pl.when(j == SEC_MV)
    def _():
        vt_out[...] = _nt_dot(wbf_ref[...], xn_ref[...]).astype(BF16)

    @pl.when(j >= SEC_O)
    def _():
        out_ref[...] = _sigmoid(proj() + vec(VEC_GATE_B + j - SEC_O)).astype(BF16)


def _inproj(xn, w_t, layer, gate_row, vecs, cast_ws, batch, tm):
    t, d = xn.shape
    n_i = t // tm
    seq = t // batch
    tiles_per_seq = seq // tm
    assert tiles_per_seq * tm == seq
    n_cast_steps = (N_SECTIONS - 1) * n_i
    cast_rows = [_cast_rows(w.shape[1], n_cast_steps) for w in cast_ws]
    cast_step = lambda j, i: jnp.minimum(j * n_i + i, n_cast_steps - 1)

    def proj_block(j, i):
        mv = j == SEC_MV
        return (j - (j >= SEC_MV).astype(jnp.int32), jnp.where(mv, n_i - 1, i), 0)

    def vt_block(j, i):
        tile = jnp.clip((j - SEC_MV) * n_i + i, 0, n_i - 1)
        return (tile // tiles_per_seq, 0, tile % tiles_per_seq)

    gate_skip = gate_row - SEC_GA * SECTION
    assert gate_skip % SUBLANES == 0
    kern = functools.partial(_inproj_kernel, tiles_per_seq=tiles_per_seq, n_cast_steps=n_cast_steps,
                             n_cast=len(cast_ws))
    return pl.pallas_call(
        kern,
        grid=(N_SECTIONS, n_i),
        in_specs=[
            pl.BlockSpec((tm, d), lambda j, i: (i, 0)),
            pl.BlockSpec((None, pl.Element(SECTION), pl.Element(d)),
                         lambda j, i: (layer, SUBLANES * (j * (SECTION // SUBLANES) + jnp.where(
                             j >= SEC_GA, gate_skip // SUBLANES, 0)), 0)),
            pl.BlockSpec((N_VEC_ROWS, SECTION), lambda j, i: (0, 0)),
        ] + [pl.BlockSpec((None, r, w.shape[2]), lambda j, i: (layer, cast_step(j, i), 0))
             for r, w in zip(cast_rows, cast_ws)],
        out_specs=[pl.BlockSpec((None, tm, SECTION), proj_block), pl.BlockSpec((None, SECTION, tm), vt_block)]
        + [pl.BlockSpec((r, w.shape[2]), lambda j, i: (cast_step(j, i), 0)) for r, w in zip(cast_rows, cast_ws)],
        out_shape=[jax.ShapeDtypeStruct((N_PROJ, t, SECTION), BF16),
                   jax.ShapeDtypeStruct((batch, SECTION, seq), BF16)]
        + [jax.ShapeDtypeStruct(w.shape[1:], BF16) for w in cast_ws],
        scratch_shapes=[
            pltpu.VMEM((SECTION, d), BF16),
            pltpu.VMEM((SUBLANES, SECTION), F32),
        ],
        compiler_params=_params("arbitrary", "arbitrary"),
        name="inproj",
    )(xn, w_t, vecs, *cast_ws)


def _gmlp_kernel(u_ref, v_ref, ws_ref, bs_ref, out_ref, *, n_chunks, n_groups):
    row = lax.broadcasted_iota(jnp.int32, (CHUNK, CHUNK), 0)
    col = lax.broadcasted_iota(jnp.int32, (CHUNK, CHUNK), 1)
    causal = col <= row
    for g in range(n_groups):
        gs = slice(g * GM_GROUP_DIM, (g + 1) * GM_GROUP_DIM)
        w = jnp.where(causal, ws_ref[g], 0.0).astype(BF16)
        vg = jnp.concatenate([v_ref[c * CHUNK:(c + 1) * CHUNK, gs] for c in range(n_chunks)], axis=1)
        s = _dot(w, vg) + bs_ref[:, g:g + 1]
        for c in range(n_chunks):
            rs = slice(c * CHUNK, (c + 1) * CHUNK)
            sc = s[:, c * GM_GROUP_DIM:(c + 1) * GM_GROUP_DIM]
            out_ref[rs, gs] = (u_ref[rs, gs].astype(F32) * sc).astype(BF16)


def _gmlp(proj, ws, bs_t, tm):
    _, t, width = proj.shape
    n_groups = width // GM_GROUP_DIM
    kern = functools.partial(_gmlp_kernel, n_chunks=tm // CHUNK, n_groups=n_groups)
    return pl.pallas_call(
        kern,
        grid=(t // tm,),
        in_specs=[
            pl.BlockSpec((None, tm, width), lambda i: (PROJ_U, i, 0)),
            pl.BlockSpec((None, tm, width), lambda i: (PROJ_V, i, 0)),
            pl.BlockSpec((n_groups, CHUNK, CHUNK), lambda i: (0, 0, 0)),
            pl.BlockSpec((CHUNK, n_groups), lambda i: (0, 0)),
        ],
        out_specs=pl.BlockSpec((tm, width), lambda i: (i, 0)),
        out_shape=jax.ShapeDtypeStruct((t, width), BF16),
        compiler_params=_params("arbitrary"),
        name="gmlp",
    )(proj, proj, ws, bs_t)


def _lanes(col):
    return jnp.broadcast_to(col, (col.shape[0], LANES))


def _wide(x, n):
    return jnp.concatenate([x] * n, axis=1)


def _mlstm_chunk(c_ref, o_ref, vt_ref, gcol_ref, grow_ref, bcol_ref, brow_ref, ng_ref,
                 st_ref, mx_ref, wqb_ref, wkb_ref, yb_ref, *, head_dim, batch):
    row = lax.broadcasted_iota(jnp.int32, (CHUNK, CHUNK), 0)
    col = lax.broadcasted_iota(jnp.int32, (CHUNK, CHUNK), 1)
    causal = col <= row
    tril = causal.astype(BF16)
    triu = (row <= col).astype(BF16)
    nd = head_dim // LANES
    groups = CHUNK // SUBLANES
    heads = [(b, h) for b in range(batch) for h in range(ML_HEADS)]
    slot = lambda b, h: b * ML_HEADS + h
    hsl = lambda h: slice(h * head_dim, (h + 1) * head_dim)


    n_if = grow_ref.shape[1]
    lane = lax.broadcasted_iota(jnp.int32, (CHUNK, LANES), 1)
    g_col = gcol_ref[0] + brow_ref[...]
    for b in range(1, batch):
        g_col = jnp.where(lane < b * n_if, g_col, pltpu.roll(gcol_ref[b] + brow_ref[...], b * n_if, 1))
    g_row = jnp.concatenate([grow_ref[b] + bcol_ref[...] for b in range(batch)], axis=0)
    lf_row = _log_sigmoid(g_row)
    bcum_col = _dot(jnp.concatenate([tril] * 3, axis=1), jnp.concatenate(_split3(_log_sigmoid(g_col)), axis=0))
    bcum_row = _dot(jnp.concatenate(_split3(lf_row), axis=1), jnp.concatenate([triu] * 3, axis=0))
    total = jnp.sum(lf_row, axis=-1, keepdims=True)

    q, k = {}, {}
    for h in range(ML_HEADS):
        c = jnp.concatenate([c_ref[b, :, hsl(h)] for b in range(batch)], axis=0)
        q_h, k_h = _dot(c, wqb_ref[h]), _dot(c, wkb_ref[h])
        for b in range(batch):
            q[b, h], k[b, h] = q_h[b * CHUNK:(b + 1) * CHUNK], k_h[b * CHUNK:(b + 1) * CHUNK]

    w_intra, w_inter, inv_floor, w_state, decay, scale, m_new = {}, {}, {}, {}, {}, {}, {}
    for b, h in heads:
        i, f = b * n_if + h, b * n_if + ML_HEADS + h
        bc = _lanes(bcum_col[:, f:f + 1])
        ig = _lanes(g_col[:, i:i + 1])
        src = g_row[i:i + 1, :] - bcum_row[f:f + 1, :]
        b_last = jnp.broadcast_to(total[f:f + 1, :], (1, LANES))
        m_prev = mx_ref[slot(b, h):slot(b, h) + 1, :]
        inter = bc + m_prev
        dlog = jnp.where(causal, bc + src, -jnp.inf)
        m_t = jnp.maximum(inter, _lanes(jnp.max(dlog, axis=-1, keepdims=True)))
        w_intra[b, h] = jnp.exp(dlog - m_t)
        w_inter[b, h] = jnp.exp(inter - m_t)
        inv_floor[b, h] = jnp.exp(-m_t)
        a = b_last - bc + ig
        a_max = jnp.max(a, axis=0, keepdims=True)
        w_state[b, h] = jnp.exp(a - a_max)
        m_new[b, h] = jnp.maximum(b_last + m_prev, a_max)
        decay[b, h] = jnp.exp(b_last + m_prev - m_new[b, h])
        scale[b, h] = jnp.exp(a_max - m_new[b, h])

    qk = {}
    for b, h in heads:
        qk[b, h] = _nt_dot(q[b, h].astype(BF16), k[b, h].astype(BF16)) * w_intra[b, h]

    for b, h in heads:
        s = slot(b, h)
        qw = q[b, h] * _wide(w_inter[b, h], nd)
        lhs = jnp.concatenate([qw.astype(BF16), qk[b, h].astype(BF16)], axis=1)
        rhs = jnp.concatenate([st_ref[s, :head_dim, :].astype(BF16), vt_ref[b, hsl(h), :]], axis=1)
        num = _nt_dot(lhs, rhs)
        qn = (qw.reshape(groups, SUBLANES, head_dim) * st_ref[s, head_dim:, :]).reshape(CHUNK, head_dim)
        part = qk[b, h] + sum(qn[:, i * LANES:(i + 1) * LANES] for i in range(nd))
        den = _lanes(jnp.sum(part, axis=-1, keepdims=True))
        hid = num * _wide(1.0 / jnp.maximum(jnp.abs(den), inv_floor[b, h]), nd)
        mu = jnp.mean(hid, axis=-1, keepdims=True)
        var = jnp.mean(jnp.square(hid - mu), axis=-1, keepdims=True)
        hid = (hid - mu) * lax.rsqrt(var + EPS) * ng_ref[:, hsl(h)]
        yb_ref[b, :, hsl(h)] = (o_ref[b, :, hsl(h)].astype(F32) * hid).astype(BF16)

    for b, h in heads:
        s = slot(b, h)
        kw = k[b, h] * _wide(w_state[b, h] * scale[b, h], nd)
        upd = _dot(vt_ref[b, hsl(h), :], kw.astype(BF16))
        kw_sum = jnp.sum(jnp.sum(kw.reshape(groups, SUBLANES, head_dim), axis=0), axis=0, keepdims=True)
        keep = _wide(decay[b, h], nd)
        st_ref[s, :head_dim, :] = keep * st_ref[s, :head_dim, :] + upd
        st_ref[s, head_dim:, :] = keep * st_ref[s, head_dim:, :] + kw_sum
        mx_ref[s:s + 1, :] = m_new[b, h]


def _mix_kernel(co_ref, vt_ref, gcol_ref, grow_ref, bcol_ref, brow_ref, wq_ref, wk_ref, ng_ref,
                x_ref, uv_ref, ws_ref, bs_ref, gate_ref, wa_ref, wb_ref, wo_ref, g2_ref, cast_ref,
                x1_ref, hn_ref, cast_out, st_ref, mx_ref, wqb_ref, wkb_ref, yb_ref, wsb_ref,
                *, head_dim, batch, n_steps):
    n = pl.program_id(0)

    @pl.when(n == 0)
    def _():
        row = lax.broadcasted_iota(jnp.int32, (CHUNK, CHUNK), 0)
        col = lax.broadcasted_iota(jnp.int32, (CHUNK, CHUNK), 1)
        wsb_ref[...] = jnp.where(col <= row, ws_ref[...], 0.0).astype(BF16)
        wqb_ref[...] = wq_ref[...].astype(BF16)
        wkb_ref[...] = (wk_ref[...] * (head_dim ** -0.5)).astype(BF16)
        st_ref[...] = jnp.zeros(st_ref.shape, F32)
        mx_ref[...] = jnp.zeros(mx_ref.shape, F32)

    @pl.when(n < n_steps)
    def _():
        cast_out[...] = cast_ref[...].astype(BF16)

    def mlstm(slot):
        _mlstm_chunk(co_ref.at[0], co_ref.at[1], vt_ref, gcol_ref, grow_ref, bcol_ref, brow_ref, ng_ref,
                     st_ref, mx_ref, wqb_ref, wkb_ref, yb_ref.at[slot], head_dim=head_dim, batch=batch)

    def spatial_gate():
        gated = [[] for _ in range(batch)]
        for g in range(wsb_ref.shape[0]):
            gs = slice(g * GM_GROUP_DIM, (g + 1) * GM_GROUP_DIM)
            vg = jnp.concatenate([uv_ref[1, b, :, gs] for b in range(batch)], axis=1)
            s = _dot(wsb_ref[g], vg) + bs_ref[:, g:g + 1]
            for b in range(batch):
                sb = s[:, b * GM_GROUP_DIM:(b + 1) * GM_GROUP_DIM]
                gated[b].append((uv_ref[0, b, :, gs].astype(F32) * sb).astype(BF16))
        return jnp.concatenate([jnp.concatenate(cols, axis=1) for cols in gated], axis=0)

    def merge(slot):
        flat = lambda v: v.reshape(batch * CHUNK, v.shape[-1])
        gate_a = jnp.concatenate([flat(gate_ref[0]), flat(gate_ref[1])], axis=1).astype(F32)
        gate_b = jnp.concatenate([flat(gate_ref[2]), flat(gate_ref[3])], axis=1).astype(F32)
        mixed = (gate_a * _dot(spatial_gate(), wa_ref[...])
                 + gate_b * _dot(flat(yb_ref[slot]), wb_ref[...]))
        x1 = flat(x_ref[...]) + _dot(mixed.astype(BF16), wo_ref[...])
        x1_ref[...] = x1.reshape(x1_ref.shape)
        hn_ref[...] = _rms_norm(x1, g2_ref[...]).astype(BF16).reshape(hn_ref.shape)

    @pl.when(n == 0)
    def _():
        mlstm(0)

    for parity in range(2):
        @pl.when((n > 0) & (n < n_steps) & (n % 2 == parity))
        def _(parity=parity):
            mlstm(parity)
            merge(1 - parity)

    @pl.when(n == n_steps)
    def _():
        merge((n_steps - 1) % 2)


def _mix(proj, vt, gates_col, gates_row, bias_col, bias_row, wq, wk, norm_g,
         x, ws, bs_t, w_a, w_b, w_out, g2, cast_w, layer, batch):
    _, t, width = proj.shape
    n_groups = width // GM_GROUP_DIM
    assert ws.shape == (n_groups, CHUNK, CHUNK) and PROJ_V == PROJ_U + 1 and PROJ_U % 2 == 0
    seq = t // batch
    d = x.shape[-1]
    head_dim = width // ML_HEADS
    n_if = gates_row.shape[1]
    n_steps = seq // CHUNK
    n_gate = N_PROJ - PROJ_GA
    assert PROJ_O == PROJ_M + 1 and PROJ_M % 2 == 0 and PROJ_GA % n_gate == 0
    assert batch * n_if <= LANES, (batch, n_if)
    proj4 = proj.reshape(proj.shape[0], batch, seq, width)
    cast_rows = _cast_rows(cast_w.shape[1], n_steps)
    at = lambda n: jnp.minimum(n, n_steps - 1)
    done = lambda n: jnp.maximum(n - 1, 0)
    const2 = lambda n: (0, 0)
    const3 = lambda n: (0, 0, 0)
    resident = dict(pipeline_mode=pl.Buffered(1))
    n_state = batch * ML_HEADS
    kern = functools.partial(_mix_kernel, head_dim=head_dim, batch=batch, n_steps=n_steps)
    x1, hn, cast = pl.pallas_call(
        kern,
        grid=(n_steps + 1,),
        in_specs=[
            pl.BlockSpec((2, batch, CHUNK, width), lambda n: (PROJ_M // 2, 0, at(n), 0)),
            pl.BlockSpec((batch, width, CHUNK), lambda n: (0, 0, at(n))),
            pl.BlockSpec((batch, CHUNK, LANES), lambda n: (0, at(n), 0)),
            pl.BlockSpec((batch, n_if, CHUNK), lambda n: (0, 0, at(n))),
            pl.BlockSpec((n_if, 1), const2),
            pl.BlockSpec((1, LANES), const2),
            pl.BlockSpec((ML_HEADS, head_dim, head_dim), const3),
            pl.BlockSpec((ML_HEADS, head_dim, head_dim), const3),
            pl.BlockSpec((1, width), const2),
            pl.BlockSpec((batch, CHUNK, d), lambda n: (0, done(n), 0)),
            pl.BlockSpec((2, batch, CHUNK, width), lambda n: (PROJ_U // 2, 0, done(n), 0)),
            pl.BlockSpec((n_groups, CHUNK, CHUNK), const3),
            pl.BlockSpec((CHUNK, n_groups), const2),
            pl.BlockSpec((n_gate, batch, CHUNK, SECTION), lambda n: (PROJ_GA // n_gate, 0, done(n), 0)),
            pl.BlockSpec((width, d), const2, **resident),
            pl.BlockSpec((width, d), const2, **resident),
            pl.BlockSpec((d, d), const2, **resident),
            pl.BlockSpec((1, d), const2),
            pl.BlockSpec((None, cast_rows, cast_w.shape[2]), lambda n: (layer, at(n), 0)),
        ],
        out_specs=[
            pl.BlockSpec((batch, CHUNK, d), lambda n: (0, done(n), 0)),
            pl.BlockSpec((batch, CHUNK, d), lambda n: (0, done(n), 0)),
            pl.BlockSpec((cast_rows, cast_w.shape[2]), lambda n: (at(n), 0)),
        ],
        out_shape=[
            jax.ShapeDtypeStruct((batch, seq, d), F32),
            jax.ShapeDtypeStruct((batch, seq, d), BF16),
            jax.ShapeDtypeStruct(cast_w.shape[1:], BF16),
        ],
        scratch_shapes=[
            pltpu.VMEM((n_state, head_dim + SUBLANES, head_dim), F32),
            pltpu.VMEM((n_state, LANES), F32),
            pltpu.VMEM((ML_HEADS, head_dim, head_dim), BF16),
            pltpu.VMEM((ML_HEADS, head_dim, head_dim), BF16),
            pltpu.VMEM((2, batch, CHUNK, width), BF16),
            pltpu.VMEM((n_groups, CHUNK, CHUNK), BF16),
        ],
        compiler_params=_params("arbitrary"),
        name="mix",
    )(proj4, vt, gates_col.reshape(batch, seq, LANES), gates_row, bias_col, bias_row, wq, wk, norm_g,
      x, proj4, ws, bs_t, proj4, w_a, w_b, w_out, g2, cast_w)
    return x1.reshape(t, d), hn.reshape(t, d), cast


def _ffn_kernel(hn_ref, x1_ref, w1_ref, w2_ref, gf_ref, out_ref, *, final_norm):
    j = pl.program_id(1)

    @pl.when(j == 0)
    def _():
        out_ref[...] = x1_ref[...]

    h = jnp.square(jnp.maximum(_dot(hn_ref[...], w1_ref[...]), 0.0)).astype(BF16)
    out_ref[...] += _dot(h, w2_ref[...])

    if final_norm:
        @pl.when(j == pl.num_programs(1) - 1)
        def _():
            out_ref[...] = _rms_norm(out_ref[...], gf_ref[...])


def _ffn(hn, x1, w1, w2, gf, tm, tf, final_norm):
    t, d = x1.shape
    d_ff = w1.shape[1]
    return pl.pallas_call(
        functools.partial(_ffn_kernel, final_norm=final_norm),
        grid=(t // tm, d_ff // tf),
        in_specs=[
            pl.BlockSpec((tm, d), lambda i, j: (i, 0)),
            pl.BlockSpec((tm, d), lambda i, j: (i, 0)),
            pl.BlockSpec((d, tf), lambda i, j: (0, j)),
            pl.BlockSpec((tf, d), lambda i, j: (j, 0)),
            pl.BlockSpec((1, d), lambda i, j: (0, 0)),
        ],
        out_specs=pl.BlockSpec((tm, d), lambda i, j: (i, 0)),
        out_shape=jax.ShapeDtypeStruct((t, d), F32),
        compiler_params=_params("arbitrary", "arbitrary"),
        name="ffn",
    )(hn, x1, w1, w2, gf)


def _tiles(t):
    return dict(prologue=min(t, 1024), inproj=min(t, 1024), gmlp=min(t, 1024), ffn_m=min(t, 512), ffn_f=2048)


def kernel(x, norm1_g, w_in, b_gate, gm_ln_g, gm_ln_b, gm_ws, gm_bs, ml_conv_w, ml_conv_b, ml_wq, ml_wk,
           ml_ig_b, ml_fg_b, ml_norm_g, w_a, w_b, w_out, norm2_g, w_ff1, w_ff2, norm_f_g):
    batch, seq, d = x.shape
    depth = w_in.shape[0]
    t = batch * seq
    tiles = _tiles(t)
    if_row = SEC_GA * SECTION
    n_if = 2 * ML_HEADS
    xt = x.reshape(t, d)
    w_t = jnp.swapaxes(w_in, 1, 2)
    for l in range(depth):
        gate_bias = jnp.concatenate([ml_ig_b[l], ml_fg_b[l]])
        vecs = jnp.concatenate([
            gm_ln_g[l][None], gm_ln_b[l][None], ml_conv_b[l][None], ml_conv_w[l],
            jnp.zeros((1, SECTION), F32), b_gate[l].reshape(4, SECTION),
            jnp.zeros((N_VEC_ROWS - VEC_GATE_B - 5, SECTION), F32)])
        xn, gates_col, gates_row = _prologue(xt, norm1_g[l][None], w_t, l, if_row, n_if, batch,
                                             tiles["prologue"])
        proj, vt, w2_bf, wa_bf, wb_bf, wo_bf = _inproj(
            xn, w_t, l, if_row + n_if, vecs, (w_ff2, w_a, w_b, w_out), batch, tiles["inproj"])
        x1, hn, w1_bf = _mix(
            proj, vt, gates_col, gates_row, gate_bias[:, None], jnp.pad(gate_bias, (0, LANES - n_if))[None],
            ml_wq[l], ml_wk[l], ml_norm_g[l][None], xt.reshape(batch, seq, d), gm_ws[l], gm_bs[l].T,
            wa_bf, wb_bf, wo_bf, norm2_g[l][None], w_ff1, l, batch)
        xt = _ffn(hn, x1, w1_bf, w2_bf, norm_f_g[None], tiles["ffn_m"], tiles["ffn_f"],
                  final_norm=l == depth - 1)
    return xt.reshape(batch, seq, d)
```

```python
import functools

import jax
import jax.numpy as jnp
from jax import lax
from jax.experimental import pallas as pl
from jax.experimental.pallas import tpu as pltpu

EPS = 1e-6
GM_GROUP_DIM = 128
CHUNK = 128
ML_HEADS = 4
CONV_K = 4
LANES = 128
SUBLANES = 8
BF16_ROWS = 16
SECTION = 1024
SEC_U, SEC_V, SEC_M, SEC_MV, SEC_O, SEC_GA, SEC_GB = 0, 1, 2, 3, 4, 5, 7
N_SECTIONS = 9
PROJ_GA, PROJ_GB, PROJ_M, PROJ_O, PROJ_V = 0, 2, 4, 5, 6
N_PROJ = 7
VMEM_LIMIT = 56 * 1024 * 1024

BF16 = jnp.bfloat16
F32 = jnp.float32


def _sigmoid(x):
    return 0.5 * jnp.tanh(0.5 * x) + 0.5


def _log_sigmoid(x):
    return jnp.minimum(x, 0.0) - jnp.log(1.0 + jnp.exp(-jnp.abs(x)))


def _rms_norm(x, g):
    return x * lax.rsqrt(jnp.mean(x * x, axis=-1, keepdims=True) + EPS) * g


def _dot(a, b):
    return jnp.dot(a, b, preferred_element_type=F32)


def _split3(x):
    hi = x.astype(BF16)
    rest = x - hi.astype(F32)
    mid = rest.astype(BF16)
    return hi, mid, (rest - mid.astype(F32)).astype(BF16)


def _params(*semantics):
    return pltpu.CompilerParams(dimension_semantics=semantics, vmem_limit_bytes=VMEM_LIMIT)


def _cast_rows(n_rows, n_steps):
    rows = n_rows // n_steps
    assert rows * n_steps == n_rows and rows % BF16_ROWS == 0, (n_rows, n_steps)
    return rows


def _nt_dot(a, b):
    return lax.dot_general(a, b, (((1,), (1,)), ((), ())), preferred_element_type=F32)


def _cast_section(w_ref, wbf_ref):
    rows = 256
    for r in range(0, wbf_ref.shape[0], rows):
        wbf_ref[r:r + rows, :] = w_ref[r:r + rows, :].astype(BF16)


def _prologue_kernel(x_ref, g1_ref, wif_ref, wu_ref, xn_ref, gcol_ref, grow_ref, u_ref, wub_ref):
    @pl.when(pl.program_id(0) == 0)
    def _():
        _cast_section(wu_ref, wub_ref)

    xn = _rms_norm(x_ref[...], g1_ref[...]).astype(BF16)
    xn_ref[...] = xn
    wif = wif_ref[...].astype(BF16)
    gcol_ref[...] = _nt_dot(xn, wif)
    grow_ref[...] = _nt_dot(wif, xn)[:grow_ref.shape[0], :]
    u_ref[...] = jax.nn.gelu(_nt_dot(xn, wub_ref[...])).astype(BF16)


def _prologue(x, g1, w_t, layer, if_row, n_if, batch, tm):
    t, d = x.shape
    seq = t // batch
    tiles_per_seq = seq // tm
    assert tiles_per_seq * tm == seq
    return pl.pallas_call(
        _prologue_kernel,
        grid=(t // tm,),
        in_specs=[
            pl.BlockSpec((tm, d), lambda i: (i, 0)),
            pl.BlockSpec((1, d), lambda i: (0, 0)),
            pl.BlockSpec((None, LANES, d), lambda i: (layer, if_row // LANES, 0)),
            pl.BlockSpec((None, SECTION, d), lambda i: (layer, SEC_U, 0), pipeline_mode=pl.Buffered(1)),
        ],
        out_specs=[
            pl.BlockSpec((tm, d), lambda i: (i, 0)),
            pl.BlockSpec((tm, LANES), lambda i: (i, 0)),
            pl.BlockSpec((None, n_if, tm), lambda i: (i // tiles_per_seq, 0, i % tiles_per_seq)),
            pl.BlockSpec((tm, SECTION), lambda i: (i, 0)),
        ],
        out_shape=[
            jax.ShapeDtypeStruct((t, d), BF16),
            jax.ShapeDtypeStruct((t, LANES), F32),
            jax.ShapeDtypeStruct((batch, n_if, seq), F32),
            jax.ShapeDtypeStruct((t, SECTION), BF16),
        ],
        scratch_shapes=[pltpu.VMEM((SECTION, d), BF16)],
        compiler_params=_params("arbitrary"),
        name="prologue",
    )(x, g1, w_t, w_t)


VEC_LN_G, VEC_LN_B, VEC_CONV_B, VEC_CONV_W, VEC_GATE_B = 0, 1, 2, 3, 7
N_VEC_ROWS = 16


def _inproj_kernel(xn_ref, w_ref, vec_ref, *rest, tiles_per_seq, n_cast_steps, n_cast):
    cast_in, (out_ref, vt_out, *cast_out), (wbf_ref, tail_ref) = (
        rest[:n_cast], rest[n_cast:2 * n_cast + 2], rest[2 * n_cast + 2:])
    j = pl.program_id(0) + SEC_V
    i = pl.program_id(1)
    tm = xn_ref.shape[0]
    vec = lambda r, n=1: vec_ref[pl.ds(r, n), :]

    @pl.when(pl.program_id(0) * pl.num_programs(1) + i < n_cast_steps)
    def _():
        for src, dst in zip(cast_in, cast_out):
            dst[...] = src[...].astype(BF16)

    @pl.when(i == 0)
    def _():
        _cast_section(w_ref, wbf_ref)

    def proj():
        return _nt_dot(xn_ref[...], wbf_ref[...])

    @pl.when(j == SEC_V)
    def _():
        v = jax.nn.gelu(proj())
        mu = jnp.mean(v, axis=-1, keepdims=True)
        var = jnp.mean(jnp.square(v - mu), axis=-1, keepdims=True)
        out_ref[...] = ((v - mu) * lax.rsqrt(var + EPS) * vec(VEC_LN_G) + vec(VEC_LN_B)).astype(BF16)

    @pl.when(j == SEC_M)
    def _():
        @pl.when(i % tiles_per_seq == 0)
        def _():
            tail_ref[...] = jnp.zeros(tail_ref.shape, F32)

        x = proj()
        xe = jnp.concatenate([tail_ref[...], x], axis=0)
        conv = vec(VEC_CONV_B) + vec(VEC_CONV_W + CONV_K - 1) * x
        for d in range(1, CONV_K):
            conv = conv + vec(VEC_CONV_W + CONV_K - 1 - d) * xe[SUBLANES - d:SUBLANES - d + tm, :]
        tail_ref[...] = x[tm - SUBLANES:, :]
        out_ref[...] = (conv * _sigmoid(conv)).astype(BF16)

    @pl.when(j == SEC_MV)
    def _():
        vt_out[...] = _nt_dot(wbf_ref[...], xn_ref[...]).astype(BF16)

    @pl.when(j >= SEC_O)
    def _():
        out_ref[...] = _sigmoid(proj() + vec(VEC_GATE_B + j - SEC_O)).astype(BF16)


def _inproj(xn, w_t, layer, gate_row, vecs, cast_ws, batch, tm):
    t, d = xn.shape
    n_i = t // tm
    seq = t // batch
    tiles_per_seq = seq // tm
    assert tiles_per_seq * tm == seq
    n_sec = N_SECTIONS - SEC_V
    n_cast_steps = n_sec * n_i
    cast_rows = [_cast_rows(w.shape[1], n_cast_steps) for w in cast_ws]
    cast_step = lambda s, i: s * n_i + i

    def proj_block(s, i):
        j = s + SEC_V
        plane = jnp.where(j == SEC_V, PROJ_V, jnp.where(
            j >= SEC_GA, j - SEC_GA + PROJ_GA, jnp.where(j >= SEC_O, PROJ_O, PROJ_M)))
        return (plane, jnp.where(j == SEC_MV, n_i - 1, i), 0)

    def vt_block(s, i):
        tile = jnp.clip((s + SEC_V - SEC_MV) * n_i + i, 0, n_i - 1)
        return (tile // tiles_per_seq, 0, tile % tiles_per_seq)

    def w_rows(s, i):
        j = s + SEC_V
        return (layer, SUBLANES * (j * (SECTION // SUBLANES) + jnp.where(j >= SEC_GA, gate_skip // SUBLANES, 0)), 0)

    gate_skip = gate_row - SEC_GA * SECTION
    assert gate_skip % SUBLANES == 0
    assert (PROJ_GB, PROJ_O) == (PROJ_GA + SEC_GB - SEC_GA, PROJ_M + 1) and SEC_M + 1 == SEC_MV == SEC_O - 1
    kern = functools.partial(_inproj_kernel, tiles_per_seq=tiles_per_seq, n_cast_steps=n_cast_steps,
                             n_cast=len(cast_ws))
    return pl.pallas_call(
        kern,
        grid=(n_sec, n_i),
        in_specs=[
            pl.BlockSpec((tm, d), lambda s, i: (i, 0)),
            pl.BlockSpec((None, pl.Element(SECTION), pl.Element(d)), w_rows),
            pl.BlockSpec((N_VEC_ROWS, SECTION), lambda s, i: (0, 0)),
        ] + [pl.BlockSpec((None, r, w.shape[2]), lambda s, i: (layer, cast_step(s, i), 0))
             for r, w in zip(cast_rows, cast_ws)],
        out_specs=[pl.BlockSpec((None, tm, SECTION), proj_block), pl.BlockSpec((None, SECTION, tm), vt_block)]
        + [pl.BlockSpec((r, w.shape[2]), lambda s, i: (cast_step(s, i), 0)) for r, w in zip(cast_rows, cast_ws)],
        out_shape=[jax.ShapeDtypeStruct((N_PROJ, t, SECTION), BF16),
                   jax.ShapeDtypeStruct((batch, SECTION, seq), BF16)]
        + [jax.ShapeDtypeStruct(w.shape[1:], BF16) for w in cast_ws],
        scratch_shapes=[
            pltpu.VMEM((SECTION, d), BF16),
            pltpu.VMEM((SUBLANES, SECTION), F32),
        ],
        compiler_params=_params("arbitrary", "arbitrary"),
        name="inproj",
    )(xn, w_t, vecs, *cast_ws)


def _lanes(col):
    return jnp.broadcast_to(col, (col.shape[0], LANES))


def _wide(x, n):
    return jnp.concatenate([x] * n, axis=1)


def _mlstm_chunk(c_ref, o_ref, vt_ref, gcol_ref, grow_ref, bcol_ref, brow_ref, ng_ref,
                 st_ref, mx_ref, wqb_ref, wkb_ref, yb_ref, *, head_dim, batch):
    row = lax.broadcasted_iota(jnp.int32, (CHUNK, CHUNK), 0)
    col = lax.broadcasted_iota(jnp.int32, (CHUNK, CHUNK), 1)
    causal = col <= row
    tril = causal.astype(BF16)
    triu = (row <= col).astype(BF16)
    nd = head_dim // LANES
    groups = CHUNK // SUBLANES
    heads = [(b, h) for b in range(batch) for h in range(ML_HEADS)]
    slot = lambda b, h: b * ML_HEADS + h
    hsl = lambda h: slice(h * head_dim, (h + 1) * head_dim)


    n_if = grow_ref.shape[1]
    lane = lax.broadcasted_iota(jnp.int32, (CHUNK, LANES), 1)
    g_col = gcol_ref[0] + brow_ref[...]
    for b in range(1, batch):
        g_col = jnp.where(lane < b * n_if, g_col, pltpu.roll(gcol_ref[b] + brow_ref[...], b * n_if, 1))
    g_row = jnp.concatenate([grow_ref[b] + bcol_ref[...] for b in range(batch)], axis=0)
    lf_row = _log_sigmoid(g_row)
    bcum_col = _dot(jnp.concatenate([tril] * 3, axis=1), jnp.concatenate(_split3(_log_sigmoid(g_col)), axis=0))
    bcum_row = _dot(jnp.concatenate(_split3(lf_row), axis=1), jnp.concatenate([triu] * 3, axis=0))
    total = jnp.sum(lf_row, axis=-1, keepdims=True)

    q, k = {}, {}
    for h in range(ML_HEADS):
        c = jnp.concatenate([c_ref[b, :, hsl(h)] for b in range(batch)], axis=0)
        q_h, k_h = _dot(c, wqb_ref[h]), _dot(c, wkb_ref[h])
        for b in range(batch):
            q[b, h], k[b, h] = q_h[b * CHUNK:(b + 1) * CHUNK], k_h[b * CHUNK:(b + 1) * CHUNK]

    w_intra, w_inter, inv_floor, w_state, decay, scale, m_new = {}, {}, {}, {}, {}, {}, {}
    for b, h in heads:
        i, f = b * n_if + h, b * n_if + ML_HEADS + h
        bc = _lanes(bcum_col[:, f:f + 1])
        ig = _lanes(g_col[:, i:i + 1])
        src = g_row[i:i + 1, :] - bcum_row[f:f + 1, :]
        b_last = jnp.broadcast_to(total[f:f + 1, :], (1, LANES))
        m_prev = mx_ref[slot(b, h):slot(b, h) + 1, :]
        inter = bc + m_prev
        dlog = jnp.where(causal, bc + src, -jnp.inf)
        m_t = jnp.maximum(inter, _lanes(jnp.max(dlog, axis=-1, keepdims=True)))
        w_intra[b, h] = jnp.exp(dlog - m_t)
        w_inter[b, h] = jnp.exp(inter - m_t)
        inv_floor[b, h] = jnp.exp(-m_t)
        a = b_last - bc + ig
        a_max = jnp.max(a, axis=0, keepdims=True)
        w_state[b, h] = jnp.exp(a - a_max)
        m_new[b, h] = jnp.maximum(b_last + m_prev, a_max)
        decay[b, h] = jnp.exp(b_last + m_prev - m_new[b, h])
        scale[b, h] = jnp.exp(a_max - m_new[b, h])

    qk = {}
    for b, h in heads:
        qk[b, h] = _nt_dot(q[b, h].astype(BF16), k[b, h].astype(BF16)) * w_intra[b, h]

    for b, h in heads:
        s = slot(b, h)
        qw = q[b, h] * _wide(w_inter[b, h], nd)
        lhs = jnp.concatenate([qw.astype(BF16), qk[b, h].astype(BF16)], axis=1)
        rhs = jnp.concatenate([st_ref[s, :head_dim, :].astype(BF16), vt_ref[b, hsl(h), :]], axis=1)
        num = _nt_dot(lhs, rhs)
        qn = (qw.reshape(groups, SUBLANES, head_dim) * st_ref[s, head_dim:, :]).reshape(CHUNK, head_dim)
        part = qk[b, h] + sum(qn[:, i * LANES:(i + 1) * LANES] for i in range(nd))
        den = _lanes(jnp.sum(part, axis=-1, keepdims=True))
        hid = num * _wide(1.0 / jnp.maximum(jnp.abs(den), inv_floor[b, h]), nd)
        mu = jnp.mean(hid, axis=-1, keepdims=True)
        var = jnp.mean(jnp.square(hid - mu), axis=-1, keepdims=True)
        hid = (hid - mu) * lax.rsqrt(var + EPS) * ng_ref[:, hsl(h)]
        yb_ref[b, :, hsl(h)] = (o_ref[b, :, hsl(h)].astype(F32) * hid).astype(BF16)

    for b, h in heads:
        s = slot(b, h)
        kw = k[b, h] * _wide(w_state[b, h] * scale[b, h], nd)
        upd = _dot(vt_ref[b, hsl(h), :], kw.astype(BF16))
        kw_sum = jnp.sum(jnp.sum(kw.reshape(groups, SUBLANES, head_dim), axis=0), axis=0, keepdims=True)
        keep = _wide(decay[b, h], nd)
        st_ref[s, :head_dim, :] = keep * st_ref[s, :head_dim, :] + upd
        st_ref[s, head_dim:, :] = keep * st_ref[s, head_dim:, :] + kw_sum
        mx_ref[s:s + 1, :] = m_new[b, h]


def _mix_kernel(co_ref, vt_ref, gcol_ref, grow_ref, bcol_ref, brow_ref, wq_ref, wk_ref, ng_ref,
                x_ref, u_ref, v_ref, ws_ref, bs_ref, gate_ref, wa_ref, wb_ref, wo_ref, g2_ref, cast_ref,
                x1_ref, hn_ref, cast_out, st_ref, mx_ref, wqb_ref, wkb_ref, yb_ref, wsb_ref,
                *, head_dim, batch, n_steps):
    n = pl.program_id(0)

    @pl.when(n == 0)
    def _():
        row = lax.broadcasted_iota(jnp.int32, (CHUNK, CHUNK), 0)
        col = lax.broadcasted_iota(jnp.int32, (CHUNK, CHUNK), 1)
        wsb_ref[...] = jnp.where(col <= row, ws_ref[...], 0.0).astype(BF16)
        wqb_ref[...] = wq_ref[...].astype(BF16)
        wkb_ref[...] = (wk_ref[...] * (head_dim ** -0.5)).astype(BF16)
        st_ref[...] = jnp.zeros(st_ref.shape, F32)
        mx_ref[...] = jnp.zeros(mx_ref.shape, F32)

    @pl.when(n < n_steps)
    def _():
        cast_out[...] = cast_ref[...].astype(BF16)

    def mlstm(slot):
        _mlstm_chunk(co_ref.at[0], co_ref.at[1], vt_ref, gcol_ref, grow_ref, bcol_ref, brow_ref, ng_ref,
                     st_ref, mx_ref, wqb_ref, wkb_ref, yb_ref.at[slot], head_dim=head_dim, batch=batch)

    def spatial_gate():
        gated = [[] for _ in range(batch)]
        for g in range(wsb_ref.shape[0]):
            gs = slice(g * GM_GROUP_DIM, (g + 1) * GM_GROUP_DIM)
            vg = jnp.concatenate([v_ref[b, :, gs] for b in range(batch)], axis=1)
            s = _dot(wsb_ref[g], vg) + bs_ref[:, g:g + 1]
            for b in range(batch):
                sb = s[:, b * GM_GROUP_DIM:(b + 1) * GM_GROUP_DIM]
                gated[b].append((u_ref[b, :, gs].astype(F32) * sb).astype(BF16))
        return jnp.concatenate([jnp.concatenate(cols, axis=1) for cols in gated], axis=0)

    def merge(slot):
        flat = lambda v: v.reshape(batch * CHUNK, v.shape[-1])
        gate_a = jnp.concatenate([flat(gate_ref[0]), flat(gate_ref[1])], axis=1).astype(F32)
        gate_b = jnp.concatenate([flat(gate_ref[2]), flat(gate_ref[3])], axis=1).astype(F32)
        mixed = (gate_a * _dot(spatial_gate(), wa_ref[...])
                 + gate_b * _dot(flat(yb_ref[slot]), wb_ref[...]))
        x1 = flat(x_ref[...]) + _dot(mixed.astype(BF16), wo_ref[...])
        x1_ref[...] = x1.reshape(x1_ref.shape)
        hn_ref[...] = _rms_norm(x1, g2_ref[...]).astype(BF16).reshape(hn_ref.shape)

    @pl.when(n == 0)
    def _():
        mlstm(0)

    for parity in range(2):
        @pl.when((n > 0) & (n < n_steps) & (n % 2 == parity))
        def _(parity=parity):
            mlstm(parity)
            merge(1 - parity)

    @pl.when(n == n_steps)
    def _():
        merge((n_steps - 1) % 2)


def _mix(proj, u, vt, gates_col, gates_row, bias_col, bias_row, wq, wk, norm_g,
         x, ws, bs_t, w_a, w_b, w_out, g2, cast_w, layer, batch):
    _, t, width = proj.shape
    n_groups = width // GM_GROUP_DIM
    assert ws.shape == (n_groups, CHUNK, CHUNK)
    seq = t // batch
    d = x.shape[-1]
    head_dim = width // ML_HEADS
    n_if = gates_row.shape[1]
    n_steps = seq // CHUNK
    n_gate = 2 * (PROJ_GB - PROJ_GA)
    assert PROJ_O == PROJ_M + 1 and PROJ_M % 2 == 0 and PROJ_GA % n_gate == 0
    assert batch * n_if <= LANES, (batch, n_if)
    proj4 = proj.reshape(proj.shape[0], batch, seq, width)
    cast_rows = _cast_rows(cast_w.shape[1], n_steps)
    at = lambda n: jnp.minimum(n, n_steps - 1)
    done = lambda n: jnp.maximum(n - 1, 0)
    const2 = lambda n: (0, 0)
    const3 = lambda n: (0, 0, 0)
    resident = dict(pipeline_mode=pl.Buffered(1))
    n_state = batch * ML_HEADS
    kern = functools.partial(_mix_kernel, head_dim=head_dim, batch=batch, n_steps=n_steps)
    x1, hn, cast = pl.pallas_call(
        kern,
        grid=(n_steps + 1,),
        in_specs=[
            pl.BlockSpec((2, batch, CHUNK, width), lambda n: (PROJ_M // 2, 0, at(n), 0)),
            pl.BlockSpec((batch, width, CHUNK), lambda n: (0, 0, at(n))),
            pl.BlockSpec((batch, CHUNK, LANES), lambda n: (0, at(n), 0)),
            pl.BlockSpec((batch, n_if, CHUNK), lambda n: (0, 0, at(n))),
            pl.BlockSpec((n_if, 1), const2),
            pl.BlockSpec((1, LANES), const2),
            pl.BlockSpec((ML_HEADS, head_dim, head_dim), const3),
            pl.BlockSpec((ML_HEADS, head_dim, head_dim), const3),
            pl.BlockSpec((1, width), const2),
            pl.BlockSpec((batch, CHUNK, d), lambda n: (0, done(n), 0)),
            pl.BlockSpec((batch, CHUNK, width), lambda n: (0, done(n), 0)),
            pl.BlockSpec((None, batch, CHUNK, width), lambda n: (PROJ_V, 0, done(n), 0)),
            pl.BlockSpec((n_groups, CHUNK, CHUNK), const3),
            pl.BlockSpec((CHUNK, n_groups), const2),
            pl.BlockSpec((n_gate, batch, CHUNK, SECTION), lambda n: (PROJ_GA // n_gate, 0, done(n), 0)),
            pl.BlockSpec((width, d), const2, **resident),
            pl.BlockSpec((width, d), const2, **resident),
            pl.BlockSpec((d, d), const2, **resident),
            pl.BlockSpec((1, d), const2),
            pl.BlockSpec((None, cast_rows, cast_w.shape[2]), lambda n: (layer, at(n), 0)),
        ],
        out_specs=[
            pl.BlockSpec((batch, CHUNK, d), lambda n: (0, done(n), 0)),
            pl.BlockSpec((batch, CHUNK, d), lambda n: (0, done(n), 0)),
            pl.BlockSpec((cast_rows, cast_w.shape[2]), lambda n: (at(n), 0)),
        ],
        out_shape=[
            jax.ShapeDtypeStruct((batch, seq, d), F32),
            jax.ShapeDtypeStruct((batch, seq, d), BF16),
            jax.ShapeDtypeStruct(cast_w.shape[1:], BF16),
        ],
        scratch_shapes=[
            pltpu.VMEM((n_state, head_dim + SUBLANES, head_dim), F32),
            pltpu.VMEM((n_state, LANES), F32),
            pltpu.VMEM((ML_HEADS, head_dim, head_dim), BF16),
            pltpu.VMEM((ML_HEADS, head_dim, head_dim), BF16),
            pltpu.VMEM((2, batch, CHUNK, width), BF16),
            pltpu.VMEM((n_groups, CHUNK, CHUNK), BF16),
        ],
        compiler_params=_params("arbitrary"),
        name="mix",
    )(proj4, vt, gates_col.reshape(batch, seq, LANES), gates_row, bias_col, bias_row, wq, wk, norm_g,
      x, u.reshape(batch, seq, width), proj4, ws, bs_t, proj4, w_a, w_b, w_out, g2, cast_w)
    return x1.reshape(t, d), hn.reshape(t, d), cast


def _ffn_kernel(hn_ref, x1_ref, w1_ref, w2_ref, gf_ref, out_ref, *, final_norm):
    j = pl.program_id(1)

    @pl.when(j == 0)
    def _():
        out_ref[...] = x1_ref[...]

    h = jnp.square(jnp.maximum(_dot(hn_ref[...], w1_ref[...]), 0.0)).astype(BF16)
    out_ref[...] += _dot(h, w2_ref[...])

    if final_norm:
        @pl.when(j == pl.num_programs(1) - 1)
        def _():
            out_ref[...] = _rms_norm(out_ref[...], gf_ref[...])


def _ffn(hn, x1, w1, w2, gf, tm, tf, final_norm):
    t, d = x1.shape
    d_ff = w1.shape[1]
    return pl.pallas_call(
        functools.partial(_ffn_kernel, final_norm=final_norm),
        grid=(t // tm, d_ff // tf),
        in_specs=[
            pl.BlockSpec((tm, d), lambda i, j: (i, 0)),
            pl.BlockSpec((tm, d), lambda i, j: (i, 0)),
            pl.BlockSpec((d, tf), lambda i, j: (0, j)),
            pl.BlockSpec((tf, d), lambda i, j: (j, 0)),
            pl.BlockSpec((1, d), lambda i, j: (0, 0)),
        ],
        out_specs=pl.BlockSpec((tm, d), lambda i, j: (i, 0)),
        out_shape=jax.ShapeDtypeStruct((t, d), F32),
        compiler_params=_params("arbitrary", "arbitrary"),
        name="ffn",
    )(hn, x1, w1, w2, gf)


def _tiles(t):
    return dict(prologue=min(t, 1024), inproj=min(t, 1024), ffn_m=min(t, 512), ffn_f=2048)


def kernel(x, norm1_g, w_in, b_gate, gm_ln_g, gm_ln_b, gm_ws, gm_bs, ml_conv_w, ml_conv_b, ml_wq, ml_wk,
           ml_ig_b, ml_fg_b, ml_norm_g, w_a, w_b, w_out, norm2_g, w_ff1, w_ff2, norm_f_g):
    batch, seq, d = x.shape
    depth = w_in.shape[0]
    t = batch * seq
    tiles = _tiles(t)
    if_row = SEC_GA * SECTION
    n_if = 2 * ML_HEADS
    xt = x.reshape(t, d)
    w_t = jnp.swapaxes(w_in, 1, 2)
    for l in range(depth):
        gate_bias = jnp.concatenate([ml_ig_b[l], ml_fg_b[l]])
        vecs = jnp.concatenate([
            gm_ln_g[l][None], gm_ln_b[l][None], ml_conv_b[l][None], ml_conv_w[l],
            jnp.zeros((1, SECTION), F32), b_gate[l].reshape(4, SECTION),
            jnp.zeros((N_VEC_ROWS - VEC_GATE_B - 5, SECTION), F32)])
        xn, gates_col, gates_row, u = _prologue(xt, norm1_g[l][None], w_t, l, if_row, n_if, batch,
                                                tiles["prologue"])
        proj, vt, w2_bf, wa_bf, wb_bf, wo_bf = _inproj(
            xn, w_t, l, if_row + n_if, vecs, (w_ff2, w_a, w_b, w_out), batch, tiles["inproj"])
        x1, hn, w1_bf = _mix(
            proj, u, vt, gates_col, gates_row, gate_bias[:, None], jnp.pad(gate_bias, (0, LANES - n_if))[None],
            ml_wq[l], ml_wk[l], ml_norm_g[l][None], xt.reshape(batch, seq, d), gm_ws[l], gm_bs[l].T,
            wa_bf, wb_bf, wo_bf, norm2_g[l][None], w_ff1, l, batch)
        xt = _ffn(hn, x1, w1_bf, w2_bf, norm_f_g[None], tiles["ffn_m"], tiles["ffn_f"],
                  final_norm=l == depth - 1)
    return xt.reshape(batch, seq, d)
```

```python
import functools

import jax
import jax.numpy as jnp
from jax import lax
from jax.experimental import pallas as pl
from jax.experimental.pallas import tpu as pltpu

EPS = 1e-6
GM_GROUP_DIM = 128
CHUNK = 128
ML_HEADS = 4
CONV_K = 4
LANES = 128
SUBLANES = 8
BF16_ROWS = 16
SECTION = 1024
SEC_U, SEC_V, SEC_M, SEC_MV, SEC_O, SEC_GA, SEC_GB = 0, 1, 2, 3, 4, 5, 7
N_SECTIONS = 9
PROJ_GA, PROJ_GB, PROJ_M, PROJ_O, PROJ_V = 0, 2, 4, 5, 6
N_PROJ = 7
VMEM_LIMIT = 56 * 1024 * 1024

BF16 = jnp.bfloat16
F32 = jnp.float32


def _sigmoid(x):
    return 0.5 * jnp.tanh(0.5 * x) + 0.5


def _log_sigmoid(x):
    return jnp.minimum(x, 0.0) - jnp.log(1.0 + jnp.exp(-jnp.abs(x)))


def _rms_norm(x, g):
    return x * lax.rsqrt(jnp.mean(x * x, axis=-1, keepdims=True) + EPS) * g


def _dot(a, b):
    return jnp.dot(a, b, preferred_element_type=F32)


def _split3(x):
    hi = x.astype(BF16)
    rest = x - hi.astype(F32)
    mid = rest.astype(BF16)
    return hi, mid, (rest - mid.astype(F32)).astype(BF16)


def _params(*semantics):
    return pltpu.CompilerParams(dimension_semantics=semantics, vmem_limit_bytes=VMEM_LIMIT)


def _cast_rows(n_rows, n_steps):
    rows = n_rows // n_steps
    assert rows * n_steps == n_rows and rows % BF16_ROWS == 0, (n_rows, n_steps)
    return rows


def _nt_dot(a, b):
    return lax.dot_general(a, b, (((1,), (1,)), ((), ())), preferred_element_type=F32)


def _cast_section(w_ref, wbf_ref):
    rows = 256
    for r in range(0, wbf_ref.shape[0], rows):
        wbf_ref[r:r + rows, :] = w_ref[r:r + rows, :].astype(BF16)


def _prologue_kernel(x_ref, g1_ref, wif_ref, wu_ref, xn_ref, gcol_ref, grow_ref, u_ref, wub_ref):
    @pl.when(pl.program_id(0) == 0)
    def _():
        _cast_section(wu_ref, wub_ref)

    xn = _rms_norm(x_ref[...], g1_ref[...]).astype(BF16)
    xn_ref[...] = xn
    wif = wif_ref[...].astype(BF16)
    gcol_ref[...] = _nt_dot(xn, wif)
    grow_ref[...] = _nt_dot(wif, xn)[:grow_ref.shape[0], :]
    u_ref[...] = jax.nn.gelu(_nt_dot(xn, wub_ref[...])).astype(BF16)


def _prologue(x, g1, w_t, layer, if_row, n_if, batch, tm):
    t, d = x.shape
    seq = t // batch
    tiles_per_seq = seq // tm
    assert tiles_per_seq * tm == seq
    return pl.pallas_call(
        _prologue_kernel,
        grid=(t // tm,),
        in_specs=[
            pl.BlockSpec((tm, d), lambda i: (i, 0)),
            pl.BlockSpec((1, d), lambda i: (0, 0)),
            pl.BlockSpec((None, LANES, d), lambda i: (layer, if_row // LANES, 0)),
            pl.BlockSpec((None, SECTION, d), lambda i: (layer, SEC_U, 0), pipeline_mode=pl.Buffered(1)),
        ],
        out_specs=[
            pl.BlockSpec((tm, d), lambda i: (i, 0)),
            pl.BlockSpec((tm, LANES), lambda i: (i, 0)),
            pl.BlockSpec((None, n_if, tm), lambda i: (i // tiles_per_seq, 0, i % tiles_per_seq)),
            pl.BlockSpec((tm, SECTION), lambda i: (i, 0)),
        ],
        out_shape=[
            jax.ShapeDtypeStruct((t, d), BF16),
            jax.ShapeDtypeStruct((t, LANES), F32),
            jax.ShapeDtypeStruct((batch, n_if, seq), F32),
            jax.ShapeDtypeStruct((t, SECTION), BF16),
        ],
        scratch_shapes=[pltpu.VMEM((SECTION, d), BF16)],
        compiler_params=_params("arbitrary"),
        name="prologue",
    )(x, g1, w_t, w_t)


VEC_LN_G, VEC_LN_B, VEC_CONV_B, VEC_CONV_W, VEC_GATE_B = 0, 1, 2, 3, 7
N_VEC_ROWS = 16


def _inproj_kernel(xn_ref, w_ref, vec_ref, *rest, tiles_per_seq, n_cast_steps, n_cast):
    cast_in, (out_ref, vt_out, *cast_out), (wbf_ref, tail_ref) = (
        rest[:n_cast], rest[n_cast:2 * n_cast + 2], rest[2 * n_cast + 2:])
    j = pl.program_id(0) + SEC_V
    i = pl.program_id(1)
    tm = xn_ref.shape[0]
    vec = lambda r, n=1: vec_ref[pl.ds(r, n), :]

    @pl.when(pl.program_id(0) * pl.num_programs(1) + i < n_cast_steps)
    def _():
        for src, dst in zip(cast_in, cast_out):
            dst[...] = src[...].astype(BF16)

    @pl.when(i == 0)
    def _():
        _cast_section(w_ref, wbf_ref)

    def proj():
        return _nt_dot(xn_ref[...], wbf_ref[...])

    @pl.when(j == SEC_V)
    def _():
        v = jax.nn.gelu(proj())
        mu = jnp.mean(v, axis=-1, keepdims=True)
        var = jnp.mean(jnp.square(v - mu), axis=-1, keepdims=True)
        out_ref[...] = ((v - mu) * lax.rsqrt(var + EPS) * vec(VEC_LN_G) + vec(VEC_LN_B)).astype(BF16)

    @pl.when(j == SEC_M)
    def _():
        @pl.when(i % tiles_per_seq == 0)
        def _():
            tail_ref[...] = jnp.zeros(tail_ref.shape, F32)

        x = proj()
        xe = jnp.concatenate([tail_ref[...], x], axis=0)
        conv = vec(VEC_CONV_B) + vec(VEC_CONV_W + CONV_K - 1) * x
        for d in range(1, CONV_K):
            conv = conv + vec(VEC_CONV_W + CONV_K - 1 - d) * xe[SUBLANES - d:SUBLANES - d + tm, :]
        tail_ref[...] = x[tm - SUBLANES:, :]
        out_ref[...] = (conv * _sigmoid(conv)).astype(BF16)

    @pl.when(j == SEC_MV)
    def _():
        vt = _nt_dot(wbf_ref[...], xn_ref[...]).astype(BF16)
        for c in range(vt_out.shape[0]):
            vt_out[c] = vt[:, c * CHUNK:(c + 1) * CHUNK]

    @pl.when(j >= SEC_O)
    def _():
        out_ref[...] = _sigmoid(proj() + vec(VEC_GATE_B + j - SEC_O)).astype(BF16)


def _inproj(xn, w_t, layer, gate_row, vecs, cast_ws, batch, tm):
    t, d = xn.shape
    n_i = t // tm
    seq = t // batch
    tiles_per_seq = seq // tm
    assert tiles_per_seq * tm == seq
    n_sec = N_SECTIONS - SEC_V
    n_cast_steps = n_sec * n_i
    cast_rows = [_cast_rows(w.shape[1], n_cast_steps) for w in cast_ws]
    cast_step = lambda s, i: s * n_i + i

    def proj_block(s, i):
        j = s + SEC_V
        plane = jnp.where(j == SEC_V, PROJ_V, jnp.where(
            j >= SEC_GA, j - SEC_GA + PROJ_GA, jnp.where(j >= SEC_O, PROJ_O, PROJ_M)))
        return (plane, jnp.where(j == SEC_MV, n_i - 1, i), 0)

    def vt_block(s, i):
        return (jnp.clip((s + SEC_V - SEC_MV) * n_i + i, 0, n_i - 1), 0, 0)

    def w_rows(s, i):
        j = s + SEC_V
        return (layer, SUBLANES * (j * (SECTION // SUBLANES) + jnp.where(j >= SEC_GA, gate_skip // SUBLANES, 0)), 0)

    gate_skip = gate_row - SEC_GA * SECTION
    assert gate_skip % SUBLANES == 0
    assert (PROJ_GB, PROJ_O) == (PROJ_GA + SEC_GB - SEC_GA, PROJ_M + 1) and SEC_M + 1 == SEC_MV == SEC_O - 1
    kern = functools.partial(_inproj_kernel, tiles_per_seq=tiles_per_seq, n_cast_steps=n_cast_steps,
                             n_cast=len(cast_ws))
    return pl.pallas_call(
        kern,
        grid=(n_sec, n_i),
        in_specs=[
            pl.BlockSpec((tm, d), lambda s, i: (i, 0)),
            pl.BlockSpec((None, pl.Element(SECTION), pl.Element(d)), w_rows),
            pl.BlockSpec((N_VEC_ROWS, SECTION), lambda s, i: (0, 0)),
        ] + [pl.BlockSpec((None, r, w.shape[2]), lambda s, i: (layer, cast_step(s, i), 0))
             for r, w in zip(cast_rows, cast_ws)],
        out_specs=[pl.BlockSpec((None, tm, SECTION), proj_block), pl.BlockSpec((tm // CHUNK, SECTION, CHUNK), vt_block)]
        + [pl.BlockSpec((r, w.shape[2]), lambda s, i: (cast_step(s, i), 0)) for r, w in zip(cast_rows, cast_ws)],
        out_shape=[jax.ShapeDtypeStruct((N_PROJ, t, SECTION), BF16),
                   jax.ShapeDtypeStruct((t // CHUNK, SECTION, CHUNK), BF16)]
        + [jax.ShapeDtypeStruct(w.shape[1:], BF16) for w in cast_ws],
        scratch_shapes=[
            pltpu.VMEM((SECTION, d), BF16),
            pltpu.VMEM((SUBLANES, SECTION), F32),
        ],
        compiler_params=_params("arbitrary", "arbitrary"),
        name="inproj",
    )(xn, w_t, vecs, *cast_ws)


def _lanes(col):
    return jnp.broadcast_to(col, (col.shape[0], LANES))


def _wide(x, n):
    return jnp.concatenate([x] * n, axis=1)


def _mlstm_chunk(c_ref, o_ref, vt_ref, gcol_ref, grow_ref, bcol_ref, brow_ref, ng_ref,
                 st_ref, mx_ref, wqb_ref, wkb_ref, yb_ref, *, head_dim, batch):
    row = lax.broadcasted_iota(jnp.int32, (CHUNK, CHUNK), 0)
    col = lax.broadcasted_iota(jnp.int32, (CHUNK, CHUNK), 1)
    causal = col <= row
    tril = causal.astype(BF16)
    triu = (row <= col).astype(BF16)
    nd = head_dim // LANES
    groups = CHUNK // SUBLANES
    heads = [(b, h) for b in range(batch) for h in range(ML_HEADS)]
    slot = lambda b, h: b * ML_HEADS + h
    hsl = lambda h: slice(h * head_dim, (h + 1) * head_dim)


    n_if = grow_ref.shape[1]
    lane = lax.broadcasted_iota(jnp.int32, (CHUNK, LANES), 1)
    g_col = gcol_ref[0] + brow_ref[...]
    for b in range(1, batch):
        g_col = jnp.where(lane < b * n_if, g_col, pltpu.roll(gcol_ref[b] + brow_ref[...], b * n_if, 1))
    g_row = jnp.concatenate([grow_ref[b] + bcol_ref[...] for b in range(batch)], axis=0)
    lf_row = _log_sigmoid(g_row)
    bcum_col = _dot(jnp.concatenate([tril] * 3, axis=1), jnp.concatenate(_split3(_log_sigmoid(g_col)), axis=0))
    bcum_row = _dot(jnp.concatenate(_split3(lf_row), axis=1), jnp.concatenate([triu] * 3, axis=0))
    total = jnp.sum(lf_row, axis=-1, keepdims=True)

    q, k = {}, {}
    for h in range(ML_HEADS):
        c = jnp.concatenate([c_ref[b, :, hsl(h)] for b in range(batch)], axis=0)
        q_h, k_h = _dot(c, wqb_ref[h]), _dot(c, wkb_ref[h])
        for b in range(batch):
            q[b, h], k[b, h] = q_h[b * CHUNK:(b + 1) * CHUNK], k_h[b * CHUNK:(b + 1) * CHUNK]

    w_intra, w_inter, inv_floor, w_state, decay, scale, m_new = {}, {}, {}, {}, {}, {}, {}
    for b, h in heads:
        i, f = b * n_if + h, b * n_if + ML_HEADS + h
        bc = _lanes(bcum_col[:, f:f + 1])
        ig = _lanes(g_col[:, i:i + 1])
        src = g_row[i:i + 1, :] - bcum_row[f:f + 1, :]
        b_last = jnp.broadcast_to(total[f:f + 1, :], (1, LANES))
        m_prev = mx_ref[slot(b, h):slot(b, h) + 1, :]
        inter = bc + m_prev
        dlog = jnp.where(causal, bc + src, -jnp.inf)
        m_t = jnp.maximum(inter, _lanes(jnp.max(dlog, axis=-1, keepdims=True)))
        w_intra[b, h] = jnp.exp(dlog - m_t)
        w_inter[b, h] = jnp.exp(inter - m_t)
        inv_floor[b, h] = jnp.exp(-m_t)
        a = b_last - bc + ig
        a_max = jnp.max(a, axis=0, keepdims=True)
        w_state[b, h] = jnp.exp(a - a_max)
        m_new[b, h] = jnp.maximum(b_last + m_prev, a_max)
        decay[b, h] = jnp.exp(b_last + m_prev - m_new[b, h])
        scale[b, h] = jnp.exp(a_max - m_new[b, h])

    qk = {}
    for b, h in heads:
        qk[b, h] = _nt_dot(q[b, h].astype(BF16), k[b, h].astype(BF16)) * w_intra[b, h]

    for b, h in heads:
        s = slot(b, h)
        qw = q[b, h] * _wide(w_inter[b, h], nd)
        lhs = jnp.concatenate([qw.astype(BF16), qk[b, h].astype(BF16)], axis=1)
        rhs = jnp.concatenate([st_ref[s, :head_dim, :].astype(BF16), vt_ref[b, hsl(h), :]], axis=1)
        num = _nt_dot(lhs, rhs)
        qn = (qw.reshape(groups, SUBLANES, head_dim) * st_ref[s, head_dim:, :]).reshape(CHUNK, head_dim)
        part = qk[b, h] + sum(qn[:, i * LANES:(i + 1) * LANES] for i in range(nd))
        den = _lanes(jnp.sum(part, axis=-1, keepdims=True))
        hid = num * _wide(1.0 / jnp.maximum(jnp.abs(den), inv_floor[b, h]), nd)
        mu = jnp.mean(hid, axis=-1, keepdims=True)
        var = jnp.mean(jnp.square(hid - mu), axis=-1, keepdims=True)
        hid = (hid - mu) * lax.rsqrt(var + EPS) * ng_ref[:, hsl(h)]
        yb_ref[b, :, hsl(h)] = (o_ref[b, :, hsl(h)].astype(F32) * hid).astype(BF16)

    for b, h in heads:
        s = slot(b, h)
        kw = k[b, h] * _wide(w_state[b, h] * scale[b, h], nd)
        upd = _dot(vt_ref[b, hsl(h), :], kw.astype(BF16))
        kw_sum = jnp.sum(jnp.sum(kw.reshape(groups, SUBLANES, head_dim), axis=0), axis=0, keepdims=True)
        keep = _wide(decay[b, h], nd)
        st_ref[s, :head_dim, :] = keep * st_ref[s, :head_dim, :] + upd
        st_ref[s, head_dim:, :] = keep * st_ref[s, head_dim:, :] + kw_sum
        mx_ref[s:s + 1, :] = m_new[b, h]


def _mix_kernel(co_ref, vt_ref, gcol_ref, grow_ref, bcol_ref, brow_ref, wq_ref, wk_ref, ng_ref,
                x_ref, u_ref, v_ref, ws_ref, bs_ref, gate_ref, wa_ref, wb_ref, wo_ref, g2_ref, cast_ref,
                x1_ref, hn_ref, cast_out, st_ref, mx_ref, wqb_ref, wkb_ref, yb_ref, wsb_ref,
                *, head_dim, batch, n_steps):
    n = pl.program_id(0)

    @pl.when(n == 0)
    def _():
        row = lax.broadcasted_iota(jnp.int32, (CHUNK, CHUNK), 0)
        col = lax.broadcasted_iota(jnp.int32, (CHUNK, CHUNK), 1)
        wsb_ref[...] = jnp.where(col <= row, ws_ref[...], 0.0).astype(BF16)
        wqb_ref[...] = wq_ref[...].astype(BF16)
        wkb_ref[...] = (wk_ref[...] * (head_dim ** -0.5)).astype(BF16)
        st_ref[...] = jnp.zeros(st_ref.shape, F32)
        mx_ref[...] = jnp.zeros(mx_ref.shape, F32)

    def mlstm(slot):
        cast_out[...] = cast_ref[...].astype(BF16)
        _mlstm_chunk(co_ref.at[0], co_ref.at[1], vt_ref, gcol_ref, grow_ref, bcol_ref, brow_ref, ng_ref,
                     st_ref, mx_ref, wqb_ref, wkb_ref, yb_ref.at[slot], head_dim=head_dim, batch=batch)

    def spatial_gate():
        gated = [[] for _ in range(batch)]
        for g in range(wsb_ref.shape[0]):
            gs = slice(g * GM_GROUP_DIM, (g + 1) * GM_GROUP_DIM)
            vg = jnp.concatenate([v_ref[b, :, gs] for b in range(batch)], axis=1)
            s = _dot(wsb_ref[g], vg) + bs_ref[:, g:g + 1]
            for b in range(batch):
                sb = s[:, b * GM_GROUP_DIM:(b + 1) * GM_GROUP_DIM]
                gated[b].append((u_ref[b, :, gs].astype(F32) * sb).astype(BF16))
        return jnp.concatenate([jnp.concatenate(cols, axis=1) for cols in gated], axis=0)

    def merge(slot):
        flat = lambda v: v.reshape(batch * CHUNK, v.shape[-1])
        gate_a = jnp.concatenate([flat(gate_ref[0]), flat(gate_ref[1])], axis=1).astype(F32)
        gate_b = jnp.concatenate([flat(gate_ref[2]), flat(gate_ref[3])], axis=1).astype(F32)
        mixed = (gate_a * _dot(spatial_gate(), wa_ref[...])
                 + gate_b * _dot(flat(yb_ref[slot]), wb_ref[...]))
        x1 = flat(x_ref[...]) + _dot(mixed.astype(BF16), wo_ref[...])
        x1_ref[...] = x1.reshape(x1_ref.shape)
        hn_ref[...] = _rms_norm(x1, g2_ref[...]).astype(BF16).reshape(hn_ref.shape)

    @pl.when(n == 0)
    def _():
        mlstm(0)

    for parity in range(2):
        @pl.when((n > 0) & (n < n_steps) & (n % 2 == parity))
        def _(parity=parity):
            mlstm(parity)
            merge(1 - parity)

    @pl.when(n == n_steps)
    def _():
        merge((n_steps - 1) % 2)


def _mix(proj, u, vt, gates_col, gates_row, bias_col, bias_row, wq, wk, norm_g,
         x, ws, bs_t, w_a, w_b, w_out, g2, cast_w, layer, batch):
    _, t, width = proj.shape
    n_groups = width // GM_GROUP_DIM
    assert ws.shape == (n_groups, CHUNK, CHUNK)
    seq = t // batch
    d = x.shape[-1]
    head_dim = width // ML_HEADS
    n_if = gates_row.shape[1]
    n_steps = seq // CHUNK
    n_gate = 2 * (PROJ_GB - PROJ_GA)
    assert PROJ_O == PROJ_M + 1 and PROJ_M % 2 == 0 and PROJ_GA % n_gate == 0
    assert batch * n_if <= LANES, (batch, n_if)
    proj4 = proj.reshape(proj.shape[0], batch, seq, width)
    cast_rows = _cast_rows(cast_w.shape[1], n_steps)
    at = lambda n: jnp.minimum(n, n_steps - 1)
    done = lambda n: jnp.maximum(n - 1, 0)
    const2 = lambda n: (0, 0)
    const3 = lambda n: (0, 0, 0)
    resident = dict(pipeline_mode=pl.Buffered(1))
    n_state = batch * ML_HEADS
    kern = functools.partial(_mix_kernel, head_dim=head_dim, batch=batch, n_steps=n_steps)
    x1, hn, cast = pl.pallas_call(
        kern,
        grid=(n_steps + 1,),
        in_specs=[
            pl.BlockSpec((2, batch, CHUNK, width), lambda n: (PROJ_M // 2, 0, at(n), 0)),
            pl.BlockSpec((batch, None, width, CHUNK), lambda n: (0, at(n), 0, 0)),
            pl.BlockSpec((batch, CHUNK, LANES), lambda n: (0, at(n), 0)),
            pl.BlockSpec((batch, n_if, CHUNK), lambda n: (0, 0, at(n))),
            pl.BlockSpec((n_if, 1), const2),
            pl.BlockSpec((1, LANES), const2),
            pl.BlockSpec((ML_HEADS, head_dim, head_dim), const3),
            pl.BlockSpec((ML_HEADS, head_dim, head_dim), const3),
            pl.BlockSpec((1, width), const2),
            pl.BlockSpec((batch, CHUNK, d), lambda n: (0, done(n), 0)),
            pl.BlockSpec((batch, CHUNK, width), lambda n: (0, done(n), 0)),
            pl.BlockSpec((None, batch, CHUNK, width), lambda n: (PROJ_V, 0, done(n), 0)),
            pl.BlockSpec((n_groups, CHUNK, CHUNK), const3),
            pl.BlockSpec((CHUNK, n_groups), const2),
            pl.BlockSpec((n_gate, batch, CHUNK, SECTION), lambda n: (PROJ_GA // n_gate, 0, done(n), 0)),
            pl.BlockSpec((width, d), const2, **resident),
            pl.BlockSpec((width, d), const2, **resident),
            pl.BlockSpec((d, d), const2, **resident),
            pl.BlockSpec((1, d), const2),
            pl.BlockSpec((None, cast_rows, cast_w.shape[2]), lambda n: (layer, at(n), 0)),
        ],
        out_specs=[
            pl.BlockSpec((batch, CHUNK, d), lambda n: (0, done(n), 0)),
            pl.BlockSpec((batch, CHUNK, d), lambda n: (0, done(n), 0)),
            pl.BlockSpec((cast_rows, cast_w.shape[2]), lambda n: (at(n), 0)),
        ],
        out_shape=[
            jax.ShapeDtypeStruct((batch, seq, d), F32),
            jax.ShapeDtypeStruct((batch, seq, d), BF16),
            jax.ShapeDtypeStruct(cast_w.shape[1:], BF16),
        ],
        scratch_shapes=[
            pltpu.VMEM((n_state, head_dim + SUBLANES, head_dim), F32),
            pltpu.VMEM((n_state, LANES), F32),
            pltpu.VMEM((ML_HEADS, head_dim, head_dim), BF16),
            pltpu.VMEM((ML_HEADS, head_dim, head_dim), BF16),
            pltpu.VMEM((2, batch, CHUNK, width), BF16),
            pltpu.VMEM((n_groups, CHUNK, CHUNK), BF16),
        ],
        compiler_params=_params("arbitrary"),
        name="mix",
    )(proj4, vt.reshape(batch, n_steps, width, CHUNK), gates_col.reshape(batch, seq, LANES), gates_row, bias_col, bias_row, wq, wk, norm_g,
      x, u.reshape(batch, seq, width), proj4, ws, bs_t, proj4, w_a, w_b, w_out, g2, cast_w)
    return x1.reshape(t, d), hn.reshape(t, d), cast


def _ffn_kernel(hn_ref, x1_ref, w1_ref, w2_ref, gf_ref, out_ref, *, final_norm):
    j = pl.program_id(1)

    @pl.when(j == 0)
    def _():
        out_ref[...] = x1_ref[...]

    h = jnp.square(jnp.maximum(_dot(hn_ref[...], w1_ref[...]), 0.0)).astype(BF16)
    out_ref[...] += _dot(h, w2_ref[...])

    if final_norm:
        @pl.when(j == pl.num_programs(1) - 1)
        def _():
            out_ref[...] = _rms_norm(out_ref[...], gf_ref[...])


def _ffn(hn, x1, w1, w2, gf, tm, tf, final_norm):
    t, d = x1.shape
    d_ff = w1.shape[1]
    return pl.pallas_call(
        functools.partial(_ffn_kernel, final_norm=final_norm),
        grid=(t // tm, d_ff // tf),
        in_specs=[
            pl.BlockSpec((tm, d), lambda i, j: (i, 0)),
            pl.BlockSpec((tm, d), lambda i, j: (i, 0)),
            pl.BlockSpec((d, tf), lambda i, j: (0, j)),
            pl.BlockSpec((tf, d), lambda i, j: (j, 0)),
            pl.BlockSpec((1, d), lambda i, j: (0, 0)),
        ],
        out_specs=pl.BlockSpec((tm, d), lambda i, j: (i, 0)),
        out_shape=jax.ShapeDtypeStruct((t, d), F32),
        compiler_params=_params("arbitrary", "arbitrary"),
        name="ffn",
    )(hn, x1, w1, w2, gf)


def _tiles(t):
    return dict(prologue=min(t, 1024), inproj=min(t, 1024), ffn_m=min(t, 512), ffn_f=2048)


def kernel(x, norm1_g, w_in, b_gate, gm_ln_g, gm_ln_b, gm_ws, gm_bs, ml_conv_w, ml_conv_b, ml_wq, ml_wk,
           ml_ig_b, ml_fg_b, ml_norm_g, w_a, w_b, w_out, norm2_g, w_ff1, w_ff2, norm_f_g):
    batch, seq, d = x.shape
    depth = w_in.shape[0]
    t = batch * seq
    tiles = _tiles(t)
    if_row = SEC_GA * SECTION
    n_if = 2 * ML_HEADS
    xt = x.reshape(t, d)
    w_t = jnp.swapaxes(w_in, 1, 2)
    for l in range(depth):
        gate_bias = jnp.concatenate([ml_ig_b[l], ml_fg_b[l]])
        vecs = jnp.concatenate([
            gm_ln_g[l][None], gm_ln_b[l][None], ml_conv_b[l][None], ml_conv_w[l],
            jnp.zeros((1, SECTION), F32), b_gate[l].reshape(4, SECTION),
            jnp.zeros((N_VEC_ROWS - VEC_GATE_B - 5, SECTION), F32)])
        xn, gates_col, gates_row, u = _prologue(xt, norm1_g[l][None], w_t, l, if_row, n_if, batch,
                                                tiles["prologue"])
        proj, vt, w2_bf, wa_bf, wb_bf, wo_bf = _inproj(
            xn, w_t, l, if_row + n_if, vecs, (w_ff2, w_a, w_b, w_out), batch, tiles["inproj"])
        x1, hn, w1_bf = _mix(
            proj, u, vt, gates_col, gates_row, gate_bias[:, None], jnp.pad(gate_bias, (0, LANES - n_if))[None],
            ml_wq[l], ml_wk[l], ml_norm_g[l][None], xt.reshape(batch, seq, d), gm_ws[l], gm_bs[l].T,
            wa_bf, wb_bf, wo_bf, norm2_g[l][None], w_ff1, l, batch)
        xt = _ffn(hn, x1, w1_bf, w2_bf, norm_f_g[None], tiles["ffn_m"], tiles["ffn_f"],
                  final_norm=l == depth - 1)
    return xt.reshape(batch, seq, d)
```

```python
import functools

import jax
import jax.numpy as jnp
from jax import lax
from jax.experimental import pallas as pl
from jax.experimental.pallas import tpu as pltpu

EPS = 1e-6
GM_GROUP_DIM = 128
CHUNK = 128
ML_HEADS = 4
CONV_K = 4
LANES = 128
SUBLANES = 8
BF16_ROWS = 16
SECTION = 1024
SEC_U, SEC_V, SEC_M, SEC_MV, SEC_O, SEC_GA, SEC_GB = 0, 1, 2, 3, 4, 5, 7
N_SECTIONS = 9
PROJ_M, PROJ_O, PROJ_GA, PROJ_GB, PROJ_V = 0, 1, 2, 4, 6
N_PROJ = 7
VMEM_LIMIT = 56 * 1024 * 1024

BF16 = jnp.bfloat16
F32 = jnp.float32


def _sigmoid(x):
    return 0.5 * jnp.tanh(0.5 * x) + 0.5


def _log_sigmoid(x):
    return jnp.minimum(x, 0.0) - jnp.log(1.0 + jnp.exp(-jnp.abs(x)))


def _rms_norm(x, g):
    return x * lax.rsqrt(jnp.mean(x * x, axis=-1, keepdims=True) + EPS) * g


def _dot(a, b):
    return jnp.dot(a, b, preferred_element_type=F32)


def _split3(x):
    hi = x.astype(BF16)
    rest = x - hi.astype(F32)
    mid = rest.astype(BF16)
    return hi, mid, (rest - mid.astype(F32)).astype(BF16)


def _params(*semantics):
    return pltpu.CompilerParams(dimension_semantics=semantics, vmem_limit_bytes=VMEM_LIMIT)


def _cast_rows(n_rows, n_steps):
    rows = n_rows // n_steps
    assert rows * n_steps == n_rows and rows % BF16_ROWS == 0, (n_rows, n_steps)
    return rows


def _nt_dot(a, b):
    return lax.dot_general(a, b, (((1,), (1,)), ((), ())), preferred_element_type=F32)


def _cast_section(w_ref, wbf_ref):
    rows = 256
    for r in range(0, wbf_ref.shape[0], rows):
        wbf_ref[r:r + rows, :] = w_ref[r:r + rows, :].astype(BF16)


def _prologue_kernel(x_ref, g1_ref, wif_ref, wu_ref, xn_ref, gcol_ref, grow_ref, u_ref, wub_ref):
    @pl.when(pl.program_id(0) == 0)
    def _():
        _cast_section(wu_ref, wub_ref)

    xn = _rms_norm(x_ref[...], g1_ref[...]).astype(BF16)
    xn_ref[...] = xn
    wif = wif_ref[...].astype(BF16)
    gcol_ref[...] = _nt_dot(xn, wif)
    grow_ref[...] = _nt_dot(wif, xn)[:grow_ref.shape[0], :]
    u_ref[...] = jax.nn.gelu(_nt_dot(xn, wub_ref[...])).astype(BF16)


def _prologue(x, g1, w_t, layer, if_row, n_if, batch, tm):
    t, d = x.shape
    seq = t // batch
    tiles_per_seq = seq // tm
    assert tiles_per_seq * tm == seq
    return pl.pallas_call(
        _prologue_kernel,
        grid=(t // tm,),
        in_specs=[
            pl.BlockSpec((tm, d), lambda i: (i, 0)),
            pl.BlockSpec((1, d), lambda i: (0, 0)),
            pl.BlockSpec((None, LANES, d), lambda i: (layer, if_row // LANES, 0)),
            pl.BlockSpec((None, SECTION, d), lambda i: (layer, SEC_U, 0), pipeline_mode=pl.Buffered(1)),
        ],
        out_specs=[
            pl.BlockSpec((tm, d), lambda i: (i, 0)),
            pl.BlockSpec((tm, LANES), lambda i: (i, 0)),
            pl.BlockSpec((None, n_if, tm), lambda i: (i // tiles_per_seq, 0, i % tiles_per_seq)),
            pl.BlockSpec((tm, SECTION), lambda i: (i, 0)),
        ],
        out_shape=[
            jax.ShapeDtypeStruct((t, d), BF16),
            jax.ShapeDtypeStruct((t, LANES), F32),
            jax.ShapeDtypeStruct((batch, n_if, seq), F32),
            jax.ShapeDtypeStruct((t, SECTION), BF16),
        ],
        scratch_shapes=[pltpu.VMEM((SECTION, d), BF16)],
        compiler_params=_params("arbitrary"),
        name="prologue",
    )(x, g1, w_t, w_t)


VEC_LN_G, VEC_LN_B, VEC_CONV_B, VEC_CONV_W, VEC_GATE_B = 0, 1, 2, 3, 7
N_VEC_ROWS = 16


def _inproj_kernel(xn_ref, w_ref, vec_ref, *rest, tiles_per_seq, n_cast_steps, n_cast):
    cast_in, (out_ref, vt_out, *cast_out), (wbf_ref, tail_ref) = (
        rest[:n_cast], rest[n_cast:2 * n_cast + 2], rest[2 * n_cast + 2:])
    j = pl.program_id(0) + SEC_V
    i = pl.program_id(1)
    tm = xn_ref.shape[0]
    vec = lambda r, n=1: vec_ref[pl.ds(r, n), :]

    @pl.when(pl.program_id(0) * pl.num_programs(1) + i < n_cast_steps)
    def _():
        for src, dst in zip(cast_in, cast_out):
            dst[...] = src[...].astype(BF16)

    @pl.when(i == 0)
    def _():
        _cast_section(w_ref, wbf_ref)

    def proj():
        return _nt_dot(xn_ref[...], wbf_ref[...])

    @pl.when(j == SEC_V)
    def _():
        v = jax.nn.gelu(proj())
        mu = jnp.mean(v, axis=-1, keepdims=True)
        var = jnp.mean(jnp.square(v - mu), axis=-1, keepdims=True)
        out_ref[...] = ((v - mu) * lax.rsqrt(var + EPS) * vec(VEC_LN_G) + vec(VEC_LN_B)).astype(BF16)

    @pl.when(j == SEC_M)
    def _():
        @pl.when(i % tiles_per_seq == 0)
        def _():
            tail_ref[...] = jnp.zeros(tail_ref.shape, F32)

        x = proj()
        xe = jnp.concatenate([tail_ref[...], x], axis=0)
        conv = vec(VEC_CONV_B) + vec(VEC_CONV_W + CONV_K - 1) * x
        for d in range(1, CONV_K):
            conv = conv + vec(VEC_CONV_W + CONV_K - 1 - d) * xe[SUBLANES - d:SUBLANES - d + tm, :]
        tail_ref[...] = x[tm - SUBLANES:, :]
        out_ref[...] = (conv * _sigmoid(conv)).astype(BF16)

    @pl.when(j == SEC_MV)
    def _():
        vt = _nt_dot(wbf_ref[...], xn_ref[...]).astype(BF16)
        for c in range(vt_out.shape[0]):
            vt_out[c] = vt[:, c * CHUNK:(c + 1) * CHUNK]

    @pl.when(j >= SEC_O)
    def _():
        out_ref[...] = _sigmoid(proj() + vec(VEC_GATE_B + j - SEC_O)).astype(BF16)


def _inproj(xn, w_t, layer, gate_row, vecs, cast_ws, batch, tm):
    t, d = xn.shape
    n_i = t // tm
    seq = t // batch
    tiles_per_seq = seq // tm
    assert tiles_per_seq * tm == seq
    n_sec = N_SECTIONS - SEC_V
    n_cast_steps = n_sec * n_i
    cast_rows = [_cast_rows(w.shape[1], n_cast_steps) for w in cast_ws]
    cast_step = lambda s, i: s * n_i + i

    def proj_block(s, i):
        j = s + SEC_V
        plane = jnp.where(j == SEC_V, PROJ_V, jnp.where(
            j >= SEC_GA, j - SEC_GA + PROJ_GA, jnp.where(j >= SEC_O, PROJ_O, PROJ_M)))
        return (plane, jnp.where(j == SEC_MV, n_i - 1, i), 0)

    def vt_block(s, i):
        return (jnp.clip((s + SEC_V - SEC_MV) * n_i + i, 0, n_i - 1), 0, 0)

    def w_rows(s, i):
        j = s + SEC_V
        return (layer, SUBLANES * (j * (SECTION // SUBLANES) + jnp.where(j >= SEC_GA, gate_skip // SUBLANES, 0)), 0)

    gate_skip = gate_row - SEC_GA * SECTION
    assert gate_skip % SUBLANES == 0
    assert (PROJ_GB, PROJ_O) == (PROJ_GA + SEC_GB - SEC_GA, PROJ_M + 1) and SEC_M + 1 == SEC_MV == SEC_O - 1
    kern = functools.partial(_inproj_kernel, tiles_per_seq=tiles_per_seq, n_cast_steps=n_cast_steps,
                             n_cast=len(cast_ws))
    return pl.pallas_call(
        kern,
        grid=(n_sec, n_i),
        in_specs=[
            pl.BlockSpec((tm, d), lambda s, i: (i, 0)),
            pl.BlockSpec((None, pl.Element(SECTION), pl.Element(d)), w_rows),
            pl.BlockSpec((N_VEC_ROWS, SECTION), lambda s, i: (0, 0)),
        ] + [pl.BlockSpec((None, r, w.shape[2]), lambda s, i: (layer, cast_step(s, i), 0))
             for r, w in zip(cast_rows, cast_ws)],
        out_specs=[pl.BlockSpec((None, tm, SECTION), proj_block), pl.BlockSpec((tm // CHUNK, SECTION, CHUNK), vt_block)]
        + [pl.BlockSpec((r, w.shape[2]), lambda s, i: (cast_step(s, i), 0)) for r, w in zip(cast_rows, cast_ws)],
        out_shape=[jax.ShapeDtypeStruct((N_PROJ, t, SECTION), BF16),
                   jax.ShapeDtypeStruct((t // CHUNK, SECTION, CHUNK), BF16)]
        + [jax.ShapeDtypeStruct(w.shape[1:], BF16) for w in cast_ws],
        scratch_shapes=[
            pltpu.VMEM((SECTION, d), BF16),
            pltpu.VMEM((SUBLANES, SECTION), F32),
        ],
        compiler_params=_params("arbitrary", "arbitrary"),
        name="inproj",
    )(xn, w_t, vecs, *cast_ws)


def _lanes(col):
    return jnp.broadcast_to(col, (col.shape[0], LANES))


def _wide(x, n):
    return jnp.concatenate([x] * n, axis=1)


def _mlstm_chunk(c_ref, o_ref, vt_ref, gcol_ref, grow_ref, bcol_ref, brow_ref, ng_ref,
                 st_ref, mx_ref, wqb_ref, wkb_ref, yb_ref, *, head_dim, batch):
    row = lax.broadcasted_iota(jnp.int32, (CHUNK, CHUNK), 0)
    col = lax.broadcasted_iota(jnp.int32, (CHUNK, CHUNK), 1)
    causal = col <= row
    tril = causal.astype(BF16)
    triu = (row <= col).astype(BF16)
    nd = head_dim // LANES
    groups = CHUNK // SUBLANES
    heads = [(b, h) for b in range(batch) for h in range(ML_HEADS)]
    slot = lambda b, h: b * ML_HEADS + h
    hsl = lambda h: slice(h * head_dim, (h + 1) * head_dim)


    n_if = grow_ref.shape[1]
    lane = lax.broadcasted_iota(jnp.int32, (CHUNK, LANES), 1)
    g_col = gcol_ref[0] + brow_ref[...]
    for b in range(1, batch):
        g_col = jnp.where(lane < b * n_if, g_col, pltpu.roll(gcol_ref[b] + brow_ref[...], b * n_if, 1))
    g_row = jnp.concatenate([grow_ref[b] + bcol_ref[...] for b in range(batch)], axis=0)
    lf_row = _log_sigmoid(g_row)
    bcum_col = _dot(jnp.concatenate([tril] * 3, axis=1), jnp.concatenate(_split3(_log_sigmoid(g_col)), axis=0))
    bcum_row = _dot(jnp.concatenate(_split3(lf_row), axis=1), jnp.concatenate([triu] * 3, axis=0))
    total = jnp.sum(lf_row, axis=-1, keepdims=True)

    q, k = {}, {}
    for h in range(ML_HEADS):
        c = jnp.concatenate([c_ref[b, :, hsl(h)] for b in range(batch)], axis=0)
        q_h, k_h = _dot(c, wqb_ref[h]), _dot(c, wkb_ref[h])
        for b in range(batch):
            q[b, h], k[b, h] = q_h[b * CHUNK:(b + 1) * CHUNK], k_h[b * CHUNK:(b + 1) * CHUNK]

    w_intra, w_inter, inv_floor, w_state, decay, scale, m_new = {}, {}, {}, {}, {}, {}, {}
    for b, h in heads:
        i, f = b * n_if + h, b * n_if + ML_HEADS + h
        bc = _lanes(bcum_col[:, f:f + 1])
        ig = _lanes(g_col[:, i:i + 1])
        src = g_row[i:i + 1, :] - bcum_row[f:f + 1, :]
        b_last = jnp.broadcast_to(total[f:f + 1, :], (1, LANES))
        m_prev = mx_ref[slot(b, h):slot(b, h) + 1, :]
        inter = bc + m_prev
        dlog = jnp.where(causal, bc + src, -jnp.inf)
        m_t = jnp.maximum(inter, _lanes(jnp.max(dlog, axis=-1, keepdims=True)))
        w_intra[b, h] = jnp.exp(dlog - m_t)
        w_inter[b, h] = jnp.exp(inter - m_t)
        inv_floor[b, h] = jnp.exp(-m_t)
        a = b_last - bc + ig
        a_max = jnp.max(a, axis=0, keepdims=True)
        w_state[b, h] = jnp.exp(a - a_max)
        m_new[b, h] = jnp.maximum(b_last + m_prev, a_max)
        decay[b, h] = jnp.exp(b_last + m_prev - m_new[b, h])
        scale[b, h] = jnp.exp(a_max - m_new[b, h])

    qk = {}
    for b, h in heads:
        qk[b, h] = _nt_dot(q[b, h].astype(BF16), k[b, h].astype(BF16)) * w_intra[b, h]

    for b, h in heads:
        s = slot(b, h)
        qw = q[b, h] * _wide(w_inter[b, h], nd)
        lhs = jnp.concatenate([qw.astype(BF16), qk[b, h].astype(BF16)], axis=1)
        rhs = jnp.concatenate([st_ref[s, :head_dim, :].astype(BF16), vt_ref[b, hsl(h), :]], axis=1)
        num = _nt_dot(lhs, rhs)
        qn = (qw.reshape(groups, SUBLANES, head_dim) * st_ref[s, head_dim:, :]).reshape(CHUNK, head_dim)
        part = qk[b, h] + sum(qn[:, i * LANES:(i + 1) * LANES] for i in range(nd))
        den = _lanes(jnp.sum(part, axis=-1, keepdims=True))
        hid = num * _wide(1.0 / jnp.maximum(jnp.abs(den), inv_floor[b, h]), nd)
        mu = jnp.mean(hid, axis=-1, keepdims=True)
        var = jnp.mean(jnp.square(hid - mu), axis=-1, keepdims=True)
        hid = (hid - mu) * lax.rsqrt(var + EPS) * ng_ref[:, hsl(h)]
        yb_ref[b, :, hsl(h)] = (o_ref[b, :, hsl(h)].astype(F32) * hid).astype(BF16)

    for b, h in heads:
        s = slot(b, h)
        kw = k[b, h] * _wide(w_state[b, h] * scale[b, h], nd)
        upd = _dot(vt_ref[b, hsl(h), :], kw.astype(BF16))
        kw_sum = jnp.sum(jnp.sum(kw.reshape(groups, SUBLANES, head_dim), axis=0), axis=0, keepdims=True)
        keep = _wide(decay[b, h], nd)
        st_ref[s, :head_dim, :] = keep * st_ref[s, :head_dim, :] + upd
        st_ref[s, head_dim:, :] = keep * st_ref[s, head_dim:, :] + kw_sum
        mx_ref[s:s + 1, :] = m_new[b, h]


def _mix_kernel(co_ref, vt_ref, gcol_ref, grow_ref, bcol_ref, brow_ref, wq_ref, wk_ref, ng_ref,
                x_ref, u_ref, ws_ref, bs_ref, gate_ref, wa_ref, wb_ref, wo_ref, g2_ref, cast_ref,
                x1_ref, hn_ref, cast_out, st_ref, mx_ref, wqb_ref, wkb_ref, yb_ref, wsb_ref,
                *, head_dim, batch, n_steps):
    n = pl.program_id(0)

    @pl.when(n == 0)
    def _():
        row = lax.broadcasted_iota(jnp.int32, (CHUNK, CHUNK), 0)
        col = lax.broadcasted_iota(jnp.int32, (CHUNK, CHUNK), 1)
        wsb_ref[...] = jnp.where(col <= row, ws_ref[...], 0.0).astype(BF16)
        wqb_ref[...] = wq_ref[...].astype(BF16)
        wkb_ref[...] = (wk_ref[...] * (head_dim ** -0.5)).astype(BF16)
        st_ref[...] = jnp.zeros(st_ref.shape, F32)
        mx_ref[...] = jnp.zeros(mx_ref.shape, F32)

    def mlstm(slot):
        cast_out[...] = cast_ref[...].astype(BF16)
        _mlstm_chunk(co_ref.at[0], co_ref.at[1], vt_ref, gcol_ref, grow_ref, bcol_ref, brow_ref, ng_ref,
                     st_ref, mx_ref, wqb_ref, wkb_ref, yb_ref.at[slot], head_dim=head_dim, batch=batch)

    def spatial_gate():
        gated = [[] for _ in range(batch)]
        for g in range(wsb_ref.shape[0]):
            gs = slice(g * GM_GROUP_DIM, (g + 1) * GM_GROUP_DIM)
            vg = jnp.concatenate([gate_ref[PROJ_V - PROJ_GA, b, :, gs] for b in range(batch)], axis=1)
            s = _dot(wsb_ref[g], vg) + bs_ref[:, g:g + 1]
            for b in range(batch):
                sb = s[:, b * GM_GROUP_DIM:(b + 1) * GM_GROUP_DIM]
                gated[b].append((u_ref[b, :, gs].astype(F32) * sb).astype(BF16))
        return jnp.concatenate([jnp.concatenate(cols, axis=1) for cols in gated], axis=0)

    def merge(slot):
        flat = lambda v: v.reshape(batch * CHUNK, v.shape[-1])
        gate_a = jnp.concatenate([flat(gate_ref[0]), flat(gate_ref[1])], axis=1).astype(F32)
        gate_b = jnp.concatenate([flat(gate_ref[2]), flat(gate_ref[3])], axis=1).astype(F32)
        mixed = (gate_a * _dot(spatial_gate(), wa_ref[...])
                 + gate_b * _dot(flat(yb_ref[slot]), wb_ref[...]))
        x1 = flat(x_ref[...]) + _dot(mixed.astype(BF16), wo_ref[...])
        x1_ref[...] = x1.reshape(x1_ref.shape)
        hn_ref[...] = _rms_norm(x1, g2_ref[...]).astype(BF16).reshape(hn_ref.shape)

    @pl.when(n == 0)
    def _():
        mlstm(0)

    for parity in range(2):
        @pl.when((n > 0) & (n < n_steps) & (n % 2 == parity))
        def _(parity=parity):
            mlstm(parity)
            merge(1 - parity)

    @pl.when(n == n_steps)
    def _():
        merge((n_steps - 1) % 2)


def _mix(proj, u, vt, gates_col, gates_row, bias_col, bias_row, wq, wk, norm_g,
         x, ws, bs_t, w_a, w_b, w_out, g2, cast_w, layer, batch):
    _, t, width = proj.shape
    n_groups = width // GM_GROUP_DIM
    assert ws.shape == (n_groups, CHUNK, CHUNK)
    seq = t // batch
    d = x.shape[-1]
    head_dim = width // ML_HEADS
    n_if = gates_row.shape[1]
    n_steps = seq // CHUNK
    n_gv = PROJ_V + 1 - PROJ_GA
    assert PROJ_O == PROJ_M + 1 and PROJ_M % 2 == 0 and PROJ_V == PROJ_GA + 2 * (PROJ_GB - PROJ_GA)
    assert batch * n_if <= LANES, (batch, n_if)
    proj4 = proj.reshape(proj.shape[0], batch, seq, width)
    cast_rows = _cast_rows(cast_w.shape[1], n_steps)
    at = lambda n: jnp.minimum(n, n_steps - 1)
    done = lambda n: jnp.maximum(n - 1, 0)
    const2 = lambda n: (0, 0)
    const3 = lambda n: (0, 0, 0)
    resident = dict(pipeline_mode=pl.Buffered(1))
    n_state = batch * ML_HEADS
    kern = functools.partial(_mix_kernel, head_dim=head_dim, batch=batch, n_steps=n_steps)
    x1, hn, cast = pl.pallas_call(
        kern,
        grid=(n_steps + 1,),
        in_specs=[
            pl.BlockSpec((2, batch, CHUNK, width), lambda n: (PROJ_M // 2, 0, at(n), 0)),
            pl.BlockSpec((batch, None, width, CHUNK), lambda n: (0, at(n), 0, 0)),
            pl.BlockSpec((batch, CHUNK, LANES), lambda n: (0, at(n), 0)),
            pl.BlockSpec((batch, n_if, CHUNK), lambda n: (0, 0, at(n))),
            pl.BlockSpec((n_if, 1), const2),
            pl.BlockSpec((1, LANES), const2),
            pl.BlockSpec((ML_HEADS, head_dim, head_dim), const3),
            pl.BlockSpec((ML_HEADS, head_dim, head_dim), const3),
            pl.BlockSpec((1, width), const2),
            pl.BlockSpec((batch, CHUNK, d), lambda n: (0, done(n), 0)),
            pl.BlockSpec((batch, CHUNK, width), lambda n: (0, done(n), 0)),
            pl.BlockSpec((n_groups, CHUNK, CHUNK), const3),
            pl.BlockSpec((CHUNK, n_groups), const2),
            pl.BlockSpec((pl.Element(n_gv), pl.Element(batch), pl.Element(CHUNK), pl.Element(SECTION)),
                         lambda n: (PROJ_GA, 0, SUBLANES * (done(n) * (CHUNK // SUBLANES)), 0)),
            pl.BlockSpec((width, d), const2, **resident),
            pl.BlockSpec((width, d), const2, **resident),
            pl.BlockSpec((d, d), const2, **resident),
            pl.BlockSpec((1, d), const2),
            pl.BlockSpec((None, cast_rows, cast_w.shape[2]), lambda n: (layer, at(n), 0)),
        ],
        out_specs=[
            pl.BlockSpec((batch, CHUNK, d), lambda n: (0, done(n), 0)),
            pl.BlockSpec((batch, CHUNK, d), lambda n: (0, done(n), 0)),
            pl.BlockSpec((cast_rows, cast_w.shape[2]), lambda n: (at(n), 0)),
        ],
        out_shape=[
            jax.ShapeDtypeStruct((batch, seq, d), F32),
            jax.ShapeDtypeStruct((batch, seq, d), BF16),
            jax.ShapeDtypeStruct(cast_w.shape[1:], BF16),
        ],
        scratch_shapes=[
            pltpu.VMEM((n_state, head_dim + SUBLANES, head_dim), F32),
            pltpu.VMEM((n_state, LANES), F32),
            pltpu.VMEM((ML_HEADS, head_dim, head_dim), BF16),
            pltpu.VMEM((ML_HEADS, head_dim, head_dim), BF16),
            pltpu.VMEM((2, batch, CHUNK, width), BF16),
            pltpu.VMEM((n_groups, CHUNK, CHUNK), BF16),
        ],
        compiler_params=_params("arbitrary"),
        name="mix",
    )(proj4, vt.reshape(batch, n_steps, width, CHUNK), gates_col.reshape(batch, seq, LANES), gates_row, bias_col, bias_row, wq, wk, norm_g,
      x, u.reshape(batch, seq, width), ws, bs_t, proj4, w_a, w_b, w_out, g2, cast_w)
    return x1.reshape(t, d), hn.reshape(t, d), cast


def _ffn_kernel(hn_ref, x1_ref, w1_ref, w2_ref, gf_ref, out_ref, *, final_norm):
    j = pl.program_id(1)

    @pl.when(j == 0)
    def _():
        out_ref[...] = x1_ref[...]

    h = jnp.square(jnp.maximum(_dot(hn_ref[...], w1_ref[...]), 0.0)).astype(BF16)
    out_ref[...] += _dot(h, w2_ref[...])

    if final_norm:
        @pl.when(j == pl.num_programs(1) - 1)
        def _():
            out_ref[...] = _rms_norm(out_ref[...], gf_ref[...])


def _ffn(hn, x1, w1, w2, gf, tm, tf, final_norm):
    t, d = x1.shape
    d_ff = w1.shape[1]
    return pl.pallas_call(
        functools.partial(_ffn_kernel, final_norm=final_norm),
        grid=(t // tm, d_ff // tf),
        in_specs=[
            pl.BlockSpec((tm, d), lambda i, j: (i, 0)),
            pl.BlockSpec((tm, d), lambda i, j: (i, 0)),
            pl.BlockSpec((d, tf), lambda i, j: (0, j)),
            pl.BlockSpec((tf, d), lambda i, j: (j, 0)),
            pl.BlockSpec((1, d), lambda i, j: (0, 0)),
        ],
        out_specs=pl.BlockSpec((tm, d), lambda i, j: (i, 0)),
        out_shape=jax.ShapeDtypeStruct((t, d), F32),
        compiler_params=_params("arbitrary", "arbitrary"),
        name="ffn",
    )(hn, x1, w1, w2, gf)


def _tiles(t):
    return dict(prologue=min(t, 1024), inproj=min(t, 1024), ffn_m=min(t, 512), ffn_f=2048)


def kernel(x, norm1_g, w_in, b_gate, gm_ln_g, gm_ln_b, gm_ws, gm_bs, ml_conv_w, ml_conv_b, ml_wq, ml_wk,
           ml_ig_b, ml_fg_b, ml_norm_g, w_a, w_b, w_out, norm2_g, w_ff1, w_ff2, norm_f_g):
    batch, seq, d = x.shape
    depth = w_in.shape[0]
    t = batch * seq
    tiles = _tiles(t)
    if_row = SEC_GA * SECTION
    n_if = 2 * ML_HEADS
    xt = x.reshape(t, d)
    w_t = jnp.swapaxes(w_in, 1, 2)
    for l in range(depth):
        gate_bias = jnp.concatenate([ml_ig_b[l], ml_fg_b[l]])
        vecs = jnp.concatenate([
            gm_ln_g[l][None], gm_ln_b[l][None], ml_conv_b[l][None], ml_conv_w[l],
            jnp.zeros((1, SECTION), F32), b_gate[l].reshape(4, SECTION),
            jnp.zeros((N_VEC_ROWS - VEC_GATE_B - 5, SECTION), F32)])
        xn, gates_col, gates_row, u = _prologue(xt, norm1_g[l][None], w_t, l, if_row, n_if, batch,
                                                tiles["prologue"])
        proj, vt, w2_bf, wa_bf, wb_bf, wo_bf = _inproj(
            xn, w_t, l, if_row + n_if, vecs, (w_ff2, w_a, w_b, w_out), batch, tiles["inproj"])
        x1, hn, w1_bf = _mix(
            proj, u, vt, gates_col, gates_row, gate_bias[:, None], jnp.pad(gate_bias, (0, LANES - n_if))[None],
            ml_wq[l], ml_wk[l], ml_norm_g[l][None], xt.reshape(batch, seq, d), gm_ws[l], gm_bs[l].T,
            wa_bf, wb_bf, wo_bf, norm2_g[l][None], w_ff1, l, batch)
        xt = _ffn(hn, x1, w1_bf, w2_bf, norm_f_g[None], tiles["ffn_m"], tiles["ffn_f"],
                  final_norm=l == depth - 1)
    return xt.reshape(batch, seq, d)
```

```python
import functools

import jax
import jax.numpy as jnp
from jax import lax
from jax.experimental import pallas as pl
from jax.experimental.pallas import tpu as pltpu

EPS = 1e-6
GM_GROUP_DIM = 128
CHUNK = 128
ML_HEADS = 4
CONV_K = 4
LANES = 128
SUBLANES = 8
BF16_ROWS = 16
SECTION = 1024
SEC_U, SEC_V, SEC_M, SEC_MV, SEC_O, SEC_GA, SEC_GB = 0, 1, 2, 3, 4, 5, 7
N_SECTIONS = 9
PROJ_GA, PROJ_GB, PROJ_M, PROJ_O, PROJ_V = 0, 2, 4, 5, 6
N_PROJ = 7
VMEM_LIMIT = 56 * 1024 * 1024

BF16 = jnp.bfloat16
F32 = jnp.float32


def _sigmoid(x):
    return 0.5 * jnp.tanh(0.5 * x) + 0.5


def _log_sigmoid(x):
    return jnp.minimum(x, 0.0) - jnp.log(1.0 + jnp.exp(-jnp.abs(x)))


def _rms_norm(x, g):
    return x * lax.rsqrt(jnp.mean(x * x, axis=-1, keepdims=True) + EPS) * g


def _dot(a, b):
    return jnp.dot(a, b, preferred_element_type=F32)


def _split3(x):
    hi = x.astype(BF16)
    rest = x - hi.astype(F32)
    mid = rest.astype(BF16)
    return hi, mid, (rest - mid.astype(F32)).astype(BF16)


def _params(*semantics):
    return pltpu.CompilerParams(dimension_semantics=semantics, vmem_limit_bytes=VMEM_LIMIT)


def _cast_rows(n_rows, n_steps):
    rows = n_rows // n_steps
    assert rows * n_steps == n_rows and rows % BF16_ROWS == 0, (n_rows, n_steps)
    return rows


def _nt_dot(a, b):
    return lax.dot_general(a, b, (((1,), (1,)), ((), ())), preferred_element_type=F32)


def _cast_section(w_ref, wbf_ref):
    rows = 256
    for r in range(0, wbf_ref.shape[0], rows):
        wbf_ref[r:r + rows, :] = w_ref[r:r + rows, :].astype(BF16)


def _prologue_kernel(x_ref, g1_ref, wif_ref, wu_ref, xn_ref, gcol_ref, grow_ref, u_ref, wub_ref):
    @pl.when(pl.program_id(0) == 0)
    def _():
        _cast_section(wu_ref, wub_ref)

    xn = _rms_norm(x_ref[...], g1_ref[...]).astype(BF16)
    xn_ref[...] = xn
    wif = wif_ref[...].astype(BF16)
    gcol = _nt_dot(xn, wif)
    gcol_ref[...] = gcol
    grow_ref[...] = jnp.concatenate(
        [gcol[r:r + LANES, :].T[:grow_ref.shape[0], :] for r in range(0, gcol.shape[0], LANES)], axis=1)
    u_ref[...] = jax.nn.gelu(_nt_dot(xn, wub_ref[...])).astype(BF16)


def _prologue(x, g1, w_t, layer, if_row, n_if, batch, tm):
    t, d = x.shape
    seq = t // batch
    tiles_per_seq = seq // tm
    assert tiles_per_seq * tm == seq
    return pl.pallas_call(
        _prologue_kernel,
        grid=(t // tm,),
        in_specs=[
            pl.BlockSpec((tm, d), lambda i: (i, 0)),
            pl.BlockSpec((1, d), lambda i: (0, 0)),
            pl.BlockSpec((None, LANES, d), lambda i: (layer, if_row // LANES, 0)),
            pl.BlockSpec((None, SECTION, d), lambda i: (layer, SEC_U, 0), pipeline_mode=pl.Buffered(1)),
        ],
        out_specs=[
            pl.BlockSpec((tm, d), lambda i: (i, 0)),
            pl.BlockSpec((tm, LANES), lambda i: (i, 0)),
            pl.BlockSpec((None, n_if, tm), lambda i: (i // tiles_per_seq, 0, i % tiles_per_seq)),
            pl.BlockSpec((tm, SECTION), lambda i: (i, 0)),
        ],
        out_shape=[
            jax.ShapeDtypeStruct((t, d), BF16),
            jax.ShapeDtypeStruct((t, LANES), F32),
            jax.ShapeDtypeStruct((batch, n_if, seq), F32),
            jax.ShapeDtypeStruct((t, SECTION), BF16),
        ],
        scratch_shapes=[pltpu.VMEM((SECTION, d), BF16)],
        compiler_params=_params("arbitrary"),
        name="prologue",
    )(x, g1, w_t, w_t)


VEC_LN_G, VEC_LN_B, VEC_CONV_B, VEC_CONV_W, VEC_GATE_B = 0, 1, 2, 3, 7
N_VEC_ROWS = 16


def _inproj_kernel(xn_ref, w_ref, vec_ref, *rest, tiles_per_seq, n_cast_steps, n_cast):
    cast_in, (out_ref, vt_out, *cast_out), (wbf_ref, tail_ref) = (
        rest[:n_cast], rest[n_cast:2 * n_cast + 2], rest[2 * n_cast + 2:])
    j = pl.program_id(0) + SEC_V
    i = pl.program_id(1)
    tm = xn_ref.shape[0]
    vec = lambda r, n=1: vec_ref[pl.ds(r, n), :]

    @pl.when(pl.program_id(0) * pl.num_programs(1) + i < n_cast_steps)
    def _():
        for src, dst in zip(cast_in, cast_out):
            dst[...] = src[...].astype(BF16)

    @pl.when(i == 0)
    def _():
        _cast_section(w_ref, wbf_ref)

    def proj():
        return _nt_dot(xn_ref[...], wbf_ref[...])

    @pl.when(j == SEC_V)
    def _():
        v = jax.nn.gelu(proj())
        mu = jnp.mean(v, axis=-1, keepdims=True)
        var = jnp.mean(jnp.square(v - mu), axis=-1, keepdims=True)
        out_ref[...] = ((v - mu) * lax.rsqrt(var + EPS) * vec(VEC_LN_G) + vec(VEC_LN_B)).astype(BF16)

    @pl.when(j == SEC_M)
    def _():
        @pl.when(i % tiles_per_seq == 0)
        def _():
            tail_ref[...] = jnp.zeros(tail_ref.shape, F32)

        x = proj()
        xe = jnp.concatenate([tail_ref[...], x], axis=0)
        conv = vec(VEC_CONV_B) + vec(VEC_CONV_W + CONV_K - 1) * x
        for d in range(1, CONV_K):
            conv = conv + vec(VEC_CONV_W + CONV_K - 1 - d) * xe[SUBLANES - d:SUBLANES - d + tm, :]
        tail_ref[...] = x[tm - SUBLANES:, :]
        out_ref[...] = (conv * _sigmoid(conv)).astype(BF16)

    @pl.when(j == SEC_MV)
    def _():
        vt = _nt_dot(wbf_ref[...], xn_ref[...]).astype(BF16)
        for c in range(vt_out.shape[0]):
            vt_out[c] = vt[:, c * CHUNK:(c + 1) * CHUNK]

    @pl.when(j >= SEC_O)
    def _():
        out_ref[...] = _sigmoid(proj() + vec(VEC_GATE_B + j - SEC_O)).astype(BF16)


def _inproj(xn, w_t, layer, gate_row, vecs, cast_ws, batch, tm):
    t, d = xn.shape
    n_i = t // tm
    seq = t // batch
    tiles_per_seq = seq // tm
    assert tiles_per_seq * tm == seq
    n_sec = N_SECTIONS - SEC_V
    n_cast_steps = n_sec * n_i
    cast_rows = [_cast_rows(w.shape[1], n_cast_steps) for w in cast_ws]
    cast_step = lambda s, i: s * n_i + i

    def proj_block(s, i):
        j = s + SEC_V
        plane = jnp.where(j == SEC_V, PROJ_V, jnp.where(
            j >= SEC_GA, j - SEC_GA + PROJ_GA, jnp.where(j >= SEC_O, PROJ_O, PROJ_M)))
        return (plane, jnp.where(j == SEC_MV, n_i - 1, i), 0)

    def vt_block(s, i):
        return (jnp.clip((s + SEC_V - SEC_MV) * n_i + i, 0, n_i - 1), 0, 0)

    def w_rows(s, i):
        j = s + SEC_V
        return (layer, SUBLANES * (j * (SECTION // SUBLANES) + jnp.where(j >= SEC_GA, gate_skip // SUBLANES, 0)), 0)

    gate_skip = gate_row - SEC_GA * SECTION
    assert gate_skip % SUBLANES == 0
    assert (PROJ_GB, PROJ_O) == (PROJ_GA + SEC_GB - SEC_GA, PROJ_M + 1) and SEC_M + 1 == SEC_MV == SEC_O - 1
    kern = functools.partial(_inproj_kernel, tiles_per_seq=tiles_per_seq, n_cast_steps=n_cast_steps,
                             n_cast=len(cast_ws))
    return pl.pallas_call(
        kern,
        grid=(n_sec, n_i),
        in_specs=[
            pl.BlockSpec((tm, d), lambda s, i: (i, 0)),
            pl.BlockSpec((None, pl.Element(SECTION), pl.Element(d)), w_rows),
            pl.BlockSpec((N_VEC_ROWS, SECTION), lambda s, i: (0, 0)),
        ] + [pl.BlockSpec((None, r, w.shape[2]), lambda s, i: (layer, cast_step(s, i), 0))
             for r, w in zip(cast_rows, cast_ws)],
        out_specs=[pl.BlockSpec((None, tm, SECTION), proj_block), pl.BlockSpec((tm // CHUNK, SECTION, CHUNK), vt_block)]
        + [pl.BlockSpec((r, w.shape[2]), lambda s, i: (cast_step(s, i), 0)) for r, w in zip(cast_rows, cast_ws)],
        out_shape=[jax.ShapeDtypeStruct((N_PROJ, t, SECTION), BF16),
                   jax.ShapeDtypeStruct((t // CHUNK, SECTION, CHUNK), BF16)]
        + [jax.ShapeDtypeStruct(w.shape[1:], BF16) for w in cast_ws],
        scratch_shapes=[
            pltpu.VMEM((SECTION, d), BF16),
            pltpu.VMEM((SUBLANES, SECTION), F32),
        ],
        compiler_params=_params("arbitrary", "arbitrary"),
        name="inproj",
    )(xn, w_t, vecs, *cast_ws)


def _lanes(col):
    return jnp.broadcast_to(col, (col.shape[0], LANES))


def _wide(x, n):
    return jnp.concatenate([x] * n, axis=1)


def _mlstm_chunk(c_ref, o_ref, vt_ref, gcol_ref, grow_ref, bcol_ref, brow_ref, ng_ref,
                 st_ref, mx_ref, wqb_ref, wkb_ref, yb_ref, *, head_dim, batch):
    row = lax.broadcasted_iota(jnp.int32, (CHUNK, CHUNK), 0)
    col = lax.broadcasted_iota(jnp.int32, (CHUNK, CHUNK), 1)
    causal = col <= row
    tril = causal.astype(BF16)
    triu = (row <= col).astype(BF16)
    nd = head_dim // LANES
    groups = CHUNK // SUBLANES
    heads = [(b, h) for b in range(batch) for h in range(ML_HEADS)]
    slot = lambda b, h: b * ML_HEADS + h
    hsl = lambda h: slice(h * head_dim, (h + 1) * head_dim)


    n_if = grow_ref.shape[1]
    lane = lax.broadcasted_iota(jnp.int32, (CHUNK, LANES), 1)
    g_col = gcol_ref[0] + brow_ref[...]
    for b in range(1, batch):
        g_col = jnp.where(lane < b * n_if, g_col, pltpu.roll(gcol_ref[b] + brow_ref[...], b * n_if, 1))
    g_row = jnp.concatenate([grow_ref[b] + bcol_ref[...] for b in range(batch)], axis=0)
    lf_row = _log_sigmoid(g_row)
    bcum_col = _dot(jnp.concatenate([tril] * 3, axis=1), jnp.concatenate(_split3(_log_sigmoid(g_col)), axis=0))
    bcum_row = _dot(jnp.concatenate(_split3(lf_row), axis=1), jnp.concatenate([triu] * 3, axis=0))
    total = jnp.sum(lf_row, axis=-1, keepdims=True)

    q, k = {}, {}
    for h in range(ML_HEADS):
        c = jnp.concatenate([c_ref[b, :, hsl(h)] for b in range(batch)], axis=0)
        q_h, k_h = _dot(c, wqb_ref[h]), _dot(c, wkb_ref[h])
        for b in range(batch):
            q[b, h], k[b, h] = q_h[b * CHUNK:(b + 1) * CHUNK], k_h[b * CHUNK:(b + 1) * CHUNK]

    w_intra, w_inter, inv_floor, w_state, decay, scale, m_new = {}, {}, {}, {}, {}, {}, {}
    for b, h in heads:
        i, f = b * n_if + h, b * n_if + ML_HEADS + h
        bc = _lanes(bcum_col[:, f:f + 1])
        ig = _lanes(g_col[:, i:i + 1])
        src = g_row[i:i + 1, :] - bcum_row[f:f + 1, :]
        b_last = jnp.broadcast_to(total[f:f + 1, :], (1, LANES))
        m_prev = mx_ref[slot(b, h):slot(b, h) + 1, :]
        inter = bc + m_prev
        dlog = jnp.where(causal, bc + src, -jnp.inf)
        m_t = jnp.maximum(inter, _lanes(jnp.max(dlog, axis=-1, keepdims=True)))
        w_intra[b, h] = jnp.exp(dlog - m_t)
        w_inter[b, h] = jnp.exp(inter - m_t)
        inv_floor[b, h] = jnp.exp(-m_t)
        a = b_last - bc + ig
        a_max = jnp.max(a, axis=0, keepdims=True)
        w_state[b, h] = jnp.exp(a - a_max)
        m_new[b, h] = jnp.maximum(b_last + m_prev, a_max)
        decay[b, h] = jnp.exp(b_last + m_prev - m_new[b, h])
        scale[b, h] = jnp.exp(a_max - m_new[b, h])

    qk = {}
    for b, h in heads:
        qk[b, h] = _nt_dot(q[b, h].astype(BF16), k[b, h].astype(BF16)) * w_intra[b, h]

    for b, h in heads:
        s = slot(b, h)
        qw = q[b, h] * _wide(w_inter[b, h], nd)
        lhs = jnp.concatenate([qw.astype(BF16), qk[b, h].astype(BF16)], axis=1)
        rhs = jnp.concatenate([st_ref[s, :head_dim, :].astype(BF16), vt_ref[b, hsl(h), :]], axis=1)
        num = _nt_dot(lhs, rhs)
        qn = (qw.reshape(groups, SUBLANES, head_dim) * st_ref[s, head_dim:, :]).reshape(CHUNK, head_dim)
        part = qk[b, h] + sum(qn[:, i * LANES:(i + 1) * LANES] for i in range(nd))
        den = _lanes(jnp.sum(part, axis=-1, keepdims=True))
        hid = num * _wide(1.0 / jnp.maximum(jnp.abs(den), inv_floor[b, h]), nd)
        mu = jnp.mean(hid, axis=-1, keepdims=True)
        var = jnp.mean(jnp.square(hid - mu), axis=-1, keepdims=True)
        hid = (hid - mu) * lax.rsqrt(var + EPS) * ng_ref[:, hsl(h)]
        yb_ref[b, :, hsl(h)] = (o_ref[b, :, hsl(h)].astype(F32) * hid).astype(BF16)

    for b, h in heads:
        s = slot(b, h)
        kw = k[b, h] * _wide(w_state[b, h] * scale[b, h], nd)
        upd = _dot(vt_ref[b, hsl(h), :], kw.astype(BF16))
        kw_sum = jnp.sum(jnp.sum(kw.reshape(groups, SUBLANES, head_dim), axis=0), axis=0, keepdims=True)
        keep = _wide(decay[b, h], nd)
        st_ref[s, :head_dim, :] = keep * st_ref[s, :head_dim, :] + upd
        st_ref[s, head_dim:, :] = keep * st_ref[s, head_dim:, :] + kw_sum
        mx_ref[s:s + 1, :] = m_new[b, h]


def _mix_kernel(co_ref, vt_ref, gcol_ref, grow_ref, bcol_ref, brow_ref, wq_ref, wk_ref, ng_ref,
                x_ref, u_ref, v_ref, ws_ref, bs_ref, gate_ref, wa_ref, wb_ref, wo_ref, g2_ref, cast_ref,
                x1_ref, hn_ref, cast_out, st_ref, mx_ref, wqb_ref, wkb_ref, yb_ref, wsb_ref,
                *, head_dim, batch, n_steps):
    n = pl.program_id(0)

    @pl.when(n == 0)
    def _():
        row = lax.broadcasted_iota(jnp.int32, (CHUNK, CHUNK), 0)
        col = lax.broadcasted_iota(jnp.int32, (CHUNK, CHUNK), 1)
        wsb_ref[...] = jnp.where(col <= row, ws_ref[...], 0.0).astype(BF16)
        wqb_ref[...] = wq_ref[...].astype(BF16)
        wkb_ref[...] = (wk_ref[...] * (head_dim ** -0.5)).astype(BF16)
        st_ref[...] = jnp.zeros(st_ref.shape, F32)
        mx_ref[...] = jnp.zeros(mx_ref.shape, F32)

    def mlstm(slot):
        cast_out[...] = cast_ref[...].astype(BF16)
        _mlstm_chunk(co_ref.at[0], co_ref.at[1], vt_ref, gcol_ref, grow_ref, bcol_ref, brow_ref, ng_ref,
                     st_ref, mx_ref, wqb_ref, wkb_ref, yb_ref.at[slot], head_dim=head_dim, batch=batch)

    def spatial_gate():
        gated = [[] for _ in range(batch)]
        for g in range(wsb_ref.shape[0]):
            gs = slice(g * GM_GROUP_DIM, (g + 1) * GM_GROUP_DIM)
            vg = jnp.concatenate([v_ref[b, :, gs] for b in range(batch)], axis=1)
            s = _dot(wsb_ref[g], vg) + bs_ref[:, g:g + 1]
            for b in range(batch):
                sb = s[:, b * GM_GROUP_DIM:(b + 1) * GM_GROUP_DIM]
                gated[b].append((u_ref[b, :, gs].astype(F32) * sb).astype(BF16))
        return jnp.concatenate([jnp.concatenate(cols, axis=1) for cols in gated], axis=0)

    def merge(slot):
        flat = lambda v: v.reshape(batch * CHUNK, v.shape[-1])
        gate_a = jnp.concatenate([flat(gate_ref[0]), flat(gate_ref[1])], axis=1).astype(F32)
        gate_b = jnp.concatenate([flat(gate_ref[2]), flat(gate_ref[3])], axis=1).astype(F32)
        mixed = (gate_a * _dot(spatial_gate(), wa_ref[...])
                 + gate_b * _dot(flat(yb_ref[slot]), wb_ref[...]))
        x1 = flat(x_ref[...]) + _dot(mixed.astype(BF16), wo_ref[...])
        x1_ref[...] = x1.reshape(x1_ref.shape)
        hn_ref[...] = _rms_norm(x1, g2_ref[...]).astype(BF16).reshape(hn_ref.shape)

    @pl.when(n == 0)
    def _():
        mlstm(0)

    for parity in range(2):
        @pl.when((n > 0) & (n < n_steps) & (n % 2 == parity))
        def _(parity=parity):
            mlstm(parity)
            merge(1 - parity)

    @pl.when(n == n_steps)
    def _():
        merge((n_steps - 1) % 2)


def _mix(proj, u, vt, gates_col, gates_row, bias_col, bias_row, wq, wk, norm_g,
         x, ws, bs_t, w_a, w_b, w_out, g2, cast_w, layer, batch):
    _, t, width = proj.shape
    n_groups = width // GM_GROUP_DIM
    assert ws.shape == (n_groups, CHUNK, CHUNK)
    seq = t // batch
    d = x.shape[-1]
    head_dim = width // ML_HEADS
    n_if = gates_row.shape[1]
    n_steps = seq // CHUNK
    n_gate = 2 * (PROJ_GB - PROJ_GA)
    assert PROJ_O == PROJ_M + 1 and PROJ_M % 2 == 0 and PROJ_GA % n_gate == 0
    assert batch * n_if <= LANES, (batch, n_if)
    proj4 = proj.reshape(proj.shape[0], batch, seq, width)
    cast_rows = _cast_rows(cast_w.shape[1], n_steps)
    at = lambda n: jnp.minimum(n, n_steps - 1)
    done = lambda n: jnp.maximum(n - 1, 0)
    const2 = lambda n: (0, 0)
    const3 = lambda n: (0, 0, 0)
    resident = dict(pipeline_mode=pl.Buffered(1))
    n_state = batch * ML_HEADS
    kern = functools.partial(_mix_kernel, head_dim=head_dim, batch=batch, n_steps=n_steps)
    x1, hn, cast = pl.pallas_call(
        kern,
        grid=(n_steps + 1,),
        in_specs=[
            pl.BlockSpec((2, batch, CHUNK, width), lambda n: (PROJ_M // 2, 0, at(n), 0)),
            pl.BlockSpec((batch, None, width, CHUNK), lambda n: (0, at(n), 0, 0)),
            pl.BlockSpec((batch, CHUNK, LANES), lambda n: (0, at(n), 0)),
            pl.BlockSpec((batch, n_if, CHUNK), lambda n: (0, 0, at(n))),
            pl.BlockSpec((n_if, 1), const2),
            pl.BlockSpec((1, LANES), const2),
            pl.BlockSpec((ML_HEADS, head_dim, head_dim), const3),
            pl.BlockSpec((ML_HEADS, head_dim, head_dim), const3),
            pl.BlockSpec((1, width), const2),
            pl.BlockSpec((batch, CHUNK, d), lambda n: (0, done(n), 0)),
            pl.BlockSpec((batch, CHUNK, width), lambda n: (0, done(n), 0)),
            pl.BlockSpec((None, batch, CHUNK, width), lambda n: (PROJ_V, 0, done(n), 0)),
            pl.BlockSpec((n_groups, CHUNK, CHUNK), const3),
            pl.BlockSpec((CHUNK, n_groups), const2),
            pl.BlockSpec((n_gate, batch, CHUNK, SECTION), lambda n: (PROJ_GA // n_gate, 0, done(n), 0)),
            pl.BlockSpec((width, d), const2, **resident),
            pl.BlockSpec((width, d), const2, **resident),
            pl.BlockSpec((d, d), const2, **resident),
            pl.BlockSpec((1, d), const2),
            pl.BlockSpec((None, cast_rows, cast_w.shape[2]), lambda n: (layer, at(n), 0)),
        ],
        out_specs=[
            pl.BlockSpec((batch, CHUNK, d), lambda n: (0, done(n), 0)),
            pl.BlockSpec((batch, CHUNK, d), lambda n: (0, done(n), 0)),
            pl.BlockSpec((cast_rows, cast_w.shape[2]), lambda n: (at(n), 0)),
        ],
        out_shape=[
            jax.ShapeDtypeStruct((batch, seq, d), F32),
            jax.ShapeDtypeStruct((batch, seq, d), BF16),
            jax.ShapeDtypeStruct(cast_w.shape[1:], BF16),
        ],
        scratch_shapes=[
            pltpu.VMEM((n_state, head_dim + SUBLANES, head_dim), F32),
            pltpu.VMEM((n_state, LANES), F32),
            pltpu.VMEM((ML_HEADS, head_dim, head_dim), BF16),
            pltpu.VMEM((ML_HEADS, head_dim, head_dim), BF16),
            pltpu.VMEM((2, batch, CHUNK, width), BF16),
            pltpu.VMEM((n_groups, CHUNK, CHUNK), BF16),
        ],
        compiler_params=_params("arbitrary"),
        name="mix",
    )(proj4, vt.reshape(batch, n_steps, width, CHUNK), gates_col.reshape(batch, seq, LANES), gates_row, bias_col, bias_row, wq, wk, norm_g,
      x, u.reshape(batch, seq, width), proj4, ws, bs_t, proj4, w_a, w_b, w_out, g2, cast_w)
    return x1.reshape(t, d), hn.reshape(t, d), cast


def _ffn_kernel(hn_ref, x1_ref, w1_ref, w2_ref, gf_ref, out_ref, *, final_norm):
    j = pl.program_id(1)

    @pl.when(j == 0)
    def _():
        out_ref[...] = x1_ref[...]

    h = jnp.square(jnp.maximum(_dot(hn_ref[...], w1_ref[...]), 0.0)).astype(BF16)
    out_ref[...] += _dot(h, w2_ref[...])

    if final_norm:
        @pl.when(j == pl.num_programs(1) - 1)
        def _():
            out_ref[...] = _rms_norm(out_ref[...], gf_ref[...])


def _ffn(hn, x1, w1, w2, gf, tm, tf, final_norm):
    t, d = x1.shape
    d_ff = w1.shape[1]
    return pl.pallas_call(
        functools.partial(_ffn_kernel, final_norm=final_norm),
        grid=(t // tm, d_ff // tf),
        in_specs=[
            pl.BlockSpec((tm, d), lambda i, j: (i, 0)),
            pl.BlockSpec((tm, d), lambda i, j: (i, 0)),
            pl.BlockSpec((d, tf), lambda i, j: (0, j)),
            pl.BlockSpec((tf, d), lambda i, j: (j, 0)),
            pl.BlockSpec((1, d), lambda i, j: (0, 0)),
        ],
        out_specs=pl.BlockSpec((tm, d), lambda i, j: (i, 0)),
        out_shape=jax.ShapeDtypeStruct((t, d), F32),
        compiler_params=_params("arbitrary", "arbitrary"),
        name="ffn",
    )(hn, x1, w1, w2, gf)


def _tiles(t):
    return dict(prologue=min(t, 1024), inproj=min(t, 1024), ffn_m=min(t, 512), ffn_f=2048)


def kernel(x, norm1_g, w_in, b_gate, gm_ln_g, gm_ln_b, gm_ws, gm_bs, ml_conv_w, ml_conv_b, ml_wq, ml_wk,
           ml_ig_b, ml_fg_b, ml_norm_g, w_a, w_b, w_out, norm2_g, w_ff1, w_ff2, norm_f_g):
    batch, seq, d = x.shape
    depth = w_in.shape[0]
    t = batch * seq
    tiles = _tiles(t)
    if_row = SEC_GA * SECTION
    n_if = 2 * ML_HEADS
    xt = x.reshape(t, d)
    w_t = jnp.swapaxes(w_in, 1, 2)
    for l in range(depth):
        gate_bias = jnp.concatenate([ml_ig_b[l], ml_fg_b[l]])
        vecs = jnp.concatenate([
            gm_ln_g[l][None], gm_ln_b[l][None], ml_conv_b[l][None], ml_conv_w[l],
            jnp.zeros((1, SECTION), F32), b_gate[l].reshape(4, SECTION),
            jnp.zeros((N_VEC_ROWS - VEC_GATE_B - 5, SECTION), F32)])
        xn, gates_col, gates_row, u = _prologue(xt, norm1_g[l][None], w_t, l, if_row, n_if, batch,
                                                tiles["prologue"])
        proj, vt, w2_bf, wa_bf, wb_bf, wo_bf = _inproj(
            xn, w_t, l, if_row + n_if, vecs, (w_ff2, w_a, w_b, w_out), batch, tiles["inproj"])
        x1, hn, w1_bf = _mix(
            proj, u, vt, gates_col, gates_row, gate_bias[:, None], jnp.pad(gate_bias, (0, LANES - n_if))[None],
            ml_wq[l], ml_wk[l], ml_norm_g[l][None], xt.reshape(batch, seq, d), gm_ws[l], gm_bs[l].T,
            wa_bf, wb_bf, wo_bf, norm2_g[l][None], w_ff1, l, batch)
        xt = _ffn(hn, x1, w1_bf, w2_bf, norm_f_g[None], tiles["ffn_m"], tiles["ffn_f"],
                  final_norm=l == depth - 1)
    return xt.reshape(batch, seq, d)
```

```python
import functools

import jax
import jax.numpy as jnp
from jax import lax
from jax.experimental import pallas as pl
from jax.experimental.pallas import tpu as pltpu

EPS = 1e-6
GM_GROUP_DIM = 128
CHUNK = 128
ML_HEADS = 4
CONV_K = 4
LANES = 128
SUBLANES = 8
BF16_ROWS = 16
SECTION = 1024
SEC_U, SEC_V, SEC_M, SEC_MV, SEC_O, SEC_GA, SEC_GB = 0, 1, 2, 3, 4, 5, 7
N_SECTIONS = 9
PROJ_GA, PROJ_GB, PROJ_M, PROJ_O, PROJ_V = 0, 2, 4, 5, 6
N_PROJ = 7
VMEM_LIMIT = 56 * 1024 * 1024

BF16 = jnp.bfloat16
F32 = jnp.float32


def _sigmoid(x):
    return 0.5 * jnp.tanh(0.5 * x) + 0.5


def _log_sigmoid(x):
    return jnp.minimum(x, 0.0) - jnp.log(1.0 + jnp.exp(-jnp.abs(x)))


def _rms_norm(x, g):
    return x * lax.rsqrt(jnp.mean(x * x, axis=-1, keepdims=True) + EPS) * g


def _dot(a, b):
    return jnp.dot(a, b, preferred_element_type=F32)


def _split3(x):
    hi = x.astype(BF16)
    rest = x - hi.astype(F32)
    mid = rest.astype(BF16)
    return hi, mid, (rest - mid.astype(F32)).astype(BF16)


def _params(*semantics):
    return pltpu.CompilerParams(dimension_semantics=semantics, vmem_limit_bytes=VMEM_LIMIT)


def _cast_rows(n_rows, n_steps):
    rows = n_rows // n_steps
    assert rows * n_steps == n_rows and rows % BF16_ROWS == 0, (n_rows, n_steps)
    return rows


def _nt_dot(a, b):
    return lax.dot_general(a, b, (((1,), (1,)), ((), ())), preferred_element_type=F32)


def _cast_section(w_ref, wbf_ref):
    rows = 256
    for r in range(0, wbf_ref.shape[0], rows):
        wbf_ref[r:r + rows, :] = w_ref[r:r + rows, :].astype(BF16)


def _prologue_kernel(x_ref, g1_ref, wif_ref, wu_ref, xn_ref, gcol_ref, grow_ref, u_ref, wub_ref):
    @pl.when(pl.program_id(0) == 0)
    def _():
        _cast_section(wu_ref, wub_ref)

    xn = _rms_norm(x_ref[...], g1_ref[...]).astype(BF16)
    xn_ref[...] = xn
    wif = wif_ref[...].astype(BF16)
    gcol = _nt_dot(xn, wif)
    gcol_ref[...] = gcol
    grow_ref[...] = jnp.concatenate(
        [gcol[r:r + LANES, :].T[:grow_ref.shape[0], :] for r in range(0, gcol.shape[0], LANES)], axis=1)
    u_ref[...] = jax.nn.gelu(_nt_dot(xn, wub_ref[...])).astype(BF16)


def _prologue(x, g1, w_t, layer, if_row, n_if, batch, tm):
    t, d = x.shape
    seq = t // batch
    tiles_per_seq = seq // tm
    assert tiles_per_seq * tm == seq
    return pl.pallas_call(
        _prologue_kernel,
        grid=(t // tm,),
        in_specs=[
            pl.BlockSpec((tm, d), lambda i: (i, 0)),
            pl.BlockSpec((1, d), lambda i: (0, 0)),
            pl.BlockSpec((None, LANES, d), lambda i: (layer, if_row // LANES, 0)),
            pl.BlockSpec((None, SECTION, d), lambda i: (layer, SEC_U, 0), pipeline_mode=pl.Buffered(1)),
        ],
        out_specs=[
            pl.BlockSpec((tm, d), lambda i: (i, 0)),
            pl.BlockSpec((tm, LANES), lambda i: (i, 0)),
            pl.BlockSpec((None, n_if, tm), lambda i: (i // tiles_per_seq, 0, i % tiles_per_seq)),
            pl.BlockSpec((tm, SECTION), lambda i: (i, 0)),
        ],
        out_shape=[
            jax.ShapeDtypeStruct((t, d), BF16),
            jax.ShapeDtypeStruct((t, LANES), F32),
            jax.ShapeDtypeStruct((batch, n_if, seq), F32),
            jax.ShapeDtypeStruct((t, SECTION), BF16),
        ],
        scratch_shapes=[pltpu.VMEM((SECTION, d), BF16)],
        compiler_params=_params("arbitrary"),
        name="prologue",
    )(x, g1, w_t, w_t)


VEC_LN_G, VEC_LN_B, VEC_CONV_B, VEC_CONV_W, VEC_GATE_B = 0, 1, 2, 3, 7
N_VEC_ROWS = 16


def _inproj_kernel(xn_ref, w_ref, vec_ref, *rest, tiles_per_seq, n_cast_steps, n_cast):
    cast_in, (out_ref, vt_out, *cast_out), (wbf_ref, tail_ref) = (
        rest[:n_cast], rest[n_cast:2 * n_cast + 2], rest[2 * n_cast + 2:])
    j = pl.program_id(0) + SEC_V
    i = pl.program_id(1)
    tm = xn_ref.shape[0]
    vec = lambda r, n=1: vec_ref[pl.ds(r, n), :]

    @pl.when(pl.program_id(0) * pl.num_programs(1) + i < n_cast_steps)
    def _():
        for src, dst in zip(cast_in, cast_out):
            dst[...] = src[...].astype(BF16)

    @pl.when(i == 0)
    def _():
        _cast_section(w_ref, wbf_ref)

    def proj():
        return _nt_dot(xn_ref[...], wbf_ref[...])

    @pl.when(j == SEC_V)
    def _():
        v = jax.nn.gelu(proj())
        mu = jnp.mean(v, axis=-1, keepdims=True)
        var = jnp.mean(jnp.square(v - mu), axis=-1, keepdims=True)
        out_ref[...] = ((v - mu) * lax.rsqrt(var + EPS) * vec(VEC_LN_G) + vec(VEC_LN_B)).astype(BF16)

    @pl.when(j == SEC_M)
    def _():
        @pl.when(i % tiles_per_seq == 0)
        def _():
            tail_ref[...] = jnp.zeros(tail_ref.shape, F32)

        x = proj()
        xe = jnp.concatenate([tail_ref[...], x], axis=0)
        conv = vec(VEC_CONV_B) + vec(VEC_CONV_W + CONV_K - 1) * x
        for d in range(1, CONV_K):
            conv = conv + vec(VEC_CONV_W + CONV_K - 1 - d) * xe[SUBLANES - d:SUBLANES - d + tm, :]
        tail_ref[...] = x[tm - SUBLANES:, :]
        out_ref[...] = (conv * _sigmoid(conv)).astype(BF16)

    @pl.when(j == SEC_MV)
    def _():
        vt = _nt_dot(wbf_ref[...], xn_ref[...]).astype(BF16)
        for c in range(vt_out.shape[0]):
            vt_out[c] = vt[:, c * CHUNK:(c + 1) * CHUNK]

    @pl.when(j >= SEC_O)
    def _():
        out_ref[...] = _sigmoid(proj() + vec(VEC_GATE_B + j - SEC_O)).astype(BF16)


def _inproj(xn, w_t, layer, gate_row, vecs, cast_ws, batch, tm):
    t, d = xn.shape
    n_i = t // tm
    seq = t // batch
    tiles_per_seq = seq // tm
    assert tiles_per_seq * tm == seq
    n_sec = N_SECTIONS - SEC_V
    n_cast_steps = n_sec * n_i
    cast_rows = [_cast_rows(w.shape[1], n_cast_steps) for w in cast_ws]
    cast_step = lambda s, i: s * n_i + i

    def proj_block(s, i):
        j = s + SEC_V
        plane = jnp.where(j == SEC_V, PROJ_V, jnp.where(
            j >= SEC_GA, j - SEC_GA + PROJ_GA, jnp.where(j >= SEC_O, PROJ_O, PROJ_M)))
        return (plane, jnp.where(j == SEC_MV, n_i - 1, i), 0)

    def vt_block(s, i):
        return (jnp.clip((s + SEC_V - SEC_MV) * n_i + i, 0, n_i - 1), 0, 0)

    def w_rows(s, i):
        j = s + SEC_V
        return (layer, SUBLANES * (j * (SECTION // SUBLANES) + jnp.where(j >= SEC_GA, gate_skip // SUBLANES, 0)), 0)

    gate_skip = gate_row - SEC_GA * SECTION
    assert gate_skip % SUBLANES == 0
    assert (PROJ_GB, PROJ_O) == (PROJ_GA + SEC_GB - SEC_GA, PROJ_M + 1) and SEC_M + 1 == SEC_MV == SEC_O - 1
    kern = functools.partial(_inproj_kernel, tiles_per_seq=tiles_per_seq, n_cast_steps=n_cast_steps,
                             n_cast=len(cast_ws))
    return pl.pallas_call(
        kern,
        grid=(n_sec, n_i),
        in_specs=[
            pl.BlockSpec((tm, d), lambda s, i: (i, 0)),
            pl.BlockSpec((None, pl.Element(SECTION), pl.Element(d)), w_rows),
            pl.BlockSpec((N_VEC_ROWS, SECTION), lambda s, i: (0, 0)),
        ] + [pl.BlockSpec((None, r, w.shape[2]), lambda s, i: (layer, cast_step(s, i), 0))
             for r, w in zip(cast_rows, cast_ws)],
        out_specs=[pl.BlockSpec((None, tm, SECTION), proj_block), pl.BlockSpec((tm // CHUNK, SECTION, CHUNK), vt_block)]
        + [pl.BlockSpec((r, w.shape[2]), lambda s, i: (cast_step(s, i), 0)) for r, w in zip(cast_rows, cast_ws)],
        out_shape=[jax.ShapeDtypeStruct((N_PROJ, t, SECTION), BF16),
                   jax.ShapeDtypeStruct((t // CHUNK, SECTION, CHUNK), BF16)]
        + [jax.ShapeDtypeStruct(w.shape[1:], BF16) for w in cast_ws],
        scratch_shapes=[
            pltpu.VMEM((SECTION, d), BF16),
            pltpu.VMEM((SUBLANES, SECTION), F32),
        ],
        compiler_params=_params("arbitrary", "arbitrary"),
        name="inproj",
    )(xn, w_t, vecs, *cast_ws)


def _lanes(col):
    return jnp.broadcast_to(col, (col.shape[0], LANES))


def _wide(x, n):
    return jnp.concatenate([x] * n, axis=1)


def _mlstm_chunk(c_ref, o_ref, vt_ref, gcol_ref, grow_ref, bcol_ref, brow_ref, ng_ref,
                 st_ref, mx_ref, wqb_ref, wkb_ref, yb_ref, *, head_dim, batch):
    row = lax.broadcasted_iota(jnp.int32, (CHUNK, CHUNK), 0)
    col = lax.broadcasted_iota(jnp.int32, (CHUNK, CHUNK), 1)
    causal = col <= row
    tril = causal.astype(BF16)
    triu = (row <= col).astype(BF16)
    nd = head_dim // LANES
    groups = CHUNK // SUBLANES
    heads = [(b, h) for b in range(batch) for h in range(ML_HEADS)]
    slot = lambda b, h: b * ML_HEADS + h
    hsl = lambda h: slice(h * head_dim, (h + 1) * head_dim)


    n_if = grow_ref.shape[1]
    lane = lax.broadcasted_iota(jnp.int32, (CHUNK, LANES), 1)
    g_col = gcol_ref[0] + brow_ref[...]
    for b in range(1, batch):
        g_col = jnp.where(lane < b * n_if, g_col, pltpu.roll(gcol_ref[b] + brow_ref[...], b * n_if, 1))
    g_row = jnp.concatenate([grow_ref[b] + bcol_ref[...] for b in range(batch)], axis=0)
    lf_row = _log_sigmoid(g_row)
    bcum_col = _dot(jnp.concatenate([tril] * 3, axis=1), jnp.concatenate(_split3(_log_sigmoid(g_col)), axis=0))
    bcum_row = _dot(jnp.concatenate(_split3(lf_row), axis=1), jnp.concatenate([triu] * 3, axis=0))
    total = jnp.sum(lf_row, axis=-1, keepdims=True)

    q, k = {}, {}
    for h in range(ML_HEADS):
        c = jnp.concatenate([c_ref[b, :, hsl(h)] for b in range(batch)], axis=0)
        q_h, k_h = _dot(c, wqb_ref[h]), _dot(c, wkb_ref[h])
        for b in range(batch):
            q[b, h], k[b, h] = q_h[b * CHUNK:(b + 1) * CHUNK], k_h[b * CHUNK:(b + 1) * CHUNK]

    w_intra, w_inter, inv_floor, w_state, decay, scale, m_new = {}, {}, {}, {}, {}, {}, {}
    for b, h in heads:
        i, f = b * n_if + h, b * n_if + ML_HEADS + h
        bc = _lanes(bcum_col[:, f:f + 1])
        ig = _lanes(g_col[:, i:i + 1])
        src = g_row[i:i + 1, :] - bcum_row[f:f + 1, :]
        b_last = jnp.broadcast_to(total[f:f + 1, :], (1, LANES))
        m_prev = mx_ref[slot(b, h):slot(b, h) + 1, :]
        inter = bc + m_prev
        dlog = jnp.where(causal, bc + src, -jnp.inf)
        m_t = jnp.maximum(inter, _lanes(jnp.max(dlog, axis=-1, keepdims=True)))
        w_intra[b, h] = jnp.exp(dlog - m_t)
        w_inter[b, h] = jnp.exp(inter - m_t)
        inv_floor[b, h] = jnp.exp(-m_t)
        a = b_last - bc + ig
        a_max = jnp.max(a, axis=0, keepdims=True)
        w_state[b, h] = jnp.exp(a - a_max)
        m_new[b, h] = jnp.maximum(b_last + m_prev, a_max)
        decay[b, h] = jnp.exp(b_last + m_prev - m_new[b, h])
        scale[b, h] = jnp.exp(a_max - m_new[b, h])

    qk = {}
    for b, h in heads:
        qk[b, h] = _nt_dot(q[b, h].astype(BF16), k[b, h].astype(BF16)) * w_intra[b, h]

    for b, h in heads:
        s = slot(b, h)
        qw = q[b, h] * _wide(w_inter[b, h], nd)
        lhs = jnp.concatenate([qw.astype(BF16), qk[b, h].astype(BF16)], axis=1)
        rhs = jnp.concatenate([st_ref[s, :head_dim, :].astype(BF16), vt_ref[b, hsl(h), :]], axis=1)
        num = _nt_dot(lhs, rhs)
        qn = (qw.reshape(groups, SUBLANES, head_dim) * st_ref[s, head_dim:, :]).reshape(CHUNK, head_dim)
        part = qk[b, h] + sum(qn[:, i * LANES:(i + 1) * LANES] for i in range(nd))
        den = _lanes(jnp.sum(part, axis=-1, keepdims=True))
        hid = num * _wide(1.0 / jnp.maximum(jnp.abs(den), inv_floor[b, h]), nd)
        mu = jnp.mean(hid, axis=-1, keepdims=True)
        var = jnp.mean(jnp.square(hid - mu), axis=-1, keepdims=True)
        hid = (hid - mu) * lax.rsqrt(var + EPS) * ng_ref[:, hsl(h)]
        yb_ref[b, :, hsl(h)] = (o_ref[b, :, hsl(h)].astype(F32) * hid).astype(BF16)

    for b, h in heads:
        s = slot(b, h)
        kw = k[b, h] * _wide(w_state[b, h] * scale[b, h], nd)
        upd = _dot(vt_ref[b, hsl(h), :], kw.astype(BF16))
        kw_sum = jnp.sum(jnp.sum(kw.reshape(groups, SUBLANES, head_dim), axis=0), axis=0, keepdims=True)
        keep = _wide(decay[b, h], nd)
        st_ref[s, :head_dim, :] = keep * st_ref[s, :head_dim, :] + upd
        st_ref[s, head_dim:, :] = keep * st_ref[s, head_dim:, :] + kw_sum
        mx_ref[s:s + 1, :] = m_new[b, h]


def _mix_kernel(co_ref, vt_ref, gcol_ref, grow_ref, bcol_ref, brow_ref, wq_ref, wk_ref, ng_ref,
                x_ref, u_ref, v_ref, ws_ref, bs_ref, gate_ref, wa_ref, wb_ref, wo_ref, g2_ref, cast_ref,
                x1_ref, hn_ref, cast_out, st_ref, mx_ref, wqb_ref, wkb_ref, yb_ref, wsb_ref,
                *, head_dim, batch):
    n = pl.program_id(0)

    @pl.when(n == 0)
    def _():
        row = lax.broadcasted_iota(jnp.int32, (CHUNK, CHUNK), 0)
        col = lax.broadcasted_iota(jnp.int32, (CHUNK, CHUNK), 1)
        wsb_ref[...] = jnp.where(col <= row, ws_ref[...], 0.0).astype(BF16)
        wqb_ref[...] = wq_ref[...].astype(BF16)
        wkb_ref[...] = (wk_ref[...] * (head_dim ** -0.5)).astype(BF16)
        st_ref[...] = jnp.zeros(st_ref.shape, F32)
        mx_ref[...] = jnp.zeros(mx_ref.shape, F32)
        yb_ref[...] = jnp.zeros(yb_ref.shape, BF16)

    def mlstm(slot):
        cast_out[...] = cast_ref[...].astype(BF16)
        _mlstm_chunk(co_ref.at[0], co_ref.at[1], vt_ref, gcol_ref, grow_ref, bcol_ref, brow_ref, ng_ref,
                     st_ref, mx_ref, wqb_ref, wkb_ref, yb_ref.at[slot], head_dim=head_dim, batch=batch)

    def spatial_gate():
        gated = [[] for _ in range(batch)]
        for g in range(wsb_ref.shape[0]):
            gs = slice(g * GM_GROUP_DIM, (g + 1) * GM_GROUP_DIM)
            vg = jnp.concatenate([v_ref[b, :, gs] for b in range(batch)], axis=1)
            s = _dot(wsb_ref[g], vg) + bs_ref[:, g:g + 1]
            for b in range(batch):
                sb = s[:, b * GM_GROUP_DIM:(b + 1) * GM_GROUP_DIM]
                gated[b].append((u_ref[b, :, gs].astype(F32) * sb).astype(BF16))
        return jnp.concatenate([jnp.concatenate(cols, axis=1) for cols in gated], axis=0)

    def merge(slot):
        flat = lambda v: v.reshape(batch * CHUNK, v.shape[-1])
        gate_a = jnp.concatenate([flat(gate_ref[0]), flat(gate_ref[1])], axis=1).astype(F32)
        gate_b = jnp.concatenate([flat(gate_ref[2]), flat(gate_ref[3])], axis=1).astype(F32)
        mixed = (gate_a * _dot(spatial_gate(), wa_ref[...])
                 + gate_b * _dot(flat(yb_ref[slot]), wb_ref[...]))
        x1 = flat(x_ref[...]) + _dot(mixed.astype(BF16), wo_ref[...])
        x1_ref[...] = x1.reshape(x1_ref.shape)
        hn_ref[...] = _rms_norm(x1, g2_ref[...]).astype(BF16).reshape(hn_ref.shape)

    mlstm(n % 2)
    merge(1 - n % 2)


def _mix(proj, u, vt, gates_col, gates_row, bias_col, bias_row, wq, wk, norm_g,
         x, ws, bs_t, w_a, w_b, w_out, g2, cast_w, layer, batch):
    _, t, width = proj.shape
    n_groups = width // GM_GROUP_DIM
    assert ws.shape == (n_groups, CHUNK, CHUNK)
    seq = t // batch
    d = x.shape[-1]
    head_dim = width // ML_HEADS
    n_if = gates_row.shape[1]
    n_steps = seq // CHUNK
    n_gate = 2 * (PROJ_GB - PROJ_GA)
    assert PROJ_O == PROJ_M + 1 and PROJ_M % 2 == 0 and PROJ_GA % n_gate == 0
    assert batch * n_if <= LANES, (batch, n_if)
    proj4 = proj.reshape(proj.shape[0], batch, seq, width)
    cast_rows = _cast_rows(cast_w.shape[1], n_steps)
    at = lambda n: jnp.minimum(n, n_steps - 1)
    done = lambda n: jnp.maximum(n - 1, 0)
    const2 = lambda n: (0, 0)
    const3 = lambda n: (0, 0, 0)
    resident = dict(pipeline_mode=pl.Buffered(1))
    n_state = batch * ML_HEADS
    kern = functools.partial(_mix_kernel, head_dim=head_dim, batch=batch)
    x1, hn, cast = pl.pallas_call(
        kern,
        grid=(n_steps + 1,),
        in_specs=[
            pl.BlockSpec((2, batch, CHUNK, width), lambda n: (PROJ_M // 2, 0, at(n), 0)),
            pl.BlockSpec((batch, None, width, CHUNK), lambda n: (0, at(n), 0, 0)),
            pl.BlockSpec((batch, CHUNK, LANES), lambda n: (0, at(n), 0)),
            pl.BlockSpec((batch, n_if, CHUNK), lambda n: (0, 0, at(n))),
            pl.BlockSpec((n_if, 1), const2),
            pl.BlockSpec((1, LANES), const2),
            pl.BlockSpec((ML_HEADS, head_dim, head_dim), const3),
            pl.BlockSpec((ML_HEADS, head_dim, head_dim), const3),
            pl.BlockSpec((1, width), const2),
            pl.BlockSpec((batch, CHUNK, d), lambda n: (0, done(n), 0)),
            pl.BlockSpec((batch, CHUNK, width), lambda n: (0, done(n), 0)),
            pl.BlockSpec((None, batch, CHUNK, width), lambda n: (PROJ_V, 0, done(n), 0)),
            pl.BlockSpec((n_groups, CHUNK, CHUNK), const3),
            pl.BlockSpec((CHUNK, n_groups), const2),
            pl.BlockSpec((n_gate, batch, CHUNK, SECTION), lambda n: (PROJ_GA // n_gate, 0, done(n), 0)),
            pl.BlockSpec((width, d), const2, **resident),
            pl.BlockSpec((width, d), const2, **resident),
            pl.BlockSpec((d, d), const2, **resident),
            pl.BlockSpec((1, d), const2),
            pl.BlockSpec((None, cast_rows, cast_w.shape[2]), lambda n: (layer, at(n), 0)),
        ],
        out_specs=[
            pl.BlockSpec((batch, CHUNK, d), lambda n: (0, done(n), 0)),
            pl.BlockSpec((batch, CHUNK, d), lambda n: (0, done(n), 0)),
            pl.BlockSpec((cast_rows, cast_w.shape[2]), lambda n: (at(n), 0)),
        ],
        out_shape=[
            jax.ShapeDtypeStruct((batch, seq, d), F32),
            jax.ShapeDtypeStruct((batch, seq, d), BF16),
            jax.ShapeDtypeStruct(cast_w.shape[1:], BF16),
        ],
        scratch_shapes=[
            pltpu.VMEM((n_state, head_dim + SUBLANES, head_dim), F32),
            pltpu.VMEM((n_state, LANES), F32),
            pltpu.VMEM((ML_HEADS, head_dim, head_dim), BF16),
            pltpu.VMEM((ML_HEADS, head_dim, head_dim), BF16),
            pltpu.VMEM((2, batch, CHUNK, width), BF16),
            pltpu.VMEM((n_groups, CHUNK, CHUNK), BF16),
        ],
        compiler_params=_params("arbitrary"),
        name="mix",
    )(proj4, vt.reshape(batch, n_steps, width, CHUNK), gates_col.reshape(batch, seq, LANES), gates_row, bias_col, bias_row, wq, wk, norm_g,
      x, u.reshape(batch, seq, width), proj4, ws, bs_t, proj4, w_a, w_b, w_out, g2, cast_w)
    return x1.reshape(t, d), hn.reshape(t, d), cast


def _ffn_kernel(hn_ref, x1_ref, w1_ref, w2_ref, gf_ref, out_ref, *, final_norm):
    j = pl.program_id(1)

    @pl.when(j == 0)
    def _():
        out_ref[...] = x1_ref[...]

    h = jnp.square(jnp.maximum(_dot(hn_ref[...], w1_ref[...]), 0.0)).astype(BF16)
    out_ref[...] += _dot(h, w2_ref[...])

    if final_norm:
        @pl.when(j == pl.num_programs(1) - 1)
        def _():
            out_ref[...] = _rms_norm(out_ref[...], gf_ref[...])


def _ffn(hn, x1, w1, w2, gf, tm, tf, final_norm):
    t, d = x1.shape
    d_ff = w1.shape[1]
    return pl.pallas_call(
        functools.partial(_ffn_kernel, final_norm=final_norm),
        grid=(t // tm, d_ff // tf),
        in_specs=[
            pl.BlockSpec((tm, d), lambda i, j: (i, 0)),
            pl.BlockSpec((tm, d), lambda i, j: (i, 0)),
            pl.BlockSpec((d, tf), lambda i, j: (0, j)),
            pl.BlockSpec((tf, d), lambda i, j: (j, 0)),
            pl.BlockSpec((1, d), lambda i, j: (0, 0)),
        ],
        out_specs=pl.BlockSpec((tm, d), lambda i, j: (i, 0)),
        out_shape=jax.ShapeDtypeStruct((t, d), F32),
        compiler_params=_params("arbitrary", "arbitrary"),
        name="ffn",
    )(hn, x1, w1, w2, gf)


def _tiles(t):
    return dict(prologue=min(t, 1024), inproj=min(t, 1024), ffn_m=min(t, 512), ffn_f=2048)


def kernel(x, norm1_g, w_in, b_gate, gm_ln_g, gm_ln_b, gm_ws, gm_bs, ml_conv_w, ml_conv_b, ml_wq, ml_wk,
           ml_ig_b, ml_fg_b, ml_norm_g, w_a, w_b, w_out, norm2_g, w_ff1, w_ff2, norm_f_g):
    batch, seq, d = x.shape
    depth = w_in.shape[0]
    t = batch * seq
    tiles = _tiles(t)
    if_row = SEC_GA * SECTION
    n_if = 2 * ML_HEADS
    xt = x.reshape(t, d)
    w_t = jnp.swapaxes(w_in, 1, 2)
    for l in range(depth):
        gate_bias = jnp.concatenate([ml_ig_b[l], ml_fg_b[l]])
        vecs = jnp.concatenate([
            gm_ln_g[l][None], gm_ln_b[l][None], ml_conv_b[l][None], ml_conv_w[l],
            jnp.zeros((1, SECTION), F32), b_gate[l].reshape(4, SECTION),
            jnp.zeros((N_VEC_ROWS - VEC_GATE_B - 5, SECTION), F32)])
        xn, gates_col, gates_row, u = _prologue(xt, norm1_g[l][None], w_t, l, if_row, n_if, batch,
                                                tiles["prologue"])
        proj, vt, w2_bf, wa_bf, wb_bf, wo_bf = _inproj(
            xn, w_t, l, if_row + n_if, vecs, (w_ff2, w_a, w_b, w_out), batch, tiles["inproj"])
        x1, hn, w1_bf = _mix(
            proj, u, vt, gates_col, gates_row, gate_bias[:, None], jnp.pad(gate_bias, (0, LANES - n_if))[None],
            ml_wq[l], ml_wk[l], ml_norm_g[l][None], xt.reshape(batch, seq, d), gm_ws[l], gm_bs[l].T,
            wa_bf, wb_bf, wo_bf, norm2_g[l][None], w_ff1, l, batch)
        xt = _ffn(hn, x1, w1_bf, w2_bf, norm_f_g[None], tiles["ffn_m"], tiles["ffn_f"],
                  final_norm=l == depth - 1)
    return xt.reshape(batch, seq, d)
```

```python
import functools

import jax
import jax.numpy as jnp
from jax import lax
from jax.experimental import pallas as pl
from jax.experimental.pallas import tpu as pltpu

EPS = 1e-6
GM_GROUP_DIM = 128
CHUNK = 128
ML_HEADS = 4
CONV_K = 4
LANES = 128
SUBLANES = 8
BF16_ROWS = 16
XN_SLOTS = 3
SECTION = 1024
SEC_U, SEC_V, SEC_M, SEC_MV, SEC_O, SEC_GA, SEC_GB = 0, 1, 2, 3, 4, 5, 7
N_SECTIONS = 9
PROJ_GA, PROJ_GB, PROJ_M, PROJ_O, PROJ_V = 0, 2, 4, 5, 6
N_PROJ = 7
VMEM_LIMIT = 56 * 1024 * 1024

BF16 = jnp.bfloat16
F32 = jnp.float32


def _sigmoid(x):
    return 0.5 * jnp.tanh(0.5 * x) + 0.5


def _log_sigmoid(x):
    return jnp.minimum(x, 0.0) - jnp.log(1.0 + jnp.exp(-jnp.abs(x)))


def _rms_norm(x, g):
    return x * lax.rsqrt(jnp.mean(x * x, axis=-1, keepdims=True) + EPS) * g


def _dot(a, b):
    return jnp.dot(a, b, preferred_element_type=F32)


def _split3(x):
    hi = x.astype(BF16)
    rest = x - hi.astype(F32)
    mid = rest.astype(BF16)
    return hi, mid, (rest - mid.astype(F32)).astype(BF16)


def _params(*semantics):
    return pltpu.CompilerParams(dimension_semantics=semantics, vmem_limit_bytes=VMEM_LIMIT)


def _cast_rows(n_rows, n_steps):
    rows = n_rows // n_steps
    assert rows * n_steps == n_rows and rows % BF16_ROWS == 0, (n_rows, n_steps)
    return rows


def _nt_dot(a, b):
    return lax.dot_general(a, b, (((1,), (1,)), ((), ())), preferred_element_type=F32)


def _cast_section(w_ref, wbf_ref):
    rows = 256
    for r in range(0, wbf_ref.shape[0], rows):
        wbf_ref[r:r + rows, :] = w_ref[r:r + rows, :].astype(BF16)


def _prologue_kernel(x_ref, g1_ref, wif_ref, wu_ref, xn_ref, gcol_ref, grow_ref, u_ref, wub_ref):
    @pl.when(pl.program_id(0) == 0)
    def _():
        _cast_section(wu_ref, wub_ref)

    xn = _rms_norm(x_ref[...], g1_ref[...]).astype(BF16)
    xn_ref[...] = xn
    wif = wif_ref[...].astype(BF16)
    gcol = _nt_dot(xn, wif)
    gcol_ref[...] = gcol
    grow_ref[...] = jnp.concatenate(
        [gcol[r:r + LANES, :].T[:grow_ref.shape[0], :] for r in range(0, gcol.shape[0], LANES)], axis=1)
    u_ref[...] = jax.nn.gelu(_nt_dot(xn, wub_ref[...])).astype(BF16)


def _prologue(x, g1, w_t, layer, if_row, n_if, batch, tm):
    t, d = x.shape
    seq = t // batch
    tiles_per_seq = seq // tm
    assert tiles_per_seq * tm == seq
    return pl.pallas_call(
        _prologue_kernel,
        grid=(t // tm,),
        in_specs=[
            pl.BlockSpec((tm, d), lambda i: (i, 0)),
            pl.BlockSpec((1, d), lambda i: (0, 0)),
            pl.BlockSpec((None, LANES, d), lambda i: (layer, if_row // LANES, 0)),
            pl.BlockSpec((None, SECTION, d), lambda i: (layer, SEC_U, 0), pipeline_mode=pl.Buffered(1)),
        ],
        out_specs=[
            pl.BlockSpec((tm, d), lambda i: (i, 0)),
            pl.BlockSpec((tm, LANES), lambda i: (i, 0)),
            pl.BlockSpec((None, n_if, tm), lambda i: (i // tiles_per_seq, 0, i % tiles_per_seq)),
            pl.BlockSpec((tm, SECTION), lambda i: (i, 0)),
        ],
        out_shape=[
            jax.ShapeDtypeStruct((t, d), BF16),
            jax.ShapeDtypeStruct((t, LANES), F32),
            jax.ShapeDtypeStruct((batch, n_if, seq), F32),
            jax.ShapeDtypeStruct((t, SECTION), BF16),
        ],
        scratch_shapes=[pltpu.VMEM((SECTION, d), BF16)],
        compiler_params=_params("arbitrary"),
        name="prologue",
    )(x, g1, w_t, w_t)


VEC_LN_G, VEC_LN_B, VEC_CONV_B, VEC_CONV_W, VEC_GATE_B = 0, 1, 2, 3, 7
N_VEC_ROWS = 16


def _inproj_kernel(xn_hbm, w_ref, vec_ref, *rest, tiles_per_seq, n_cast_steps, n_cast):
    cast_in, (out_ref, vt_out, *cast_out), (wbf_ref, tail_ref, xbuf_ref, xsem) = (
        rest[:n_cast], rest[n_cast:2 * n_cast + 2], rest[2 * n_cast + 2:])
    j = pl.program_id(0) + SEC_V
    i = pl.program_id(1)
    n_i = pl.num_programs(1)
    n_slots, tm, _ = xbuf_ref.shape
    vec = lambda r, n=1: vec_ref[pl.ds(r, n), :]

    step = pl.program_id(0) * n_i + i
    ahead = n_slots - 1

    def tile_copy(s):
        rows = pl.ds(pl.multiple_of((s % n_i) * tm, tm), tm)
        return pltpu.make_async_copy(xn_hbm.at[rows, :], xbuf_ref.at[s % n_slots], xsem.at[s % n_slots])

    @pl.when(step == 0)
    def _():
        for s in range(ahead):
            tile_copy(jnp.int32(s)).start()

    @pl.when(step + ahead < pl.num_programs(0) * n_i)
    def _():
        tile_copy(step + ahead).start()

    tile_copy(step).wait()
    xn_ref = xbuf_ref.at[step % n_slots]

    @pl.when(pl.program_id(0) * pl.num_programs(1) + i < n_cast_steps)
    def _():
        for src, dst in zip(cast_in, cast_out):
            dst[...] = src[...].astype(BF16)

    @pl.when(i == 0)
    def _():
        _cast_section(w_ref, wbf_ref)

    def proj():
        return _nt_dot(xn_ref[...], wbf_ref[...])

    @pl.when(j == SEC_V)
    def _():
        v = jax.nn.gelu(proj())
        mu = jnp.mean(v, axis=-1, keepdims=True)
        var = jnp.mean(jnp.square(v - mu), axis=-1, keepdims=True)
        out_ref[...] = ((v - mu) * lax.rsqrt(var + EPS) * vec(VEC_LN_G) + vec(VEC_LN_B)).astype(BF16)

    @pl.when(j == SEC_M)
    def _():
        @pl.when(i % tiles_per_seq == 0)
        def _():
            tail_ref[...] = jnp.zeros(tail_ref.shape, F32)

        x = proj()
        xe = jnp.concatenate([tail_ref[...], x], axis=0)
        conv = vec(VEC_CONV_B) + vec(VEC_CONV_W + CONV_K - 1) * x
        for d in range(1, CONV_K):
            conv = conv + vec(VEC_CONV_W + CONV_K - 1 - d) * xe[SUBLANES - d:SUBLANES - d + tm, :]
        tail_ref[...] = x[tm - SUBLANES:, :]
        out_ref[...] = (conv * _sigmoid(conv)).astype(BF16)

    @pl.when(j == SEC_MV)
    def _():
        vt = _nt_dot(wbf_ref[...], xn_ref[...]).astype(BF16)
        for c in range(vt_out.shape[0]):
            vt_out[c] = vt[:, c * CHUNK:(c + 1) * CHUNK]

    @pl.when(j >= SEC_O)
    def _():
        out_ref[...] = _sigmoid(proj() + vec(VEC_GATE_B + j - SEC_O)).astype(BF16)


def _inproj(xn, w_t, layer, gate_row, vecs, cast_ws, batch, tm):
    t, d = xn.shape
    n_i = t // tm
    seq = t // batch
    tiles_per_seq = seq // tm
    assert tiles_per_seq * tm == seq
    n_sec = N_SECTIONS - SEC_V
    n_cast_steps = n_sec * n_i
    cast_rows = [_cast_rows(w.shape[1], n_cast_steps) for w in cast_ws]
    cast_step = lambda s, i: s * n_i + i

    def proj_block(s, i):
        j = s + SEC_V
        plane = jnp.where(j == SEC_V, PROJ_V, jnp.where(
            j >= SEC_GA, j - SEC_GA + PROJ_GA, jnp.where(j >= SEC_O, PROJ_O, PROJ_M)))
        return (plane, jnp.where(j == SEC_MV, n_i - 1, i), 0)

    def vt_block(s, i):
        return (jnp.clip((s + SEC_V - SEC_MV) * n_i + i, 0, n_i - 1), 0, 0)

    def w_rows(s, i):
        j = s + SEC_V
        return (layer, SUBLANES * (j * (SECTION // SUBLANES) + jnp.where(j >= SEC_GA, gate_skip // SUBLANES, 0)), 0)

    gate_skip = gate_row - SEC_GA * SECTION
    assert gate_skip % SUBLANES == 0
    assert (PROJ_GB, PROJ_O) == (PROJ_GA + SEC_GB - SEC_GA, PROJ_M + 1) and SEC_M + 1 == SEC_MV == SEC_O - 1
    kern = functools.partial(_inproj_kernel, tiles_per_seq=tiles_per_seq, n_cast_steps=n_cast_steps,
                             n_cast=len(cast_ws))
    return pl.pallas_call(
        kern,
        grid=(n_sec, n_i),
        in_specs=[
            pl.BlockSpec(memory_space=pl.ANY),
            pl.BlockSpec((None, pl.Element(SECTION), pl.Element(d)), w_rows),
            pl.BlockSpec((N_VEC_ROWS, SECTION), lambda s, i: (0, 0)),
        ] + [pl.BlockSpec((None, r, w.shape[2]), lambda s, i: (layer, cast_step(s, i), 0))
             for r, w in zip(cast_rows, cast_ws)],
        out_specs=[pl.BlockSpec((None, tm, SECTION), proj_block), pl.BlockSpec((tm // CHUNK, SECTION, CHUNK), vt_block)]
        + [pl.BlockSpec((r, w.shape[2]), lambda s, i: (cast_step(s, i), 0)) for r, w in zip(cast_rows, cast_ws)],
        out_shape=[jax.ShapeDtypeStruct((N_PROJ, t, SECTION), BF16),
                   jax.ShapeDtypeStruct((t // CHUNK, SECTION, CHUNK), BF16)]
        + [jax.ShapeDtypeStruct(w.shape[1:], BF16) for w in cast_ws],
        scratch_shapes=[
            pltpu.VMEM((SECTION, d), BF16),
            pltpu.VMEM((SUBLANES, SECTION), F32),
            pltpu.VMEM((XN_SLOTS, tm, d), BF16),
            pltpu.SemaphoreType.DMA((XN_SLOTS,)),
        ],
        compiler_params=_params("arbitrary", "arbitrary"),
        name="inproj",
    )(xn, w_t, vecs, *cast_ws)


def _lanes(col):
    return jnp.broadcast_to(col, (col.shape[0], LANES))


def _wide(x, n):
    return jnp.concatenate([x] * n, axis=1)


def _mlstm_chunk(c_ref, o_ref, vt_ref, gcol_ref, grow_ref, bcol_ref, brow_ref, ng_ref,
                 st_ref, mx_ref, wqb_ref, wkb_ref, yb_ref, *, head_dim, batch):
    row = lax.broadcasted_iota(jnp.int32, (CHUNK, CHUNK), 0)
    col = lax.broadcasted_iota(jnp.int32, (CHUNK, CHUNK), 1)
    causal = col <= row
    tril = causal.astype(BF16)
    triu = (row <= col).astype(BF16)
    nd = head_dim // LANES
    groups = CHUNK // SUBLANES
    heads = [(b, h) for b in range(batch) for h in range(ML_HEADS)]
    slot = lambda b, h: b * ML_HEADS + h
    hsl = lambda h: slice(h * head_dim, (h + 1) * head_dim)


    n_if = grow_ref.shape[1]
    lane = lax.broadcasted_iota(jnp.int32, (CHUNK, LANES), 1)
    g_col = gcol_ref[0] + brow_ref[...]
    for b in range(1, batch):
        g_col = jnp.where(lane < b * n_if, g_col, pltpu.roll(gcol_ref[b] + brow_ref[...], b * n_if, 1))
    g_row = jnp.concatenate([grow_ref[b] + bcol_ref[...] for b in range(batch)], axis=0)
    lf_row = _log_sigmoid(g_row)
    bcum_col = _dot(jnp.concatenate([tril] * 3, axis=1), jnp.concatenate(_split3(_log_sigmoid(g_col)), axis=0))
    bcum_row = _dot(jnp.concatenate(_split3(lf_row), axis=1), jnp.concatenate([triu] * 3, axis=0))
    total = jnp.sum(lf_row, axis=-1, keepdims=True)

    q, k = {}, {}
    for h in range(ML_HEADS):
        c = jnp.concatenate([c_ref[b, :, hsl(h)] for b in range(batch)], axis=0)
        q_h, k_h = _dot(c, wqb_ref[h]), _dot(c, wkb_ref[h])
        for b in range(batch):
            q[b, h], k[b, h] = q_h[b * CHUNK:(b + 1) * CHUNK], k_h[b * CHUNK:(b + 1) * CHUNK]

    w_intra, w_inter, inv_floor, w_state, decay, scale, m_new = {}, {}, {}, {}, {}, {}, {}
    for b, h in heads:
        i, f = b * n_if + h, b * n_if + ML_HEADS + h
        bc = _lanes(bcum_col[:, f:f + 1])
        ig = _lanes(g_col[:, i:i + 1])
        src = g_row[i:i + 1, :] - bcum_row[f:f + 1, :]
        b_last = jnp.broadcast_to(total[f:f + 1, :], (1, LANES))
        m_prev = mx_ref[slot(b, h):slot(b, h) + 1, :]
        inter = bc + m_prev
        dlog = jnp.where(causal, bc + src, -jnp.inf)
        m_t = jnp.maximum(inter, _lanes(jnp.max(dlog, axis=-1, keepdims=True)))
        w_intra[b, h] = jnp.exp(dlog - m_t)
        w_inter[b, h] = jnp.exp(inter - m_t)
        inv_floor[b, h] = jnp.exp(-m_t)
        a = b_last - bc + ig
        a_max = jnp.max(a, axis=0, keepdims=True)
        w_state[b, h] = jnp.exp(a - a_max)
        m_new[b, h] = jnp.maximum(b_last + m_prev, a_max)
        decay[b, h] = jnp.exp(b_last + m_prev - m_new[b, h])
        scale[b, h] = jnp.exp(a_max - m_new[b, h])

    qk = {}
    for b, h in heads:
        qk[b, h] = _nt_dot(q[b, h].astype(BF16), k[b, h].astype(BF16)) * w_intra[b, h]

    for b, h in heads:
        s = slot(b, h)
        qw = q[b, h] * _wide(w_inter[b, h], nd)
        lhs = jnp.concatenate([qw.astype(BF16), qk[b, h].astype(BF16)], axis=1)
        rhs = jnp.concatenate([st_ref[s, :head_dim, :].astype(BF16), vt_ref[b, hsl(h), :]], axis=1)
        num = _nt_dot(lhs, rhs)
        qn = (qw.reshape(groups, SUBLANES, head_dim) * st_ref[s, head_dim:, :]).reshape(CHUNK, head_dim)
        part = qk[b, h] + sum(qn[:, i * LANES:(i + 1) * LANES] for i in range(nd))
        den = _lanes(jnp.sum(part, axis=-1, keepdims=True))
        hid = num * _wide(1.0 / jnp.maximum(jnp.abs(den), inv_floor[b, h]), nd)
        mu = jnp.mean(hid, axis=-1, keepdims=True)
        var = jnp.mean(jnp.square(hid - mu), axis=-1, keepdims=True)
        hid = (hid - mu) * lax.rsqrt(var + EPS) * ng_ref[:, hsl(h)]
        yb_ref[b, :, hsl(h)] = (o_ref[b, :, hsl(h)].astype(F32) * hid).astype(BF16)

    for b, h in heads:
        s = slot(b, h)
        kw = k[b, h] * _wide(w_state[b, h] * scale[b, h], nd)
        upd = _dot(vt_ref[b, hsl(h), :], kw.astype(BF16))
        kw_sum = jnp.sum(jnp.sum(kw.reshape(groups, SUBLANES, head_dim), axis=0), axis=0, keepdims=True)
        keep = _wide(decay[b, h], nd)
        st_ref[s, :head_dim, :] = keep * st_ref[s, :head_dim, :] + upd
        st_ref[s, head_dim:, :] = keep * st_ref[s, head_dim:, :] + kw_sum
        mx_ref[s:s + 1, :] = m_new[b, h]


def _mix_kernel(co_ref, vt_ref, gcol_ref, grow_ref, bcol_ref, brow_ref, wq_ref, wk_ref, ng_ref,
                x_ref, u_ref, v_ref, ws_ref, bs_ref, gate_ref, wa_ref, wb_ref, wo_ref, g2_ref, cast_ref,
                x1_ref, hn_ref, cast_out, st_ref, mx_ref, wqb_ref, wkb_ref, yb_ref, wsb_ref,
                *, head_dim, batch, n_steps):
    n = pl.program_id(0)

    @pl.when(n == 0)
    def _():
        row = lax.broadcasted_iota(jnp.int32, (CHUNK, CHUNK), 0)
        col = lax.broadcasted_iota(jnp.int32, (CHUNK, CHUNK), 1)
        wsb_ref[...] = jnp.where(col <= row, ws_ref[...], 0.0).astype(BF16)
        wqb_ref[...] = wq_ref[...].astype(BF16)
        wkb_ref[...] = (wk_ref[...] * (head_dim ** -0.5)).astype(BF16)
        st_ref[...] = jnp.zeros(st_ref.shape, F32)
        mx_ref[...] = jnp.zeros(mx_ref.shape, F32)

    def mlstm(slot):
        cast_out[...] = cast_ref[...].astype(BF16)
        _mlstm_chunk(co_ref.at[0], co_ref.at[1], vt_ref, gcol_ref, grow_ref, bcol_ref, brow_ref, ng_ref,
                     st_ref, mx_ref, wqb_ref, wkb_ref, yb_ref.at[slot], head_dim=head_dim, batch=batch)

    def spatial_gate():
        gated = [[] for _ in range(batch)]
        for g in range(wsb_ref.shape[0]):
            gs = slice(g * GM_GROUP_DIM, (g + 1) * GM_GROUP_DIM)
            vg = jnp.concatenate([v_ref[b, :, gs] for b in range(batch)], axis=1)
            s = _dot(wsb_ref[g], vg) + bs_ref[:, g:g + 1]
            for b in range(batch):
                sb = s[:, b * GM_GROUP_DIM:(b + 1) * GM_GROUP_DIM]
                gated[b].append((u_ref[b, :, gs].astype(F32) * sb).astype(BF16))
        return jnp.concatenate([jnp.concatenate(cols, axis=1) for cols in gated], axis=0)

    def merge(slot):
        flat = lambda v: v.reshape(batch * CHUNK, v.shape[-1])
        gate_a = jnp.concatenate([flat(gate_ref[0]), flat(gate_ref[1])], axis=1).astype(F32)
        gate_b = jnp.concatenate([flat(gate_ref[2]), flat(gate_ref[3])], axis=1).astype(F32)
        mixed = (gate_a * _dot(spatial_gate(), wa_ref[...])
                 + gate_b * _dot(flat(yb_ref[slot]), wb_ref[...]))
        x1 = flat(x_ref[...]) + _dot(mixed.astype(BF16), wo_ref[...])
        x1_ref[...] = x1.reshape(x1_ref.shape)
        hn_ref[...] = _rms_norm(x1, g2_ref[...]).astype(BF16).reshape(hn_ref.shape)

    @pl.when(n == 0)
    def _():
        mlstm(0)

    @pl.when((n > 0) & (n < n_steps))
    def _():
        mlstm(n % 2)
        merge(1 - n % 2)

    @pl.when(n == n_steps)
    def _():
        merge((n_steps - 1) % 2)


def _mix(proj, u, vt, gates_col, gates_row, bias_col, bias_row, wq, wk, norm_g,
         x, ws, bs_t, w_a, w_b, w_out, g2, cast_w, layer, batch):
    _, t, width = proj.shape
    n_groups = width // GM_GROUP_DIM
    assert ws.shape == (n_groups, CHUNK, CHUNK)
    seq = t // batch
    d = x.shape[-1]
    head_dim = width // ML_HEADS
    n_if = gates_row.shape[1]
    n_steps = seq // CHUNK
    n_gate = 2 * (PROJ_GB - PROJ_GA)
    assert PROJ_O == PROJ_M + 1 and PROJ_M % 2 == 0 and PROJ_GA % n_gate == 0
    assert batch * n_if <= LANES, (batch, n_if)
    proj4 = proj.reshape(proj.shape[0], batch, seq, width)
    cast_rows = _cast_rows(cast_w.shape[1], n_steps)
    at = lambda n: jnp.minimum(n, n_steps - 1)
    done = lambda n: jnp.maximum(n - 1, 0)
    const2 = lambda n: (0, 0)
    const3 = lambda n: (0, 0, 0)
    resident = dict(pipeline_mode=pl.Buffered(1))
    n_state = batch * ML_HEADS
    kern = functools.partial(_mix_kernel, head_dim=head_dim, batch=batch, n_steps=n_steps)
    x1, hn, cast = pl.pallas_call(
        kern,
        grid=(n_steps + 1,),
        in_specs=[
            pl.BlockSpec((2, batch, CHUNK, width), lambda n: (PROJ_M // 2, 0, at(n), 0)),
            pl.BlockSpec((batch, None, width, CHUNK), lambda n: (0, at(n), 0, 0)),
            pl.BlockSpec((batch, CHUNK, LANES), lambda n: (0, at(n), 0)),
            pl.BlockSpec((batch, n_if, CHUNK), lambda n: (0, 0, at(n))),
            pl.BlockSpec((n_if, 1), const2),
            pl.BlockSpec((1, LANES), const2),
            pl.BlockSpec((ML_HEADS, head_dim, head_dim), const3),
            pl.BlockSpec((ML_HEADS, head_dim, head_dim), const3),
            pl.BlockSpec((1, width), const2),
            pl.BlockSpec((batch, CHUNK, d), lambda n: (0, done(n), 0)),
            pl.BlockSpec((batch, CHUNK, width), lambda n: (0, done(n), 0)),
            pl.BlockSpec((None, batch, CHUNK, width), lambda n: (PROJ_V, 0, done(n), 0)),
            pl.BlockSpec((n_groups, CHUNK, CHUNK), const3),
            pl.BlockSpec((CHUNK, n_groups), const2),
            pl.BlockSpec((n_gate, batch, CHUNK, SECTION), lambda n: (PROJ_GA // n_gate, 0, done(n), 0)),
            pl.BlockSpec((width, d), const2, **resident),
            pl.BlockSpec((width, d), const2, **resident),
            pl.BlockSpec((d, d), const2, **resident),
            pl.BlockSpec((1, d), const2),
            pl.BlockSpec((None, cast_rows, cast_w.shape[2]), lambda n: (layer, at(n), 0)),
        ],
        out_specs=[
            pl.BlockSpec((batch, CHUNK, d), lambda n: (0, done(n), 0)),
            pl.BlockSpec((batch, CHUNK, d), lambda n: (0, done(n), 0)),
            pl.BlockSpec((cast_rows, cast_w.shape[2]), lambda n: (at(n), 0)),
        ],
        out_shape=[
            jax.ShapeDtypeStruct((batch, seq, d), F32),
            jax.ShapeDtypeStruct((batch, seq, d), BF16),
            jax.ShapeDtypeStruct(cast_w.shape[1:], BF16),
        ],
        scratch_shapes=[
            pltpu.VMEM((n_state, head_dim + SUBLANES, head_dim), F32),
            pltpu.VMEM((n_state, LANES), F32),
            pltpu.VMEM((ML_HEADS, head_dim, head_dim), BF16),
            pltpu.VMEM((ML_HEADS, head_dim, head_dim), BF16),
            pltpu.VMEM((2, batch, CHUNK, width), BF16),
            pltpu.VMEM((n_groups, CHUNK, CHUNK), BF16),
        ],
        compiler_params=_params("arbitrary"),
        name="mix",
    )(proj4, vt.reshape(batch, n_steps, width, CHUNK), gates_col.reshape(batch, seq, LANES), gates_row, bias_col, bias_row, wq, wk, norm_g,
      x, u.reshape(batch, seq, width), proj4, ws, bs_t, proj4, w_a, w_b, w_out, g2, cast_w)
    return x1.reshape(t, d), hn.reshape(t, d), cast


def _ffn_kernel(hn_ref, x1_ref, w1_ref, w2_ref, gf_ref, out_ref, *, final_norm):
    j = pl.program_id(1)

    @pl.when(j == 0)
    def _():
        out_ref[...] = x1_ref[...]

    h = jnp.square(jnp.maximum(_dot(hn_ref[...], w1_ref[...]), 0.0)).astype(BF16)
    out_ref[...] += _dot(h, w2_ref[...])

    if final_norm:
        @pl.when(j == pl.num_programs(1) - 1)
        def _():
            out_ref[...] = _rms_norm(out_ref[...], gf_ref[...])


def _ffn(hn, x1, w1, w2, gf, tm, tf, final_norm):
    t, d = x1.shape
    d_ff = w1.shape[1]
    return pl.pallas_call(
        functools.partial(_ffn_kernel, final_norm=final_norm),
        grid=(t // tm, d_ff // tf),
        in_specs=[
            pl.BlockSpec((tm, d), lambda i, j: (i, 0)),
            pl.BlockSpec((tm, d), lambda i, j: (i, 0)),
            pl.BlockSpec((d, tf), lambda i, j: (0, j)),
            pl.BlockSpec((tf, d), lambda i, j: (j, 0)),
            pl.BlockSpec((1, d), lambda i, j: (0, 0)),
        ],
        out_specs=pl.BlockSpec((tm, d), lambda i, j: (i, 0)),
        out_shape=jax.ShapeDtypeStruct((t, d), F32),
        compiler_params=_params("arbitrary", "arbitrary"),
        name="ffn",
    )(hn, x1, w1, w2, gf)


def _tiles(t):
    return dict(prologue=min(t, 1024), inproj=min(t, 1024), ffn_m=min(t, 512), ffn_f=2048)


def kernel(x, norm1_g, w_in, b_gate, gm_ln_g, gm_ln_b, gm_ws, gm_bs, ml_conv_w, ml_conv_b, ml_wq, ml_wk,
           ml_ig_b, ml_fg_b, ml_norm_g, w_a, w_b, w_out, norm2_g, w_ff1, w_ff2, norm_f_g):
    batch, seq, d = x.shape
    depth = w_in.shape[0]
    t = batch * seq
    tiles = _tiles(t)
    if_row = SEC_GA * SECTION
    n_if = 2 * ML_HEADS
    xt = x.reshape(t, d)
    w_t = jnp.swapaxes(w_in, 1, 2)
    for l in range(depth):
        gate_bias = jnp.concatenate([ml_ig_b[l], ml_fg_b[l]])
        vecs = jnp.concatenate([
            gm_ln_g[l][None], gm_ln_b[l][None], ml_conv_b[l][None], ml_conv_w[l],
            jnp.zeros((1, SECTION), F32), b_gate[l].reshape(4, SECTION),
            jnp.zeros((N_VEC_ROWS - VEC_GATE_B - 5, SECTION), F32)])
        xn, gates_col, gates_row, u = _prologue(xt, norm1_g[l][None], w_t, l, if_row, n_if, batch,
                                                tiles["prologue"])
        proj, vt, w2_bf, wa_bf, wb_bf, wo_bf = _inproj(
            xn, w_t, l, if_row + n_if, vecs, (w_ff2, w_a, w_b, w_out), batch, tiles["inproj"])
        x1, hn, w1_bf = _mix(
            proj, u, vt, gates_col, gates_row, gate_bias[:, None], jnp.pad(gate_bias, (0, LANES - n_if))[None],
            ml_wq[l], ml_wk[l], ml_norm_g[l][None], xt.reshape(batch, seq, d), gm_ws[l], gm_bs[l].T,
            wa_bf, wb_bf, wo_bf, norm2_g[l][None], w_ff1, l, batch)
        xt = _ffn(hn, x1, w1_bf, w2_bf, norm_f_g[None], tiles["ffn_m"], tiles["ffn_f"],
                  final_norm=l == depth - 1)
    return xt.reshape(batch, seq, d)
```

```python
import functools

import jax
import jax.numpy as jnp
from jax import lax
from jax.experimental import pallas as pl
from jax.experimental.pallas import tpu as pltpu

EPS = 1e-6
GM_GROUP_DIM = 128
CHUNK = 128
ML_HEADS = 4
CONV_K = 4
LANES = 128
SUBLANES = 8
BF16_ROWS = 16
XN_SLOTS = 3
SECTION = 1024
SEC_U, SEC_V, SEC_M, SEC_MV, SEC_O, SEC_GA, SEC_GB = 0, 1, 2, 3, 4, 5, 7
N_SECTIONS = 9
PROJ_GA, PROJ_GB, PROJ_M, PROJ_O, PROJ_V = 0, 2, 4, 5, 6
N_PROJ = 7
VMEM_LIMIT = 56 * 1024 * 1024

BF16 = jnp.bfloat16
F32 = jnp.float32


def _sigmoid(x):
    return 0.5 * jnp.tanh(0.5 * x) + 0.5


def _log_sigmoid(x):
    return jnp.minimum(x, 0.0) - jnp.log(1.0 + jnp.exp(-jnp.abs(x)))


def _rms_norm(x, g):
    return x * lax.rsqrt(jnp.mean(x * x, axis=-1, keepdims=True) + EPS) * g


def _dot(a, b):
    return jnp.dot(a, b, preferred_element_type=F32)


def _split3(x):
    hi = x.astype(BF16)
    rest = x - hi.astype(F32)
    mid = rest.astype(BF16)
    return hi, mid, (rest - mid.astype(F32)).astype(BF16)


def _params(*semantics):
    return pltpu.CompilerParams(dimension_semantics=semantics, vmem_limit_bytes=VMEM_LIMIT)


def _cast_rows(n_rows, n_steps):
    rows = n_rows // n_steps
    assert rows * n_steps == n_rows and rows % BF16_ROWS == 0, (n_rows, n_steps)
    return rows


def _nt_dot(a, b):
    return lax.dot_general(a, b, (((1,), (1,)), ((), ())), preferred_element_type=F32)


def _cast_section(w_ref, wbf_ref):
    rows = 256
    for r in range(0, wbf_ref.shape[0], rows):
        wbf_ref[r:r + rows, :] = w_ref[r:r + rows, :].astype(BF16)


def _prologue_kernel(x_ref, g1_ref, wif_ref, wu_ref, xn_ref, gcol_ref, grow_ref, u_ref, wub_ref):
    @pl.when(pl.program_id(0) == 0)
    def _():
        _cast_section(wu_ref, wub_ref)

    xn = _rms_norm(x_ref[...], g1_ref[...]).astype(BF16)
    xn_ref[...] = xn
    wif = wif_ref[...].astype(BF16)
    gcol = _nt_dot(xn, wif)
    gcol_ref[...] = gcol
    grow_ref[...] = jnp.concatenate(
        [gcol[r:r + LANES, :].T[:grow_ref.shape[0], :] for r in range(0, gcol.shape[0], LANES)], axis=1)
    u_ref[...] = jax.nn.gelu(_nt_dot(xn, wub_ref[...])).astype(BF16)


def _prologue(x, g1, w_t, layer, if_row, n_if, batch, tm):
    t, d = x.shape
    seq = t // batch
    tiles_per_seq = seq // tm
    assert tiles_per_seq * tm == seq
    return pl.pallas_call(
        _prologue_kernel,
        grid=(t // tm,),
        in_specs=[
            pl.BlockSpec((tm, d), lambda i: (i, 0)),
            pl.BlockSpec((1, d), lambda i: (0, 0)),
            pl.BlockSpec((None, LANES, d), lambda i: (layer, if_row // LANES, 0)),
            pl.BlockSpec((None, SECTION, d), lambda i: (layer, SEC_U, 0), pipeline_mode=pl.Buffered(1)),
        ],
        out_specs=[
            pl.BlockSpec((tm, d), lambda i: (i, 0)),
            pl.BlockSpec((tm, LANES), lambda i: (i, 0)),
            pl.BlockSpec((None, n_if, tm), lambda i: (i // tiles_per_seq, 0, i % tiles_per_seq)),
            pl.BlockSpec((tm, SECTION), lambda i: (i, 0)),
        ],
        out_shape=[
            jax.ShapeDtypeStruct((t, d), BF16),
            jax.ShapeDtypeStruct((t, LANES), F32),
            jax.ShapeDtypeStruct((batch, n_if, seq), F32),
            jax.ShapeDtypeStruct((t, SECTION), BF16),
        ],
        scratch_shapes=[pltpu.VMEM((SECTION, d), BF16)],
        compiler_params=_params("arbitrary"),
        name="prologue",
    )(x, g1, w_t, w_t)


VEC_LN_G, VEC_LN_B, VEC_CONV_B, VEC_CONV_W, VEC_GATE_B = 0, 1, 2, 3, 7
N_VEC_ROWS = 16


def _inproj_kernel(xn_hbm, w_ref, vec_ref, *rest, tiles_per_seq, n_cast_steps, n_cast):
    cast_in, (out_ref, vt_out, *cast_out), (wbf_ref, tail_ref, xbuf_ref, xsem) = (
        rest[:n_cast], rest[n_cast:2 * n_cast + 2], rest[2 * n_cast + 2:])
    j = pl.program_id(0) + SEC_V
    i = pl.program_id(1)
    n_i = pl.num_programs(1)
    n_slots, tm, _ = xbuf_ref.shape
    vec = lambda r, n=1: vec_ref[pl.ds(r, n), :]

    step = pl.program_id(0) * n_i + i
    ahead = n_slots - 1

    def tile_copy(s):
        rows = pl.ds(pl.multiple_of((s % n_i) * tm, tm), tm)
        return pltpu.make_async_copy(xn_hbm.at[rows, :], xbuf_ref.at[s % n_slots], xsem.at[s % n_slots])

    @pl.when(step == 0)
    def _():
        for s in range(ahead):
            tile_copy(jnp.int32(s)).start()

    @pl.when(step + ahead < pl.num_programs(0) * n_i)
    def _():
        tile_copy(step + ahead).start()

    tile_copy(step).wait()
    xn_ref = xbuf_ref.at[step % n_slots]

    @pl.when(pl.program_id(0) * pl.num_programs(1) + i < n_cast_steps)
    def _():
        for src, dst in zip(cast_in, cast_out):
            dst[...] = src[...].astype(BF16)

    @pl.when(i == 0)
    def _():
        _cast_section(w_ref, wbf_ref)

    def proj():
        return _nt_dot(xn_ref[...], wbf_ref[...])

    @pl.when(j == SEC_V)
    def _():
        v = jax.nn.gelu(proj())
        mu = jnp.mean(v, axis=-1, keepdims=True)
        var = jnp.mean(jnp.square(v - mu), axis=-1, keepdims=True)
        out_ref[...] = ((v - mu) * lax.rsqrt(var + EPS) * vec(VEC_LN_G) + vec(VEC_LN_B)).astype(BF16)

    @pl.when(j == SEC_M)
    def _():
        @pl.when(i % tiles_per_seq == 0)
        def _():
            tail_ref[...] = jnp.zeros(tail_ref.shape, F32)

        x = proj()
        xe = jnp.concatenate([tail_ref[...], x], axis=0)
        conv = vec(VEC_CONV_B) + vec(VEC_CONV_W + CONV_K - 1) * x
        for d in range(1, CONV_K):
            conv = conv + vec(VEC_CONV_W + CONV_K - 1 - d) * xe[SUBLANES - d:SUBLANES - d + tm, :]
        tail_ref[...] = x[tm - SUBLANES:, :]
        out_ref[...] = (conv * _sigmoid(conv)).astype(BF16)

    @pl.when(j == SEC_MV)
    def _():
        vt = _nt_dot(wbf_ref[...], xn_ref[...]).astype(BF16)
        for c in range(vt_out.shape[0]):
            vt_out[c] = vt[:, c * CHUNK:(c + 1) * CHUNK]

    @pl.when(j >= SEC_O)
    def _():
        out_ref[...] = _sigmoid(proj() + vec(VEC_GATE_B + j - SEC_O)).astype(BF16)


def _inproj(xn, w_t, layer, gate_row, vecs, cast_ws, batch, tm):
    t, d = xn.shape
    n_i = t // tm
    seq = t // batch
    tiles_per_seq = seq // tm
    assert tiles_per_seq * tm == seq
    n_sec = N_SECTIONS - SEC_V
    n_cast_steps = n_sec * n_i
    cast_rows = [_cast_rows(w.shape[1], n_cast_steps) for w in cast_ws]
    cast_step = lambda s, i: s * n_i + i

    def proj_block(s, i):
        j = s + SEC_V
        plane = jnp.where(j == SEC_V, PROJ_V, jnp.where(
            j >= SEC_GA, j - SEC_GA + PROJ_GA, jnp.where(j >= SEC_O, PROJ_O, PROJ_M)))
        return (plane, jnp.where(j == SEC_MV, n_i - 1, i), 0)

    def vt_block(s, i):
        return (jnp.clip((s + SEC_V - SEC_MV) * n_i + i, 0, n_i - 1), 0, 0)

    def w_rows(s, i):
        j = s + SEC_V
        return (layer, SUBLANES * (j * (SECTION // SUBLANES) + jnp.where(j >= SEC_GA, gate_skip // SUBLANES, 0)), 0)

    gate_skip = gate_row - SEC_GA * SECTION
    assert gate_skip % SUBLANES == 0
    assert (PROJ_GB, PROJ_O) == (PROJ_GA + SEC_GB - SEC_GA, PROJ_M + 1) and SEC_M + 1 == SEC_MV == SEC_O - 1
    kern = functools.partial(_inproj_kernel, tiles_per_seq=tiles_per_seq, n_cast_steps=n_cast_steps,
                             n_cast=len(cast_ws))
    return pl.pallas_call(
        kern,
        grid=(n_sec, n_i),
        in_specs=[
            pl.BlockSpec(memory_space=pl.ANY),
            pl.BlockSpec((None, pl.Element(SECTION), pl.Element(d)), w_rows),
            pl.BlockSpec((N_VEC_ROWS, SECTION), lambda s, i: (0, 0)),
        ] + [pl.BlockSpec((None, r, w.shape[2]), lambda s, i: (layer, cast_step(s, i), 0))
             for r, w in zip(cast_rows, cast_ws)],
        out_specs=[pl.BlockSpec((None, tm, SECTION), proj_block), pl.BlockSpec((tm // CHUNK, SECTION, CHUNK), vt_block)]
        + [pl.BlockSpec((r, w.shape[2]), lambda s, i: (cast_step(s, i), 0)) for r, w in zip(cast_rows, cast_ws)],
        out_shape=[jax.ShapeDtypeStruct((N_PROJ, t, SECTION), BF16),
                   jax.ShapeDtypeStruct((t // CHUNK, SECTION, CHUNK), BF16)]
        + [jax.ShapeDtypeStruct(w.shape[1:], BF16) for w in cast_ws],
        scratch_shapes=[
            pltpu.VMEM((SECTION, d), BF16),
            pltpu.VMEM((SUBLANES, SECTION), F32),
            pltpu.VMEM((XN_SLOTS, tm, d), BF16),
            pltpu.SemaphoreType.DMA((XN_SLOTS,)),
        ],
        compiler_params=_params("arbitrary", "arbitrary"),
        name="inproj",
    )(xn, w_t, vecs, *cast_ws)


def _lanes(col):
    return jnp.broadcast_to(col, (col.shape[0], LANES))


def _wide(x, n):
    return jnp.concatenate([x] * n, axis=1)


def _mlstm_chunk(c_ref, o_ref, vt_ref, gcol_ref, grow_ref, bcol_ref, brow_ref, ng_ref,
                 st_ref, mx_ref, wqb_ref, wkb_ref, yb_ref, *, head_dim, batch):
    row = lax.broadcasted_iota(jnp.int32, (CHUNK, CHUNK), 0)
    col = lax.broadcasted_iota(jnp.int32, (CHUNK, CHUNK), 1)
    causal = col <= row
    tril = causal.astype(BF16)
    triu = (row <= col).astype(BF16)
    nd = head_dim // LANES
    groups = CHUNK // SUBLANES
    heads = [(b, h) for b in range(batch) for h in range(ML_HEADS)]
    slot = lambda b, h: b * ML_HEADS + h
    hsl = lambda h: slice(h * head_dim, (h + 1) * head_dim)


    n_if = grow_ref.shape[1]
    lane = lax.broadcasted_iota(jnp.int32, (CHUNK, LANES), 1)
    g_col = gcol_ref[0] + brow_ref[...]
    for b in range(1, batch):
        g_col = jnp.where(lane < b * n_if, g_col, pltpu.roll(gcol_ref[b] + brow_ref[...], b * n_if, 1))
    g_row = jnp.concatenate([grow_ref[b] + bcol_ref[...] for b in range(batch)], axis=0)
    lf_row = _log_sigmoid(g_row)
    bcum_col = _dot(jnp.concatenate([tril] * 3, axis=1), jnp.concatenate(_split3(_log_sigmoid(g_col)), axis=0))
    bcum_row = _dot(jnp.concatenate(_split3(lf_row), axis=1), jnp.concatenate([triu] * 3, axis=0))
    total = jnp.sum(lf_row, axis=-1, keepdims=True)

    q, k = {}, {}
    for h in range(ML_HEADS):
        c = jnp.concatenate([c_ref[b, :, hsl(h)] for b in range(batch)], axis=0)
        q_h, k_h = _dot(c, wqb_ref[h]), _dot(c, wkb_ref[h])
        for b in range(batch):
            q[b, h], k[b, h] = q_h[b * CHUNK:(b + 1) * CHUNK], k_h[b * CHUNK:(b + 1) * CHUNK]

    w_intra, w_inter, inv_floor, w_state, decay, scale, m_new = {}, {}, {}, {}, {}, {}, {}
    for b, h in heads:
        i, f = b * n_if + h, b * n_if + ML_HEADS + h
        bc = _lanes(bcum_col[:, f:f + 1])
        ig = _lanes(g_col[:, i:i + 1])
        src = g_row[i:i + 1, :] - bcum_row[f:f + 1, :]
        b_last = jnp.broadcast_to(total[f:f + 1, :], (1, LANES))
        m_prev = mx_ref[slot(b, h):slot(b, h) + 1, :]
        inter = bc + m_prev
        dlog = jnp.where(causal, bc + src, -jnp.inf)
        m_t = jnp.maximum(inter, _lanes(jnp.max(dlog, axis=-1, keepdims=True)))
        w_intra[b, h] = jnp.exp(dlog - m_t)
        w_inter[b, h] = jnp.exp(inter - m_t)
        inv_floor[b, h] = jnp.exp(-m_t)
        a = b_last - bc + ig
        a_max = jnp.max(a, axis=0, keepdims=True)
        w_state[b, h] = jnp.exp(a - a_max)
        m_new[b, h] = jnp.maximum(b_last + m_prev, a_max)
        decay[b, h] = jnp.exp(b_last + m_prev - m_new[b, h])
        scale[b, h] = jnp.exp(a_max - m_new[b, h])

    qk = {}
    for b, h in heads:
        qk[b, h] = _nt_dot(q[b, h].astype(BF16), k[b, h].astype(BF16)) * w_intra[b, h]

    for b, h in heads:
        s = slot(b, h)
        qw = q[b, h] * _wide(w_inter[b, h], nd)
        lhs = jnp.concatenate([qw.astype(BF16), qk[b, h].astype(BF16)], axis=1)
        rhs = jnp.concatenate([st_ref[s, :head_dim, :].astype(BF16), vt_ref[b, hsl(h), :]], axis=1)
        num = _nt_dot(lhs, rhs)
        qn = (qw.reshape(groups, SUBLANES, head_dim) * st_ref[s, head_dim:, :]).reshape(CHUNK, head_dim)
        part = qk[b, h] + sum(qn[:, i * LANES:(i + 1) * LANES] for i in range(nd))
        den = _lanes(jnp.sum(part, axis=-1, keepdims=True))
        hid = num * _wide(1.0 / jnp.maximum(jnp.abs(den), inv_floor[b, h]), nd)
        mu = jnp.mean(hid, axis=-1, keepdims=True)
        var = jnp.mean(jnp.square(hid - mu), axis=-1, keepdims=True)
        hid = (hid - mu) * lax.rsqrt(var + EPS) * ng_ref[:, hsl(h)]
        yb_ref[b, :, hsl(h)] = (o_ref[b, :, hsl(h)].astype(F32) * hid).astype(BF16)

    for b, h in heads:
        s = slot(b, h)
        kw = k[b, h] * _wide(w_state[b, h] * scale[b, h], nd)
        upd = _dot(vt_ref[b, hsl(h), :], kw.astype(BF16))
        kw_sum = jnp.sum(jnp.sum(kw.reshape(groups, SUBLANES, head_dim), axis=0), axis=0, keepdims=True)
        keep = _wide(decay[b, h], nd)
        st_ref[s, :head_dim, :] = keep * st_ref[s, :head_dim, :] + upd
        st_ref[s, head_dim:, :] = keep * st_ref[s, head_dim:, :] + kw_sum
        mx_ref[s:s + 1, :] = m_new[b, h]


def _mix_kernel(co_ref, vt_ref, gcol_ref, grow_ref, bcol_ref, brow_ref, wq_ref, wk_ref, ng_ref,
                x_hbm, u_ref, v_ref, ws_ref, bs_ref, gate_ref, wa_ref, wb_ref, wo_ref, g2_ref, cast_ref,
                x1_ref, hn_ref, cast_out, st_ref, mx_ref, wqb_ref, wkb_ref, yb_ref, wsb_ref, xbuf_ref, xsem,
                *, head_dim, batch, n_steps):
    n = pl.program_id(0)

    @pl.when(n == 0)
    def _():
        row = lax.broadcasted_iota(jnp.int32, (CHUNK, CHUNK), 0)
        col = lax.broadcasted_iota(jnp.int32, (CHUNK, CHUNK), 1)
        wsb_ref[...] = jnp.where(col <= row, ws_ref[...], 0.0).astype(BF16)
        wqb_ref[...] = wq_ref[...].astype(BF16)
        wkb_ref[...] = (wk_ref[...] * (head_dim ** -0.5)).astype(BF16)
        st_ref[...] = jnp.zeros(st_ref.shape, F32)
        mx_ref[...] = jnp.zeros(mx_ref.shape, F32)

    n_slots = xbuf_ref.shape[0]
    ahead = n_slots - 1

    def x_copy(s):
        rows = pl.ds(pl.multiple_of(jnp.maximum(s - 1, 0) * CHUNK, CHUNK), CHUNK)
        return pltpu.make_async_copy(x_hbm.at[:, rows, :], xbuf_ref.at[s % n_slots], xsem.at[s % n_slots])

    @pl.when(n == 0)
    def _():
        for s in range(ahead):
            x_copy(jnp.int32(s)).start()

    @pl.when(n + ahead < pl.num_programs(0))
    def _():
        x_copy(n + ahead).start()

    x_copy(n).wait()
    x_ref = xbuf_ref.at[n % n_slots]

    def mlstm(slot):
        cast_out[...] = cast_ref[...].astype(BF16)
        _mlstm_chunk(co_ref.at[0], co_ref.at[1], vt_ref, gcol_ref, grow_ref, bcol_ref, brow_ref, ng_ref,
                     st_ref, mx_ref, wqb_ref, wkb_ref, yb_ref.at[slot], head_dim=head_dim, batch=batch)

    def spatial_gate():
        gated = [[] for _ in range(batch)]
        for g in range(wsb_ref.shape[0]):
            gs = slice(g * GM_GROUP_DIM, (g + 1) * GM_GROUP_DIM)
            vg = jnp.concatenate([v_ref[b, :, gs] for b in range(batch)], axis=1)
            s = _dot(wsb_ref[g], vg) + bs_ref[:, g:g + 1]
            for b in range(batch):
                sb = s[:, b * GM_GROUP_DIM:(b + 1) * GM_GROUP_DIM]
                gated[b].append((u_ref[b, :, gs].astype(F32) * sb).astype(BF16))
        return jnp.concatenate([jnp.concatenate(cols, axis=1) for cols in gated], axis=0)

    def merge(slot):
        flat = lambda v: v.reshape(batch * CHUNK, v.shape[-1])
        gate_a = jnp.concatenate([flat(gate_ref[0]), flat(gate_ref[1])], axis=1).astype(F32)
        gate_b = jnp.concatenate([flat(gate_ref[2]), flat(gate_ref[3])], axis=1).astype(F32)
        mixed = (gate_a * _dot(spatial_gate(), wa_ref[...])
                 + gate_b * _dot(flat(yb_ref[slot]), wb_ref[...]))
        x1 = flat(x_ref[...]) + _dot(mixed.astype(BF16), wo_ref[...])
        x1_ref[...] = x1.reshape(x1_ref.shape)
        hn_ref[...] = _rms_norm(x1, g2_ref[...]).astype(BF16).reshape(hn_ref.shape)

    @pl.when(n == 0)
    def _():
        mlstm(0)

    @pl.when((n > 0) & (n < n_steps))
    def _():
        mlstm(n % 2)
        merge(1 - n % 2)

    @pl.when(n == n_steps)
    def _():
        merge((n_steps - 1) % 2)


def _mix(proj, u, vt, gates_col, gates_row, bias_col, bias_row, wq, wk, norm_g,
         x, ws, bs_t, w_a, w_b, w_out, g2, cast_w, layer, batch):
    _, t, width = proj.shape
    n_groups = width // GM_GROUP_DIM
    assert ws.shape == (n_groups, CHUNK, CHUNK)
    seq = t // batch
    d = x.shape[-1]
    head_dim = width // ML_HEADS
    n_if = gates_row.shape[1]
    n_steps = seq // CHUNK
    n_gate = 2 * (PROJ_GB - PROJ_GA)
    assert PROJ_O == PROJ_M + 1 and PROJ_M % 2 == 0 and PROJ_GA % n_gate == 0
    assert batch * n_if <= LANES, (batch, n_if)
    proj4 = proj.reshape(proj.shape[0], batch, seq, width)
    cast_rows = _cast_rows(cast_w.shape[1], n_steps)
    at = lambda n: jnp.minimum(n, n_steps - 1)
    done = lambda n: jnp.maximum(n - 1, 0)
    const2 = lambda n: (0, 0)
    const3 = lambda n: (0, 0, 0)
    resident = dict(pipeline_mode=pl.Buffered(1))
    n_state = batch * ML_HEADS
    kern = functools.partial(_mix_kernel, head_dim=head_dim, batch=batch, n_steps=n_steps)
    x1, hn, cast = pl.pallas_call(
        kern,
        grid=(n_steps + 1,),
        in_specs=[
            pl.BlockSpec((2, batch, CHUNK, width), lambda n: (PROJ_M // 2, 0, at(n), 0)),
            pl.BlockSpec((batch, None, width, CHUNK), lambda n: (0, at(n), 0, 0)),
            pl.BlockSpec((batch, CHUNK, LANES), lambda n: (0, at(n), 0)),
            pl.BlockSpec((batch, n_if, CHUNK), lambda n: (0, 0, at(n))),
            pl.BlockSpec((n_if, 1), const2),
            pl.BlockSpec((1, LANES), const2),
            pl.BlockSpec((ML_HEADS, head_dim, head_dim), const3),
            pl.BlockSpec((ML_HEADS, head_dim, head_dim), const3),
            pl.BlockSpec((1, width), const2),
            pl.BlockSpec(memory_space=pl.ANY),
            pl.BlockSpec((batch, CHUNK, width), lambda n: (0, done(n), 0)),
            pl.BlockSpec((None, batch, CHUNK, width), lambda n: (PROJ_V, 0, done(n), 0)),
            pl.BlockSpec((n_groups, CHUNK, CHUNK), const3),
            pl.BlockSpec((CHUNK, n_groups), const2),
            pl.BlockSpec((n_gate, batch, CHUNK, SECTION), lambda n: (PROJ_GA // n_gate, 0, done(n), 0)),
            pl.BlockSpec((width, d), const2, **resident),
            pl.BlockSpec((width, d), const2, **resident),
            pl.BlockSpec((d, d), const2, **resident),
            pl.BlockSpec((1, d), const2),
            pl.BlockSpec((None, cast_rows, cast_w.shape[2]), lambda n: (layer, at(n), 0)),
        ],
        out_specs=[
            pl.BlockSpec((batch, CHUNK, d), lambda n: (0, done(n), 0)),
            pl.BlockSpec((batch, CHUNK, d), lambda n: (0, done(n), 0)),
            pl.BlockSpec((cast_rows, cast_w.shape[2]), lambda n: (at(n), 0)),
        ],
        out_shape=[
            jax.ShapeDtypeStruct((batch, seq, d), F32),
            jax.ShapeDtypeStruct((batch, seq, d), BF16),
            jax.ShapeDtypeStruct(cast_w.shape[1:], BF16),
        ],
        scratch_shapes=[
            pltpu.VMEM((n_state, head_dim + SUBLANES, head_dim), F32),
            pltpu.VMEM((n_state, LANES), F32),
            pltpu.VMEM((ML_HEADS, head_dim, head_dim), BF16),
            pltpu.VMEM((ML_HEADS, head_dim, head_dim), BF16),
            pltpu.VMEM((2, batch, CHUNK, width), BF16),
            pltpu.VMEM((n_groups, CHUNK, CHUNK), BF16),
            pltpu.VMEM((XN_SLOTS, batch, CHUNK, d), F32),
            pltpu.SemaphoreType.DMA((XN_SLOTS,)),
        ],
        compiler_params=_params("arbitrary"),
        name="mix",
    )(proj4, vt.reshape(batch, n_steps, width, CHUNK), gates_col.reshape(batch, seq, LANES), gates_row, bias_col, bias_row, wq, wk, norm_g,
      x, u.reshape(batch, seq, width), proj4, ws, bs_t, proj4, w_a, w_b, w_out, g2, cast_w)
    return x1.reshape(t, d), hn.reshape(t, d), cast


def _ffn_kernel(hn_ref, x1_ref, w1_ref, w2_ref, gf_ref, out_ref, *, final_norm):
    j = pl.program_id(1)

    @pl.when(j == 0)
    def _():
        out_ref[...] = x1_ref[...]

    h = jnp.square(jnp.maximum(_dot(hn_ref[...], w1_ref[...]), 0.0)).astype(BF16)
    out_ref[...] += _dot(h, w2_ref[...])

    if final_norm:
        @pl.when(j == pl.num_programs(1) - 1)
        def _():
            out_ref[...] = _rms_norm(out_ref[...], gf_ref[...])


def _ffn(hn, x1, w1, w2, gf, tm, tf, final_norm):
    t, d = x1.shape
    d_ff = w1.shape[1]
    return pl.pallas_call(
        functools.partial(_ffn_kernel, final_norm=final_norm),
        grid=(t // tm, d_ff // tf),
        in_specs=[
            pl.BlockSpec((tm, d), lambda i, j: (i, 0)),
            pl.BlockSpec((tm, d), lambda i, j: (i, 0)),
            pl.BlockSpec((d, tf), lambda i, j: (0, j)),
            pl.BlockSpec((tf, d), lambda i, j: (j, 0)),
            pl.BlockSpec((1, d), lambda i, j: (0, 0)),
        ],
        out_specs=pl.BlockSpec((tm, d), lambda i, j: (i, 0)),
        out_shape=jax.ShapeDtypeStruct((t, d), F32),
        compiler_params=_params("arbitrary", "arbitrary"),
        name="ffn",
    )(hn, x1, w1, w2, gf)


def _tiles(t):
    return dict(prologue=min(t, 1024), inproj=min(t, 1024), ffn_m=min(t, 512), ffn_f=2048)


def kernel(x, norm1_g, w_in, b_gate, gm_ln_g, gm_ln_b, gm_ws, gm_bs, ml_conv_w, ml_conv_b, ml_wq, ml_wk,
           ml_ig_b, ml_fg_b, ml_norm_g, w_a, w_b, w_out, norm2_g, w_ff1, w_ff2, norm_f_g):
    batch, seq, d = x.shape
    depth = w_in.shape[0]
    t = batch * seq
    tiles = _tiles(t)
    if_row = SEC_GA * SECTION
    n_if = 2 * ML_HEADS
    xt = x.reshape(t, d)
    w_t = jnp.swapaxes(w_in, 1, 2)
    for l in range(depth):
        gate_bias = jnp.concatenate([ml_ig_b[l], ml_fg_b[l]])
        vecs = jnp.concatenate([
            gm_ln_g[l][None], gm_ln_b[l][None], ml_conv_b[l][None], ml_conv_w[l],
            jnp.zeros((1, SECTION), F32), b_gate[l].reshape(4, SECTION),
            jnp.zeros((N_VEC_ROWS - VEC_GATE_B - 5, SECTION), F32)])
        xn, gates_col, gates_row, u = _prologue(xt, norm1_g[l][None], w_t, l, if_row, n_if, batch,
                                                tiles["prologue"])
        proj, vt, w2_bf, wa_bf, wb_bf, wo_bf = _inproj(
            xn, w_t, l, if_row + n_if, vecs, (w_ff2, w_a, w_b, w_out), batch, tiles["inproj"])
        x1, hn, w1_bf = _mix(
            proj, u, vt, gates_col, gates_row, gate_bias[:, None], jnp.pad(gate_bias, (0, LANES - n_if))[None],
            ml_wq[l], ml_wk[l], ml_norm_g[l][None], xt.reshape(batch, seq, d), gm_ws[l], gm_bs[l].T,
            wa_bf, wb_bf, wo_bf, norm2_g[l][None], w_ff1, l, batch)
        xt = _ffn(hn, x1, w1_bf, w2_bf, norm_f_g[None], tiles["ffn_m"], tiles["ffn_f"],
                  final_norm=l == depth - 1)
    return xt.reshape(batch, seq, d)
```
